```python
import math
import jax, jax.numpy as jnp
from jax import lax
import numpy as np

D_MODEL = 2048
BATCH = 4
SEQ = 4096
DEPTH = 1

CHUNK = 64
N_META = 16
Q_BLOCK = 128
RMS_EPS = 1e-5

SB_HEAD_DIM = 128
SB_HEADS = (D_MODEL // 2) // SB_HEAD_DIM
SB_WIDTH = SB_HEADS * SB_HEAD_DIM
DA_HEAD_DIM = 64
DA_V_DIM = 2 * DA_HEAD_DIM
DA_HEADS = (D_MODEL // 2) // DA_V_DIM
DA_QK_WIDTH = DA_HEADS * 2 * DA_HEAD_DIM
DA_WIDTH = DA_HEADS * DA_V_DIM
MIX_WIDTH = SB_WIDTH + DA_WIDTH
IN_COLS = 3 * SB_WIDTH + 2 * DA_QK_WIDTH + DA_WIDTH
SPLIT_POINTS = (SB_WIDTH, 2 * SB_WIDTH, 3 * SB_WIDTH,
                3 * SB_WIDTH + DA_QK_WIDTH, 3 * SB_WIDTH + 2 * DA_QK_WIDTH)
ROPE_THETA = 500000.0
ROPE_DIM = DA_HEAD_DIM // 4

N_EXPERTS = 32
TOP_K = 4
D_EXPERT = D_MODEL
SWIGLU_LIMIT = 7.0
SWIGLU_ALPHA = 1.702
MOE_BLOCK = 256

kernel_name = "hymba_stickbreak_diffattn_moe"


def _rmsnorm(x, g):
    xf = x.astype(jnp.float32)
    y = xf * lax.rsqrt(jnp.mean(xf * xf, axis=-1, keepdims=True) + RMS_EPS)
    return (y * g.astype(jnp.float32)).astype(x.dtype)


def _partial_rope(x, pos):
    half = ROPE_DIM // 2
    inv_freq = ROPE_THETA ** (-(jnp.arange(half, dtype=jnp.float32) * 2.0 / ROPE_DIM))
    ang = pos.astype(jnp.float32)[:, None] * inv_freq[None, :]
    cos = jnp.cos(ang)[None, :, None, None, :]
    sin = jnp.sin(ang)[None, :, None, None, :]
    xr = x[..., :ROPE_DIM].astype(jnp.float32)
    x1, x2 = xr[..., :half], xr[..., half:]
    rot = jnp.concatenate([x1 * cos - x2 * sin, x2 * cos + x1 * sin], axis=-1)
    return jnp.concatenate([rot.astype(x.dtype), x[..., ROPE_DIM:]], axis=-1)


def _chunk_ids(pos, n_valid):
    real = 1 + (pos - N_META) // CHUNK
    c = jnp.where(pos < N_META, 0, real)
    return jnp.where(pos < n_valid, c, jnp.iinfo(jnp.int32).max).astype(jnp.int32)


def _stick_breaking_attention(q, k, v):
    b, h, lp, d = q.shape
    nb = lp // Q_BLOCK
    q_blocks = q.reshape(b, h, nb, Q_BLOCK, d).transpose(2, 0, 1, 3, 4)
    k_pos = jnp.arange(lp, dtype=jnp.int32)
    scale = d ** -0.5

    def one_block(args):
        q_blk, blk = args
        z = jnp.einsum('bhqd,bhkd->bhqk', q_blk, k).astype(jnp.float32) * scale
        q_pos = blk * Q_BLOCK + jnp.arange(Q_BLOCK, dtype=jnp.int32)
        strict = k_pos[None, :] < q_pos[:, None]
        log_keep = jnp.where(strict, jax.nn.log_sigmoid(-z), 0.0)
        log_survive = lax.cumsum(log_keep, axis=3, reverse=True) - log_keep
        w = jnp.where(strict, jnp.exp(jax.nn.log_sigmoid(z) + log_survive), 0.0)
        return jnp.einsum('bhqk,bhkd->bhqd', w.astype(v.dtype), v)

    out = lax.map(one_block, (q_blocks, jnp.arange(nb, dtype=jnp.int32)))
    return out.transpose(1, 0, 3, 2, 4).reshape(b, lp, h, v.shape[-1])


def _differential_attention(q, k, v, lam, chunk):
    b, h, _, lp, d = q.shape
    nb = lp // Q_BLOCK
    q_blocks = q.reshape(b, h, 2, nb, Q_BLOCK, d).transpose(3, 0, 1, 2, 4, 5)
    q_chunks = chunk.reshape(nb, Q_BLOCK)
    scale = d ** -0.5

    def one_block(args):
        q_blk, q_chunk = args
        s = jnp.einsum('bhcqd,bhckd->bhcqk', q_blk, k).astype(jnp.float32) * scale
        visible = chunk[None, :] <= q_chunk[:, None]
        p = jax.nn.softmax(jnp.where(visible, s, -jnp.inf), axis=-1)
        a = p[:, :, 0] - lam * p[:, :, 1]
        return jnp.einsum('bhqk,bhkd->bhqd', a.astype(v.dtype), v)

    out = lax.map(one_block, (q_blocks, q_chunks))
    return out.transpose(1, 0, 3, 2, 4).reshape(b, lp, h, v.shape[-1])


def _moe(x, w_router, b_router, w_gate_up, b_gate_up, w_down, b_down):
    n_tok, d = x.shape
    n_assign = n_tok * TOP_K
    logits = (x @ w_router + b_router).astype(jnp.float32)
    top_logit, top_e = lax.top_k(logits, TOP_K)
    gates = jax.nn.softmax(top_logit, axis=-1)
    flat_e = top_e.reshape(-1).astype(jnp.int32)
    flat_tok = jnp.repeat(jnp.arange(n_tok, dtype=jnp.int32), TOP_K)
    flat_g = gates.reshape(-1)
    order = jnp.argsort(flat_e, stable=True)
    e_s, tok_s, g_s = flat_e[order], flat_tok[order], flat_g[order]
    counts = jnp.bincount(flat_e, length=N_EXPERTS).astype(jnp.int32)
    padded = (counts + MOE_BLOCK - 1) // MOE_BLOCK * MOE_BLOCK
    pad_end = jnp.cumsum(padded).astype(jnp.int32)
    pad_start = pad_end - padded
    grp_start = jnp.cumsum(counts).astype(jnp.int32) - counts
    dest = pad_start[e_s] + (jnp.arange(n_assign, dtype=jnp.int32) - grp_start[e_s])
    n_blocks = -(-n_assign // MOE_BLOCK) + N_EXPERTS
    rows = jnp.zeros((n_blocks * MOE_BLOCK, d), x.dtype).at[dest].set(x[tok_s])
    blk_start = jnp.arange(n_blocks, dtype=jnp.int32) * MOE_BLOCK
    blk_e = jnp.minimum(jnp.searchsorted(pad_end, blk_start, side='right'), N_EXPERTS - 1)
    blk_used = blk_start < pad_end[-1]

    def expert_ffn(xb, e):
        hgu = xb @ w_gate_up[e] + b_gate_up[e]
        gate = jnp.minimum(hgu[:, :D_EXPERT], SWIGLU_LIMIT)
        up = jnp.clip(hgu[:, D_EXPERT:], -SWIGLU_LIMIT, SWIGLU_LIMIT)
        act = (up + 1.0) * gate * jax.nn.sigmoid(SWIGLU_ALPHA * gate)
        return (act @ w_down[e] + b_down[e]).astype(xb.dtype)

    def one_block(args):
        xb, e, used = args
        return lax.cond(used, expert_ffn, lambda xb_, e_: jnp.zeros_like(xb_), xb, e)

    out_rows = lax.map(one_block, (rows.reshape(n_blocks, MOE_BLOCK, d), blk_e, blk_used))
    out_rows = out_rows.reshape(n_blocks * MOE_BLOCK, d)
    y = out_rows[dest] * g_s[:, None].astype(x.dtype)
    return jax.ops.segment_sum(y, tok_s, num_segments=n_tok)


def setup_inputs(seed: int = 0) -> dict:
    key = jax.random.key(seed)
    ks = jax.random.split(key, 19)
    f32 = jnp.float32
    nrm = lambda k, shape: jax.random.normal(k, shape, f32)
    return {
        'x': nrm(ks[0], (BATCH, SEQ, D_MODEL)),
        'meta_tokens': nrm(ks[1], (N_META, D_MODEL)),
        'g_mix': 1.0 + 0.02 * nrm(ks[2], (DEPTH, D_MODEL)),
        'w_in': nrm(ks[3], (DEPTH, D_MODEL, IN_COLS)) * D_MODEL ** -0.5,
        'lam_q1': 0.1 * nrm(ks[4], (DEPTH, DA_HEAD_DIM)),
        'lam_k1': 0.1 * nrm(ks[5], (DEPTH, DA_HEAD_DIM)),
        'lam_q2': 0.1 * nrm(ks[6], (DEPTH, DA_HEAD_DIM)),
        'lam_k2': 0.1 * nrm(ks[7], (DEPTH, DA_HEAD_DIM)),
        'g_sb_out': 1.0 + 0.02 * nrm(ks[8], (DEPTH, SB_HEADS, SB_HEAD_DIM)),
        'g_da_out': 1.0 + 0.02 * nrm(ks[9], (DEPTH, DA_HEADS, DA_V_DIM)),
        'w_out': nrm(ks[10], (DEPTH, MIX_WIDTH, D_MODEL)) * MIX_WIDTH ** -0.5,
        'g_ffn': 1.0 + 0.02 * nrm(ks[11], (DEPTH, D_MODEL)),
        'w_router': nrm(ks[12], (DEPTH, D_MODEL, N_EXPERTS)) * D_MODEL ** -0.5,
        'b_router': 0.01 * nrm(ks[13], (DEPTH, N_EXPERTS)),
        'w_gate_up': nrm(ks[14], (DEPTH, N_EXPERTS, D_MODEL, 2 * D_EXPERT)) * D_MODEL ** -0.5,
        'b_gate_up': 0.01 * nrm(ks[15], (DEPTH, N_EXPERTS, 2 * D_EXPERT)),
        'w_down': nrm(ks[16], (DEPTH, N_EXPERTS, D_EXPERT, D_MODEL)) * D_EXPERT ** -0.5,
        'b_down': 0.01 * nrm(ks[17], (DEPTH, N_EXPERTS, D_MODEL)),
        'g_final': 1.0 + 0.02 * nrm(ks[18], (D_MODEL,)),
    }


def reference(x, meta_tokens, g_mix, w_in, lam_q1, lam_k1, lam_q2, lam_k2, g_sb_out, g_da_out,
              w_out, g_ffn, w_router, b_router, w_gate_up, b_gate_up, w_down, b_down, g_final):
    b, s, d = x.shape
    n_len = N_META + s
    lp = -(-n_len // Q_BLOCK) * Q_BLOCK
    pos = jnp.arange(lp, dtype=jnp.int32)
    chunk = _chunk_ids(pos, n_len)
    meta = jnp.broadcast_to(meta_tokens.astype(x.dtype)[None], (b, N_META, d))
    h = jnp.concatenate([meta, x], axis=1)

    for layer in range(DEPTH):
        lam_init = 0.8 - 0.6 * math.exp(-0.3 * layer)
        u = _rmsnorm(h, g_mix[layer])
        proj = jnp.pad(u @ w_in[layer], ((0, 0), (0, lp - n_len), (0, 0)))
        sb_q, sb_k, sb_v, da_q, da_k, da_v = jnp.split(proj, SPLIT_POINTS, axis=-1)

        def sb_heads(t):
            return t.reshape(b, lp, SB_HEADS, SB_HEAD_DIM).transpose(0, 2, 1, 3)

        def da_pair(t):
            t = _partial_rope(t.reshape(b, lp, DA_HEADS, 2, DA_HEAD_DIM), pos)
            return t.transpose(0, 2, 3, 1, 4)

        o_sb = _stick_breaking_attention(sb_heads(sb_q), sb_heads(sb_k), sb_heads(sb_v))[:, :n_len]
        lam = (jnp.exp(jnp.sum(lam_q1[layer].astype(jnp.float32) * lam_k1[layer].astype(jnp.float32)))
               - jnp.exp(jnp.sum(lam_q2[layer].astype(jnp.float32) * lam_k2[layer].astype(jnp.float32)))
               + lam_init)
        da_vh = da_v.reshape(b, lp, DA_HEADS, DA_V_DIM).transpose(0, 2, 1, 3)
        o_da = _differential_attention(da_pair(da_q), da_pair(da_k), da_vh, lam, chunk)[:, :n_len]

        o_sb = _rmsnorm(o_sb, g_sb_out[layer])
        o_da = _rmsnorm(o_da, g_da_out[layer]) * (1.0 - lam_init)
        mixed = jnp.concatenate([o_sb.reshape(b, n_len, SB_WIDTH),
                                 o_da.reshape(b, n_len, DA_WIDTH)], axis=-1)
        h = h + mixed @ w_out[layer]
        if layer == DEPTH - 1:
            h = h[:, N_META:]
        u = _rmsnorm(h, g_ffn[layer])
        n_tok = h.shape[0] * h.shape[1]
        h = h + _moe(u.reshape(n_tok, d), w_router[layer], b_router[layer], w_gate_up[layer],
                     b_gate_up[layer], w_down[layer], b_down[layer]).reshape(h.shape)

    return _rmsnorm(h, g_final)
```

```python
import functools
import math

import jax
import jax.numpy as jnp
from jax import lax
from jax.experimental import pallas as pl
from jax.experimental.pallas import tpu as pltpu

F32 = jnp.float32
BF16 = jnp.bfloat16

CHUNK = 64
N_META = 16
RMS_EPS = 1e-5
SB_HEAD_DIM = 128
DA_HEAD_DIM = 64
DA_V_DIM = 2 * DA_HEAD_DIM
ROPE_THETA = 500000.0
ROPE_DIM = DA_HEAD_DIM // 4
TOP_K = 4
SWIGLU_LIMIT = 7.0
SWIGLU_ALPHA = 1.702

LANES = 128
META_PAD = 128
NEG_BIG = -1e30
VMEM_LIMIT = 56 * 1024 * 1024


def _cparams(sem):
    return pltpu.CompilerParams(dimension_semantics=sem, vmem_limit_bytes=VMEM_LIMIT)


def _in_proj_kernel(x_ref, g_ref, w_ref, c_ref, sa_ref, sb_ref, o_ref, u_scr, *, tn, rope_lo, rope_hi):
    n = pl.program_id(1)

    @pl.when(n == 0)
    def _():
        x = x_ref[...]
        ms = jnp.mean(x * x, axis=-1, keepdims=True)
        u_scr[...] = ((x * lax.rsqrt(ms + RMS_EPS)) * g_ref[...]).astype(BF16)

    acc = jnp.dot(u_scr[...], w_ref[...], preferred_element_type=F32)
    is_rope = jnp.logical_and(n >= rope_lo, n < rope_hi)

    @pl.when(is_rope)
    def _():
        for c in range(tn // LANES):
            xc = acc[:, c * LANES:(c + 1) * LANES]
            r = (xc * c_ref[...] + pltpu.roll(xc, LANES - ROPE_DIM // 2, 1) * sa_ref[...]
                 + pltpu.roll(xc, ROPE_DIM // 2, 1) * sb_ref[...])
            o_ref[:, c * LANES:(c + 1) * LANES] = r.astype(BF16)

    @pl.when(jnp.logical_not(is_rope))
    def _():
        o_ref[...] = acc.astype(BF16)


def _rope_tables(pos):
    half = ROPE_DIM // 2
    inv_freq = ROPE_THETA ** (-(jnp.arange(half, dtype=F32) * 2.0 / ROPE_DIM))
    ang = pos.astype(F32)[:, None] * inv_freq[None, :]
    cos, sin = jnp.cos(ang), jnp.sin(ang)
    p = pos.shape[0]
    ones = jnp.ones((p, DA_HEAD_DIM - ROPE_DIM), F32)
    zeros8 = jnp.zeros((p, half), F32)
    zeros48 = jnp.zeros((p, DA_HEAD_DIM - ROPE_DIM), F32)
    c64 = jnp.concatenate([cos, cos, ones], axis=1)
    sa64 = jnp.concatenate([-sin, zeros8, zeros48], axis=1)
    sb64 = jnp.concatenate([zeros8, sin, zeros48], axis=1)
    tile2 = lambda t: jnp.concatenate([t, t], axis=1)
    return tile2(c64), tile2(sa64), tile2(sb64)


def _in_proj(x2d, g, w_bf, tables, *, tm, tn, rope_cols, pos_blocks):
    m, d = x2d.shape
    n_cols = w_bf.shape[1]
    c_t, sa_t, sb_t = tables
    kern = functools.partial(_in_proj_kernel, tn=tn, rope_lo=rope_cols[0] // tn, rope_hi=rope_cols[1] // tn)
    tab_spec = pl.BlockSpec((tm, LANES), lambda i, n: (i % pos_blocks, 0))
    return pl.pallas_call(
        kern,
        grid=(m // tm, n_cols // tn),
        in_specs=[
            pl.BlockSpec((tm, d), lambda i, n: (i, 0)),
            pl.BlockSpec((1, d), lambda i, n: (0, 0)),
            pl.BlockSpec((d, tn), lambda i, n: (0, n)),
            tab_spec, tab_spec, tab_spec,
        ],
        out_specs=pl.BlockSpec((tm, tn), lambda i, n: (i, n)),
        out_shape=jax.ShapeDtypeStruct((m, n_cols), BF16),
        scratch_shapes=[pltpu.VMEM((tm, d), BF16)],
        compiler_params=_cparams(("parallel", "arbitrary")),
        name="in_proj",
    )(x2d, g, w_bf, c_t, sa_t, sb_t)


def _dot_nt(a, b):
    return lax.dot_general(a, b, (((1,), (1,)), ((), ())), preferred_element_type=F32)


def _strict_upper_ones(n):
    j = lax.broadcasted_iota(jnp.int32, (n, n), 0)
    s = lax.broadcasted_iota(jnp.int32, (n, n), 1)
    return jnp.where(j > s, 1.0, 0.0).astype(BF16)


def _sb_block(q, kb, vb, u_mat, run, acc, mask, scale):
    z = _dot_nt(q, kb) * scale
    lk = -(jnp.maximum(z, 0.0) + jnp.log1p(jnp.exp(-jnp.abs(z))))
    if mask is not None:
        lk = jnp.where(mask, lk, 0.0)
    hi = lk.astype(BF16)
    lo = (lk - hi.astype(F32)).astype(BF16)
    cs = jnp.dot(hi, u_mat, preferred_element_type=F32) + jnp.dot(lo, u_mat, preferred_element_type=F32)
    w = jnp.exp(z + lk + cs + run)
    if mask is not None:
        w = jnp.where(mask, w, 0.0)
    acc = acc + jnp.dot(w.astype(BF16), vb, preferred_element_type=F32)
    run = run + (cs[:, :1] + lk[:, :1])
    return run, acc


def _sb_kernel(q_ref, k_ref, v_ref, km_ref, vm_ref, g_ref, o_ref, *, tq, scale):
    i = pl.program_id(2)
    q = q_ref[...]
    u_blk = _strict_upper_ones(tq)
    row = lax.broadcasted_iota(jnp.int32, (tq, tq), 0)
    col = lax.broadcasted_iota(jnp.int32, (tq, tq), 1)

    start = pl.multiple_of(i * tq, tq)
    run = jnp.zeros((tq, 1), F32)
    acc = jnp.zeros((tq, SB_HEAD_DIM), F32)
    run, acc = _sb_block(q, k_ref[pl.ds(start, tq), :], v_ref[pl.ds(start, tq), :], u_blk,
                         run, acc, col < row, scale)

    def body(jj, carry):
        r, a = carry
        s0 = pl.multiple_of((i - 1 - jj) * tq, tq)
        return _sb_block(q, k_ref[pl.ds(s0, tq), :], v_ref[pl.ds(s0, tq), :], u_blk, r, a, None, scale)

    run, acc = lax.fori_loop(0, i, body, (run, acc))

    mcol = lax.broadcasted_iota(jnp.int32, (tq, META_PAD), 1)
    run, acc = _sb_block(q, km_ref[...], vm_ref[...], _strict_upper_ones(META_PAD), run, acc,
                         mcol < N_META, scale)

    ms = jnp.mean(acc * acc, axis=-1, keepdims=True)
    o_ref[...] = ((acc * lax.rsqrt(ms + RMS_EPS)) * g_ref[...]).astype(BF16)


def _sb_attention(proj, proj_meta, g_sb, *, batch, seq, heads, tq):
    hd = SB_HEAD_DIM
    kern = functools.partial(_sb_kernel, tq=tq, scale=hd ** -0.5)
    return pl.pallas_call(
        kern,
        grid=(batch, heads, seq // tq),
        in_specs=[
            pl.BlockSpec((None, tq, hd), lambda b, h, i: (b, i, h)),
            pl.BlockSpec((None, seq, hd), lambda b, h, i: (b, 0, heads + h)),
            pl.BlockSpec((None, seq, hd), lambda b, h, i: (b, 0, 2 * heads + h)),
            pl.BlockSpec((META_PAD, hd), lambda b, h, i: (0, heads + h)),
            pl.BlockSpec((META_PAD, hd), lambda b, h, i: (0, 2 * heads + h)),
            pl.BlockSpec((1, hd), lambda b, h, i: (0, h)),
        ],
        out_specs=pl.BlockSpec((None, tq, hd), lambda b, h, i: (b, i, h)),
        out_shape=jax.ShapeDtypeStruct((batch, seq, heads * hd), BF16),
        compiler_params=_cparams(("parallel", "parallel", "arbitrary")),
        name="sb_attn",
    )(proj, proj, proj, proj_meta, proj_meta, g_sb)


def _da_block(q1, q2, kb, vb, st, mask):
    m1, l1, a1, m2, l2, a2 = st

    def one(qc, m, l, a):
        s = _dot_nt(qc, kb)
        if mask is not None:
            s = jnp.where(mask, s, NEG_BIG)
        mn = jnp.maximum(m, jnp.max(s, axis=-1, keepdims=True))
        p = jnp.exp(s - mn)
        alpha = jnp.exp(m - mn)
        l = alpha * l + jnp.sum(p, axis=-1, keepdims=True)
        a = alpha * a + jnp.dot(p.astype(BF16), vb, preferred_element_type=F32)
        return mn, l, a

    m1, l1, a1 = one(q1, m1, l1, a1)
    m2, l2, a2 = one(q2, m2, l2, a2)
    return m1, l1, a1, m2, l2, a2


def _da_kernel(lq1_ref, lk1_ref, lq2_ref, lk2_ref, q_ref, k_ref, v_ref, km_ref, vm_ref, g_ref, o_ref,
               *, tq, scale, lam_init):
    i = pl.program_id(2)
    lam = (jnp.exp(jnp.sum(lq1_ref[...] * lk1_ref[...], axis=-1, keepdims=True))
           - jnp.exp(jnp.sum(lq2_ref[...] * lk2_ref[...], axis=-1, keepdims=True)) + lam_init)

    q = q_ref[...]
    lane = lax.broadcasted_iota(jnp.int32, q.shape, 1)
    qs = q * jnp.asarray(scale, BF16)
    zero = jnp.zeros_like(qs)
    q1 = jnp.where(lane < DA_HEAD_DIM, qs, zero)
    q2 = jnp.where(lane >= DA_HEAD_DIM, qs, zero)

    neg = jnp.full((tq, 1), NEG_BIG, F32)
    z1 = jnp.zeros((tq, 1), F32)
    za = jnp.zeros((tq, DA_V_DIM), F32)
    st = (neg, z1, za, neg, z1, za)

    mcol = lax.broadcasted_iota(jnp.int32, (tq, META_PAD), 1)
    st = _da_block(q1, q2, km_ref[...], vm_ref[...], st, mcol < N_META)

    def body(j, carry):
        s0 = pl.multiple_of(j * tq, tq)
        return _da_block(q1, q2, k_ref[pl.ds(s0, tq), :], v_ref[pl.ds(s0, tq), :], carry, None)

    st = lax.fori_loop(0, i, body, st)

    row = lax.broadcasted_iota(jnp.int32, (tq, tq), 0)
    col = lax.broadcasted_iota(jnp.int32, (tq, tq), 1)
    start = pl.multiple_of(i * tq, tq)
    shift = CHUNK.bit_length() - 1
    vis = jnp.right_shift(col, shift) <= jnp.right_shift(row, shift)
    m1, l1, a1, m2, l2, a2 = _da_block(q1, q2, k_ref[pl.ds(start, tq), :], v_ref[pl.ds(start, tq), :], st, vis)

    o = a1 / l1 - lam * (a2 / l2)
    ms = jnp.mean(o * o, axis=-1, keepdims=True)
    o_ref[...] = (((o * lax.rsqrt(ms + RMS_EPS)) * g_ref[...]) * (1.0 - lam_init)).astype(BF16)


def _da_attention(proj, proj_meta, lams, g_da, *, batch, seq, heads, tq, col0, lam_init):
    hd = DA_V_DIM
    qb, kb, vb = col0 // hd, col0 // hd + heads, col0 // hd + 2 * heads
    kern = functools.partial(_da_kernel, tq=tq, scale=DA_HEAD_DIM ** -0.5, lam_init=lam_init)
    lam_spec = pl.BlockSpec((1, DA_HEAD_DIM), lambda b, h, i: (0, 0))
    return pl.pallas_call(
        kern,
        grid=(batch, heads, seq // tq),
        in_specs=[
            lam_spec, lam_spec, lam_spec, lam_spec,
            pl.BlockSpec((None, tq, hd), lambda b, h, i: (b, i, qb + h)),
            pl.BlockSpec((None, seq, hd), lambda b, h, i: (b, 0, kb + h)),
            pl.BlockSpec((None, seq, hd), lambda b, h, i: (b, 0, vb + h)),
            pl.BlockSpec((META_PAD, hd), lambda b, h, i: (0, kb + h)),
            pl.BlockSpec((META_PAD, hd), lambda b, h, i: (0, vb + h)),
            pl.BlockSpec((1, hd), lambda b, h, i: (0, h)),
        ],
        out_specs=pl.BlockSpec((None, tq, hd), lambda b, h, i: (b, i, h)),
        out_shape=jax.ShapeDtypeStruct((batch, seq, heads * hd), BF16),
        compiler_params=_cparams(("parallel", "parallel", "arbitrary")),
        name="da_attn",
    )(*lams, proj, proj, proj, proj_meta, proj_meta, g_da)


def _split3_dot(a, b_hi, b_lo):
    a_hi = a.astype(BF16)
    a_lo = (a - a_hi.astype(F32)).astype(BF16)
    return (jnp.dot(a_hi, b_hi, preferred_element_type=F32)
            + jnp.dot(a_hi, b_lo, preferred_element_type=F32)
            + jnp.dot(a_lo, b_hi, preferred_element_type=F32))


def _out_proj_kernel(ms_ref, md_ref, ws_ref, wd_ref, x_ref, g_ref, wr_hi_ref, wr_lo_ref, br_ref,
                     h_ref, u_ref, e_ref, p_ref, *, n_exp):
    h = (x_ref[...]
         + jnp.dot(ms_ref[...], ws_ref[...], preferred_element_type=F32)
         + jnp.dot(md_ref[...], wd_ref[...], preferred_element_type=F32))
    h_ref[...] = h
    msq = jnp.mean(h * h, axis=-1, keepdims=True)
    u = (h * lax.rsqrt(msq + RMS_EPS)) * g_ref[...]
    u_ref[...] = u.astype(BF16)

    logits = _split3_dot(u, wr_hi_ref[...], wr_lo_ref[...]) + br_ref[...]
    lane = lax.broadcasted_iota(jnp.int32, logits.shape, 1).astype(F32)
    work = logits
    tops, idxs = [], []
    for _ in range(TOP_K):
        mx = jnp.max(work, axis=-1, keepdims=True)
        ix = jnp.min(jnp.where(work == mx, lane, float(n_exp)), axis=-1, keepdims=True)
        tops.append(mx)
        idxs.append(ix)
        work = jnp.where(lane == ix, -jnp.inf, work)
    ex = [jnp.exp(t - tops[0]) for t in tops]
    den = ex[0] + ex[1] + ex[2] + ex[3]
    kl = lax.broadcasted_iota(jnp.int32, (logits.shape[0], TOP_K), 1)
    e_out = jnp.zeros((logits.shape[0], TOP_K), F32)
    p_out = jnp.zeros((logits.shape[0], TOP_K), F32)
    for k in range(TOP_K):
        e_out = jnp.where(kl == k, idxs[k], e_out)
        p_out = jnp.where(kl == k, ex[k] / den, p_out)
    e_ref[...] = e_out.astype(jnp.int32)
    p_ref[...] = p_out


def _out_proj(mix_sb, mix_da, w_sb, w_da, x2d, g_ffn, wr_hi, wr_lo, b_r, *, tm):
    m, d = x2d.shape
    ks, kd = mix_sb.shape[1], mix_da.shape[1]
    n_exp = wr_hi.shape[1]
    kern = functools.partial(_out_proj_kernel, n_exp=n_exp)
    const = lambda shape: pl.BlockSpec(shape, lambda i: (0, 0))
    return pl.pallas_call(
        kern,
        grid=(m // tm,),
        in_specs=[
            pl.BlockSpec((tm, ks), lambda i: (i, 0)),
            pl.BlockSpec((tm, kd), lambda i: (i, 0)),
            const((ks, d)), const((kd, d)),
            pl.BlockSpec((tm, d), lambda i: (i, 0)),
            const((1, d)), const((d, n_exp)), const((d, n_exp)), const((1, n_exp)),
        ],
        out_specs=[
            pl.BlockSpec((tm, d), lambda i: (i, 0)),
            pl.BlockSpec((tm, d), lambda i: (i, 0)),
            pl.BlockSpec((tm, TOP_K), lambda i: (i, 0)),
            pl.BlockSpec((tm, TOP_K), lambda i: (i, 0)),
        ],
        out_shape=[
            jax.ShapeDtypeStruct((m, d), F32),
            jax.ShapeDtypeStruct((m, d), BF16),
            jax.ShapeDtypeStruct((m, TOP_K), jnp.int32),
            jax.ShapeDtypeStruct((m, TOP_K), F32),
        ],
        compiler_params=_cparams(("parallel",)),
        name="out_proj_router",
    )(mix_sb, mix_da, w_sb, w_da, x2d, g_ffn, wr_hi, wr_lo, b_r)


def _moe_kernel(be_ref, bu_ref, rb_ref, x_ref, wg_ref, wu_ref, wd_ref, bg_ref, bup_ref, bd_ref, gt_ref, o_ref,
                acc_ref):
    blk = pl.program_id(0)
    f = pl.program_id(1)
    nf = pl.num_programs(1)

    @pl.when(bu_ref[blk] > 0)
    def _():
        x = x_ref[...]
        g = jnp.dot(x, wg_ref[...], preferred_element_type=F32) + bg_ref[...]
        u = jnp.dot(x, wu_ref[...], preferred_element_type=F32) + bup_ref[...]
        gate = jnp.minimum(g, SWIGLU_LIMIT)
        up = jnp.clip(u, -SWIGLU_LIMIT, SWIGLU_LIMIT)
        act = (up + 1.0) * gate * jax.nn.sigmoid(SWIGLU_ALPHA * gate)
        part = jnp.dot(act.astype(BF16), wd_ref[...], preferred_element_type=F32)

        @pl.when(f == 0)
        def _():
            acc_ref[...] = part

        @pl.when(f > 0)
        def _():
            acc_ref[...] += part

        @pl.when(f == nf - 1)
        def _():
            o_ref[...] = (acc_ref[...] + bd_ref[...]) * gt_ref[...]


def _moe_ffn(blk_e, blk_used, blk_row, xs, w_gu, w_dn, b_gu, b_dn, row_gate, *, tm, tf):
    n_rows, d = xs.shape
    n_exp, _, f2 = w_gu.shape
    d_ff = f2 // 2
    nf = d_ff // tf
    n_blk = n_rows // tm
    b_gu3 = b_gu.reshape(n_exp, 1, f2)
    b_dn3 = b_dn.reshape(n_exp, 1, d)

    def fi(f, bu, b):
        return jnp.where(bu[b] > 0, f, nf - 1)

    grid_spec = pltpu.PrefetchScalarGridSpec(
        num_scalar_prefetch=3,
        grid=(n_blk, nf),
        in_specs=[
            pl.BlockSpec((tm, d), lambda b, f, be, bu, rb: (rb[b], 0)),
            pl.BlockSpec((None, d, tf), lambda b, f, be, bu, rb: (be[b], 0, fi(f, bu, b))),
            pl.BlockSpec((None, d, tf), lambda b, f, be, bu, rb: (be[b], 0, nf + fi(f, bu, b))),
            pl.BlockSpec((None, tf, d), lambda b, f, be, bu, rb: (be[b], fi(f, bu, b), 0)),
            pl.BlockSpec((None, 1, tf), lambda b, f, be, bu, rb: (be[b], 0, fi(f, bu, b))),
            pl.BlockSpec((None, 1, tf), lambda b, f, be, bu, rb: (be[b], 0, nf + fi(f, bu, b))),
            pl.BlockSpec((None, 1, d), lambda b, f, be, bu, rb: (be[b], 0, 0)),
            pl.BlockSpec((tm, 1), lambda b, f, be, bu, rb: (rb[b], 0)),
        ],
        out_specs=pl.BlockSpec((tm, d), lambda b, f, be, bu, rb: (rb[b], 0)),
        scratch_shapes=[pltpu.VMEM((tm, d), F32)],
    )
    return pl.pallas_call(
        _moe_kernel,
        grid_spec=grid_spec,
        out_shape=jax.ShapeDtypeStruct((n_rows, d), F32),
        compiler_params=_cparams(("arbitrary", "arbitrary")),
        name="moe_ffn",
    )(blk_e, blk_used, blk_row, xs, w_gu, w_gu, w_dn, b_gu3, b_gu3, b_dn3, row_gate)


def _final_kernel(h_ref, y_ref, g_ref, o_ref):
    h = h_ref[...] + y_ref[...]
    ms = jnp.mean(h * h, axis=-1, keepdims=True)
    o_ref[...] = (h * lax.rsqrt(ms + RMS_EPS)) * g_ref[...]


def _final(h1, y, g, *, tm):
    m, d = h1.shape
    return pl.pallas_call(
        _final_kernel,
        grid=(m // tm,),
        in_specs=[pl.BlockSpec((tm, d), lambda i: (i, 0)),
                  pl.BlockSpec((tm, d), lambda i: (i, 0)),
                  pl.BlockSpec((1, d), lambda i: (0, 0))],
        out_specs=pl.BlockSpec((tm, d), lambda i: (i, 0)),
        out_shape=jax.ShapeDtypeStruct((m, d), F32),
        compiler_params=_cparams(("parallel",)),
        name="final_norm",
    )(h1, y, g)


def _routing(top_e, gates, n_exp, tm):
    n_tok = top_e.shape[0]
    n_assign = n_tok * TOP_K
    flat_e = top_e.reshape(-1)
    flat_g = gates.reshape(-1)
    order = jnp.argsort(flat_e, stable=True).astype(jnp.int32)
    e_s = flat_e[order]
    counts = jnp.bincount(flat_e, length=n_exp).astype(jnp.int32)
    padded = (counts + tm - 1) // tm * tm
    pad_end = jnp.cumsum(padded).astype(jnp.int32)
    pad_start = pad_end - padded
    grp_start = jnp.cumsum(counts).astype(jnp.int32) - counts
    dest_sorted = pad_start[e_s] + (jnp.arange(n_assign, dtype=jnp.int32) - grp_start[e_s])
    n_blk = n_assign // tm + n_exp
    n_rows = n_blk * tm
    row_tok = jnp.zeros((n_rows,), jnp.int32).at[dest_sorted].set(order // TOP_K)
    row_gate = jnp.zeros((n_rows,), F32).at[dest_sorted].set(flat_g[order])
    dest_flat = jnp.zeros((n_assign,), jnp.int32).at[order].set(dest_sorted)
    blk_start = jnp.arange(n_blk, dtype=jnp.int32) * tm
    blk_used = (blk_start < pad_end[-1]).astype(jnp.int32)
    last_used = jnp.maximum(pad_end[-1] // tm - 1, 0)
    blk_e = jnp.minimum(jnp.searchsorted(pad_end, blk_start, side='right'), n_exp - 1).astype(jnp.int32)
    blk_e = jnp.where(blk_used > 0, blk_e, blk_e[last_used])
    blk_row = jnp.where(blk_used > 0, jnp.arange(n_blk, dtype=jnp.int32), last_used).astype(jnp.int32)
    return row_tok, row_gate, dest_flat, blk_e, blk_used, blk_row


def kernel(x, meta_tokens, g_mix, w_in, lam_q1, lam_k1, lam_q2, lam_k2, g_sb_out, g_da_out, w_out, g_ffn,
           w_router, b_router, w_gate_up, b_gate_up, w_down, b_down, g_final):
    b, s, d = x.shape
    depth = w_in.shape[0]
    assert depth == 1, "single-layer trunk"
    layer = 0
    sb_heads = (d // 2) // SB_HEAD_DIM
    da_heads = (d // 2) // DA_V_DIM
    sb_w = sb_heads * SB_HEAD_DIM
    da_col0 = 3 * sb_w
    da_qk_w = da_heads * 2 * DA_HEAD_DIM
    n_exp = w_router.shape[-1]
    lam_init = 0.8 - 0.6 * math.exp(-0.3 * layer)

    tm_proj, tn_proj = 512, 1024
    tq = 256
    tm_out = 512
    tm_moe, tf_moe = 512, 512
    tm_fin = 512

    x2d = x.reshape(b * s, d)
    w_in_bf = w_in[layer].astype(BF16)
    g_mix2 = g_mix[layer].reshape(1, d)
    rope_cols = (da_col0, da_col0 + 2 * da_qk_w)

    pos_f = N_META + jnp.arange(s, dtype=jnp.int32)
    proj = _in_proj(x2d, g_mix2, w_in_bf, _rope_tables(pos_f), tm=tm_proj, tn=tn_proj,
                    rope_cols=rope_cols, pos_blocks=s // tm_proj)
    meta_pad = jnp.zeros((META_PAD, d), x.dtype).at[:N_META].set(meta_tokens.astype(x.dtype))
    pos_m = jnp.arange(META_PAD, dtype=jnp.int32)
    proj_meta = _in_proj(meta_pad, g_mix2, w_in_bf, _rope_tables(pos_m), tm=META_PAD, tn=tn_proj,
                         rope_cols=rope_cols, pos_blocks=1)
    proj3 = proj.reshape(b, s, -1)

    mix_sb = _sb_attention(proj3, proj_meta, g_sb_out[layer].reshape(1, -1), batch=b, seq=s, heads=sb_heads, tq=tq)
    lams = tuple(t[layer].reshape(1, DA_HEAD_DIM).astype(F32) for t in (lam_q1, lam_k1, lam_q2, lam_k2))
    mix_da = _da_attention(proj3, proj_meta, lams, g_da_out[layer].reshape(1, -1), batch=b, seq=s, heads=da_heads, tq=tq,
                           col0=da_col0, lam_init=lam_init)

    w_out_bf = w_out[layer].astype(BF16)
    wr = w_router[layer]
    wr_hi = wr.astype(BF16)
    wr_lo = (wr - wr_hi.astype(F32)).astype(BF16)
    h1, u_bf, top_e, gates = _out_proj(
        mix_sb.reshape(b * s, -1), mix_da.reshape(b * s, -1), w_out_bf[:sb_w], w_out_bf[sb_w:], x2d,
        g_ffn[layer].reshape(1, d), wr_hi, wr_lo, b_router[layer].reshape(1, n_exp), tm=tm_out)

    row_tok, row_gate, dest_flat, blk_e, blk_used, blk_row = _routing(top_e, gates, n_exp, tm_moe)
    xs = jnp.take(u_bf, row_tok, axis=0)
    rows = _moe_ffn(blk_e, blk_used, blk_row, xs,w_gate_up[layer].astype(BF16), w_down[layer].astype(BF16),
                    b_gate_up[layer], b_down[layer], row_gate.reshape(-1, 1), tm=tm_moe, tf=tf_moe)
    y = jnp.take(rows, dest_flat, axis=0).reshape(b * s, TOP_K, d).sum(axis=1)

    out = _final(h1, y, g_final.reshape(1, d), tm=tm_fin)
    return out.reshape(b, s, d)
```

```python
import functools
import math

import jax
import jax.numpy as jnp
from jax import lax
from jax.experimental import pallas as pl
from jax.experimental.pallas import tpu as pltpu

F32 = jnp.float32
BF16 = jnp.bfloat16

CHUNK = 64
N_META = 16
RMS_EPS = 1e-5
SB_HEAD_DIM = 128
DA_HEAD_DIM = 64
DA_V_DIM = 2 * DA_HEAD_DIM
ROPE_THETA = 500000.0
ROPE_DIM = DA_HEAD_DIM // 4
TOP_K = 4
SWIGLU_LIMIT = 7.0
SWIGLU_ALPHA = 1.702

LANES = 128
META_PAD = 128
NEG_BIG = -1e30
VMEM_LIMIT = 56 * 1024 * 1024

TM_PROJ, TN_PROJ = 512, 1024
TQ_ATTN = 512
TK_SB = 256
TS_DA = 256
TM_OUT = 512
TM_MOE, TF_MOE = 1024, 256
TM_FINAL = 512


def _cparams(sem):
    return pltpu.CompilerParams(dimension_semantics=sem, vmem_limit_bytes=VMEM_LIMIT)


def _in_proj_kernel(x_ref, g_ref, w_ref, c_ref, sa_ref, sb_ref, o_ref, u_scr, *, tn, rope_lo, rope_hi):
    n = pl.program_id(1)

    @pl.when(n == 0)
    def _():
        x = x_ref[...]
        ms = jnp.mean(x * x, axis=-1, keepdims=True)
        u_scr[...] = ((x * lax.rsqrt(ms + RMS_EPS)) * g_ref[...]).astype(BF16)

    acc = jnp.dot(u_scr[...], w_ref[...], preferred_element_type=F32)
    is_rope = jnp.logical_and(n >= rope_lo, n < rope_hi)

    @pl.when(is_rope)
    def _():
        for c in range(tn // LANES):
            xc = acc[:, c * LANES:(c + 1) * LANES]
            r = (xc * c_ref[...] + pltpu.roll(xc, LANES - ROPE_DIM // 2, 1) * sa_ref[...]
                 + pltpu.roll(xc, ROPE_DIM // 2, 1) * sb_ref[...])
            o_ref[:, c * LANES:(c + 1) * LANES] = r.astype(BF16)

    @pl.when(jnp.logical_not(is_rope))
    def _():
        o_ref[...] = acc.astype(BF16)


def _rope_tables(pos):
    half = ROPE_DIM // 2
    inv_freq = ROPE_THETA ** (-(jnp.arange(half, dtype=F32) * 2.0 / ROPE_DIM))
    ang = pos.astype(F32)[:, None] * inv_freq[None, :]
    cos, sin = jnp.cos(ang), jnp.sin(ang)
    p = pos.shape[0]
    ones = jnp.ones((p, DA_HEAD_DIM - ROPE_DIM), F32)
    zeros8 = jnp.zeros((p, half), F32)
    zeros48 = jnp.zeros((p, DA_HEAD_DIM - ROPE_DIM), F32)
    c64 = jnp.concatenate([cos, cos, ones], axis=1)
    sa64 = jnp.concatenate([-sin, zeros8, zeros48], axis=1)
    sb64 = jnp.concatenate([zeros8, sin, zeros48], axis=1)
    tile2 = lambda t: jnp.concatenate([t, t], axis=1)
    return tile2(c64), tile2(sa64), tile2(sb64)


def _in_proj(x2d, g, w_bf, tables, *, tm, tn, rope_cols, pos_blocks):
    m, d = x2d.shape
    n_cols = w_bf.shape[1]
    c_t, sa_t, sb_t = tables
    kern = functools.partial(_in_proj_kernel, tn=tn, rope_lo=rope_cols[0] // tn, rope_hi=rope_cols[1] // tn)
    tab_spec = pl.BlockSpec((tm, LANES), lambda i, n: (i % pos_blocks, 0))
    return pl.pallas_call(
        kern,
        grid=(m // tm, n_cols // tn),
        in_specs=[
            pl.BlockSpec((tm, d), lambda i, n: (i, 0)),
            pl.BlockSpec((1, d), lambda i, n: (0, 0)),
            pl.BlockSpec((d, tn), lambda i, n: (0, n)),
            tab_spec, tab_spec, tab_spec,
        ],
        out_specs=pl.BlockSpec((tm, tn), lambda i, n: (i, n)),
        out_shape=jax.ShapeDtypeStruct((m, n_cols), BF16),
        scratch_shapes=[pltpu.VMEM((tm, d), BF16)],
        compiler_params=_cparams(("parallel", "arbitrary")),
        name="in_proj",
    )(x2d, g, w_bf, c_t, sa_t, sb_t)


def _dot_nt(a, b):
    return lax.dot_general(a, b, (((1,), (1,)), ((), ())), preferred_element_type=F32)


def _strict_upper_ones(n):
    j = lax.broadcasted_iota(jnp.int32, (n, n), 0)
    s = lax.broadcasted_iota(jnp.int32, (n, n), 1)
    return jnp.where(j > s, 1.0, 0.0).astype(BF16)


def _sb_block(q, kb, vb, u_mat, run, acc, mask, scale):
    z = _dot_nt(q, kb) * scale
    lk = -(jnp.maximum(z, 0.0) + jnp.log1p(jnp.exp(-jnp.abs(z))))
    if mask is not None:
        lk = jnp.where(mask, lk, 0.0)
    hi = lk.astype(BF16)
    lo = (lk - hi.astype(F32)).astype(BF16)
    cs = jnp.dot(hi, u_mat, preferred_element_type=F32) + jnp.dot(lo, u_mat, preferred_element_type=F32)
    w = jnp.exp(z + lk + cs + run)
    if mask is not None:
        w = jnp.where(mask, w, 0.0)
    acc = acc + jnp.dot(w.astype(BF16), vb, preferred_element_type=F32)
    run = run + (cs[:, :1] + lk[:, :1])
    return run, acc


def _sb_kernel(q_ref, k_ref, v_ref, km_ref, vm_ref, g_ref, o_ref, *, tq, tk, scale):
    i = pl.program_id(2)
    nsub = tq // tk
    u_blk = _strict_upper_ones(tk)
    row = lax.broadcasted_iota(jnp.int32, (tk, tk), 0)
    col = lax.broadcasted_iota(jnp.int32, (tk, tk), 1)
    diag_mask = col < row

    def kv(blk):
        s0 = pl.multiple_of(blk * tk, tk)
        return k_ref[pl.ds(s0, tk), :], v_ref[pl.ds(s0, tk), :]

    runs, accs = [], []
    for h in range(nsub):
        qh = q_ref[h * tk:(h + 1) * tk, :]
        run = jnp.zeros((tk, 1), F32)
        acc = jnp.zeros((tk, SB_HEAD_DIM), F32)
        for c in range(h, -1, -1):
            kb, vb = kv(i * nsub + c)
            run, acc = _sb_block(qh, kb, vb, u_blk, run, acc, diag_mask if c == h else None, scale)
        runs.append(run)
        accs.append(acc)
    run = jnp.concatenate(runs, axis=0)
    acc = jnp.concatenate(accs, axis=0)
    q = q_ref[...]

    def body(jj, carry):
        r, a = carry
        for c in range(nsub - 1, -1, -1):
            kb, vb = kv((i - 1 - jj) * nsub + c)
            r, a = _sb_block(q, kb, vb, u_blk, r, a, None, scale)
        return r, a

    run, acc = lax.fori_loop(0, i, body, (run, acc))

    mcol = lax.broadcasted_iota(jnp.int32, (tq, META_PAD), 1)
    run, acc = _sb_block(q, km_ref[...], vm_ref[...], _strict_upper_ones(META_PAD), run, acc,
                         mcol < N_META, scale)

    ms = jnp.mean(acc * acc, axis=-1, keepdims=True)
    o_ref[...] = ((acc * lax.rsqrt(ms + RMS_EPS)) * g_ref[...]).astype(BF16)


def _sb_attention(proj, proj_meta, g_sb, *, batch, seq, heads, tq, tk):
    hd = SB_HEAD_DIM
    kern = functools.partial(_sb_kernel, tq=tq, tk=tk, scale=hd ** -0.5)
    return pl.pallas_call(
        kern,
        grid=(batch, heads, seq // tq),
        in_specs=[
            pl.BlockSpec((None, tq, hd), lambda b, h, i: (b, i, h)),
            pl.BlockSpec((None, seq, hd), lambda b, h, i: (b, 0, heads + h)),
            pl.BlockSpec((None, seq, hd), lambda b, h, i: (b, 0, 2 * heads + h)),
            pl.BlockSpec((META_PAD, hd), lambda b, h, i: (0, heads + h)),
            pl.BlockSpec((META_PAD, hd), lambda b, h, i: (0, 2 * heads + h)),
            pl.BlockSpec((1, hd), lambda b, h, i: (0, h)),
        ],
        out_specs=pl.BlockSpec((None, tq, hd), lambda b, h, i: (b, i, h)),
        out_shape=jax.ShapeDtypeStruct((batch, seq, heads * hd), BF16),
        compiler_params=_cparams(("parallel", "parallel", "arbitrary")),
        name="sb_attn",
    )(proj, proj, proj, proj_meta, proj_meta, g_sb)


def _with_ones(vb):
    return jnp.concatenate([vb, jnp.ones_like(vb)], axis=1)


def _da_block(q1, q2, kb, vb1, st, mask):
    m1, a1, m2, a2 = st

    def one(qc, m, a):
        s = _dot_nt(qc, kb)
        if mask is not None:
            s = jnp.where(mask, s, NEG_BIG)
        mn = jnp.maximum(m, jnp.max(s, axis=-1, keepdims=True))
        p = jnp.exp(s - mn)
        a = jnp.exp(m - mn) * a + jnp.dot(p.astype(BF16), vb1, preferred_element_type=F32)
        return mn, a

    m1, a1 = one(q1, m1, a1)
    m2, a2 = one(q2, m2, a2)
    return m1, a1, m2, a2


def _da_kernel(lq1_ref, lk1_ref, lq2_ref, lk2_ref, q_ref, k_ref, v_ref, km_ref, vm_ref, g_ref, o_ref,
               *, tq, ts, scale, lam_init):
    i = pl.program_id(2)
    nsub = tq // ts
    lam = (jnp.exp(jnp.sum(lq1_ref[...] * lk1_ref[...], axis=-1, keepdims=True))
           - jnp.exp(jnp.sum(lq2_ref[...] * lk2_ref[...], axis=-1, keepdims=True)) + lam_init)

    q = q_ref[...]
    lane = lax.broadcasted_iota(jnp.int32, q.shape, 1)
    qs = q * jnp.asarray(scale, BF16)
    zero = jnp.zeros_like(qs)
    q1 = jnp.where(lane < DA_HEAD_DIM, qs, zero)
    q2 = jnp.where(lane >= DA_HEAD_DIM, qs, zero)

    neg = jnp.full((tq, 1), NEG_BIG, F32)
    za = jnp.zeros((tq, 2 * DA_V_DIM), F32)
    st = (neg, za, neg, za)

    mcol = lax.broadcasted_iota(jnp.int32, (tq, META_PAD), 1)
    st = _da_block(q1, q2, km_ref[...], _with_ones(vm_ref[...]), st, mcol < N_META)

    def body(j, carry):
        s0 = pl.multiple_of(j * tq, tq)
        return _da_block(q1, q2, k_ref[pl.ds(s0, tq), :], _with_ones(v_ref[pl.ds(s0, tq), :]), carry, None)

    st = lax.fori_loop(0, i, body, st)

    row = lax.broadcasted_iota(jnp.int32, (ts, ts), 0)
    col = lax.broadcasted_iota(jnp.int32, (ts, ts), 1)
    shift = CHUNK.bit_length() - 1
    vis = jnp.right_shift(col, shift) <= jnp.right_shift(row, shift)
    outs = []
    for h in range(nsub):
        rows = slice(h * ts, (h + 1) * ts)
        sth = tuple(t[rows] for t in st)
        for c in range(h + 1):
            s0 = pl.multiple_of(i * tq + c * ts, ts)
            sth = _da_block(q1[rows], q2[rows], k_ref[pl.ds(s0, ts), :], _with_ones(v_ref[pl.ds(s0, ts), :]),
                            sth, vis if c == h else None)
        _, a1, _, a2 = sth
        outs.append(a1[:, :DA_V_DIM] / a1[:, DA_V_DIM:] - lam * (a2[:, :DA_V_DIM] / a2[:, DA_V_DIM:]))
    o = jnp.concatenate(outs, axis=0)
    ms = jnp.mean(o * o, axis=-1, keepdims=True)
    o_ref[...] = (((o * lax.rsqrt(ms + RMS_EPS)) * g_ref[...]) * (1.0 - lam_init)).astype(BF16)


def _da_attention(proj, proj_meta, lams, g_da, *, batch, seq, heads, tq, ts, col0, lam_init):
    hd = DA_V_DIM
    qb, kb, vb = col0 // hd, col0 // hd + heads, col0 // hd + 2 * heads
    kern = functools.partial(_da_kernel, tq=tq, ts=ts, scale=DA_HEAD_DIM ** -0.5, lam_init=lam_init)
    lam_spec = pl.BlockSpec((1, DA_HEAD_DIM), lambda b, h, i: (0, 0))
    return pl.pallas_call(
        kern,
        grid=(batch, heads, seq // tq),
        in_specs=[
            lam_spec, lam_spec, lam_spec, lam_spec,
            pl.BlockSpec((None, tq, hd), lambda b, h, i: (b, i, qb + h)),
            pl.BlockSpec((None, seq, hd), lambda b, h, i: (b, 0, kb + h)),
            pl.BlockSpec((None, seq, hd), lambda b, h, i: (b, 0, vb + h)),
            pl.BlockSpec((META_PAD, hd), lambda b, h, i: (0, kb + h)),
            pl.BlockSpec((META_PAD, hd), lambda b, h, i: (0, vb + h)),
            pl.BlockSpec((1, hd), lambda b, h, i: (0, h)),
        ],
        out_specs=pl.BlockSpec((None, tq, hd), lambda b, h, i: (b, i, h)),
        out_shape=jax.ShapeDtypeStruct((batch, seq, heads * hd), BF16),
        compiler_params=_cparams(("parallel", "parallel", "arbitrary")),
        name="da_attn",
    )(*lams, proj, proj, proj, proj_meta, proj_meta, g_da)


def _split3_dot(a, b_hi, b_lo):
    a_hi = a.astype(BF16)
    a_lo = (a - a_hi.astype(F32)).astype(BF16)
    return (jnp.dot(a_hi, b_hi, preferred_element_type=F32)
            + jnp.dot(a_hi, b_lo, preferred_element_type=F32)
            + jnp.dot(a_lo, b_hi, preferred_element_type=F32))


def _out_proj_kernel(ms_ref, md_ref, ws_ref, wd_ref, x_ref, g_ref, wr_hi_ref, wr_lo_ref, br_ref,
                     h_ref, u_ref, e_ref, p_ref, *, n_exp):
    h = (x_ref[...]
         + jnp.dot(ms_ref[...], ws_ref[...], preferred_element_type=F32)
         + jnp.dot(md_ref[...], wd_ref[...], preferred_element_type=F32))
    h_ref[...] = h
    msq = jnp.mean(h * h, axis=-1, keepdims=True)
    u = (h * lax.rsqrt(msq + RMS_EPS)) * g_ref[...]
    u_ref[...] = u.astype(BF16)

    logits = _split3_dot(u, wr_hi_ref[...], wr_lo_ref[...]) + br_ref[...]
    lane = lax.broadcasted_iota(jnp.int32, logits.shape, 1).astype(F32)
    work = logits
    tops, idxs = [], []
    for _ in range(TOP_K):
        mx = jnp.max(work, axis=-1, keepdims=True)
        ix = jnp.min(jnp.where(work == mx, lane, float(n_exp)), axis=-1, keepdims=True)
        tops.append(mx)
        idxs.append(ix)
        work = jnp.where(lane == ix, -jnp.inf, work)
    ex = [jnp.exp(t - tops[0]) for t in tops]
    den = ex[0] + ex[1] + ex[2] + ex[3]
    kl = lax.broadcasted_iota(jnp.int32, (logits.shape[0], TOP_K), 1)
    e_out = jnp.zeros((logits.shape[0], TOP_K), F32)
    p_out = jnp.zeros((logits.shape[0], TOP_K), F32)
    for k in range(TOP_K):
        e_out = jnp.where(kl == k, idxs[k], e_out)
        p_out = jnp.where(kl == k, ex[k] / den, p_out)
    e_ref[...] = e_out.astype(jnp.int32)
    p_ref[...] = p_out


def _out_proj(mix_sb, mix_da, w_sb, w_da, x2d, g_ffn, wr_hi, wr_lo, b_r, *, tm):
    m, d = x2d.shape
    ks, kd = mix_sb.shape[1], mix_da.shape[1]
    n_exp = wr_hi.shape[1]
    kern = functools.partial(_out_proj_kernel, n_exp=n_exp)
    const = lambda shape: pl.BlockSpec(shape, lambda i: (0, 0))
    return pl.pallas_call(
        kern,
        grid=(m // tm,),
        in_specs=[
            pl.BlockSpec((tm, ks), lambda i: (i, 0)),
            pl.BlockSpec((tm, kd), lambda i: (i, 0)),
            const((ks, d)), const((kd, d)),
            pl.BlockSpec((tm, d), lambda i: (i, 0)),
            const((1, d)), const((d, n_exp)), const((d, n_exp)), const((1, n_exp)),
        ],
        out_specs=[
            pl.BlockSpec((tm, d), lambda i: (i, 0)),
            pl.BlockSpec((tm, d), lambda i: (i, 0)),
            pl.BlockSpec((tm, TOP_K), lambda i: (i, 0)),
            pl.BlockSpec((tm, TOP_K), lambda i: (i, 0)),
        ],
        out_shape=[
            jax.ShapeDtypeStruct((m, d), F32),
            jax.ShapeDtypeStruct((m, d), BF16),
            jax.ShapeDtypeStruct((m, TOP_K), jnp.int32),
            jax.ShapeDtypeStruct((m, TOP_K), F32),
        ],
        compiler_params=_cparams(("parallel",)),
        name="out_proj_router",
    )(mix_sb, mix_da, w_sb, w_da, x2d, g_ffn, wr_hi, wr_lo, b_r)


def _moe_kernel(be_ref, bu_ref, rb_ref, x_ref, wg_ref, wu_ref, wd_ref, bg_ref, bup_ref, bd_ref, gt_ref, o_ref):
    blk = pl.program_id(0)
    f = pl.program_id(1)
    nf = pl.num_programs(1)

    @pl.when(bu_ref[blk] > 0)
    def _():
        x = x_ref[...]
        g = jnp.dot(x, wg_ref[...].astype(BF16), preferred_element_type=F32) + bg_ref[...]
        u = jnp.dot(x, wu_ref[...].astype(BF16), preferred_element_type=F32) + bup_ref[...]
        gate = jnp.minimum(g, SWIGLU_LIMIT)
        up = jnp.clip(u, -SWIGLU_LIMIT, SWIGLU_LIMIT)
        act = (up + 1.0) * gate * jax.nn.sigmoid(SWIGLU_ALPHA * gate)
        part = jnp.dot(act.astype(BF16), wd_ref[...].astype(BF16), preferred_element_type=F32)

        @pl.when(f == 0)
        def _():
            o_ref[...] = part

        @pl.when(jnp.logical_and(f > 0, f < nf - 1))
        def _():
            o_ref[...] += part

        @pl.when(f == nf - 1)
        def _():
            o_ref[...] = (o_ref[...] + part + bd_ref[...]) * gt_ref[...]


def _moe_ffn(blk_e, blk_used, blk_row, xs, w_gu, w_dn, b_gu, b_dn, row_gate, *, tm, tf):
    n_rows, d = xs.shape
    n_exp, _, f2 = w_gu.shape
    d_ff = f2 // 2
    nf = d_ff // tf
    assert nf >= 2, "first and last f steps must be distinct"
    n_blk = n_rows // tm
    b_gu3 = b_gu.reshape(n_exp, 1, f2)
    b_dn3 = b_dn.reshape(n_exp, 1, d)

    def fi(f, bu, b):
        return jnp.where(bu[b] > 0, f, nf - 1)

    grid_spec = pltpu.PrefetchScalarGridSpec(
        num_scalar_prefetch=3,
        grid=(n_blk, nf),
        in_specs=[
            pl.BlockSpec((tm, d), lambda b, f, be, bu, rb: (rb[b], 0)),
            pl.BlockSpec((None, d, tf), lambda b, f, be, bu, rb: (be[b], 0, fi(f, bu, b))),
            pl.BlockSpec((None, d, tf), lambda b, f, be, bu, rb: (be[b], 0, nf + fi(f, bu, b))),
            pl.BlockSpec((None, tf, d), lambda b, f, be, bu, rb: (be[b], fi(f, bu, b), 0)),
            pl.BlockSpec((None, 1, tf), lambda b, f, be, bu, rb: (be[b], 0, fi(f, bu, b))),
            pl.BlockSpec((None, 1, tf), lambda b, f, be, bu, rb: (be[b], 0, nf + fi(f, bu, b))),
            pl.BlockSpec((None, 1, d), lambda b, f, be, bu, rb: (be[b], 0, 0)),
            pl.BlockSpec((tm, 1), lambda b, f, be, bu, rb: (rb[b], 0)),
        ],
        out_specs=pl.BlockSpec((tm, d), lambda b, f, be, bu, rb: (rb[b], 0)),
    )
    return pl.pallas_call(
        _moe_kernel,
        grid_spec=grid_spec,
        out_shape=jax.ShapeDtypeStruct((n_rows, d), F32),
        compiler_params=_cparams(("arbitrary", "arbitrary")),
        name="moe_ffn",
    )(blk_e, blk_used, blk_row, xs, w_gu, w_gu, w_dn, b_gu3, b_gu3, b_dn3, row_gate)


def _final_kernel(h_ref, y_ref, g_ref, o_ref):
    h = h_ref[...] + y_ref[...]
    ms = jnp.mean(h * h, axis=-1, keepdims=True)
    o_ref[...] = (h * lax.rsqrt(ms + RMS_EPS)) * g_ref[...]


def _final(h1, y, g, *, tm):
    m, d = h1.shape
    return pl.pallas_call(
        _final_kernel,
        grid=(m // tm,),
        in_specs=[pl.BlockSpec((tm, d), lambda i: (i, 0)),
                  pl.BlockSpec((tm, d), lambda i: (i, 0)),
                  pl.BlockSpec((1, d), lambda i: (0, 0))],
        out_specs=pl.BlockSpec((tm, d), lambda i: (i, 0)),
        out_shape=jax.ShapeDtypeStruct((m, d), F32),
        compiler_params=_cparams(("parallel",)),
        name="final_norm",
    )(h1, y, g)


def _routing(top_e, gates, n_exp, tm):
    n_tok = top_e.shape[0]
    n_assign = n_tok * TOP_K
    flat_e = top_e.reshape(-1)
    flat_g = gates.reshape(-1)
    order = jnp.argsort(flat_e, stable=True).astype(jnp.int32)
    e_s = flat_e[order]
    counts = jnp.bincount(flat_e, length=n_exp).astype(jnp.int32)
    padded = (counts + tm - 1) // tm * tm
    pad_end = jnp.cumsum(padded).astype(jnp.int32)
    pad_start = pad_end - padded
    grp_start = jnp.cumsum(counts).astype(jnp.int32) - counts
    dest_sorted = pad_start[e_s] + (jnp.arange(n_assign, dtype=jnp.int32) - grp_start[e_s])
    n_blk = n_assign // tm + n_exp
    n_rows = n_blk * tm
    row_tok = jnp.zeros((n_rows,), jnp.int32).at[dest_sorted].set(order // TOP_K)
    row_gate = jnp.zeros((n_rows,), F32).at[dest_sorted].set(flat_g[order])
    dest_flat = jnp.zeros((n_assign,), jnp.int32).at[order].set(dest_sorted)
    blk_start = jnp.arange(n_blk, dtype=jnp.int32) * tm
    blk_used = (blk_start < pad_end[-1]).astype(jnp.int32)
    last_used = jnp.maximum(pad_end[-1] // tm - 1, 0)
    blk_e = jnp.minimum(jnp.searchsorted(pad_end, blk_start, side='right'), n_exp - 1).astype(jnp.int32)
    blk_e = jnp.where(blk_used > 0, blk_e, blk_e[last_used])
    blk_row = jnp.where(blk_used > 0, jnp.arange(n_blk, dtype=jnp.int32), last_used).astype(jnp.int32)
    return row_tok, row_gate, dest_flat, blk_e, blk_used, blk_row


def kernel(x, meta_tokens, g_mix, w_in, lam_q1, lam_k1, lam_q2, lam_k2, g_sb_out, g_da_out, w_out, g_ffn,
           w_router, b_router, w_gate_up, b_gate_up, w_down, b_down, g_final):
    b, s, d = x.shape
    depth = w_in.shape[0]
    assert depth == 1, "single-layer trunk"
    layer = 0
    sb_heads = (d // 2) // SB_HEAD_DIM
    da_heads = (d // 2) // DA_V_DIM
    sb_w = sb_heads * SB_HEAD_DIM
    da_col0 = 3 * sb_w
    da_qk_w = da_heads * 2 * DA_HEAD_DIM
    n_exp = w_router.shape[-1]
    lam_init = 0.8 - 0.6 * math.exp(-0.3 * layer)

    tm_proj, tn_proj = TM_PROJ, TN_PROJ
    tq, tk_sb, ts_da = TQ_ATTN, TK_SB, TS_DA
    tm_out = TM_OUT
    tm_moe, tf_moe = TM_MOE, TF_MOE
    tm_fin = TM_FINAL

    x2d = x.reshape(b * s, d)
    w_in_bf = w_in.reshape(w_in.shape[1:]).astype(BF16)
    g_mix2 = g_mix[layer].reshape(1, d)
    rope_cols = (da_col0, da_col0 + 2 * da_qk_w)

    pos_f = N_META + jnp.arange(s, dtype=jnp.int32)
    proj = _in_proj(x2d, g_mix2, w_in_bf, _rope_tables(pos_f), tm=tm_proj, tn=tn_proj,
                    rope_cols=rope_cols, pos_blocks=s // tm_proj)
    meta_pad = jnp.zeros((META_PAD, d), x.dtype).at[:N_META].set(meta_tokens.astype(x.dtype))
    pos_m = jnp.arange(META_PAD, dtype=jnp.int32)
    proj_meta = _in_proj(meta_pad, g_mix2, w_in_bf, _rope_tables(pos_m), tm=META_PAD, tn=tn_proj,
                         rope_cols=rope_cols, pos_blocks=1)
    proj3 = proj.reshape(b, s, -1)

    mix_sb = _sb_attention(proj3, proj_meta, g_sb_out[layer].reshape(1, -1), batch=b, seq=s, heads=sb_heads,
                           tq=tq, tk=tk_sb)
    lams = tuple(t[layer].reshape(1, DA_HEAD_DIM).astype(F32) for t in (lam_q1, lam_k1, lam_q2, lam_k2))
    mix_da = _da_attention(proj3, proj_meta, lams, g_da_out[layer].reshape(1, -1), batch=b, seq=s, heads=da_heads,
                           tq=tq, ts=ts_da, col0=da_col0, lam_init=lam_init)

    w_out_bf = w_out[layer].astype(BF16)
    wr = w_router[layer]
    wr_hi = wr.astype(BF16)
    wr_lo = (wr - wr_hi.astype(F32)).astype(BF16)
    h1, u_bf, top_e, gates = _out_proj(
        mix_sb.reshape(b * s, -1), mix_da.reshape(b * s, -1), w_out_bf[:sb_w], w_out_bf[sb_w:], x2d,
        g_ffn[layer].reshape(1, d), wr_hi, wr_lo, b_router[layer].reshape(1, n_exp), tm=tm_out)

    row_tok, row_gate, dest_flat, blk_e, blk_used, blk_row = _routing(top_e, gates, n_exp, tm_moe)
    xs = jnp.take(u_bf, row_tok, axis=0)
    rows = _moe_ffn(blk_e, blk_used, blk_row, xs, w_gate_up.reshape(w_gate_up.shape[1:]), w_down.reshape(w_down.shape[1:]),
                    b_gate_up[layer], b_down[layer], row_gate.reshape(-1, 1), tm=tm_moe, tf=tf_moe)
    y = jnp.take(rows, dest_flat, axis=0).reshape(b * s, TOP_K, d).sum(axis=1)

    out = _final(h1, y, g_final.reshape(1, d), tm=tm_fin)
    return out.reshape(b, s, d)
```

```python
import functools
import math

import jax
import jax.numpy as jnp
from jax import lax
from jax.experimental import pallas as pl
from jax.experimental.pallas import tpu as pltpu

F32 = jnp.float32
BF16 = jnp.bfloat16

CHUNK = 64
N_META = 16
RMS_EPS = 1e-5
SB_HEAD_DIM = 128
DA_HEAD_DIM = 64
DA_V_DIM = 2 * DA_HEAD_DIM
ROPE_THETA = 500000.0
ROPE_DIM = DA_HEAD_DIM // 4
TOP_K = 4
SWIGLU_LIMIT = 7.0
SWIGLU_ALPHA = 1.702

LANES = 128
META_PAD = 128
NEG_BIG = -1e30
VMEM_LIMIT = 56 * 1024 * 1024

TM_PROJ, TN_PROJ = 512, 1024
TQ_ATTN = 512
TK_SB = 256
TS_DA = 256
TM_OUT = 512
TM_MOE, TF_MOE, TN_MOE = 1024, 512, 512
TM_FINAL = 512


def _cparams(sem):
    return pltpu.CompilerParams(dimension_semantics=sem, vmem_limit_bytes=VMEM_LIMIT)


def _in_proj_kernel(x_ref, g_ref, w_ref, c_ref, sa_ref, sb_ref, o_ref, u_scr, *, tn, rope_lo, rope_hi):
    n = pl.program_id(1)

    @pl.when(n == 0)
    def _():
        x = x_ref[...]
        ms = jnp.mean(x * x, axis=-1, keepdims=True)
        u_scr[...] = ((x * lax.rsqrt(ms + RMS_EPS)) * g_ref[...]).astype(BF16)

    acc = jnp.dot(u_scr[...], w_ref[...], preferred_element_type=F32)
    is_rope = jnp.logical_and(n >= rope_lo, n < rope_hi)

    @pl.when(is_rope)
    def _():
        for c in range(tn // LANES):
            xc = acc[:, c * LANES:(c + 1) * LANES]
            r = (xc * c_ref[...] + pltpu.roll(xc, LANES - ROPE_DIM // 2, 1) * sa_ref[...]
                 + pltpu.roll(xc, ROPE_DIM // 2, 1) * sb_ref[...])
            o_ref[:, c * LANES:(c + 1) * LANES] = r.astype(BF16)

    @pl.when(jnp.logical_not(is_rope))
    def _():
        o_ref[...] = acc.astype(BF16)


def _rope_tables(pos):
    half = ROPE_DIM // 2
    inv_freq = ROPE_THETA ** (-(jnp.arange(half, dtype=F32) * 2.0 / ROPE_DIM))
    ang = pos.astype(F32)[:, None] * inv_freq[None, :]
    cos, sin = jnp.cos(ang), jnp.sin(ang)
    p = pos.shape[0]
    ones = jnp.ones((p, DA_HEAD_DIM - ROPE_DIM), F32)
    zeros8 = jnp.zeros((p, half), F32)
    zeros48 = jnp.zeros((p, DA_HEAD_DIM - ROPE_DIM), F32)
    c64 = jnp.concatenate([cos, cos, ones], axis=1)
    sa64 = jnp.concatenate([-sin, zeros8, zeros48], axis=1)
    sb64 = jnp.concatenate([zeros8, sin, zeros48], axis=1)
    tile2 = lambda t: jnp.concatenate([t, t], axis=1)
    return tile2(c64), tile2(sa64), tile2(sb64)


def _in_proj(x2d, g, w_bf, tables, *, tm, tn, rope_cols, pos_blocks):
    m, d = x2d.shape
    n_cols = w_bf.shape[1]
    c_t, sa_t, sb_t = tables
    kern = functools.partial(_in_proj_kernel, tn=tn, rope_lo=rope_cols[0] // tn, rope_hi=rope_cols[1] // tn)
    tab_spec = pl.BlockSpec((tm, LANES), lambda i, n: (i % pos_blocks, 0))
    return pl.pallas_call(
        kern,
        grid=(m // tm, n_cols // tn),
        in_specs=[
            pl.BlockSpec((tm, d), lambda i, n: (i, 0)),
            pl.BlockSpec((1, d), lambda i, n: (0, 0)),
            pl.BlockSpec((d, tn), lambda i, n: (0, n)),
            tab_spec, tab_spec, tab_spec,
        ],
        out_specs=pl.BlockSpec((tm, tn), lambda i, n: (i, n)),
        out_shape=jax.ShapeDtypeStruct((m, n_cols), BF16),
        scratch_shapes=[pltpu.VMEM((tm, d), BF16)],
        compiler_params=_cparams(("parallel", "arbitrary")),
        name="in_proj",
    )(x2d, g, w_bf, c_t, sa_t, sb_t)


def _dot_nt(a, b):
    return lax.dot_general(a, b, (((1,), (1,)), ((), ())), preferred_element_type=F32)


def _suffix_sum_matrix(n):
    j = lax.broadcasted_iota(jnp.int32, (2 * n, n), 0)
    s = lax.broadcasted_iota(jnp.int32, (2 * n, n), 1)
    return jnp.where(jnp.where(j >= n, j - n, j) > s, 1.0, 0.0).astype(BF16)


def _sb_block(q, kb, vb, u2, run, acc, mask, scale):
    z = _dot_nt(q, kb) * scale
    sp = jnp.maximum(z, 0.0) + jnp.log(1.0 + jnp.exp(-jnp.abs(z)))
    if mask is not None:
        sp = jnp.where(mask, sp, 0.0)
    hi = sp.astype(BF16)
    lo = (sp - hi.astype(F32)).astype(BF16)
    cs = jnp.dot(jnp.concatenate([hi, lo], axis=1), u2, preferred_element_type=F32)
    w = jnp.exp(z - sp - cs - run)
    if mask is not None:
        w = jnp.where(mask, w, 0.0)
    acc = acc + jnp.dot(w.astype(BF16), vb, preferred_element_type=F32)
    run = run + (cs[:, :1] + sp[:, :1])
    return run, acc


def _sb_kernel(q_ref, k_ref, v_ref, km_ref, vm_ref, g_ref, o_ref, *, tq, tk, scale):
    i = pl.program_id(2)
    nsub = tq // tk
    u_blk = _suffix_sum_matrix(tk)
    row = lax.broadcasted_iota(jnp.int32, (tk, tk), 0)
    col = lax.broadcasted_iota(jnp.int32, (tk, tk), 1)
    diag_mask = col < row

    def kv(blk):
        s0 = pl.multiple_of(blk * tk, tk)
        return k_ref[pl.ds(s0, tk), :], v_ref[pl.ds(s0, tk), :]

    runs, accs = [], []
    for h in range(nsub):
        qh = q_ref[h * tk:(h + 1) * tk, :]
        run = jnp.zeros((tk, 1), F32)
        acc = jnp.zeros((tk, SB_HEAD_DIM), F32)
        for c in range(h, -1, -1):
            kb, vb = kv(i * nsub + c)
            run, acc = _sb_block(qh, kb, vb, u_blk, run, acc, diag_mask if c == h else None, scale)
        runs.append(run)
        accs.append(acc)
    run = jnp.concatenate(runs, axis=0)
    acc = jnp.concatenate(accs, axis=0)
    q = q_ref[...]

    def body(jj, carry):
        r, a = carry
        for c in range(nsub - 1, -1, -1):
            kb, vb = kv((i - 1 - jj) * nsub + c)
            r, a = _sb_block(q, kb, vb, u_blk, r, a, None, scale)
        return r, a

    run, acc = lax.fori_loop(0, i, body, (run, acc))

    mcol = lax.broadcasted_iota(jnp.int32, (tq, META_PAD), 1)
    run, acc = _sb_block(q, km_ref[...], vm_ref[...], _suffix_sum_matrix(META_PAD), run, acc,
                         mcol < N_META, scale)

    ms = jnp.mean(acc * acc, axis=-1, keepdims=True)
    o_ref[...] = ((acc * lax.rsqrt(ms + RMS_EPS)) * g_ref[...]).astype(BF16)


def _sb_attention(proj, proj_meta, g_sb, *, batch, seq, heads, tq, tk):
    hd = SB_HEAD_DIM
    kern = functools.partial(_sb_kernel, tq=tq, tk=tk, scale=hd ** -0.5)
    return pl.pallas_call(
        kern,
        grid=(batch, heads, seq // tq),
        in_specs=[
            pl.BlockSpec((None, tq, hd), lambda b, h, i: (b, i, h)),
            pl.BlockSpec((None, seq, hd), lambda b, h, i: (b, 0, heads + h)),
            pl.BlockSpec((None, seq, hd), lambda b, h, i: (b, 0, 2 * heads + h)),
            pl.BlockSpec((META_PAD, hd), lambda b, h, i: (0, heads + h)),
            pl.BlockSpec((META_PAD, hd), lambda b, h, i: (0, 2 * heads + h)),
            pl.BlockSpec((1, hd), lambda b, h, i: (0, h)),
        ],
        out_specs=pl.BlockSpec((None, tq, hd), lambda b, h, i: (b, i, h)),
        out_shape=jax.ShapeDtypeStruct((batch, seq, heads * hd), BF16),
        compiler_params=_cparams(("parallel", "parallel", "arbitrary")),
        name="sb_attn",
    )(proj, proj, proj, proj_meta, proj_meta, g_sb)


def _with_ones(vb):
    return jnp.concatenate([vb, jnp.ones_like(vb)], axis=1)


def _da_block(q1, q2, kb, vb1, st, mask):
    m1, a1, m2, a2 = st

    def one(qc, m, a):
        s = _dot_nt(qc, kb)
        if mask is not None:
            s = jnp.where(mask, s, NEG_BIG)
        mn = jnp.maximum(m, jnp.max(s, axis=-1, keepdims=True))
        p = jnp.exp(s - mn)
        a = jnp.exp(m - mn) * a + jnp.dot(p.astype(BF16), vb1, preferred_element_type=F32)
        return mn, a

    m1, a1 = one(q1, m1, a1)
    m2, a2 = one(q2, m2, a2)
    return m1, a1, m2, a2


def _da_kernel(lq1_ref, lk1_ref, lq2_ref, lk2_ref, q_ref, k_ref, v_ref, km_ref, vm_ref, g_ref, o_ref,
               *, tq, ts, scale, lam_init):
    i = pl.program_id(2)
    nsub = tq // ts
    lam = (jnp.exp(jnp.sum(lq1_ref[...] * lk1_ref[...], axis=-1, keepdims=True))
           - jnp.exp(jnp.sum(lq2_ref[...] * lk2_ref[...], axis=-1, keepdims=True)) + lam_init)

    q = q_ref[...]
    lane = lax.broadcasted_iota(jnp.int32, q.shape, 1)
    qs = q * jnp.asarray(scale, BF16)
    zero = jnp.zeros_like(qs)
    q1 = jnp.where(lane < DA_HEAD_DIM, qs, zero)
    q2 = jnp.where(lane >= DA_HEAD_DIM, qs, zero)

    neg = jnp.full((tq, 1), NEG_BIG, F32)
    za = jnp.zeros((tq, 2 * DA_V_DIM), F32)
    st = (neg, za, neg, za)

    def body(j, carry):
        s0 = pl.multiple_of(j * tq, tq)
        return _da_block(q1, q2, k_ref[pl.ds(s0, tq), :], _with_ones(v_ref[pl.ds(s0, tq), :]), carry, None)

    st = lax.fori_loop(0, i, body, st)

    shift = CHUNK.bit_length() - 1
    s0 = pl.multiple_of(i * tq, tq)
    outs = []
    for h in range(nsub):
        rows = slice(h * ts, (h + 1) * ts)
        width = (h + 1) * ts
        kb = jnp.concatenate([km_ref[...], k_ref[pl.ds(s0, width), :]], axis=0)
        vb = jnp.concatenate([vm_ref[...], v_ref[pl.ds(s0, width), :]], axis=0)
        col = lax.broadcasted_iota(jnp.int32, (ts, META_PAD + width), 1)
        row = lax.broadcasted_iota(jnp.int32, (ts, META_PAD + width), 0) + h * ts
        frame_vis = jnp.logical_and(col >= META_PAD,
                                    jnp.right_shift(col - META_PAD, shift) <= jnp.right_shift(row, shift))
        vis = jnp.logical_or(col < N_META, frame_vis)
        sth = _da_block(q1[rows], q2[rows], kb, _with_ones(vb), tuple(t[rows] for t in st), vis)
        _, a1, _, a2 = sth
        outs.append(a1[:, :DA_V_DIM] / a1[:, DA_V_DIM:] - lam * (a2[:, :DA_V_DIM] / a2[:, DA_V_DIM:]))
    o = jnp.concatenate(outs, axis=0)
    ms = jnp.mean(o * o, axis=-1, keepdims=True)
    o_ref[...] = (((o * lax.rsqrt(ms + RMS_EPS)) * g_ref[...]) * (1.0 - lam_init)).astype(BF16)


def _da_attention(proj, proj_meta, lams, g_da, *, batch, seq, heads, tq, ts, col0, lam_init):
    hd = DA_V_DIM
    qb, kb, vb = col0 // hd, col0 // hd + heads, col0 // hd + 2 * heads
    kern = functools.partial(_da_kernel, tq=tq, ts=ts, scale=DA_HEAD_DIM ** -0.5, lam_init=lam_init)
    lam_spec = pl.BlockSpec((1, DA_HEAD_DIM), lambda b, h, i: (0, 0))
    return pl.pallas_call(
        kern,
        grid=(batch, heads, seq // tq),
        in_specs=[
            lam_spec, lam_spec, lam_spec, lam_spec,
            pl.BlockSpec((None, tq, hd), lambda b, h, i: (b, i, qb + h)),
            pl.BlockSpec((None, seq, hd), lambda b, h, i: (b, 0, kb + h)),
            pl.BlockSpec((None, seq, hd), lambda b, h, i: (b, 0, vb + h)),
            pl.BlockSpec((META_PAD, hd), lambda b, h, i: (0, kb + h)),
            pl.BlockSpec((META_PAD, hd), lambda b, h, i: (0, vb + h)),
            pl.BlockSpec((1, hd), lambda b, h, i: (0, h)),
        ],
        out_specs=pl.BlockSpec((None, tq, hd), lambda b, h, i: (b, i, h)),
        out_shape=jax.ShapeDtypeStruct((batch, seq, heads * hd), BF16),
        compiler_params=_cparams(("parallel", "parallel", "arbitrary")),
        name="da_attn",
    )(*lams, proj, proj, proj, proj_meta, proj_meta, g_da)


def _split3_dot(a, b_hi, b_lo):
    a_hi = a.astype(BF16)
    a_lo = (a - a_hi.astype(F32)).astype(BF16)
    return (jnp.dot(a_hi, b_hi, preferred_element_type=F32)
            + jnp.dot(a_hi, b_lo, preferred_element_type=F32)
            + jnp.dot(a_lo, b_hi, preferred_element_type=F32))


def _out_proj_kernel(ms_ref, md_ref, ws_ref, wd_ref, x_ref, g_ref, wr_hi_ref, wr_lo_ref, br_ref,
                     h_ref, u_ref, e_ref, p_ref, *, n_exp):
    h = (x_ref[...]
         + jnp.dot(ms_ref[...], ws_ref[...], preferred_element_type=F32)
         + jnp.dot(md_ref[...], wd_ref[...], preferred_element_type=F32))
    h_ref[...] = h
    msq = jnp.mean(h * h, axis=-1, keepdims=True)
    u = (h * lax.rsqrt(msq + RMS_EPS)) * g_ref[...]
    u_ref[...] = u.astype(BF16)

    logits = _split3_dot(u, wr_hi_ref[...], wr_lo_ref[...]) + br_ref[...]
    lane = lax.broadcasted_iota(jnp.int32, logits.shape, 1).astype(F32)
    work = logits
    tops, idxs = [], []
    for _ in range(TOP_K):
        mx = jnp.max(work, axis=-1, keepdims=True)
        ix = jnp.min(jnp.where(work == mx, lane, float(n_exp)), axis=-1, keepdims=True)
        tops.append(mx)
        idxs.append(ix)
        work = jnp.where(lane == ix, -jnp.inf, work)
    ex = [jnp.exp(t - tops[0]) for t in tops]
    den = ex[0] + ex[1] + ex[2] + ex[3]
    kl = lax.broadcasted_iota(jnp.int32, (logits.shape[0], TOP_K), 1)
    e_out = jnp.zeros((logits.shape[0], TOP_K), F32)
    p_out = jnp.zeros((logits.shape[0], TOP_K), F32)
    for k in range(TOP_K):
        e_out = jnp.where(kl == k, idxs[k], e_out)
        p_out = jnp.where(kl == k, ex[k] / den, p_out)
    e_ref[...] = e_out.astype(jnp.int32)
    p_ref[...] = p_out


def _out_proj(mix_sb, mix_da, w_sb, w_da, x2d, g_ffn, wr_hi, wr_lo, b_r, *, tm):
    m, d = x2d.shape
    ks, kd = mix_sb.shape[1], mix_da.shape[1]
    n_exp = wr_hi.shape[1]
    kern = functools.partial(_out_proj_kernel, n_exp=n_exp)
    const = lambda shape: pl.BlockSpec(shape, lambda i: (0, 0))
    return pl.pallas_call(
        kern,
        grid=(m // tm,),
        in_specs=[
            pl.BlockSpec((tm, ks), lambda i: (i, 0)),
            pl.BlockSpec((tm, kd), lambda i: (i, 0)),
            const((ks, d)), const((kd, d)),
            pl.BlockSpec((tm, d), lambda i: (i, 0)),
            const((1, d)), const((d, n_exp)), const((d, n_exp)), const((1, n_exp)),
        ],
        out_specs=[
            pl.BlockSpec((tm, d), lambda i: (i, 0)),
            pl.BlockSpec((tm, d), lambda i: (i, 0)),
            pl.BlockSpec((tm, TOP_K), lambda i: (i, 0)),
            pl.BlockSpec((tm, TOP_K), lambda i: (i, 0)),
        ],
        out_shape=[
            jax.ShapeDtypeStruct((m, d), F32),
            jax.ShapeDtypeStruct((m, d), BF16),
            jax.ShapeDtypeStruct((m, TOP_K), jnp.int32),
            jax.ShapeDtypeStruct((m, TOP_K), F32),
        ],
        compiler_params=_cparams(("parallel",)),
        name="out_proj_router",
    )(mix_sb, mix_da, w_sb, w_da, x2d, g_ffn, wr_hi, wr_lo, b_r)


def _moe_kernel(be_ref, bu_ref, rb_ref, x_ref, wg_ref, wu_ref, wd_ref, bg_ref, bup_ref, bd_ref, gt_ref, o_ref,
                act_ref, *, nf):
    blk = pl.program_id(0)
    s = pl.program_id(1)
    used = bu_ref[blk] > 0

    @pl.when(jnp.logical_and(used, s < nf))
    def _():
        x = x_ref[...]
        g = jnp.dot(x, wg_ref[...].astype(BF16), preferred_element_type=F32) + bg_ref[...]
        u = jnp.dot(x, wu_ref[...].astype(BF16), preferred_element_type=F32) + bup_ref[...]
        gate = jnp.minimum(g, SWIGLU_LIMIT)
        up = jnp.clip(u, -SWIGLU_LIMIT, SWIGLU_LIMIT)
        act = (up + 1.0) * gate * jax.nn.sigmoid(SWIGLU_ALPHA * gate)
        act_ref[jnp.minimum(s, nf - 1)] = act.astype(BF16)

    @pl.when(jnp.logical_and(used, s >= nf))
    def _():
        tf = act_ref.shape[2]
        y = bd_ref[...]
        for f in range(nf):
            y = y + jnp.dot(act_ref[f], wd_ref[f * tf:(f + 1) * tf, :].astype(BF16), preferred_element_type=F32)
        o_ref[...] = y * gt_ref[...]


def _moe_ffn(blk_e, blk_used, blk_row, xs, w_gu, w_dn, b_gu, b_dn, row_gate, *, tm, tf, tn):
    n_rows, d = xs.shape
    n_exp, _, f2 = w_gu.shape
    d_ff = f2 // 2
    nf = d_ff // tf
    nn = d // tn
    n_blk = n_rows // tm
    b_gu3 = b_gu.reshape(n_exp, 1, f2)
    b_dn3 = b_dn.reshape(n_exp, 1, d)

    def fi(s, bu, b):
        return jnp.where(bu[b] > 0, jnp.minimum(s, nf - 1), nf - 1)

    def ni(s, bu, b):
        return jnp.where(bu[b] > 0, jnp.clip(s - nf, 0, nn - 1), nn - 1)

    grid_spec = pltpu.PrefetchScalarGridSpec(
        num_scalar_prefetch=3,
        grid=(n_blk, nf + nn),
        in_specs=[
            pl.BlockSpec((tm, d), lambda b, s, be, bu, rb: (rb[b], 0)),
            pl.BlockSpec((None, d, tf), lambda b, s, be, bu, rb: (be[b], 0, fi(s, bu, b))),
            pl.BlockSpec((None, d, tf), lambda b, s, be, bu, rb: (be[b], 0, nf + fi(s, bu, b))),
            pl.BlockSpec((None, d_ff, tn), lambda b, s, be, bu, rb: (be[b], 0, ni(s, bu, b))),
            pl.BlockSpec((None, 1, tf), lambda b, s, be, bu, rb: (be[b], 0, fi(s, bu, b))),
            pl.BlockSpec((None, 1, tf), lambda b, s, be, bu, rb: (be[b], 0, nf + fi(s, bu, b))),
            pl.BlockSpec((None, 1, tn), lambda b, s, be, bu, rb: (be[b], 0, ni(s, bu, b))),
            pl.BlockSpec((tm, 1), lambda b, s, be, bu, rb: (rb[b], 0)),
        ],
        out_specs=pl.BlockSpec((tm, tn), lambda b, s, be, bu, rb: (rb[b], ni(s, bu, b))),
        scratch_shapes=[pltpu.VMEM((nf, tm, tf), BF16)],
    )
    return pl.pallas_call(
        functools.partial(_moe_kernel, nf=nf),
        grid_spec=grid_spec,
        out_shape=jax.ShapeDtypeStruct((n_rows, d), F32),
        compiler_params=_cparams(("arbitrary", "arbitrary")),
        name="moe_ffn",
    )(blk_e, blk_used, blk_row, xs, w_gu, w_gu, w_dn, b_gu3, b_gu3, b_dn3, row_gate)


def _final_kernel(h_ref, y0_ref, y1_ref, y2_ref, y3_ref, g_ref, o_ref):
    y = ((y0_ref[...] + y1_ref[...]) + y2_ref[...]) + y3_ref[...]
    h = h_ref[...] + y
    ms = jnp.mean(h * h, axis=-1, keepdims=True)
    o_ref[...] = (h * lax.rsqrt(ms + RMS_EPS)) * g_ref[...]


def _final(h1, y_km, g, *, tm):
    m, d = h1.shape
    nb = m // tm
    assert TOP_K == 4
    y_specs = [pl.BlockSpec((tm, d), functools.partial(lambda i, k: (k * nb + i, 0), k=k)) for k in range(TOP_K)]
    return pl.pallas_call(
        _final_kernel,
        grid=(nb,),
        in_specs=[pl.BlockSpec((tm, d), lambda i: (i, 0))] + y_specs + [pl.BlockSpec((1, d), lambda i: (0, 0))],
        out_specs=pl.BlockSpec((tm, d), lambda i: (i, 0)),
        out_shape=jax.ShapeDtypeStruct((m, d), F32),
        compiler_params=_cparams(("parallel",)),
        name="final_norm",
    )(h1, y_km, y_km, y_km, y_km, g)


def _routing(top_e, gates, n_exp, tm):
    n_tok = top_e.shape[0]
    n_assign = n_tok * TOP_K
    flat_e = top_e.reshape(-1)
    flat_g = gates.reshape(-1)
    order = jnp.argsort(flat_e, stable=True).astype(jnp.int32)
    rank = jnp.argsort(order).astype(jnp.int32)
    counts = jnp.sum((flat_e[:, None] == jnp.arange(n_exp, dtype=jnp.int32)[None, :]).astype(jnp.int32), axis=0)
    padded = (counts + tm - 1) // tm * tm
    pad_end = jnp.cumsum(padded).astype(jnp.int32)
    pad_start = pad_end - padded
    grp_start = jnp.cumsum(counts).astype(jnp.int32) - counts
    n_blk = -(-n_assign // tm) + n_exp
    n_rows = n_blk * tm
    blk_start = jnp.arange(n_blk, dtype=jnp.int32) * tm
    blk_used = (blk_start < pad_end[-1]).astype(jnp.int32)
    last_used = jnp.maximum(pad_end[-1] // tm - 1, 0)
    blk_e = jnp.minimum(jnp.sum((blk_start[:, None] >= pad_end[None, :]).astype(jnp.int32), axis=1), n_exp - 1)
    row_e = jnp.repeat(blk_e, tm)
    off = jnp.arange(n_rows, dtype=jnp.int32) - pad_start[row_e]
    valid = off < counts[row_e]
    row_asg = order[jnp.where(valid, grp_start[row_e] + off, 0)]
    row_tok = jnp.where(valid, row_asg // TOP_K, 0)
    row_gate = jnp.where(valid, flat_g[row_asg], 0.0)
    dest_flat = pad_start[flat_e] + rank - grp_start[flat_e]
    blk_e = jnp.where(blk_used > 0, blk_e, blk_e[last_used]).astype(jnp.int32)
    blk_row = jnp.where(blk_used > 0, jnp.arange(n_blk, dtype=jnp.int32), last_used).astype(jnp.int32)
    return row_tok, row_gate, dest_flat, blk_e, blk_used, blk_row


def kernel(x, meta_tokens, g_mix, w_in, lam_q1, lam_k1, lam_q2, lam_k2, g_sb_out, g_da_out, w_out, g_ffn,
           w_router, b_router, w_gate_up, b_gate_up, w_down, b_down, g_final):
    b, s, d = x.shape
    depth = w_in.shape[0]
    assert depth == 1, "single-layer trunk"
    layer = 0
    sb_heads = (d // 2) // SB_HEAD_DIM
    da_heads = (d // 2) // DA_V_DIM
    sb_w = sb_heads * SB_HEAD_DIM
    da_col0 = 3 * sb_w
    da_qk_w = da_heads * 2 * DA_HEAD_DIM
    n_exp = w_router.shape[-1]
    lam_init = 0.8 - 0.6 * math.exp(-0.3 * layer)

    tm_proj, tn_proj = TM_PROJ, TN_PROJ
    tq, tk_sb, ts_da = TQ_ATTN, TK_SB, TS_DA
    tm_out = TM_OUT
    tm_moe, tf_moe = TM_MOE, TF_MOE
    tm_fin = TM_FINAL

    x2d = x.reshape(b * s, d)
    w_in_bf = w_in.reshape(w_in.shape[1:]).astype(BF16)
    g_mix2 = g_mix[layer].reshape(1, d)
    rope_cols = (da_col0, da_col0 + 2 * da_qk_w)

    pos_f = N_META + jnp.arange(s, dtype=jnp.int32)
    proj = _in_proj(x2d, g_mix2, w_in_bf, _rope_tables(pos_f), tm=tm_proj, tn=tn_proj,
                    rope_cols=rope_cols, pos_blocks=s // tm_proj)
    meta_pad = jnp.zeros((META_PAD, d), x.dtype).at[:N_META].set(meta_tokens.astype(x.dtype))
    pos_m = jnp.arange(META_PAD, dtype=jnp.int32)
    proj_meta = _in_proj(meta_pad, g_mix2, w_in_bf, _rope_tables(pos_m), tm=META_PAD, tn=tn_proj,
                         rope_cols=rope_cols, pos_blocks=1)
    proj3 = proj.reshape(b, s, -1)

    mix_sb = _sb_attention(proj3, proj_meta, g_sb_out[layer].reshape(1, -1), batch=b, seq=s, heads=sb_heads,
                           tq=tq, tk=tk_sb)
    lams = tuple(t[layer].reshape(1, DA_HEAD_DIM).astype(F32) for t in (lam_q1, lam_k1, lam_q2, lam_k2))
    mix_da = _da_attention(proj3, proj_meta, lams, g_da_out[layer].reshape(1, -1), batch=b, seq=s, heads=da_heads,
                           tq=tq, ts=ts_da, col0=da_col0, lam_init=lam_init)

    w_out_bf = w_out[layer].astype(BF16)
    wr = w_router[layer]
    wr_hi = wr.astype(BF16)
    wr_lo = (wr - wr_hi.astype(F32)).astype(BF16)
    h1, u_bf, top_e, gates = _out_proj(
        mix_sb.reshape(b * s, -1), mix_da.reshape(b * s, -1), w_out_bf[:sb_w], w_out_bf[sb_w:], x2d,
        g_ffn[layer].reshape(1, d), wr_hi, wr_lo, b_router[layer].reshape(1, n_exp), tm=tm_out)

    row_tok, row_gate, dest_flat, blk_e, blk_used, blk_row = _routing(top_e, gates, n_exp, tm_moe)
    xs = jnp.take(u_bf, row_tok, axis=0)
    rows = _moe_ffn(blk_e, blk_used, blk_row, xs, w_gate_up.reshape(w_gate_up.shape[1:]), w_down.reshape(w_down.shape[1:]),
                    b_gate_up[layer], b_down[layer], row_gate.reshape(-1, 1), tm=tm_moe, tf=tf_moe, tn=TN_MOE)
    dest_km = dest_flat.reshape(b * s, TOP_K).T.reshape(-1)
    y_km = jnp.take(rows, dest_km, axis=0)

    out = _final(h1, y_km, g_final.reshape(1, d), tm=tm_fin)
    return out.reshape(b, s, d)
```

```python
import functools
import math

import jax
import jax.numpy as jnp
from jax import lax
from jax.experimental import pallas as pl
from jax.experimental.pallas import tpu as pltpu

F32 = jnp.float32
BF16 = jnp.bfloat16

CHUNK = 64
N_META = 16
RMS_EPS = 1e-5
SB_HEAD_DIM = 128
DA_HEAD_DIM = 64
DA_V_DIM = 2 * DA_HEAD_DIM
ROPE_THETA = 500000.0
ROPE_DIM = DA_HEAD_DIM // 4
TOP_K = 4
SWIGLU_LIMIT = 7.0
SWIGLU_ALPHA = 1.702

LANES = 128
META_PAD = 128
NEG_BIG = -1e30
VMEM_LIMIT = 56 * 1024 * 1024

TM_PROJ, TN_PROJ = 512, 1024
TQ_ATTN = 512
TK_SB = 256
TS_DA = 256
TM_OUT = 512
TM_MOE, TS_MOE = 2048, 512
TF_MOE, TN_MOE = 256, 256
TM_FINAL = 512


def _cparams(sem):
    return pltpu.CompilerParams(dimension_semantics=sem, vmem_limit_bytes=VMEM_LIMIT)


def _in_proj_kernel(x_ref, g_ref, w_ref, c_ref, sa_ref, sb_ref, o_ref, u_scr, *, tn, rope_lo, rope_hi):
    n = pl.program_id(1)

    @pl.when(n == 0)
    def _():
        x = x_ref[...]
        ms = jnp.mean(x * x, axis=-1, keepdims=True)
        u_scr[...] = ((x * lax.rsqrt(ms + RMS_EPS)) * g_ref[...]).astype(BF16)

    acc = jnp.dot(u_scr[...], w_ref[...], preferred_element_type=F32)
    is_rope = jnp.logical_and(n >= rope_lo, n < rope_hi)

    @pl.when(is_rope)
    def _():
        for c in range(tn // LANES):
            xc = acc[:, c * LANES:(c + 1) * LANES]
            r = (xc * c_ref[...] + pltpu.roll(xc, LANES - ROPE_DIM // 2, 1) * sa_ref[...]
                 + pltpu.roll(xc, ROPE_DIM // 2, 1) * sb_ref[...])
            o_ref[:, c * LANES:(c + 1) * LANES] = r.astype(BF16)

    @pl.when(jnp.logical_not(is_rope))
    def _():
        o_ref[...] = acc.astype(BF16)


def _rope_tables(pos):
    half = ROPE_DIM // 2
    inv_freq = ROPE_THETA ** (-(jnp.arange(half, dtype=F32) * 2.0 / ROPE_DIM))
    ang = pos.astype(F32)[:, None] * inv_freq[None, :]
    cos, sin = jnp.cos(ang), jnp.sin(ang)
    p = pos.shape[0]
    ones = jnp.ones((p, DA_HEAD_DIM - ROPE_DIM), F32)
    zeros8 = jnp.zeros((p, half), F32)
    zeros48 = jnp.zeros((p, DA_HEAD_DIM - ROPE_DIM), F32)
    c64 = jnp.concatenate([cos, cos, ones], axis=1)
    sa64 = jnp.concatenate([-sin, zeros8, zeros48], axis=1)
    sb64 = jnp.concatenate([zeros8, sin, zeros48], axis=1)
    tile2 = lambda t: jnp.concatenate([t, t], axis=1)
    return tile2(c64), tile2(sa64), tile2(sb64)


def _in_proj(x2d, g, w_bf, tables, *, tm, tn, rope_cols, pos_blocks):
    m, d = x2d.shape
    n_cols = w_bf.shape[1]
    c_t, sa_t, sb_t = tables
    kern = functools.partial(_in_proj_kernel, tn=tn, rope_lo=rope_cols[0] // tn, rope_hi=rope_cols[1] // tn)
    tab_spec = pl.BlockSpec((tm, LANES), lambda i, n: (i % pos_blocks, 0))
    return pl.pallas_call(
        kern,
        grid=(m // tm, n_cols // tn),
        in_specs=[
            pl.BlockSpec((tm, d), lambda i, n: (i, 0)),
            pl.BlockSpec((1, d), lambda i, n: (0, 0)),
            pl.BlockSpec((d, tn), lambda i, n: (0, n)),
            tab_spec, tab_spec, tab_spec,
        ],
        out_specs=pl.BlockSpec((tm, tn), lambda i, n: (i, n)),
        out_shape=jax.ShapeDtypeStruct((m, n_cols), BF16),
        scratch_shapes=[pltpu.VMEM((tm, d), BF16)],
        compiler_params=_cparams(("parallel", "arbitrary")),
        name="in_proj",
    )(x2d, g, w_bf, c_t, sa_t, sb_t)


def _dot_nt(a, b):
    return lax.dot_general(a, b, (((1,), (1,)), ((), ())), preferred_element_type=F32)


def _suffix_sum_matrix(n):
    j = lax.broadcasted_iota(jnp.int32, (2 * n, n), 0)
    s = lax.broadcasted_iota(jnp.int32, (2 * n, n), 1)
    return jnp.where(jnp.where(j >= n, j - n, j) > s, 1.0, 0.0).astype(BF16)


def _sb_block(q, kb, vb, u2, run, acc, mask, scale):
    z = _dot_nt(q, kb) * scale
    sp = jnp.maximum(z, 0.0) + jnp.log(1.0 + jnp.exp(-jnp.abs(z)))
    if mask is not None:
        sp = jnp.where(mask, sp, 0.0)
    hi = sp.astype(BF16)
    lo = (sp - hi.astype(F32)).astype(BF16)
    cs = jnp.dot(jnp.concatenate([hi, lo], axis=1), u2, preferred_element_type=F32)
    w = jnp.exp(z - sp - cs - run)
    if mask is not None:
        w = jnp.where(mask, w, 0.0)
    acc = acc + jnp.dot(w.astype(BF16), vb, preferred_element_type=F32)
    run = run + (cs[:, :1] + sp[:, :1])
    return run, acc


def _sb_kernel(q_ref, k_ref, v_ref, km_ref, vm_ref, g_ref, o_ref, *, tq, tk, scale):
    i = pl.program_id(2)
    nsub = tq // tk
    u_blk = _suffix_sum_matrix(tk)
    row = lax.broadcasted_iota(jnp.int32, (tk, tk), 0)
    col = lax.broadcasted_iota(jnp.int32, (tk, tk), 1)
    diag_mask = col < row

    def kv(blk):
        s0 = pl.multiple_of(blk * tk, tk)
        return k_ref[pl.ds(s0, tk), :], v_ref[pl.ds(s0, tk), :]

    runs, accs = [], []
    for h in range(nsub):
        qh = q_ref[h * tk:(h + 1) * tk, :]
        run = jnp.zeros((tk, 1), F32)
        acc = jnp.zeros((tk, SB_HEAD_DIM), F32)
        for c in range(h, -1, -1):
            kb, vb = kv(i * nsub + c)
            run, acc = _sb_block(qh, kb, vb, u_blk, run, acc, diag_mask if c == h else None, scale)
        runs.append(run)
        accs.append(acc)
    run = jnp.concatenate(runs, axis=0)
    acc = jnp.concatenate(accs, axis=0)
    q = q_ref[...]

    def body(jj, carry):
        r, a = carry
        for c in range(nsub - 1, -1, -1):
            kb, vb = kv((i - 1 - jj) * nsub + c)
            r, a = _sb_block(q, kb, vb, u_blk, r, a, None, scale)
        return r, a

    run, acc = lax.fori_loop(0, i, body, (run, acc))

    mcol = lax.broadcasted_iota(jnp.int32, (tq, META_PAD), 1)
    run, acc = _sb_block(q, km_ref[...], vm_ref[...], _suffix_sum_matrix(META_PAD), run, acc,
                         mcol < N_META, scale)

    ms = jnp.mean(acc * acc, axis=-1, keepdims=True)
    o_ref[...] = ((acc * lax.rsqrt(ms + RMS_EPS)) * g_ref[...]).astype(BF16)


def _sb_attention(proj, proj_meta, g_sb, *, batch, seq, heads, tq, tk):
    hd = SB_HEAD_DIM
    kern = functools.partial(_sb_kernel, tq=tq, tk=tk, scale=hd ** -0.5)
    return pl.pallas_call(
        kern,
        grid=(batch, heads, seq // tq),
        in_specs=[
            pl.BlockSpec((None, tq, hd), lambda b, h, i: (b, i, h)),
            pl.BlockSpec((None, seq, hd), lambda b, h, i: (b, 0, heads + h)),
            pl.BlockSpec((None, seq, hd), lambda b, h, i: (b, 0, 2 * heads + h)),
            pl.BlockSpec((META_PAD, hd), lambda b, h, i: (0, heads + h)),
            pl.BlockSpec((META_PAD, hd), lambda b, h, i: (0, 2 * heads + h)),
            pl.BlockSpec((1, hd), lambda b, h, i: (0, h)),
        ],
        out_specs=pl.BlockSpec((None, tq, hd), lambda b, h, i: (b, i, h)),
        out_shape=jax.ShapeDtypeStruct((batch, seq, heads * hd), BF16),
        compiler_params=_cparams(("parallel", "parallel", "arbitrary")),
        name="sb_attn",
    )(proj, proj, proj, proj_meta, proj_meta, g_sb)


def _with_ones(vb):
    return jnp.concatenate([vb, jnp.ones_like(vb)], axis=1)


def _da_block(q1, q2, kb, vb1, st, mask):
    m1, a1, m2, a2 = st

    def one(qc, m, a):
        s = _dot_nt(qc, kb)
        if mask is not None:
            s = jnp.where(mask, s, NEG_BIG)
        mn = jnp.maximum(m, jnp.max(s, axis=-1, keepdims=True))
        p = jnp.exp(s - mn)
        a = jnp.exp(m - mn) * a + jnp.dot(p.astype(BF16), vb1, preferred_element_type=F32)
        return mn, a

    m1, a1 = one(q1, m1, a1)
    m2, a2 = one(q2, m2, a2)
    return m1, a1, m2, a2


def _da_kernel(lq1_ref, lk1_ref, lq2_ref, lk2_ref, q_ref, k_ref, v_ref, km_ref, vm_ref, g_ref, o_ref,
               *, tq, ts, scale, lam_init):
    i = pl.program_id(2)
    nsub = tq // ts
    lam = (jnp.exp(jnp.sum(lq1_ref[...] * lk1_ref[...], axis=-1, keepdims=True))
           - jnp.exp(jnp.sum(lq2_ref[...] * lk2_ref[...], axis=-1, keepdims=True)) + lam_init)

    q = q_ref[...]
    lane = lax.broadcasted_iota(jnp.int32, q.shape, 1)
    qs = q * jnp.asarray(scale, BF16)
    zero = jnp.zeros_like(qs)
    q1 = jnp.where(lane < DA_HEAD_DIM, qs, zero)
    q2 = jnp.where(lane >= DA_HEAD_DIM, qs, zero)

    neg = jnp.full((tq, 1), NEG_BIG, F32)
    za = jnp.zeros((tq, 2 * DA_V_DIM), F32)
    st = (neg, za, neg, za)

    def body(j, carry):
        s0 = pl.multiple_of(j * tq, tq)
        return _da_block(q1, q2, k_ref[pl.ds(s0, tq), :], _with_ones(v_ref[pl.ds(s0, tq), :]), carry, None)

    st = lax.fori_loop(0, i, body, st)

    shift = CHUNK.bit_length() - 1
    s0 = pl.multiple_of(i * tq, tq)
    outs = []
    for h in range(nsub):
        rows = slice(h * ts, (h + 1) * ts)
        width = (h + 1) * ts
        kb = jnp.concatenate([km_ref[...], k_ref[pl.ds(s0, width), :]], axis=0)
        vb = jnp.concatenate([vm_ref[...], v_ref[pl.ds(s0, width), :]], axis=0)
        col = lax.broadcasted_iota(jnp.int32, (ts, META_PAD + width), 1)
        row = lax.broadcasted_iota(jnp.int32, (ts, META_PAD + width), 0) + h * ts
        frame_vis = jnp.logical_and(col >= META_PAD,
                                    jnp.right_shift(col - META_PAD, shift) <= jnp.right_shift(row, shift))
        vis = jnp.logical_or(col < N_META, frame_vis)
        sth = _da_block(q1[rows], q2[rows], kb, _with_ones(vb), tuple(t[rows] for t in st), vis)
        _, a1, _, a2 = sth
        outs.append(a1[:, :DA_V_DIM] / a1[:, DA_V_DIM:] - lam * (a2[:, :DA_V_DIM] / a2[:, DA_V_DIM:]))
    o = jnp.concatenate(outs, axis=0)
    ms = jnp.mean(o * o, axis=-1, keepdims=True)
    o_ref[...] = (((o * lax.rsqrt(ms + RMS_EPS)) * g_ref[...]) * (1.0 - lam_init)).astype(BF16)


def _da_attention(proj, proj_meta, lams, g_da, *, batch, seq, heads, tq, ts, col0, lam_init):
    hd = DA_V_DIM
    qb, kb, vb = col0 // hd, col0 // hd + heads, col0 // hd + 2 * heads
    kern = functools.partial(_da_kernel, tq=tq, ts=ts, scale=DA_HEAD_DIM ** -0.5, lam_init=lam_init)
    lam_spec = pl.BlockSpec((1, DA_HEAD_DIM), lambda b, h, i: (0, 0))
    return pl.pallas_call(
        kern,
        grid=(batch, heads, seq // tq),
        in_specs=[
            lam_spec, lam_spec, lam_spec, lam_spec,
            pl.BlockSpec((None, tq, hd), lambda b, h, i: (b, i, qb + h)),
            pl.BlockSpec((None, seq, hd), lambda b, h, i: (b, 0, kb + h)),
            pl.BlockSpec((None, seq, hd), lambda b, h, i: (b, 0, vb + h)),
            pl.BlockSpec((META_PAD, hd), lambda b, h, i: (0, kb + h)),
            pl.BlockSpec((META_PAD, hd), lambda b, h, i: (0, vb + h)),
            pl.BlockSpec((1, hd), lambda b, h, i: (0, h)),
        ],
        out_specs=pl.BlockSpec((None, tq, hd), lambda b, h, i: (b, i, h)),
        out_shape=jax.ShapeDtypeStruct((batch, seq, heads * hd), BF16),
        compiler_params=_cparams(("parallel", "parallel", "arbitrary")),
        name="da_attn",
    )(*lams, proj, proj, proj, proj_meta, proj_meta, g_da)


def _split3_dot(a, b_hi, b_lo):
    a_hi = a.astype(BF16)
    a_lo = (a - a_hi.astype(F32)).astype(BF16)
    return (jnp.dot(a_hi, b_hi, preferred_element_type=F32)
            + jnp.dot(a_hi, b_lo, preferred_element_type=F32)
            + jnp.dot(a_lo, b_hi, preferred_element_type=F32))


def _out_proj_kernel(ms_ref, md_ref, ws_ref, wd_ref, x_ref, g_ref, wr_hi_ref, wr_lo_ref, br_ref,
                     h_ref, u_ref, e_ref, p_ref, c_ref, *, n_exp):
    h = (x_ref[...]
         + jnp.dot(ms_ref[...], ws_ref[...], preferred_element_type=F32)
         + jnp.dot(md_ref[...], wd_ref[...], preferred_element_type=F32))
    h_ref[...] = h
    msq = jnp.mean(h * h, axis=-1, keepdims=True)
    u = (h * lax.rsqrt(msq + RMS_EPS)) * g_ref[...]
    u_ref[...] = u.astype(BF16)

    logits = _split3_dot(u, wr_hi_ref[...], wr_lo_ref[...]) + br_ref[...]
    lane = lax.broadcasted_iota(jnp.int32, logits.shape, 1).astype(F32)
    work = logits
    tops, idxs = [], []
    for _ in range(TOP_K):
        mx = jnp.max(work, axis=-1, keepdims=True)
        ix = jnp.min(jnp.where(work == mx, lane, float(n_exp)), axis=-1, keepdims=True)
        tops.append(mx)
        idxs.append(ix)
        work = jnp.where(lane == ix, -jnp.inf, work)
    ex = [jnp.exp(t - tops[0]) for t in tops]
    den = ex[0] + ex[1] + ex[2] + ex[3]
    kl = lax.broadcasted_iota(jnp.int32, (logits.shape[0], TOP_K), 1)
    e_out = jnp.zeros((logits.shape[0], TOP_K), F32)
    p_out = jnp.zeros((logits.shape[0], TOP_K), F32)
    for k in range(TOP_K):
        e_out = jnp.where(kl == k, idxs[k], e_out)
        p_out = jnp.where(kl == k, ex[k] / den, p_out)
    e_ref[...] = e_out.astype(jnp.int32)
    p_ref[...] = p_out
    hits = jnp.zeros(logits.shape, F32)
    for k in range(TOP_K):
        hits = hits + jnp.where(lane == idxs[k], 1.0, 0.0)
    c_ref[...] = jnp.sum(hits, axis=0, keepdims=True)


def _out_proj(mix_sb, mix_da, w_sb, w_da, x2d, g_ffn, wr_hi, wr_lo, b_r, *, tm):
    m, d = x2d.shape
    ks, kd = mix_sb.shape[1], mix_da.shape[1]
    n_exp = wr_hi.shape[1]
    kern = functools.partial(_out_proj_kernel, n_exp=n_exp)
    const = lambda shape: pl.BlockSpec(shape, lambda i: (0, 0))
    return pl.pallas_call(
        kern,
        grid=(m // tm,),
        in_specs=[
            pl.BlockSpec((tm, ks), lambda i: (i, 0)),
            pl.BlockSpec((tm, kd), lambda i: (i, 0)),
            const((ks, d)), const((kd, d)),
            pl.BlockSpec((tm, d), lambda i: (i, 0)),
            const((1, d)), const((d, n_exp)), const((d, n_exp)), const((1, n_exp)),
        ],
        out_specs=[
            pl.BlockSpec((tm, d), lambda i: (i, 0)),
            pl.BlockSpec((tm, d), lambda i: (i, 0)),
            pl.BlockSpec((tm, TOP_K), lambda i: (i, 0)),
            pl.BlockSpec((tm, TOP_K), lambda i: (i, 0)),
            pl.BlockSpec((None, 1, n_exp), lambda i: (i, 0, 0)),
        ],
        out_shape=[
            jax.ShapeDtypeStruct((m, d), F32),
            jax.ShapeDtypeStruct((m, d), BF16),
            jax.ShapeDtypeStruct((m, TOP_K), jnp.int32),
            jax.ShapeDtypeStruct((m, TOP_K), F32),
            jax.ShapeDtypeStruct((m // tm, 1, n_exp), F32),
        ],
        compiler_params=_cparams(("parallel",)),
        name="out_proj_router",
    )(mix_sb, mix_da, w_sb, w_da, x2d, g_ffn, wr_hi, wr_lo, b_r)


def _moe_kernel(be_ref, bc_ref, rb_ref, x_ref, wg_ref, wu_ref, wd_ref, bg_ref, bup_ref, bd_ref, o_ref,
                act_ref, *, nf, ts):
    blk = pl.program_id(0)
    s = pl.program_id(1)
    cnt = bc_ref[blk]
    nsub = x_ref.shape[0] // ts

    @pl.when(jnp.logical_and(cnt > 0, s < nf))
    def _():
        wg = wg_ref[...].astype(BF16)
        wu = wu_ref[...].astype(BF16)
        slot = jnp.minimum(s, nf - 1)
        for j in range(nsub):
            @pl.when(cnt > j * ts)
            def _():
                x = x_ref[j * ts:(j + 1) * ts, :]
                g = jnp.dot(x, wg, preferred_element_type=F32) + bg_ref[...]
                u = jnp.dot(x, wu, preferred_element_type=F32) + bup_ref[...]
                gate = jnp.minimum(g, SWIGLU_LIMIT)
                up = jnp.clip(u, -SWIGLU_LIMIT, SWIGLU_LIMIT)
                act = (up + 1.0) * gate * jax.nn.sigmoid(SWIGLU_ALPHA * gate)
                act_ref[slot, j * ts:(j + 1) * ts, :] = act.astype(BF16)

    @pl.when(jnp.logical_and(cnt > 0, s >= nf))
    def _():
        tf = act_ref.shape[2]
        wd = wd_ref[...].astype(BF16)
        for j in range(nsub):
            @pl.when(cnt > j * ts)
            def _():
                y = bd_ref[...]
                for f in range(nf):
                    y = y + jnp.dot(act_ref[f, j * ts:(j + 1) * ts, :], wd[f * tf:(f + 1) * tf, :],
                                    preferred_element_type=F32)
                o_ref[j * ts:(j + 1) * ts, :] = y.astype(o_ref.dtype)

            @pl.when(cnt <= j * ts)
            def _():
                o_ref[j * ts:(j + 1) * ts, :] = jnp.zeros((ts, o_ref.shape[1]), o_ref.dtype)


def _moe_ffn(blk_e, blk_cnt, blk_row, xs, w_gu, w_dn, b_gu, b_dn, *, tm, ts, tf, tn):
    n_rows, d = xs.shape
    n_exp, _, f2 = w_gu.shape
    d_ff = f2 // 2
    nf = d_ff // tf
    nn = d // tn
    n_blk = n_rows // tm
    b_gu3 = b_gu.reshape(n_exp, 1, f2)
    b_dn3 = b_dn.reshape(n_exp, 1, d)

    def fi(s, bc, b):
        return jnp.where(bc[b] > 0, jnp.minimum(s, nf - 1), nf - 1)

    def ni(s, bc, b):
        return jnp.where(bc[b] > 0, jnp.clip(s - nf, 0, nn - 1), nn - 1)

    grid_spec = pltpu.PrefetchScalarGridSpec(
        num_scalar_prefetch=3,
        grid=(n_blk, nf + nn),
        in_specs=[
            pl.BlockSpec((tm, d), lambda b, s, be, bc, rb: (rb[b], 0)),
            pl.BlockSpec((None, d, tf), lambda b, s, be, bc, rb: (be[b], 0, fi(s, bc, b))),
            pl.BlockSpec((None, d, tf), lambda b, s, be, bc, rb: (be[b], 0, nf + fi(s, bc, b))),
            pl.BlockSpec((None, d_ff, tn), lambda b, s, be, bc, rb: (be[b], 0, ni(s, bc, b))),
            pl.BlockSpec((None, 1, tf), lambda b, s, be, bc, rb: (be[b], 0, fi(s, bc, b))),
            pl.BlockSpec((None, 1, tf), lambda b, s, be, bc, rb: (be[b], 0, nf + fi(s, bc, b))),
            pl.BlockSpec((None, 1, tn), lambda b, s, be, bc, rb: (be[b], 0, ni(s, bc, b))),
        ],
        out_specs=pl.BlockSpec((tm, tn), lambda b, s, be, bc, rb: (rb[b], ni(s, bc, b))),
        scratch_shapes=[pltpu.VMEM((nf, tm, tf), BF16)],
    )
    return pl.pallas_call(
        functools.partial(_moe_kernel, nf=nf, ts=ts),
        grid_spec=grid_spec,
        out_shape=jax.ShapeDtypeStruct((n_rows, d), BF16),
        compiler_params=_cparams(("arbitrary", "arbitrary")),
        name="moe_ffn",
    )(blk_e, blk_cnt, blk_row, xs, w_gu, w_gu, w_dn, b_gu3, b_gu3, b_dn3)


def _final_kernel(h_ref, y0_ref, y1_ref, y2_ref, y3_ref, p_ref, g_ref, o_ref):
    p = p_ref[...]
    y = None
    for k, y_ref in enumerate((y0_ref, y1_ref, y2_ref, y3_ref)):
        t = y_ref[...].astype(F32) * p[:, k:k + 1]
        y = t if y is None else y + t
    h = h_ref[...] + y
    ms = jnp.mean(h * h, axis=-1, keepdims=True)
    o_ref[...] = (h * lax.rsqrt(ms + RMS_EPS)) * g_ref[...]


def _final(h1, y_km, gates, g, *, tm):
    m, d = h1.shape
    nb = m // tm
    assert TOP_K == 4
    y_specs = [pl.BlockSpec((tm, d), functools.partial(lambda i, k: (k * nb + i, 0), k=k)) for k in range(TOP_K)]
    return pl.pallas_call(
        _final_kernel,
        grid=(nb,),
        in_specs=([pl.BlockSpec((tm, d), lambda i: (i, 0))] + y_specs
                  + [pl.BlockSpec((tm, TOP_K), lambda i: (i, 0)), pl.BlockSpec((1, d), lambda i: (0, 0))]),
        out_specs=pl.BlockSpec((tm, d), lambda i: (i, 0)),
        out_shape=jax.ShapeDtypeStruct((m, d), F32),
        compiler_params=_cparams(("parallel",)),
        name="final_norm",
    )(h1, y_km, y_km, y_km, y_km, gates, g)


def _routing(top_e, counts, tm):
    n_exp = counts.shape[0]
    n_assign = top_e.shape[0] * TOP_K
    flat_e = top_e.reshape(-1)
    order = jnp.argsort(flat_e, stable=True).astype(jnp.int32)
    rank = jnp.argsort(order).astype(jnp.int32)
    padded = (counts + tm - 1) // tm * tm
    pad_end = jnp.cumsum(padded).astype(jnp.int32)
    pad_start = pad_end - padded
    grp_start = jnp.cumsum(counts).astype(jnp.int32) - counts
    n_blk = -(-n_assign // tm) + n_exp
    blk_start = jnp.arange(n_blk, dtype=jnp.int32) * tm
    blk_e = jnp.minimum(jnp.sum((blk_start[:, None] >= pad_end[None, :]).astype(jnp.int32), axis=1), n_exp - 1)
    blk_off = blk_start - pad_start[blk_e]
    blk_cnt = jnp.clip(counts[blk_e] - blk_off, 0, tm).astype(jnp.int32)
    lane = jnp.arange(tm, dtype=jnp.int32)[None, :]
    valid = lane < blk_cnt[:, None]
    src = jnp.where(valid, (grp_start[blk_e] + blk_off)[:, None] + lane, 0)
    row_tok = jnp.where(valid, order[src] // TOP_K, 0).reshape(-1)
    delta = pad_start - grp_start
    hit = flat_e[None, :] == jnp.arange(n_exp, dtype=jnp.int32)[:, None]
    dest_flat = rank + jnp.sum(jnp.where(hit, delta[:, None], 0), axis=0)
    last_used = jnp.maximum(pad_end[-1] // tm - 1, 0)
    blk_e = jnp.where(blk_cnt > 0, blk_e, blk_e[last_used]).astype(jnp.int32)
    blk_row = jnp.where(blk_cnt > 0, jnp.arange(n_blk, dtype=jnp.int32), last_used).astype(jnp.int32)
    return row_tok, dest_flat, blk_e, blk_cnt, blk_row


def kernel(x, meta_tokens, g_mix, w_in, lam_q1, lam_k1, lam_q2, lam_k2, g_sb_out, g_da_out, w_out, g_ffn,
           w_router, b_router, w_gate_up, b_gate_up, w_down, b_down, g_final):
    b, s, d = x.shape
    depth = w_in.shape[0]
    assert depth == 1, "single-layer trunk"
    layer = 0
    sb_heads = (d // 2) // SB_HEAD_DIM
    da_heads = (d // 2) // DA_V_DIM
    sb_w = sb_heads * SB_HEAD_DIM
    da_col0 = 3 * sb_w
    da_qk_w = da_heads * 2 * DA_HEAD_DIM
    n_exp = w_router.shape[-1]
    lam_init = 0.8 - 0.6 * math.exp(-0.3 * layer)

    tm_proj, tn_proj = TM_PROJ, TN_PROJ
    tq, tk_sb, ts_da = TQ_ATTN, TK_SB, TS_DA
    tm_out = TM_OUT
    tm_moe, tf_moe = TM_MOE, TF_MOE
    tm_fin = TM_FINAL

    x2d = x.reshape(b * s, d)
    w_in_bf = w_in.reshape(w_in.shape[1:]).astype(BF16)
    g_mix2 = g_mix[layer].reshape(1, d)
    rope_cols = (da_col0, da_col0 + 2 * da_qk_w)

    pos_f = N_META + jnp.arange(s, dtype=jnp.int32)
    proj = _in_proj(x2d, g_mix2, w_in_bf, _rope_tables(pos_f), tm=tm_proj, tn=tn_proj,
                    rope_cols=rope_cols, pos_blocks=s // tm_proj)
    meta_pad = jnp.zeros((META_PAD, d), x.dtype).at[:N_META].set(meta_tokens.astype(x.dtype))
    pos_m = jnp.arange(META_PAD, dtype=jnp.int32)
    proj_meta = _in_proj(meta_pad, g_mix2, w_in_bf, _rope_tables(pos_m), tm=META_PAD, tn=tn_proj,
                         rope_cols=rope_cols, pos_blocks=1)
    proj3 = proj.reshape(b, s, -1)

    mix_sb = _sb_attention(proj3, proj_meta, g_sb_out[layer].reshape(1, -1), batch=b, seq=s, heads=sb_heads,
                           tq=tq, tk=tk_sb)
    lams = tuple(t[layer].reshape(1, DA_HEAD_DIM).astype(F32) for t in (lam_q1, lam_k1, lam_q2, lam_k2))
    mix_da = _da_attention(proj3, proj_meta, lams, g_da_out[layer].reshape(1, -1), batch=b, seq=s, heads=da_heads,
                           tq=tq, ts=ts_da, col0=da_col0, lam_init=lam_init)

    w_out_bf = w_out[layer].astype(BF16)
    wr = w_router[layer]
    wr_hi = wr.astype(BF16)
    wr_lo = (wr - wr_hi.astype(F32)).astype(BF16)
    h1, u_bf, top_e, gates, tile_counts = _out_proj(
        mix_sb.reshape(b * s, -1), mix_da.reshape(b * s, -1), w_out_bf[:sb_w], w_out_bf[sb_w:], x2d,
        g_ffn[layer].reshape(1, d), wr_hi, wr_lo, b_router[layer].reshape(1, n_exp), tm=tm_out)

    counts = jnp.sum(tile_counts, axis=(0, 1)).astype(jnp.int32)
    row_tok, dest_flat, blk_e, blk_cnt, blk_row = _routing(top_e, counts, tm_moe)
    xs = jnp.take(u_bf, row_tok, axis=0)
    rows = _moe_ffn(blk_e, blk_cnt, blk_row, xs, w_gate_up.reshape(w_gate_up.shape[1:]),
                    w_down.reshape(w_down.shape[1:]), b_gate_up[layer], b_down[layer],
                    tm=tm_moe, ts=TS_MOE, tf=tf_moe, tn=TN_MOE)
    dest_km = dest_flat.reshape(b * s, TOP_K).T.reshape(-1)
    y_km = jnp.take(rows, dest_km, axis=0)

    out = _final(h1, y_km, gates, g_final.reshape(1, d), tm=tm_fin)
    return out.reshape(b, s, d)
```

```python
import functools
import math

import jax
import jax.numpy as jnp
from jax import lax
from jax.experimental import pallas as pl
from jax.experimental.pallas import tpu as pltpu
from jax.experimental.pallas import tpu_sc as plsc

F32 = jnp.float32
BF16 = jnp.bfloat16

CHUNK = 64
N_META = 16
RMS_EPS = 1e-5
SB_HEAD_DIM = 128
DA_HEAD_DIM = 64
DA_V_DIM = 2 * DA_HEAD_DIM
ROPE_THETA = 500000.0
ROPE_DIM = DA_HEAD_DIM // 4
TOP_K = 4
SWIGLU_LIMIT = 7.0
SWIGLU_ALPHA = 1.702

LANES = 128
META_PAD = 128
NEG_BIG = -1e30
VMEM_LIMIT = 56 * 1024 * 1024
SC_CORES, SC_SUBCORES = 2, 16
SC_GATHER_ROWS = 64

TM_PROJ, TN_PROJ = 512, 1024
TQ_ATTN = 512
TK_SB = 256
TS_DA = 256
TM_OUT = 512
TM_MOE, TS_MOE = 2048, 512
TF_MOE, TN_MOE = 256, 256
TM_FINAL = 512


def _cparams(sem):
    return pltpu.CompilerParams(dimension_semantics=sem, vmem_limit_bytes=VMEM_LIMIT)


def _in_proj_kernel(x_ref, g_ref, w_ref, c_ref, sa_ref, sb_ref, o_ref, u_scr, *, tn, rope_lo, rope_hi):
    n = pl.program_id(1)

    @pl.when(n == 0)
    def _():
        x = x_ref[...]
        ms = jnp.mean(x * x, axis=-1, keepdims=True)
        u_scr[...] = ((x * lax.rsqrt(ms + RMS_EPS)) * g_ref[...]).astype(BF16)

    acc = jnp.dot(u_scr[...], w_ref[...], preferred_element_type=F32)
    is_rope = jnp.logical_and(n >= rope_lo, n < rope_hi)

    @pl.when(is_rope)
    def _():
        for c in range(tn // LANES):
            xc = acc[:, c * LANES:(c + 1) * LANES]
            r = (xc * c_ref[...] + pltpu.roll(xc, LANES - ROPE_DIM // 2, 1) * sa_ref[...]
                 + pltpu.roll(xc, ROPE_DIM // 2, 1) * sb_ref[...])
            o_ref[:, c * LANES:(c + 1) * LANES] = r.astype(BF16)

    @pl.when(jnp.logical_not(is_rope))
    def _():
        o_ref[...] = acc.astype(BF16)


def _rope_tables(pos):
    half = ROPE_DIM // 2
    inv_freq = ROPE_THETA ** (-(jnp.arange(half, dtype=F32) * 2.0 / ROPE_DIM))
    ang = pos.astype(F32)[:, None] * inv_freq[None, :]
    cos, sin = jnp.cos(ang), jnp.sin(ang)
    p = pos.shape[0]
    ones = jnp.ones((p, DA_HEAD_DIM - ROPE_DIM), F32)
    zeros8 = jnp.zeros((p, half), F32)
    zeros48 = jnp.zeros((p, DA_HEAD_DIM - ROPE_DIM), F32)
    c64 = jnp.concatenate([cos, cos, ones], axis=1)
    sa64 = jnp.concatenate([-sin, zeros8, zeros48], axis=1)
    sb64 = jnp.concatenate([zeros8, sin, zeros48], axis=1)
    tile2 = lambda t: jnp.concatenate([t, t], axis=1)
    return tile2(c64), tile2(sa64), tile2(sb64)


def _in_proj(x2d, g, w_bf, tables, *, tm, tn, rope_cols, pos_blocks):
    m, d = x2d.shape
    n_cols = w_bf.shape[1]
    c_t, sa_t, sb_t = tables
    kern = functools.partial(_in_proj_kernel, tn=tn, rope_lo=rope_cols[0] // tn, rope_hi=rope_cols[1] // tn)
    tab_spec = pl.BlockSpec((tm, LANES), lambda i, n: (i % pos_blocks, 0))
    return pl.pallas_call(
        kern,
        grid=(m // tm, n_cols // tn),
        in_specs=[
            pl.BlockSpec((tm, d), lambda i, n: (i, 0)),
            pl.BlockSpec((1, d), lambda i, n: (0, 0)),
            pl.BlockSpec((d, tn), lambda i, n: (0, n)),
            tab_spec, tab_spec, tab_spec,
        ],
        out_specs=pl.BlockSpec((tm, tn), lambda i, n: (i, n)),
        out_shape=jax.ShapeDtypeStruct((m, n_cols), BF16),
        scratch_shapes=[pltpu.VMEM((tm, d), BF16)],
        compiler_params=_cparams(("parallel", "arbitrary")),
        name="in_proj",
    )(x2d, g, w_bf, c_t, sa_t, sb_t)


def _dot_nt(a, b):
    return lax.dot_general(a, b, (((1,), (1,)), ((), ())), preferred_element_type=F32)


def _suffix_sum_matrix(n):
    j = lax.broadcasted_iota(jnp.int32, (2 * n, n), 0)
    s = lax.broadcasted_iota(jnp.int32, (2 * n, n), 1)
    return jnp.where(jnp.where(j >= n, j - n, j) > s, 1.0, 0.0).astype(BF16)


def _sb_block(q, kb, vb, u2, run, acc, mask, scale):
    z = _dot_nt(q, kb) * scale
    sp = jnp.maximum(z, 0.0) + jnp.log(1.0 + jnp.exp(-jnp.abs(z)))
    if mask is not None:
        sp = jnp.where(mask, sp, 0.0)
    hi = sp.astype(BF16)
    lo = (sp - hi.astype(F32)).astype(BF16)
    cs = jnp.dot(jnp.concatenate([hi, lo], axis=1), u2, preferred_element_type=F32)
    w = jnp.exp(z - sp - cs - run)
    if mask is not None:
        w = jnp.where(mask, w, 0.0)
    acc = acc + jnp.dot(w.astype(BF16), vb, preferred_element_type=F32)
    run = run + (cs[:, :1] + sp[:, :1])
    return run, acc


def _sb_kernel(q_ref, k_ref, v_ref, km_ref, vm_ref, g_ref, o_ref, *, tq, tk, scale):
    i = pl.program_id(2)
    nsub = tq // tk
    u_blk = _suffix_sum_matrix(tk)
    row = lax.broadcasted_iota(jnp.int32, (tk, tk), 0)
    col = lax.broadcasted_iota(jnp.int32, (tk, tk), 1)
    diag_mask = col < row

    def kv(blk):
        s0 = pl.multiple_of(blk * tk, tk)
        return k_ref[pl.ds(s0, tk), :], v_ref[pl.ds(s0, tk), :]

    runs, accs = [], []
    for h in range(nsub):
        qh = q_ref[h * tk:(h + 1) * tk, :]
        run = jnp.zeros((tk, 1), F32)
        acc = jnp.zeros((tk, SB_HEAD_DIM), F32)
        for c in range(h, -1, -1):
            kb, vb = kv(i * nsub + c)
            run, acc = _sb_block(qh, kb, vb, u_blk, run, acc, diag_mask if c == h else None, scale)
        runs.append(run)
        accs.append(acc)
    run = jnp.concatenate(runs, axis=0)
    acc = jnp.concatenate(accs, axis=0)
    q = q_ref[...]

    def body(jj, carry):
        r, a = carry
        for c in range(nsub - 1, -1, -1):
            kb, vb = kv((i - 1 - jj) * nsub + c)
            r, a = _sb_block(q, kb, vb, u_blk, r, a, None, scale)
        return r, a

    run, acc = lax.fori_loop(0, i, body, (run, acc))

    mcol = lax.broadcasted_iota(jnp.int32, (tq, META_PAD), 1)
    run, acc = _sb_block(q, km_ref[...], vm_ref[...], _suffix_sum_matrix(META_PAD), run, acc,
                         mcol < N_META, scale)

    ms = jnp.mean(acc * acc, axis=-1, keepdims=True)
    o_ref[...] = ((acc * lax.rsqrt(ms + RMS_EPS)) * g_ref[...]).astype(BF16)


def _sb_attention(proj, proj_meta, g_sb, *, batch, seq, heads, tq, tk):
    hd = SB_HEAD_DIM
    kern = functools.partial(_sb_kernel, tq=tq, tk=tk, scale=hd ** -0.5)
    return pl.pallas_call(
        kern,
        grid=(batch, heads, seq // tq),
        in_specs=[
            pl.BlockSpec((None, tq, hd), lambda b, h, i: (b, i, h)),
            pl.BlockSpec((None, seq, hd), lambda b, h, i: (b, 0, heads + h)),
            pl.BlockSpec((None, seq, hd), lambda b, h, i: (b, 0, 2 * heads + h)),
            pl.BlockSpec((META_PAD, hd), lambda b, h, i: (0, heads + h)),
            pl.BlockSpec((META_PAD, hd), lambda b, h, i: (0, 2 * heads + h)),
            pl.BlockSpec((1, hd), lambda b, h, i: (0, h)),
        ],
        out_specs=pl.BlockSpec((None, tq, hd), lambda b, h, i: (b, i, h)),
        out_shape=jax.ShapeDtypeStruct((batch, seq, heads * hd), BF16),
        compiler_params=_cparams(("parallel", "parallel", "arbitrary")),
        name="sb_attn",
    )(proj, proj, proj, proj_meta, proj_meta, g_sb)


def _with_ones(vb):
    return jnp.concatenate([vb, jnp.ones_like(vb)], axis=1)


def _da_block(q1, q2, kb, vb1, st, mask):
    m1, a1, m2, a2 = st

    def one(qc, m, a):
        s = _dot_nt(qc, kb)
        if mask is not None:
            s = jnp.where(mask, s, NEG_BIG)
        mn = jnp.maximum(m, jnp.max(s, axis=-1, keepdims=True))
        p = jnp.exp(s - mn)
        a = jnp.exp(m - mn) * a + jnp.dot(p.astype(BF16), vb1, preferred_element_type=F32)
        return mn, a

    m1, a1 = one(q1, m1, a1)
    m2, a2 = one(q2, m2, a2)
    return m1, a1, m2, a2


def _da_kernel(lq1_ref, lk1_ref, lq2_ref, lk2_ref, q_ref, k_ref, v_ref, km_ref, vm_ref, g_ref, o_ref,
               *, tq, ts, scale, lam_init):
    i = pl.program_id(2)
    nsub = tq // ts
    lam = (jnp.exp(jnp.sum(lq1_ref[...] * lk1_ref[...], axis=-1, keepdims=True))
           - jnp.exp(jnp.sum(lq2_ref[...] * lk2_ref[...], axis=-1, keepdims=True)) + lam_init)

    q = q_ref[...]
    lane = lax.broadcasted_iota(jnp.int32, q.shape, 1)
    qs = q * jnp.asarray(scale, BF16)
    zero = jnp.zeros_like(qs)
    q1 = jnp.where(lane < DA_HEAD_DIM, qs, zero)
    q2 = jnp.where(lane >= DA_HEAD_DIM, qs, zero)

    neg = jnp.full((tq, 1), NEG_BIG, F32)
    za = jnp.zeros((tq, 2 * DA_V_DIM), F32)
    st = (neg, za, neg, za)

    def body(j, carry):
        s0 = pl.multiple_of(j * tq, tq)
        return _da_block(q1, q2, k_ref[pl.ds(s0, tq), :], _with_ones(v_ref[pl.ds(s0, tq), :]), carry, None)

    st = lax.fori_loop(0, i, body, st)

    shift = CHUNK.bit_length() - 1
    s0 = pl.multiple_of(i * tq, tq)
    outs = []
    for h in range(nsub):
        rows = slice(h * ts, (h + 1) * ts)
        width = (h + 1) * ts
        kb = jnp.concatenate([km_ref[...], k_ref[pl.ds(s0, width), :]], axis=0)
        vb = jnp.concatenate([vm_ref[...], v_ref[pl.ds(s0, width), :]], axis=0)
        col = lax.broadcasted_iota(jnp.int32, (ts, META_PAD + width), 1)
        row = lax.broadcasted_iota(jnp.int32, (ts, META_PAD + width), 0) + h * ts
        frame_vis = jnp.logical_and(col >= META_PAD,
                                    jnp.right_shift(col - META_PAD, shift) <= jnp.right_shift(row, shift))
        vis = jnp.logical_or(col < N_META, frame_vis)
        sth = _da_block(q1[rows], q2[rows], kb, _with_ones(vb), tuple(t[rows] for t in st), vis)
        _, a1, _, a2 = sth
        outs.append(a1[:, :DA_V_DIM] / a1[:, DA_V_DIM:] - lam * (a2[:, :DA_V_DIM] / a2[:, DA_V_DIM:]))
    o = jnp.concatenate(outs, axis=0)
    ms = jnp.mean(o * o, axis=-1, keepdims=True)
    o_ref[...] = (((o * lax.rsqrt(ms + RMS_EPS)) * g_ref[...]) * (1.0 - lam_init)).astype(BF16)


def _da_attention(proj, proj_meta, lams, g_da, *, batch, seq, heads, tq, ts, col0, lam_init):
    hd = DA_V_DIM
    qb, kb, vb = col0 // hd, col0 // hd + heads, col0 // hd + 2 * heads
    kern = functools.partial(_da_kernel, tq=tq, ts=ts, scale=DA_HEAD_DIM ** -0.5, lam_init=lam_init)
    lam_spec = pl.BlockSpec((1, DA_HEAD_DIM), lambda b, h, i: (0, 0))
    return pl.pallas_call(
        kern,
        grid=(batch, heads, seq // tq),
        in_specs=[
            lam_spec, lam_spec, lam_spec, lam_spec,
            pl.BlockSpec((None, tq, hd), lambda b, h, i: (b, i, qb + h)),
            pl.BlockSpec((None, seq, hd), lambda b, h, i: (b, 0, kb + h)),
            pl.BlockSpec((None, seq, hd), lambda b, h, i: (b, 0, vb + h)),
            pl.BlockSpec((META_PAD, hd), lambda b, h, i: (0, kb + h)),
            pl.BlockSpec((META_PAD, hd), lambda b, h, i: (0, vb + h)),
            pl.BlockSpec((1, hd), lambda b, h, i: (0, h)),
        ],
        out_specs=pl.BlockSpec((None, tq, hd), lambda b, h, i: (b, i, h)),
        out_shape=jax.ShapeDtypeStruct((batch, seq, heads * hd), BF16),
        compiler_params=_cparams(("parallel", "parallel", "arbitrary")),
        name="da_attn",
    )(*lams, proj, proj, proj, proj_meta, proj_meta, g_da)


def _pack_bf16_pairs(x):
    n = x.shape[1] // 2
    lo = lax.bitcast_convert_type(x[:, :n].astype(BF16).astype(F32), jnp.int32)
    hi = lax.bitcast_convert_type(x[:, n:].astype(BF16).astype(F32), jnp.int32)
    return jnp.bitwise_or(lax.shift_right_logical(lo, 16), jnp.bitwise_and(hi, jnp.int32(-65536)))


def _unpack_bf16_pairs(w):
    lo = lax.bitcast_convert_type(lax.shift_left(w, 16), F32)
    hi = lax.bitcast_convert_type(jnp.bitwise_and(w, jnp.int32(-65536)), F32)
    return jnp.concatenate([lo.astype(BF16), hi.astype(BF16)], axis=1)


def _sc_gather_rows(table, idx):
    n_idx = idx.shape[0]
    width = table.shape[1]
    n_workers = SC_CORES * SC_SUBCORES
    per_worker = n_idx // n_workers
    win = SC_GATHER_ROWS
    assert n_idx % (n_workers * win) == 0
    mesh = plsc.VectorSubcoreMesh(core_axis_name="c", subcore_axis_name="s")

    @functools.partial(
        pl.kernel, mesh=mesh,
        out_type=jax.ShapeDtypeStruct((n_idx, width), table.dtype),
        scratch_types=[pltpu.VMEM((win,), jnp.int32), pltpu.VMEM((win, width), table.dtype),
                       pltpu.SemaphoreType.DMA],
    )
    def gather(table_hbm, idx_hbm, out_hbm, idx_v, rows_v, sem):
        worker = lax.axis_index("s") * SC_CORES + lax.axis_index("c")
        base = worker * per_worker

        @pl.loop(0, per_worker // win)
        def _(t):
            off = pl.multiple_of(base + t * win, win)
            pltpu.sync_copy(idx_hbm.at[pl.ds(off, win)], idx_v)
            pltpu.async_copy(table_hbm.at[idx_v], rows_v, sem).wait()
            pltpu.sync_copy(rows_v, out_hbm.at[pl.ds(off, win)])

    return gather(table, idx)


def _split3_dot(a, b_hi, b_lo):
    a_hi = a.astype(BF16)
    a_lo = (a - a_hi.astype(F32)).astype(BF16)
    return (jnp.dot(a_hi, b_hi, preferred_element_type=F32)
            + jnp.dot(a_hi, b_lo, preferred_element_type=F32)
            + jnp.dot(a_lo, b_hi, preferred_element_type=F32))


def _out_proj_kernel(ms_ref, md_ref, ws_ref, wd_ref, x_ref, g_ref, wr_hi_ref, wr_lo_ref, br_ref,
                     h_ref, u_ref, e_ref, p_ref, c_ref, *, n_exp):
    h = (x_ref[...]
         + jnp.dot(ms_ref[...], ws_ref[...], preferred_element_type=F32)
         + jnp.dot(md_ref[...], wd_ref[...], preferred_element_type=F32))
    h_ref[...] = h
    msq = jnp.mean(h * h, axis=-1, keepdims=True)
    u = (h * lax.rsqrt(msq + RMS_EPS)) * g_ref[...]
    u_ref[...] = _pack_bf16_pairs(u)

    logits = _split3_dot(u, wr_hi_ref[...], wr_lo_ref[...]) + br_ref[...]
    lane = lax.broadcasted_iota(jnp.int32, logits.shape, 1).astype(F32)
    work = logits
    tops, idxs = [], []
    for _ in range(TOP_K):
        mx = jnp.max(work, axis=-1, keepdims=True)
        ix = jnp.min(jnp.where(work == mx, lane, float(n_exp)), axis=-1, keepdims=True)
        tops.append(mx)
        idxs.append(ix)
        work = jnp.where(lane == ix, -jnp.inf, work)
    ex = [jnp.exp(t - tops[0]) for t in tops]
    den = ex[0] + ex[1] + ex[2] + ex[3]
    kl = lax.broadcasted_iota(jnp.int32, (logits.shape[0], TOP_K), 1)
    e_out = jnp.zeros((logits.shape[0], TOP_K), F32)
    p_out = jnp.zeros((logits.shape[0], TOP_K), F32)
    for k in range(TOP_K):
        e_out = jnp.where(kl == k, idxs[k], e_out)
        p_out = jnp.where(kl == k, ex[k] / den, p_out)
    e_ref[...] = e_out.astype(jnp.int32)
    p_ref[...] = p_out
    hits = jnp.zeros(logits.shape, F32)
    for k in range(TOP_K):
        hits = hits + jnp.where(lane == idxs[k], 1.0, 0.0)
    c_ref[...] = jnp.sum(hits, axis=0, keepdims=True)


def _out_proj(mix_sb, mix_da, w_sb, w_da, x2d, g_ffn, wr_hi, wr_lo, b_r, *, tm):
    m, d = x2d.shape
    ks, kd = mix_sb.shape[1], mix_da.shape[1]
    n_exp = wr_hi.shape[1]
    kern = functools.partial(_out_proj_kernel, n_exp=n_exp)
    const = lambda shape: pl.BlockSpec(shape, lambda i: (0, 0))
    return pl.pallas_call(
        kern,
        grid=(m // tm,),
        in_specs=[
            pl.BlockSpec((tm, ks), lambda i: (i, 0)),
            pl.BlockSpec((tm, kd), lambda i: (i, 0)),
            const((ks, d)), const((kd, d)),
            pl.BlockSpec((tm, d), lambda i: (i, 0)),
            const((1, d)), const((d, n_exp)), const((d, n_exp)), const((1, n_exp)),
        ],
        out_specs=[
            pl.BlockSpec((tm, d), lambda i: (i, 0)),
            pl.BlockSpec((tm, d // 2), lambda i: (i, 0)),
            pl.BlockSpec((tm, TOP_K), lambda i: (i, 0)),
            pl.BlockSpec((tm, TOP_K), lambda i: (i, 0)),
            pl.BlockSpec((None, 1, n_exp), lambda i: (i, 0, 0)),
        ],
        out_shape=[
            jax.ShapeDtypeStruct((m, d), F32),
            jax.ShapeDtypeStruct((m, d // 2), jnp.int32),
            jax.ShapeDtypeStruct((m, TOP_K), jnp.int32),
            jax.ShapeDtypeStruct((m, TOP_K), F32),
            jax.ShapeDtypeStruct((m // tm, 1, n_exp), F32),
        ],
        compiler_params=_cparams(("parallel",)),
        name="out_proj_router",
    )(mix_sb, mix_da, w_sb, w_da, x2d, g_ffn, wr_hi, wr_lo, b_r)


def _moe_kernel(be_ref, bc_ref, rb_ref, x_ref, wg_ref, wu_ref, wd_ref, bg_ref, bup_ref, bd_ref, o_ref,
                act_ref, *, nf, ts):
    blk = pl.program_id(0)
    s = pl.program_id(1)
    cnt = bc_ref[blk]
    nsub = x_ref.shape[0] // ts

    @pl.when(jnp.logical_and(cnt > 0, s < nf))
    def _():
        wg = wg_ref[...].astype(BF16)
        wu = wu_ref[...].astype(BF16)
        slot = jnp.minimum(s, nf - 1)
        for j in range(nsub):
            @pl.when(cnt > j * ts)
            def _():
                x = _unpack_bf16_pairs(x_ref[j * ts:(j + 1) * ts, :])
                g = jnp.dot(x, wg, preferred_element_type=F32) + bg_ref[...]
                u = jnp.dot(x, wu, preferred_element_type=F32) + bup_ref[...]
                gate = jnp.minimum(g, SWIGLU_LIMIT)
                up = jnp.clip(u, -SWIGLU_LIMIT, SWIGLU_LIMIT)
                act = (up + 1.0) * gate * jax.nn.sigmoid(SWIGLU_ALPHA * gate)
                act_ref[slot, j * ts:(j + 1) * ts, :] = act.astype(BF16)

    @pl.when(jnp.logical_and(cnt > 0, s >= nf))
    def _():
        tf = act_ref.shape[2]
        wd = wd_ref[...].astype(BF16)
        for j in range(nsub):
            @pl.when(cnt > j * ts)
            def _():
                y = bd_ref[...]
                for f in range(nf):
                    y = y + jnp.dot(act_ref[f, j * ts:(j + 1) * ts, :], wd[f * tf:(f + 1) * tf, :],
                                    preferred_element_type=F32)
                o_ref[j * ts:(j + 1) * ts, :] = y.astype(o_ref.dtype)

            @pl.when(cnt <= j * ts)
            def _():
                o_ref[j * ts:(j + 1) * ts, :] = jnp.zeros((ts, o_ref.shape[1]), o_ref.dtype)


def _moe_ffn(blk_e, blk_cnt, blk_row, xs, w_gu, w_dn, b_gu, b_dn, *, tm, ts, tf, tn):
    n_rows = xs.shape[0]
    n_exp, d, f2 = w_gu.shape
    d_ff = f2 // 2
    nf = d_ff // tf
    nn = d // tn
    n_blk = n_rows // tm
    b_gu3 = b_gu.reshape(n_exp, 1, f2)
    b_dn3 = b_dn.reshape(n_exp, 1, d)

    def fi(s, bc, b):
        return jnp.where(bc[b] > 0, jnp.minimum(s, nf - 1), nf - 1)

    def ni(s, bc, b):
        return jnp.where(bc[b] > 0, jnp.clip(s - nf, 0, nn - 1), nn - 1)

    grid_spec = pltpu.PrefetchScalarGridSpec(
        num_scalar_prefetch=3,
        grid=(n_blk, nf + nn),
        in_specs=[
            pl.BlockSpec((tm, d // 2), lambda b, s, be, bc, rb: (rb[b], 0)),
            pl.BlockSpec((None, d, tf), lambda b, s, be, bc, rb: (be[b], 0, fi(s, bc, b))),
            pl.BlockSpec((None, d, tf), lambda b, s, be, bc, rb: (be[b], 0, nf + fi(s, bc, b))),
            pl.BlockSpec((None, d_ff, tn), lambda b, s, be, bc, rb: (be[b], 0, ni(s, bc, b))),
            pl.BlockSpec((None, 1, tf), lambda b, s, be, bc, rb: (be[b], 0, fi(s, bc, b))),
            pl.BlockSpec((None, 1, tf), lambda b, s, be, bc, rb: (be[b], 0, nf + fi(s, bc, b))),
            pl.BlockSpec((None, 1, tn), lambda b, s, be, bc, rb: (be[b], 0, ni(s, bc, b))),
        ],
        out_specs=pl.BlockSpec((tm, tn), lambda b, s, be, bc, rb: (rb[b], ni(s, bc, b))),
        scratch_shapes=[pltpu.VMEM((nf, tm, tf), BF16)],
    )
    return pl.pallas_call(
        functools.partial(_moe_kernel, nf=nf, ts=ts),
        grid_spec=grid_spec,
        out_shape=jax.ShapeDtypeStruct((n_rows, d), BF16),
        compiler_params=_cparams(("arbitrary", "arbitrary")),
        name="moe_ffn",
    )(blk_e, blk_cnt, blk_row, xs, w_gu, w_gu, w_dn, b_gu3, b_gu3, b_dn3)


def _final_kernel(h_ref, y0_ref, y1_ref, y2_ref, y3_ref, p_ref, g_ref, o_ref):
    p = p_ref[...]
    y = None
    for k, y_ref in enumerate((y0_ref, y1_ref, y2_ref, y3_ref)):
        t = y_ref[...].astype(F32) * p[:, k:k + 1]
        y = t if y is None else y + t
    h = h_ref[...] + y
    ms = jnp.mean(h * h, axis=-1, keepdims=True)
    o_ref[...] = (h * lax.rsqrt(ms + RMS_EPS)) * g_ref[...]


def _final(h1, y_km, gates, g, *, tm):
    m, d = h1.shape
    nb = m // tm
    assert TOP_K == 4
    y_specs = [pl.BlockSpec((tm, d), functools.partial(lambda i, k: (k * nb + i, 0), k=k)) for k in range(TOP_K)]
    return pl.pallas_call(
        _final_kernel,
        grid=(nb,),
        in_specs=([pl.BlockSpec((tm, d), lambda i: (i, 0))] + y_specs
                  + [pl.BlockSpec((tm, TOP_K), lambda i: (i, 0)), pl.BlockSpec((1, d), lambda i: (0, 0))]),
        out_specs=pl.BlockSpec((tm, d), lambda i: (i, 0)),
        out_shape=jax.ShapeDtypeStruct((m, d), F32),
        compiler_params=_cparams(("parallel",)),
        name="final_norm",
    )(h1, y_km, y_km, y_km, y_km, gates, g)


def _routing(top_e, counts, tm):
    n_exp = counts.shape[0]
    n_assign = top_e.shape[0] * TOP_K
    flat_e = top_e.reshape(-1)
    order = jnp.argsort(flat_e, stable=True).astype(jnp.int32)
    rank = jnp.argsort(order).astype(jnp.int32)
    padded = (counts + tm - 1) // tm * tm
    pad_end = jnp.cumsum(padded).astype(jnp.int32)
    pad_start = pad_end - padded
    grp_start = jnp.cumsum(counts).astype(jnp.int32) - counts
    n_blk = -(-n_assign // tm) + n_exp
    blk_start = jnp.arange(n_blk, dtype=jnp.int32) * tm
    blk_e = jnp.minimum(jnp.sum((blk_start[:, None] >= pad_end[None, :]).astype(jnp.int32), axis=1), n_exp - 1)
    blk_off = blk_start - pad_start[blk_e]
    blk_cnt = jnp.clip(counts[blk_e] - blk_off, 0, tm).astype(jnp.int32)
    lane = jnp.arange(tm, dtype=jnp.int32)[None, :]
    valid = lane < blk_cnt[:, None]
    src = jnp.where(valid, (grp_start[blk_e] + blk_off)[:, None] + lane, 0)
    filler = (blk_start[:, None] + lane) % top_e.shape[0]
    row_tok = jnp.where(valid, order[src] // TOP_K, filler).reshape(-1)
    delta = pad_start - grp_start
    hit = flat_e[None, :] == jnp.arange(n_exp, dtype=jnp.int32)[:, None]
    dest_flat = rank + jnp.sum(jnp.where(hit, delta[:, None], 0), axis=0)
    last_used = jnp.maximum(pad_end[-1] // tm - 1, 0)
    blk_e = jnp.where(blk_cnt > 0, blk_e, blk_e[last_used]).astype(jnp.int32)
    blk_row = jnp.where(blk_cnt > 0, jnp.arange(n_blk, dtype=jnp.int32), last_used).astype(jnp.int32)
    return row_tok, dest_flat, blk_e, blk_cnt, blk_row


def kernel(x, meta_tokens, g_mix, w_in, lam_q1, lam_k1, lam_q2, lam_k2, g_sb_out, g_da_out, w_out, g_ffn,
           w_router, b_router, w_gate_up, b_gate_up, w_down, b_down, g_final):
    b, s, d = x.shape
    depth = w_in.shape[0]
    assert depth == 1, "single-layer trunk"
    layer = 0
    sb_heads = (d // 2) // SB_HEAD_DIM
    da_heads = (d // 2) // DA_V_DIM
    sb_w = sb_heads * SB_HEAD_DIM
    da_col0 = 3 * sb_w
    da_qk_w = da_heads * 2 * DA_HEAD_DIM
    n_exp = w_router.shape[-1]
    lam_init = 0.8 - 0.6 * math.exp(-0.3 * layer)

    tm_proj, tn_proj = TM_PROJ, TN_PROJ
    tq, tk_sb, ts_da = TQ_ATTN, TK_SB, TS_DA
    tm_out = TM_OUT
    tm_moe, tf_moe = TM_MOE, TF_MOE
    tm_fin = TM_FINAL

    x2d = x.reshape(b * s, d)
    w_in_bf = w_in.reshape(w_in.shape[1:]).astype(BF16)
    g_mix2 = g_mix[layer].reshape(1, d)
    rope_cols = (da_col0, da_col0 + 2 * da_qk_w)

    pos_f = N_META + jnp.arange(s, dtype=jnp.int32)
    proj = _in_proj(x2d, g_mix2, w_in_bf, _rope_tables(pos_f), tm=tm_proj, tn=tn_proj,
                    rope_cols=rope_cols, pos_blocks=s // tm_proj)
    meta_pad = jnp.zeros((META_PAD, d), x.dtype).at[:N_META].set(meta_tokens.astype(x.dtype))
    pos_m = jnp.arange(META_PAD, dtype=jnp.int32)
    proj_meta = _in_proj(meta_pad, g_mix2, w_in_bf, _rope_tables(pos_m), tm=META_PAD, tn=tn_proj,
                         rope_cols=rope_cols, pos_blocks=1)
    proj3 = proj.reshape(b, s, -1)

    mix_sb = _sb_attention(proj3, proj_meta, g_sb_out[layer].reshape(1, -1), batch=b, seq=s, heads=sb_heads,
                           tq=tq, tk=tk_sb)
    lams = tuple(t[layer].reshape(1, DA_HEAD_DIM).astype(F32) for t in (lam_q1, lam_k1, lam_q2, lam_k2))
    mix_da = _da_attention(proj3, proj_meta, lams, g_da_out[layer].reshape(1, -1), batch=b, seq=s, heads=da_heads,
                           tq=tq, ts=ts_da, col0=da_col0, lam_init=lam_init)

    w_out_bf = w_out[layer].astype(BF16)
    wr = w_router[layer]
    wr_hi = wr.astype(BF16)
    wr_lo = (wr - wr_hi.astype(F32)).astype(BF16)
    h1, u_packed, top_e, gates, tile_counts = _out_proj(
        mix_sb.reshape(b * s, -1), mix_da.reshape(b * s, -1), w_out_bf[:sb_w], w_out_bf[sb_w:], x2d,
        g_ffn[layer].reshape(1, d), wr_hi, wr_lo, b_router[layer].reshape(1, n_exp), tm=tm_out)

    counts = jnp.sum(tile_counts, axis=(0, 1)).astype(jnp.int32)
    row_tok, dest_flat, blk_e, blk_cnt, blk_row = _routing(top_e, counts, tm_moe)
    xs = _sc_gather_rows(u_packed, row_tok)
    rows = _moe_ffn(blk_e, blk_cnt, blk_row, xs, w_gate_up.reshape(w_gate_up.shape[1:]),
                    w_down.reshape(w_down.shape[1:]), b_gate_up[layer], b_down[layer],
                    tm=tm_moe, ts=TS_MOE, tf=tf_moe, tn=TN_MOE)
    dest_km = dest_flat.reshape(b * s, TOP_K).T.reshape(-1)
    y_km = jnp.take(rows, dest_km, axis=0)

    out = _final(h1, y_km, gates, g_final.reshape(1, d), tm=tm_fin)
    return out.reshape(b, s, d)
```

```python
import functools
import math

import jax
import jax.numpy as jnp
from jax import lax
from jax.experimental import pallas as pl
from jax.experimental.pallas import tpu as pltpu
from jax.experimental.pallas import tpu_sc as plsc

F32 = jnp.float32
BF16 = jnp.bfloat16

CHUNK = 64
N_META = 16
RMS_EPS = 1e-5
SB_HEAD_DIM = 128
DA_HEAD_DIM = 64
DA_V_DIM = 2 * DA_HEAD_DIM
ROPE_THETA = 500000.0
ROPE_DIM = DA_HEAD_DIM // 4
TOP_K = 4
SWIGLU_LIMIT = 7.0
SWIGLU_ALPHA = 1.702

LANES = 128
META_PAD = 128
NEG_BIG = -1e30
VMEM_LIMIT = 56 * 1024 * 1024
SC_CORES, SC_SUBCORES = 2, 16
SC_GATHER_ROWS = 64

TM_PROJ, TN_PROJ = 512, 1024
TQ_ATTN = 512
TK_SB = 256
TS_DA = 256
TM_OUT = 512
TM_MOE, TS_MOE = 2048, 512
TF_MOE, TN_MOE = 256, 256
TM_FINAL = 512


def _cparams(sem):
    return pltpu.CompilerParams(dimension_semantics=sem, vmem_limit_bytes=VMEM_LIMIT)


def _in_proj_kernel(x_ref, g_ref, w_ref, c_ref, sa_ref, sb_ref, o_ref, u_scr, *, tn, rope_lo, rope_hi):
    n = pl.program_id(1)

    @pl.when(n == 0)
    def _():
        x = x_ref[...]
        ms = jnp.mean(x * x, axis=-1, keepdims=True)
        u_scr[...] = ((x * lax.rsqrt(ms + RMS_EPS)) * g_ref[...]).astype(BF16)

    acc = jnp.dot(u_scr[...], w_ref[...], preferred_element_type=F32)
    is_rope = jnp.logical_and(n >= rope_lo, n < rope_hi)

    @pl.when(is_rope)
    def _():
        for c in range(tn // LANES):
            xc = acc[:, c * LANES:(c + 1) * LANES]
            r = (xc * c_ref[...] + pltpu.roll(xc, LANES - ROPE_DIM // 2, 1) * sa_ref[...]
                 + pltpu.roll(xc, ROPE_DIM // 2, 1) * sb_ref[...])
            o_ref[:, c * LANES:(c + 1) * LANES] = r.astype(BF16)

    @pl.when(jnp.logical_not(is_rope))
    def _():
        o_ref[...] = acc.astype(BF16)


def _rope_tables(pos):
    half = ROPE_DIM // 2
    inv_freq = ROPE_THETA ** (-(jnp.arange(half, dtype=F32) * 2.0 / ROPE_DIM))
    ang = pos.astype(F32)[:, None] * inv_freq[None, :]
    cos, sin = jnp.cos(ang), jnp.sin(ang)
    p = pos.shape[0]
    ones = jnp.ones((p, DA_HEAD_DIM - ROPE_DIM), F32)
    zeros8 = jnp.zeros((p, half), F32)
    zeros48 = jnp.zeros((p, DA_HEAD_DIM - ROPE_DIM), F32)
    c64 = jnp.concatenate([cos, cos, ones], axis=1)
    sa64 = jnp.concatenate([-sin, zeros8, zeros48], axis=1)
    sb64 = jnp.concatenate([zeros8, sin, zeros48], axis=1)
    tile2 = lambda t: jnp.concatenate([t, t], axis=1)
    return tile2(c64), tile2(sa64), tile2(sb64)


def _in_proj(x2d, g, w_bf, tables, *, tm, tn, rope_cols, pos_blocks):
    m, d = x2d.shape
    n_cols = w_bf.shape[1]
    c_t, sa_t, sb_t = tables
    kern = functools.partial(_in_proj_kernel, tn=tn, rope_lo=rope_cols[0] // tn, rope_hi=rope_cols[1] // tn)
    tab_spec = pl.BlockSpec((tm, LANES), lambda i, n: (i % pos_blocks, 0))
    return pl.pallas_call(
        kern,
        grid=(m // tm, n_cols // tn),
        in_specs=[
            pl.BlockSpec((tm, d), lambda i, n: (i, 0)),
            pl.BlockSpec((1, d), lambda i, n: (0, 0)),
            pl.BlockSpec((d, tn), lambda i, n: (0, n)),
            tab_spec, tab_spec, tab_spec,
        ],
        out_specs=pl.BlockSpec((tm, tn), lambda i, n: (i, n)),
        out_shape=jax.ShapeDtypeStruct((m, n_cols), BF16),
        scratch_shapes=[pltpu.VMEM((tm, d), BF16)],
        compiler_params=_cparams(("parallel", "arbitrary")),
        name="in_proj",
    )(x2d, g, w_bf, c_t, sa_t, sb_t)


def _dot_nt(a, b):
    return lax.dot_general(a, b, (((1,), (1,)), ((), ())), preferred_element_type=F32)


def _suffix_sum_matrix(n):
    j = lax.broadcasted_iota(jnp.int32, (2 * n, n), 0)
    s = lax.broadcasted_iota(jnp.int32, (2 * n, n), 1)
    return jnp.where(jnp.where(j >= n, j - n, j) > s, 1.0, 0.0).astype(BF16)


def _sb_block(q, kb, vb, u2, run, acc, mask, scale):
    z = _dot_nt(q, kb) * scale
    sp = jnp.maximum(z, 0.0) + jnp.log(1.0 + jnp.exp(-jnp.abs(z)))
    if mask is not None:
        sp = jnp.where(mask, sp, 0.0)
    hi = sp.astype(BF16)
    lo = (sp - hi.astype(F32)).astype(BF16)
    cs = jnp.dot(jnp.concatenate([hi, lo], axis=1), u2, preferred_element_type=F32)
    w = jnp.exp(z - sp - cs - run)
    if mask is not None:
        w = jnp.where(mask, w, 0.0)
    acc = acc + jnp.dot(w.astype(BF16), vb, preferred_element_type=F32)
    run = run + (cs[:, :1] + sp[:, :1])
    return run, acc


def _sb_kernel(q_ref, k_ref, v_ref, km_ref, vm_ref, g_ref, o_ref, *, tq, tk, scale):
    i = pl.program_id(2)
    nsub = tq // tk
    u_blk = _suffix_sum_matrix(tk)
    row = lax.broadcasted_iota(jnp.int32, (tk, tk), 0)
    col = lax.broadcasted_iota(jnp.int32, (tk, tk), 1)
    diag_mask = col < row

    def kv(blk):
        s0 = pl.multiple_of(blk * tk, tk)
        return k_ref[pl.ds(s0, tk), :], v_ref[pl.ds(s0, tk), :]

    runs, accs = [], []
    for h in range(nsub):
        qh = q_ref[h * tk:(h + 1) * tk, :]
        run = jnp.zeros((tk, 1), F32)
        acc = jnp.zeros((tk, SB_HEAD_DIM), F32)
        for c in range(h, -1, -1):
            kb, vb = kv(i * nsub + c)
            run, acc = _sb_block(qh, kb, vb, u_blk, run, acc, diag_mask if c == h else None, scale)
        runs.append(run)
        accs.append(acc)
    run = jnp.concatenate(runs, axis=0)
    acc = jnp.concatenate(accs, axis=0)
    q = q_ref[...]

    def body(jj, carry):
        r, a = carry
        for c in range(nsub - 1, -1, -1):
            kb, vb = kv((i - 1 - jj) * nsub + c)
            r, a = _sb_block(q, kb, vb, u_blk, r, a, None, scale)
        return r, a

    run, acc = lax.fori_loop(0, i, body, (run, acc))

    mcol = lax.broadcasted_iota(jnp.int32, (tq, META_PAD), 1)
    run, acc = _sb_block(q, km_ref[...], vm_ref[...], _suffix_sum_matrix(META_PAD), run, acc,
                         mcol < N_META, scale)

    ms = jnp.mean(acc * acc, axis=-1, keepdims=True)
    o_ref[...] = ((acc * lax.rsqrt(ms + RMS_EPS)) * g_ref[...]).astype(BF16)


def _sb_attention(proj, proj_meta, g_sb, *, batch, seq, heads, tq, tk):
    hd = SB_HEAD_DIM
    kern = functools.partial(_sb_kernel, tq=tq, tk=tk, scale=hd ** -0.5)
    return pl.pallas_call(
        kern,
        grid=(batch, heads, seq // tq),
        in_specs=[
            pl.BlockSpec((None, tq, hd), lambda b, h, i: (b, i, h)),
            pl.BlockSpec((None, seq, hd), lambda b, h, i: (b, 0, heads + h)),
            pl.BlockSpec((None, seq, hd), lambda b, h, i: (b, 0, 2 * heads + h)),
            pl.BlockSpec((META_PAD, hd), lambda b, h, i: (0, heads + h)),
            pl.BlockSpec((META_PAD, hd), lambda b, h, i: (0, 2 * heads + h)),
            pl.BlockSpec((1, hd), lambda b, h, i: (0, h)),
        ],
        out_specs=pl.BlockSpec((None, tq, hd), lambda b, h, i: (b, i, h)),
        out_shape=jax.ShapeDtypeStruct((batch, seq, heads * hd), BF16),
        compiler_params=_cparams(("parallel", "parallel", "arbitrary")),
        name="sb_attn",
    )(proj, proj, proj, proj_meta, proj_meta, g_sb)


def _with_ones(vb):
    return jnp.concatenate([vb, jnp.ones_like(vb)], axis=1)


def _da_block(q1, q2, kb, vb1, st, mask):
    m1, a1, m2, a2 = st

    def one(qc, m, a):
        s = _dot_nt(qc, kb)
        if mask is not None:
            s = jnp.where(mask, s, NEG_BIG)
        mn = jnp.maximum(m, jnp.max(s, axis=-1, keepdims=True))
        p = jnp.exp(s - mn)
        a = jnp.exp(m - mn) * a + jnp.dot(p.astype(BF16), vb1, preferred_element_type=F32)
        return mn, a

    m1, a1 = one(q1, m1, a1)
    m2, a2 = one(q2, m2, a2)
    return m1, a1, m2, a2


def _da_kernel(lq1_ref, lk1_ref, lq2_ref, lk2_ref, q_ref, k_ref, v_ref, km_ref, vm_ref, g_ref, o_ref,
               *, tq, ts, scale, lam_init):
    i = pl.program_id(2)
    nsub = tq // ts
    lam = (jnp.exp(jnp.sum(lq1_ref[...] * lk1_ref[...], axis=-1, keepdims=True))
           - jnp.exp(jnp.sum(lq2_ref[...] * lk2_ref[...], axis=-1, keepdims=True)) + lam_init)

    q = q_ref[...]
    lane = lax.broadcasted_iota(jnp.int32, q.shape, 1)
    qs = q * jnp.asarray(scale, BF16)
    zero = jnp.zeros_like(qs)
    q1 = jnp.where(lane < DA_HEAD_DIM, qs, zero)
    q2 = jnp.where(lane >= DA_HEAD_DIM, qs, zero)

    neg = jnp.full((tq, 1), NEG_BIG, F32)
    za = jnp.zeros((tq, 2 * DA_V_DIM), F32)
    st = (neg, za, neg, za)

    def body(j, carry):
        s0 = pl.multiple_of(j * tq, tq)
        return _da_block(q1, q2, k_ref[pl.ds(s0, tq), :], _with_ones(v_ref[pl.ds(s0, tq), :]), carry, None)

    st = lax.fori_loop(0, i, body, st)

    shift = CHUNK.bit_length() - 1
    s0 = pl.multiple_of(i * tq, tq)
    outs = []
    for h in range(nsub):
        rows = slice(h * ts, (h + 1) * ts)
        width = (h + 1) * ts
        kb = jnp.concatenate([km_ref[...], k_ref[pl.ds(s0, width), :]], axis=0)
        vb = jnp.concatenate([vm_ref[...], v_ref[pl.ds(s0, width), :]], axis=0)
        col = lax.broadcasted_iota(jnp.int32, (ts, META_PAD + width), 1)
        row = lax.broadcasted_iota(jnp.int32, (ts, META_PAD + width), 0) + h * ts
        frame_vis = jnp.logical_and(col >= META_PAD,
                                    jnp.right_shift(col - META_PAD, shift) <= jnp.right_shift(row, shift))
        vis = jnp.logical_or(col < N_META, frame_vis)
        sth = _da_block(q1[rows], q2[rows], kb, _with_ones(vb), tuple(t[rows] for t in st), vis)
        _, a1, _, a2 = sth
        outs.append(a1[:, :DA_V_DIM] / a1[:, DA_V_DIM:] - lam * (a2[:, :DA_V_DIM] / a2[:, DA_V_DIM:]))
    o = jnp.concatenate(outs, axis=0)
    ms = jnp.mean(o * o, axis=-1, keepdims=True)
    o_ref[...] = (((o * lax.rsqrt(ms + RMS_EPS)) * g_ref[...]) * (1.0 - lam_init)).astype(BF16)


def _da_attention(proj, proj_meta, lams, g_da, *, batch, seq, heads, tq, ts, col0, lam_init):
    hd = DA_V_DIM
    qb, kb, vb = col0 // hd, col0 // hd + heads, col0 // hd + 2 * heads
    kern = functools.partial(_da_kernel, tq=tq, ts=ts, scale=DA_HEAD_DIM ** -0.5, lam_init=lam_init)
    lam_spec = pl.BlockSpec((1, DA_HEAD_DIM), lambda b, h, i: (0, 0))
    return pl.pallas_call(
        kern,
        grid=(batch, heads, seq // tq),
        in_specs=[
            lam_spec, lam_spec, lam_spec, lam_spec,
            pl.BlockSpec((None, tq, hd), lambda b, h, i: (b, i, qb + h)),
            pl.BlockSpec((None, seq, hd), lambda b, h, i: (b, 0, kb + h)),
            pl.BlockSpec((None, seq, hd), lambda b, h, i: (b, 0, vb + h)),
            pl.BlockSpec((META_PAD, hd), lambda b, h, i: (0, kb + h)),
            pl.BlockSpec((META_PAD, hd), lambda b, h, i: (0, vb + h)),
            pl.BlockSpec((1, hd), lambda b, h, i: (0, h)),
        ],
        out_specs=pl.BlockSpec((None, tq, hd), lambda b, h, i: (b, i, h)),
        out_shape=jax.ShapeDtypeStruct((batch, seq, heads * hd), BF16),
        compiler_params=_cparams(("parallel", "parallel", "arbitrary")),
        name="da_attn",
    )(*lams, proj, proj, proj, proj_meta, proj_meta, g_da)


def _pack_bf16_pairs(x):
    n = x.shape[1] // 2
    lo = lax.bitcast_convert_type(x[:, :n].astype(BF16).astype(F32), jnp.int32)
    hi = lax.bitcast_convert_type(x[:, n:].astype(BF16).astype(F32), jnp.int32)
    return jnp.bitwise_or(lax.shift_right_logical(lo, 16), jnp.bitwise_and(hi, jnp.int32(-65536)))


def _unpack_bf16_pairs(w):
    lo = lax.bitcast_convert_type(lax.shift_left(w, 16), F32)
    hi = lax.bitcast_convert_type(jnp.bitwise_and(w, jnp.int32(-65536)), F32)
    return jnp.concatenate([lo.astype(BF16), hi.astype(BF16)], axis=1)


def _sc_gather_rows(table, idx):
    n_idx = idx.shape[0]
    width = table.shape[1]
    n_workers = SC_CORES * SC_SUBCORES
    per_worker = n_idx // n_workers
    win = SC_GATHER_ROWS
    assert n_idx % (n_workers * win) == 0
    mesh = plsc.VectorSubcoreMesh(core_axis_name="c", subcore_axis_name="s")

    @functools.partial(
        pl.kernel, mesh=mesh,
        out_type=jax.ShapeDtypeStruct((n_idx, width), table.dtype),
        scratch_types=[pltpu.VMEM((win,), jnp.int32), pltpu.VMEM((win, width), table.dtype),
                       pltpu.SemaphoreType.DMA],
    )
    def gather(table_hbm, idx_hbm, out_hbm, idx_v, rows_v, sem):
        worker = lax.axis_index("s") * SC_CORES + lax.axis_index("c")
        base = worker * per_worker

        @pl.loop(0, per_worker // win)
        def _(t):
            off = pl.multiple_of(base + t * win, win)
            pltpu.sync_copy(idx_hbm.at[pl.ds(off, win)], idx_v)
            pltpu.async_copy(table_hbm.at[idx_v], rows_v, sem).wait()
            pltpu.sync_copy(rows_v, out_hbm.at[pl.ds(off, win)])

    return gather(table, idx)


def _split3_dot(a, b_hi, b_lo):
    a_hi = a.astype(BF16)
    a_lo = (a - a_hi.astype(F32)).astype(BF16)
    return (jnp.dot(a_hi, b_hi, preferred_element_type=F32)
            + jnp.dot(a_hi, b_lo, preferred_element_type=F32)
            + jnp.dot(a_lo, b_hi, preferred_element_type=F32))


def _out_proj_kernel(ms_ref, md_ref, ws_ref, wd_ref, x_ref, g_ref, wr_hi_ref, wr_lo_ref, br_ref,
                     h_ref, u_ref, e_ref, p_ref, c_ref, *, n_exp):
    h = (x_ref[...]
         + jnp.dot(ms_ref[...], ws_ref[...], preferred_element_type=F32)
         + jnp.dot(md_ref[...], wd_ref[...], preferred_element_type=F32))
    h_ref[...] = h
    msq = jnp.mean(h * h, axis=-1, keepdims=True)
    u = (h * lax.rsqrt(msq + RMS_EPS)) * g_ref[...]
    u_ref[...] = _pack_bf16_pairs(u)

    logits = _split3_dot(u, wr_hi_ref[...], wr_lo_ref[...]) + br_ref[...]
    lane = lax.broadcasted_iota(jnp.int32, logits.shape, 1).astype(F32)
    work = logits
    tops, idxs = [], []
    for _ in range(TOP_K):
        mx = jnp.max(work, axis=-1, keepdims=True)
        ix = jnp.min(jnp.where(work == mx, lane, float(n_exp)), axis=-1, keepdims=True)
        tops.append(mx)
        idxs.append(ix)
        work = jnp.where(lane == ix, -jnp.inf, work)
    ex = [jnp.exp(t - tops[0]) for t in tops]
    den = ex[0] + ex[1] + ex[2] + ex[3]
    kl = lax.broadcasted_iota(jnp.int32, (logits.shape[0], TOP_K), 1)
    e_out = jnp.zeros((logits.shape[0], TOP_K), F32)
    p_out = jnp.zeros((logits.shape[0], TOP_K), F32)
    for k in range(TOP_K):
        e_out = jnp.where(kl == k, idxs[k], e_out)
        p_out = jnp.where(kl == k, ex[k] / den, p_out)
    e_ref[...] = e_out.astype(jnp.int32)
    p_ref[...] = p_out
    hits = jnp.zeros(logits.shape, F32)
    for k in range(TOP_K):
        hits = hits + jnp.where(lane == idxs[k], 1.0, 0.0)
    c_ref[...] = jnp.sum(hits, axis=0, keepdims=True)


def _out_proj(mix_sb, mix_da, w_sb, w_da, x2d, g_ffn, wr_hi, wr_lo, b_r, *, tm):
    m, d = x2d.shape
    ks, kd = mix_sb.shape[1], mix_da.shape[1]
    n_exp = wr_hi.shape[1]
    kern = functools.partial(_out_proj_kernel, n_exp=n_exp)
    const = lambda shape: pl.BlockSpec(shape, lambda i: (0, 0))
    return pl.pallas_call(
        kern,
        grid=(m // tm,),
        in_specs=[
            pl.BlockSpec((tm, ks), lambda i: (i, 0)),
            pl.BlockSpec((tm, kd), lambda i: (i, 0)),
            const((ks, d)), const((kd, d)),
            pl.BlockSpec((tm, d), lambda i: (i, 0)),
            const((1, d)), const((d, n_exp)), const((d, n_exp)), const((1, n_exp)),
        ],
        out_specs=[
            pl.BlockSpec((tm, d), lambda i: (i, 0)),
            pl.BlockSpec((tm, d // 2), lambda i: (i, 0)),
            pl.BlockSpec((tm, TOP_K), lambda i: (i, 0)),
            pl.BlockSpec((tm, TOP_K), lambda i: (i, 0)),
            pl.BlockSpec((None, 1, n_exp), lambda i: (i, 0, 0)),
        ],
        out_shape=[
            jax.ShapeDtypeStruct((m, d), F32),
            jax.ShapeDtypeStruct((m, d // 2), jnp.int32),
            jax.ShapeDtypeStruct((m, TOP_K), jnp.int32),
            jax.ShapeDtypeStruct((m, TOP_K), F32),
            jax.ShapeDtypeStruct((m // tm, 1, n_exp), F32),
        ],
        compiler_params=_cparams(("parallel",)),
        name="out_proj_router",
    )(mix_sb, mix_da, w_sb, w_da, x2d, g_ffn, wr_hi, wr_lo, b_r)


def _moe_kernel(be_ref, bc_ref, rb_ref, x_ref, wg_ref, wu_ref, wda_ref, wdb_ref, bg_ref, bup_ref, bda_ref, bdb_ref,
                o_ref, act_ref, *, nf, ts):
    blk = pl.program_id(0)
    s = pl.program_id(1)
    cnt = bc_ref[blk]
    nsub = x_ref.shape[0] // ts

    @pl.when(jnp.logical_and(cnt > 0, s < nf))
    def _():
        wg = wg_ref[...].astype(BF16)
        wu = wu_ref[...].astype(BF16)
        slot = jnp.minimum(s, nf - 1)
        for j in range(nsub):
            @pl.when(cnt > j * ts)
            def _():
                x = _unpack_bf16_pairs(x_ref[j * ts:(j + 1) * ts, :])
                g = jnp.dot(x, wg, preferred_element_type=F32) + bg_ref[...]
                u = jnp.dot(x, wu, preferred_element_type=F32) + bup_ref[...]
                gate = jnp.minimum(g, SWIGLU_LIMIT)
                up = jnp.clip(u, -SWIGLU_LIMIT, SWIGLU_LIMIT)
                act = (up + 1.0) * gate * jax.nn.sigmoid(SWIGLU_ALPHA * gate)
                act_ref[slot, j * ts:(j + 1) * ts, :] = act.astype(BF16)

    @pl.when(jnp.logical_and(cnt > 0, s >= nf))
    def _():
        tf = act_ref.shape[2]
        wda = wda_ref[...].astype(BF16)
        wdb = wdb_ref[...].astype(BF16)
        for j in range(nsub):
            @pl.when(cnt > j * ts)
            def _():
                ya = bda_ref[...]
                yb = bdb_ref[...]
                for f in range(nf):
                    a = act_ref[f, j * ts:(j + 1) * ts, :]
                    ya = ya + jnp.dot(a, wda[f * tf:(f + 1) * tf, :], preferred_element_type=F32)
                    yb = yb + jnp.dot(a, wdb[f * tf:(f + 1) * tf, :], preferred_element_type=F32)
                o_ref[j * ts:(j + 1) * ts, :] = _pack_bf16_pairs(jnp.concatenate([ya, yb], axis=1))

            @pl.when(cnt <= j * ts)
            def _():
                o_ref[j * ts:(j + 1) * ts, :] = jnp.zeros((ts, o_ref.shape[1]), o_ref.dtype)


def _moe_ffn(blk_e, blk_cnt, blk_row, xs, w_gu, w_dn, b_gu, b_dn, *, tm, ts, tf, tn):
    n_rows = xs.shape[0]
    n_exp, d, f2 = w_gu.shape
    d_ff = f2 // 2
    nf = d_ff // tf
    nn = d // tn
    n_blk = n_rows // tm
    b_gu3 = b_gu.reshape(n_exp, 1, f2)
    b_dn3 = b_dn.reshape(n_exp, 1, d)

    def fi(s, bc, b):
        return jnp.where(bc[b] > 0, jnp.minimum(s, nf - 1), nf - 1)

    nh = nn // 2

    def ni(s, bc, b):
        return jnp.where(bc[b] > 0, jnp.clip(s - nf, 0, nh - 1), nh - 1)

    grid_spec = pltpu.PrefetchScalarGridSpec(
        num_scalar_prefetch=3,
        grid=(n_blk, nf + nh),
        in_specs=[
            pl.BlockSpec((tm, d // 2), lambda b, s, be, bc, rb: (rb[b], 0)),
            pl.BlockSpec((None, d, tf), lambda b, s, be, bc, rb: (be[b], 0, fi(s, bc, b))),
            pl.BlockSpec((None, d, tf), lambda b, s, be, bc, rb: (be[b], 0, nf + fi(s, bc, b))),
            pl.BlockSpec((None, d_ff, tn), lambda b, s, be, bc, rb: (be[b], 0, ni(s, bc, b))),
            pl.BlockSpec((None, d_ff, tn), lambda b, s, be, bc, rb: (be[b], 0, nh + ni(s, bc, b))),
            pl.BlockSpec((None, 1, tf), lambda b, s, be, bc, rb: (be[b], 0, fi(s, bc, b))),
            pl.BlockSpec((None, 1, tf), lambda b, s, be, bc, rb: (be[b], 0, nf + fi(s, bc, b))),
            pl.BlockSpec((None, 1, tn), lambda b, s, be, bc, rb: (be[b], 0, ni(s, bc, b))),
            pl.BlockSpec((None, 1, tn), lambda b, s, be, bc, rb: (be[b], 0, nh + ni(s, bc, b))),
        ],
        out_specs=pl.BlockSpec((tm, tn), lambda b, s, be, bc, rb: (rb[b], ni(s, bc, b))),
        scratch_shapes=[pltpu.VMEM((nf, tm, tf), BF16)],
    )
    return pl.pallas_call(
        functools.partial(_moe_kernel, nf=nf, ts=ts),
        grid_spec=grid_spec,
        out_shape=jax.ShapeDtypeStruct((n_rows, d // 2), jnp.int32),
        compiler_params=_cparams(("arbitrary", "arbitrary")),
        name="moe_ffn",
    )(blk_e, blk_cnt, blk_row, xs, w_gu, w_gu, w_dn, w_dn, b_gu3, b_gu3, b_dn3, b_dn3)


def _final_kernel(h_ref, y0_ref, y1_ref, y2_ref, y3_ref, p_ref, g_ref, o_ref):
    p = p_ref[...]
    y = None
    for k, y_ref in enumerate((y0_ref, y1_ref, y2_ref, y3_ref)):
        t = _unpack_bf16_pairs(y_ref[...]).astype(F32) * p[:, k:k + 1]
        y = t if y is None else y + t
    h = h_ref[...] + y
    ms = jnp.mean(h * h, axis=-1, keepdims=True)
    o_ref[...] = (h * lax.rsqrt(ms + RMS_EPS)) * g_ref[...]


def _final(h1, y_km, gates, g, *, tm):
    m, d = h1.shape
    nb = m // tm
    assert TOP_K == 4
    y_specs = [pl.BlockSpec((tm, d // 2), functools.partial(lambda i, k: (k * nb + i, 0), k=k))
               for k in range(TOP_K)]
    return pl.pallas_call(
        _final_kernel,
        grid=(nb,),
        in_specs=([pl.BlockSpec((tm, d), lambda i: (i, 0))] + y_specs
                  + [pl.BlockSpec((tm, TOP_K), lambda i: (i, 0)), pl.BlockSpec((1, d), lambda i: (0, 0))]),
        out_specs=pl.BlockSpec((tm, d), lambda i: (i, 0)),
        out_shape=jax.ShapeDtypeStruct((m, d), F32),
        compiler_params=_cparams(("parallel",)),
        name="final_norm",
    )(h1, y_km, y_km, y_km, y_km, gates, g)


def _routing(top_e, counts, tm):
    n_exp = counts.shape[0]
    n_assign = top_e.shape[0] * TOP_K
    flat_e = top_e.reshape(-1)
    order = jnp.argsort(flat_e, stable=True).astype(jnp.int32)
    rank = jnp.argsort(order).astype(jnp.int32)
    padded = (counts + tm - 1) // tm * tm
    pad_end = jnp.cumsum(padded).astype(jnp.int32)
    pad_start = pad_end - padded
    grp_start = jnp.cumsum(counts).astype(jnp.int32) - counts
    n_blk = -(-n_assign // tm) + n_exp
    blk_start = jnp.arange(n_blk, dtype=jnp.int32) * tm
    blk_e = jnp.minimum(jnp.sum((blk_start[:, None] >= pad_end[None, :]).astype(jnp.int32), axis=1), n_exp - 1)
    blk_off = blk_start - pad_start[blk_e]
    blk_cnt = jnp.clip(counts[blk_e] - blk_off, 0, tm).astype(jnp.int32)
    tok_sorted = jnp.concatenate([order // TOP_K, jnp.zeros((tm,), jnp.int32)])
    win_start = jnp.clip(grp_start[blk_e] + blk_off, 0, n_assign)
    row_tok = jax.vmap(lambda st: lax.dynamic_slice(tok_sorted, (st,), (tm,)))(win_start).reshape(-1)
    delta = pad_start - grp_start
    e2d = flat_e.reshape(-1, LANES)
    hit = e2d[None] == jnp.arange(n_exp, dtype=jnp.int32)[:, None, None]
    dest_flat = rank + jnp.sum(jnp.where(hit, delta[:, None, None], 0), axis=0).reshape(-1)
    last_used = jnp.maximum(pad_end[-1] // tm - 1, 0)
    blk_e = jnp.where(blk_cnt > 0, blk_e, blk_e[last_used]).astype(jnp.int32)
    blk_row = jnp.where(blk_cnt > 0, jnp.arange(n_blk, dtype=jnp.int32), last_used).astype(jnp.int32)
    return row_tok, dest_flat, blk_e, blk_cnt, blk_row


def kernel(x, meta_tokens, g_mix, w_in, lam_q1, lam_k1, lam_q2, lam_k2, g_sb_out, g_da_out, w_out, g_ffn,
           w_router, b_router, w_gate_up, b_gate_up, w_down, b_down, g_final):
    b, s, d = x.shape
    depth = w_in.shape[0]
    assert depth == 1, "single-layer trunk"
    layer = 0
    sb_heads = (d // 2) // SB_HEAD_DIM
    da_heads = (d // 2) // DA_V_DIM
    sb_w = sb_heads * SB_HEAD_DIM
    da_col0 = 3 * sb_w
    da_qk_w = da_heads * 2 * DA_HEAD_DIM
    n_exp = w_router.shape[-1]
    lam_init = 0.8 - 0.6 * math.exp(-0.3 * layer)

    tm_proj, tn_proj = TM_PROJ, TN_PROJ
    tq, tk_sb, ts_da = TQ_ATTN, TK_SB, TS_DA
    tm_out = TM_OUT
    tm_moe, tf_moe = TM_MOE, TF_MOE
    tm_fin = TM_FINAL

    x2d = x.reshape(b * s, d)
    w_in_bf = w_in.reshape(w_in.shape[1:]).astype(BF16)
    g_mix2 = g_mix[layer].reshape(1, d)
    rope_cols = (da_col0, da_col0 + 2 * da_qk_w)

    pos_f = N_META + jnp.arange(s, dtype=jnp.int32)
    proj = _in_proj(x2d, g_mix2, w_in_bf, _rope_tables(pos_f), tm=tm_proj, tn=tn_proj,
                    rope_cols=rope_cols, pos_blocks=s // tm_proj)
    meta_pad = jnp.zeros((META_PAD, d), x.dtype).at[:N_META].set(meta_tokens.astype(x.dtype))
    pos_m = jnp.arange(META_PAD, dtype=jnp.int32)
    proj_meta = _in_proj(meta_pad, g_mix2, w_in_bf, _rope_tables(pos_m), tm=META_PAD, tn=tn_proj,
                         rope_cols=rope_cols, pos_blocks=1)
    proj3 = proj.reshape(b, s, -1)

    mix_sb = _sb_attention(proj3, proj_meta, g_sb_out[layer].reshape(1, -1), batch=b, seq=s, heads=sb_heads,
                           tq=tq, tk=tk_sb)
    lams = tuple(t[layer].reshape(1, DA_HEAD_DIM).astype(F32) for t in (lam_q1, lam_k1, lam_q2, lam_k2))
    mix_da = _da_attention(proj3, proj_meta, lams, g_da_out[layer].reshape(1, -1), batch=b, seq=s, heads=da_heads,
                           tq=tq, ts=ts_da, col0=da_col0, lam_init=lam_init)

    w_out_bf = w_out[layer].astype(BF16)
    wr = w_router[layer]
    wr_hi = wr.astype(BF16)
    wr_lo = (wr - wr_hi.astype(F32)).astype(BF16)
    h1, u_packed, top_e, gates, tile_counts = _out_proj(
        mix_sb.reshape(b * s, -1), mix_da.reshape(b * s, -1), w_out_bf[:sb_w], w_out_bf[sb_w:], x2d,
        g_ffn[layer].reshape(1, d), wr_hi, wr_lo, b_router[layer].reshape(1, n_exp), tm=tm_out)

    counts = jnp.sum(tile_counts, axis=(0, 1)).astype(jnp.int32)
    row_tok, dest_flat, blk_e, blk_cnt, blk_row = _routing(top_e, counts, tm_moe)
    xs = _sc_gather_rows(u_packed, row_tok)
    rows = _moe_ffn(blk_e, blk_cnt, blk_row, xs, w_gate_up.reshape(w_gate_up.shape[1:]),
                    w_down.reshape(w_down.shape[1:]), b_gate_up[layer], b_down[layer],
                    tm=tm_moe, ts=TS_MOE, tf=tf_moe, tn=TN_MOE)
    dest_km = dest_flat.reshape(b * s, TOP_K).T.reshape(-1)
    y_km = _sc_gather_rows(rows, dest_km)

    out = _final(h1, y_km, gates, g_final.reshape(1, d), tm=tm_fin)
    return out.reshape(b, s, d)
```

```python
import functools
import math

import jax
import jax.numpy as jnp
from jax import lax
from jax.experimental import pallas as pl
from jax.experimental.pallas import tpu as pltpu
from jax.experimental.pallas import tpu_sc as plsc

F32 = jnp.float32
BF16 = jnp.bfloat16

CHUNK = 64
N_META = 16
RMS_EPS = 1e-5
SB_HEAD_DIM = 128
DA_HEAD_DIM = 64
DA_V_DIM = 2 * DA_HEAD_DIM
ROPE_THETA = 500000.0
ROPE_DIM = DA_HEAD_DIM // 4
TOP_K = 4
SWIGLU_LIMIT = 7.0
SWIGLU_ALPHA = 1.702

LANES = 128
META_PAD = 128
NEG_BIG = -1e30
VMEM_LIMIT = 56 * 1024 * 1024
SC_CORES, SC_SUBCORES = 2, 16
SC_GATHER_ROWS = 64

TM_PROJ, TN_PROJ = 1024, 1024
TQ_ATTN = 512
TQ_DA, TKV_DA = 1024, 512
TK_SB = 256
TS_DA = 256
TM_OUT = 512
TM_MOE, TS_MOE = 2048, 512
TF_MOE, TN_MOE = 256, 256
TM_FINAL = 512


def _cparams(sem):
    return pltpu.CompilerParams(dimension_semantics=sem, vmem_limit_bytes=VMEM_LIMIT)


def _in_proj_kernel(x_ref, g_ref, w_ref, c_ref, sa_ref, sb_ref, o_ref, u_scr, *, tn, rope_lo, rope_hi):
    n = pl.program_id(1)

    @pl.when(n == 0)
    def _():
        x = x_ref[...]
        ms = jnp.mean(x * x, axis=-1, keepdims=True)
        u_scr[...] = ((x * lax.rsqrt(ms + RMS_EPS)) * g_ref[...]).astype(BF16)

    acc = jnp.dot(u_scr[...], w_ref[...], preferred_element_type=F32)
    is_rope = jnp.logical_and(n >= rope_lo, n < rope_hi)

    @pl.when(is_rope)
    def _():
        for c in range(tn // LANES):
            xc = acc[:, c * LANES:(c + 1) * LANES]
            r = (xc * c_ref[...] + pltpu.roll(xc, LANES - ROPE_DIM // 2, 1) * sa_ref[...]
                 + pltpu.roll(xc, ROPE_DIM // 2, 1) * sb_ref[...])
            o_ref[:, c * LANES:(c + 1) * LANES] = r.astype(BF16)

    @pl.when(jnp.logical_not(is_rope))
    def _():
        o_ref[...] = acc.astype(BF16)


def _rope_tables(pos):
    half = ROPE_DIM // 2
    inv_freq = ROPE_THETA ** (-(jnp.arange(half, dtype=F32) * 2.0 / ROPE_DIM))
    ang = pos.astype(F32)[:, None] * inv_freq[None, :]
    cos, sin = jnp.cos(ang), jnp.sin(ang)
    p = pos.shape[0]
    ones = jnp.ones((p, DA_HEAD_DIM - ROPE_DIM), F32)
    zeros8 = jnp.zeros((p, half), F32)
    zeros48 = jnp.zeros((p, DA_HEAD_DIM - ROPE_DIM), F32)
    c64 = jnp.concatenate([cos, cos, ones], axis=1)
    sa64 = jnp.concatenate([-sin, zeros8, zeros48], axis=1)
    sb64 = jnp.concatenate([zeros8, sin, zeros48], axis=1)
    tile2 = lambda t: jnp.concatenate([t, t], axis=1)
    return tile2(c64), tile2(sa64), tile2(sb64)


def _in_proj(x2d, g, w_bf, tables, *, tm, tn, rope_cols, pos_blocks):
    m, d = x2d.shape
    n_cols = w_bf.shape[1]
    c_t, sa_t, sb_t = tables
    kern = functools.partial(_in_proj_kernel, tn=tn, rope_lo=rope_cols[0] // tn, rope_hi=rope_cols[1] // tn)
    tab_spec = pl.BlockSpec((tm, LANES), lambda i, n: (i % pos_blocks, 0))
    return pl.pallas_call(
        kern,
        grid=(m // tm, n_cols // tn),
        in_specs=[
            pl.BlockSpec((tm, d), lambda i, n: (i, 0)),
            pl.BlockSpec((1, d), lambda i, n: (0, 0)),
            pl.BlockSpec((d, tn), lambda i, n: (0, n)),
            tab_spec, tab_spec, tab_spec,
        ],
        out_specs=pl.BlockSpec((tm, tn), lambda i, n: (i, n)),
        out_shape=jax.ShapeDtypeStruct((m, n_cols), BF16),
        scratch_shapes=[pltpu.VMEM((tm, d), BF16)],
        compiler_params=_cparams(("parallel", "arbitrary")),
        name="in_proj",
    )(x2d, g, w_bf, c_t, sa_t, sb_t)


def _dot_nt(a, b):
    return lax.dot_general(a, b, (((1,), (1,)), ((), ())), preferred_element_type=F32)


def _suffix_sum_matrix(n):
    j = lax.broadcasted_iota(jnp.int32, (2 * n, n), 0)
    s = lax.broadcasted_iota(jnp.int32, (2 * n, n), 1)
    return jnp.where(jnp.where(j >= n, j - n, j) > s, 1.0, 0.0).astype(BF16)


def _sb_block(q, kb, vb, u2, run, acc, mask, scale):
    z = _dot_nt(q, kb) * scale
    sp = jnp.maximum(z, 0.0) + jnp.log(1.0 + jnp.exp(-jnp.abs(z)))
    if mask is not None:
        sp = jnp.where(mask, sp, 0.0)
    hi = sp.astype(BF16)
    lo = (sp - hi.astype(F32)).astype(BF16)
    cs = jnp.dot(jnp.concatenate([hi, lo], axis=1), u2, preferred_element_type=F32)
    w = jnp.exp(z - sp - cs - run)
    if mask is not None:
        w = jnp.where(mask, w, 0.0)
    acc = acc + jnp.dot(w.astype(BF16), vb, preferred_element_type=F32)
    run = run + (cs[:, :1] + sp[:, :1])
    return run, acc


def _sb_kernel(q_ref, k_ref, v_ref, km_ref, vm_ref, g_ref, o_ref, *, tq, tk, scale):
    i = pl.program_id(2)
    nsub = tq // tk
    u_blk = _suffix_sum_matrix(tk)
    row = lax.broadcasted_iota(jnp.int32, (tk, tk), 0)
    col = lax.broadcasted_iota(jnp.int32, (tk, tk), 1)
    diag_mask = col < row

    def kv(blk):
        s0 = pl.multiple_of(blk * tk, tk)
        return k_ref[pl.ds(s0, tk), :], v_ref[pl.ds(s0, tk), :]

    runs, accs = [], []
    for h in range(nsub):
        qh = q_ref[h * tk:(h + 1) * tk, :]
        run = jnp.zeros((tk, 1), F32)
        acc = jnp.zeros((tk, SB_HEAD_DIM), F32)
        for c in range(h, -1, -1):
            kb, vb = kv(i * nsub + c)
            run, acc = _sb_block(qh, kb, vb, u_blk, run, acc, diag_mask if c == h else None, scale)
        runs.append(run)
        accs.append(acc)
    run = jnp.concatenate(runs, axis=0)
    acc = jnp.concatenate(accs, axis=0)
    q = q_ref[...]

    def body(jj, carry):
        r, a = carry
        for c in range(nsub - 1, -1, -1):
            kb, vb = kv((i - 1 - jj) * nsub + c)
            r, a = _sb_block(q, kb, vb, u_blk, r, a, None, scale)
        return r, a

    run, acc = lax.fori_loop(0, i, body, (run, acc))

    mcol = lax.broadcasted_iota(jnp.int32, (tq, META_PAD), 1)
    run, acc = _sb_block(q, km_ref[...], vm_ref[...], _suffix_sum_matrix(META_PAD), run, acc,
                         mcol < N_META, scale)

    ms = jnp.mean(acc * acc, axis=-1, keepdims=True)
    o_ref[...] = ((acc * lax.rsqrt(ms + RMS_EPS)) * g_ref[...]).astype(BF16)


def _sb_attention(proj, proj_meta, g_sb, *, batch, seq, heads, tq, tk):
    hd = SB_HEAD_DIM
    kern = functools.partial(_sb_kernel, tq=tq, tk=tk, scale=hd ** -0.5)
    return pl.pallas_call(
        kern,
        grid=(batch, heads, seq // tq),
        in_specs=[
            pl.BlockSpec((None, tq, hd), lambda b, h, i: (b, i, h)),
            pl.BlockSpec((None, seq, hd), lambda b, h, i: (b, 0, heads + h)),
            pl.BlockSpec((None, seq, hd), lambda b, h, i: (b, 0, 2 * heads + h)),
            pl.BlockSpec((META_PAD, hd), lambda b, h, i: (0, heads + h)),
            pl.BlockSpec((META_PAD, hd), lambda b, h, i: (0, 2 * heads + h)),
            pl.BlockSpec((1, hd), lambda b, h, i: (0, h)),
        ],
        out_specs=pl.BlockSpec((None, tq, hd), lambda b, h, i: (b, i, h)),
        out_shape=jax.ShapeDtypeStruct((batch, seq, heads * hd), BF16),
        compiler_params=_cparams(("parallel", "parallel", "arbitrary")),
        name="sb_attn",
    )(proj, proj, proj, proj_meta, proj_meta, g_sb)


def _with_ones(vb):
    return jnp.concatenate([vb, jnp.ones_like(vb)], axis=1)


def _da_block(q1, q2, kb, vb1, st, mask):
    m1, a1, m2, a2 = st

    def one(qc, m, a):
        s = _dot_nt(qc, kb)
        if mask is not None:
            s = jnp.where(mask, s, NEG_BIG)
        mn = jnp.maximum(m, jnp.max(s, axis=-1, keepdims=True))
        p = jnp.exp(s - mn)
        a = jnp.exp(m - mn) * a + jnp.dot(p.astype(BF16), vb1, preferred_element_type=F32)
        return mn, a

    m1, a1 = one(q1, m1, a1)
    m2, a2 = one(q2, m2, a2)
    return m1, a1, m2, a2


def _da_kernel(lq1_ref, lk1_ref, lq2_ref, lk2_ref, q_ref, k_ref, v_ref, km_ref, vm_ref, g_ref, o_ref,
               *, tq, tkv, ts, scale, lam_init):
    i = pl.program_id(2)
    nsub = tq // ts
    lam = (jnp.exp(jnp.sum(lq1_ref[...] * lk1_ref[...], axis=-1, keepdims=True))
           - jnp.exp(jnp.sum(lq2_ref[...] * lk2_ref[...], axis=-1, keepdims=True)) + lam_init)

    q = q_ref[...]
    lane = lax.broadcasted_iota(jnp.int32, q.shape, 1)
    qs = q * jnp.asarray(scale, BF16)
    zero = jnp.zeros_like(qs)
    q1 = jnp.where(lane < DA_HEAD_DIM, qs, zero)
    q2 = jnp.where(lane >= DA_HEAD_DIM, qs, zero)

    neg = jnp.full((tq, 1), NEG_BIG, F32)
    za = jnp.zeros((tq, 2 * DA_V_DIM), F32)
    st = (neg, za, neg, za)

    def body(j, carry):
        s0 = pl.multiple_of(j * tkv, tkv)
        return _da_block(q1, q2, k_ref[pl.ds(s0, tkv), :], _with_ones(v_ref[pl.ds(s0, tkv), :]), carry, None)

    st = lax.fori_loop(0, i * (tq // tkv), body, st)

    shift = CHUNK.bit_length() - 1
    s0 = pl.multiple_of(i * tq, tq)
    outs = []
    for h in range(nsub):
        rows = slice(h * ts, (h + 1) * ts)
        width = (h + 1) * ts
        kb = jnp.concatenate([km_ref[...], k_ref[pl.ds(s0, width), :]], axis=0)
        vb = jnp.concatenate([vm_ref[...], v_ref[pl.ds(s0, width), :]], axis=0)
        col = lax.broadcasted_iota(jnp.int32, (ts, META_PAD + width), 1)
        row = lax.broadcasted_iota(jnp.int32, (ts, META_PAD + width), 0) + h * ts
        frame_vis = jnp.logical_and(col >= META_PAD,
                                    jnp.right_shift(col - META_PAD, shift) <= jnp.right_shift(row, shift))
        vis = jnp.logical_or(col < N_META, frame_vis)
        sth = _da_block(q1[rows], q2[rows], kb, _with_ones(vb), tuple(t[rows] for t in st), vis)
        _, a1, _, a2 = sth
        outs.append(a1[:, :DA_V_DIM] / a1[:, DA_V_DIM:] - lam * (a2[:, :DA_V_DIM] / a2[:, DA_V_DIM:]))
    o = jnp.concatenate(outs, axis=0)
    ms = jnp.mean(o * o, axis=-1, keepdims=True)
    o_ref[...] = (((o * lax.rsqrt(ms + RMS_EPS)) * g_ref[...]) * (1.0 - lam_init)).astype(BF16)


def _da_attention(proj, proj_meta, lams, g_da, *, batch, seq, heads, tq, tkv, ts, col0, lam_init):
    hd = DA_V_DIM
    qb, kb, vb = col0 // hd, col0 // hd + heads, col0 // hd + 2 * heads
    kern = functools.partial(_da_kernel, tq=tq, tkv=tkv, ts=ts, scale=DA_HEAD_DIM ** -0.5, lam_init=lam_init)
    lam_spec = pl.BlockSpec((1, DA_HEAD_DIM), lambda b, h, i: (0, 0))
    return pl.pallas_call(
        kern,
        grid=(batch, heads, seq // tq),
        in_specs=[
            lam_spec, lam_spec, lam_spec, lam_spec,
            pl.BlockSpec((None, tq, hd), lambda b, h, i: (b, i, qb + h)),
            pl.BlockSpec((None, seq, hd), lambda b, h, i: (b, 0, kb + h)),
            pl.BlockSpec((None, seq, hd), lambda b, h, i: (b, 0, vb + h)),
            pl.BlockSpec((META_PAD, hd), lambda b, h, i: (0, kb + h)),
            pl.BlockSpec((META_PAD, hd), lambda b, h, i: (0, vb + h)),
            pl.BlockSpec((1, hd), lambda b, h, i: (0, h)),
        ],
        out_specs=pl.BlockSpec((None, tq, hd), lambda b, h, i: (b, i, h)),
        out_shape=jax.ShapeDtypeStruct((batch, seq, heads * hd), BF16),
        compiler_params=_cparams(("parallel", "parallel", "arbitrary")),
        name="da_attn",
    )(*lams, proj, proj, proj, proj_meta, proj_meta, g_da)


def _pack_bf16_pairs(x):
    n = x.shape[1] // 2
    lo = lax.bitcast_convert_type(x[:, :n].astype(BF16).astype(F32), jnp.int32)
    hi = lax.bitcast_convert_type(x[:, n:].astype(BF16).astype(F32), jnp.int32)
    return jnp.bitwise_or(lax.shift_right_logical(lo, 16), jnp.bitwise_and(hi, jnp.int32(-65536)))


def _unpack_bf16_pairs(w):
    lo = lax.bitcast_convert_type(lax.shift_left(w, 16), F32)
    hi = lax.bitcast_convert_type(jnp.bitwise_and(w, jnp.int32(-65536)), F32)
    return jnp.concatenate([lo.astype(BF16), hi.astype(BF16)], axis=1)


def _sc_gather_rows(table, idx):
    n_idx = idx.shape[0]
    width = table.shape[1]
    n_workers = SC_CORES * SC_SUBCORES
    per_worker = n_idx // n_workers
    win = SC_GATHER_ROWS
    assert n_idx % (n_workers * win) == 0
    mesh = plsc.VectorSubcoreMesh(core_axis_name="c", subcore_axis_name="s")

    @functools.partial(
        pl.kernel, mesh=mesh,
        out_type=jax.ShapeDtypeStruct((n_idx, width), table.dtype),
        scratch_types=[pltpu.VMEM((win,), jnp.int32), pltpu.VMEM((win, width), table.dtype),
                       pltpu.SemaphoreType.DMA],
    )
    def gather(table_hbm, idx_hbm, out_hbm, idx_v, rows_v, sem):
        worker = lax.axis_index("s") * SC_CORES + lax.axis_index("c")
        base = worker * per_worker

        @pl.loop(0, per_worker // win)
        def _(t):
            off = pl.multiple_of(base + t * win, win)
            pltpu.sync_copy(idx_hbm.at[pl.ds(off, win)], idx_v)
            pltpu.async_copy(table_hbm.at[idx_v], rows_v, sem).wait()
            pltpu.sync_copy(rows_v, out_hbm.at[pl.ds(off, win)])

    return gather(table, idx)


def _split3_dot(a, b_hi, b_lo):
    a_hi = a.astype(BF16)
    a_lo = (a - a_hi.astype(F32)).astype(BF16)
    return (jnp.dot(a_hi, b_hi, preferred_element_type=F32)
            + jnp.dot(a_hi, b_lo, preferred_element_type=F32)
            + jnp.dot(a_lo, b_hi, preferred_element_type=F32))


def _out_proj_kernel(ms_ref, md_ref, ws_ref, wd_ref, x_ref, g_ref, wr_hi_ref, wr_lo_ref, br_ref,
                     h_ref, u_ref, e_ref, p_ref, c_ref, *, n_exp):
    h = (x_ref[...]
         + jnp.dot(ms_ref[...], ws_ref[...], preferred_element_type=F32)
         + jnp.dot(md_ref[...], wd_ref[...], preferred_element_type=F32))
    h_ref[...] = h
    msq = jnp.mean(h * h, axis=-1, keepdims=True)
    u = (h * lax.rsqrt(msq + RMS_EPS)) * g_ref[...]
    u_ref[...] = _pack_bf16_pairs(u)

    logits = _split3_dot(u, wr_hi_ref[...], wr_lo_ref[...]) + br_ref[...]
    lane = lax.broadcasted_iota(jnp.int32, logits.shape, 1).astype(F32)
    work = logits
    tops, idxs = [], []
    for _ in range(TOP_K):
        mx = jnp.max(work, axis=-1, keepdims=True)
        ix = jnp.min(jnp.where(work == mx, lane, float(n_exp)), axis=-1, keepdims=True)
        tops.append(mx)
        idxs.append(ix)
        work = jnp.where(lane == ix, -jnp.inf, work)
    ex = [jnp.exp(t - tops[0]) for t in tops]
    den = ex[0] + ex[1] + ex[2] + ex[3]
    kl = lax.broadcasted_iota(jnp.int32, (logits.shape[0], TOP_K), 1)
    e_out = jnp.zeros((logits.shape[0], TOP_K), F32)
    p_out = jnp.zeros((logits.shape[0], TOP_K), F32)
    for k in range(TOP_K):
        e_out = jnp.where(kl == k, idxs[k], e_out)
        p_out = jnp.where(kl == k, ex[k] / den, p_out)
    e_ref[...] = e_out.astype(jnp.int32)
    p_ref[...] = p_out
    hits = jnp.zeros(logits.shape, F32)
    for k in range(TOP_K):
        hits = hits + jnp.where(lane == idxs[k], 1.0, 0.0)
    c_ref[...] = jnp.sum(hits, axis=0, keepdims=True)


def _out_proj(mix_sb, mix_da, w_sb, w_da, x2d, g_ffn, wr_hi, wr_lo, b_r, *, tm):
    m, d = x2d.shape
    ks, kd = mix_sb.shape[1], mix_da.shape[1]
    n_exp = wr_hi.shape[1]
    kern = functools.partial(_out_proj_kernel, n_exp=n_exp)
    const = lambda shape: pl.BlockSpec(shape, lambda i: (0, 0))
    return pl.pallas_call(
        kern,
        grid=(m // tm,),
        in_specs=[
            pl.BlockSpec((tm, ks), lambda i: (i, 0)),
            pl.BlockSpec((tm, kd), lambda i: (i, 0)),
            const((ks, d)), const((kd, d)),
            pl.BlockSpec((tm, d), lambda i: (i, 0)),
            const((1, d)), const((d, n_exp)), const((d, n_exp)), const((1, n_exp)),
        ],
        out_specs=[
            pl.BlockSpec((tm, d), lambda i: (i, 0)),
            pl.BlockSpec((tm, d // 2), lambda i: (i, 0)),
            pl.BlockSpec((tm, TOP_K), lambda i: (i, 0)),
            pl.BlockSpec((tm, TOP_K), lambda i: (i, 0)),
            pl.BlockSpec((None, 1, n_exp), lambda i: (i, 0, 0)),
        ],
        out_shape=[
            jax.ShapeDtypeStruct((m, d), F32),
            jax.ShapeDtypeStruct((m, d // 2), jnp.int32),
            jax.ShapeDtypeStruct((m, TOP_K), jnp.int32),
            jax.ShapeDtypeStruct((m, TOP_K), F32),
            jax.ShapeDtypeStruct((m // tm, 1, n_exp), F32),
        ],
        compiler_params=_cparams(("parallel",)),
        name="out_proj_router",
    )(mix_sb, mix_da, w_sb, w_da, x2d, g_ffn, wr_hi, wr_lo, b_r)


def _moe_kernel(be_ref, bc_ref, rb_ref, x_ref, wg_ref, wu_ref, wda_ref, wdb_ref, bg_ref, bup_ref, bda_ref, bdb_ref,
                o_ref, act_ref, *, nf, ts):
    blk = pl.program_id(0)
    s = pl.program_id(1)
    cnt = bc_ref[blk]
    nsub = x_ref.shape[0] // ts

    @pl.when(jnp.logical_and(cnt > 0, s < nf))
    def _():
        wg = wg_ref[...].astype(BF16)
        wu = wu_ref[...].astype(BF16)
        slot = jnp.minimum(s, nf - 1)
        for j in range(nsub):
            @pl.when(cnt > j * ts)
            def _():
                x = _unpack_bf16_pairs(x_ref[j * ts:(j + 1) * ts, :])
                g = jnp.dot(x, wg, preferred_element_type=F32) + bg_ref[...]
                u = jnp.dot(x, wu, preferred_element_type=F32) + bup_ref[...]
                gate = jnp.minimum(g, SWIGLU_LIMIT)
                up = jnp.clip(u, -SWIGLU_LIMIT, SWIGLU_LIMIT)
                act = (up + 1.0) * gate * jax.nn.sigmoid(SWIGLU_ALPHA * gate)
                act_ref[slot, j * ts:(j + 1) * ts, :] = act.astype(BF16)

    @pl.when(jnp.logical_and(cnt > 0, s >= nf))
    def _():
        tf = act_ref.shape[2]
        wda = wda_ref[...].astype(BF16)
        wdb = wdb_ref[...].astype(BF16)
        for j in range(nsub):
            @pl.when(cnt > j * ts)
            def _():
                ya = bda_ref[...]
                yb = bdb_ref[...]
                for f in range(nf):
                    a = act_ref[f, j * ts:(j + 1) * ts, :]
                    ya = ya + jnp.dot(a, wda[f * tf:(f + 1) * tf, :], preferred_element_type=F32)
                    yb = yb + jnp.dot(a, wdb[f * tf:(f + 1) * tf, :], preferred_element_type=F32)
                o_ref[j * ts:(j + 1) * ts, :] = _pack_bf16_pairs(jnp.concatenate([ya, yb], axis=1))

            @pl.when(cnt <= j * ts)
            def _():
                o_ref[j * ts:(j + 1) * ts, :] = jnp.zeros((ts, o_ref.shape[1]), o_ref.dtype)


def _moe_ffn(blk_e, blk_cnt, blk_row, xs, w_gu, w_dn, b_gu, b_dn, *, tm, ts, tf, tn):
    n_rows = xs.shape[0]
    n_exp, d, f2 = w_gu.shape
    d_ff = f2 // 2
    nf = d_ff // tf
    nn = d // tn
    n_blk = n_rows // tm
    b_gu3 = b_gu.reshape(n_exp, 1, f2)
    b_dn3 = b_dn.reshape(n_exp, 1, d)

    def fi(s, bc, b):
        return jnp.where(bc[b] > 0, jnp.minimum(s, nf - 1), nf - 1)

    nh = nn // 2

    def ni(s, bc, b):
        return jnp.where(bc[b] > 0, jnp.clip(s - nf, 0, nh - 1), nh - 1)

    grid_spec = pltpu.PrefetchScalarGridSpec(
        num_scalar_prefetch=3,
        grid=(n_blk, nf + nh),
        in_specs=[
            pl.BlockSpec((tm, d // 2), lambda b, s, be, bc, rb: (rb[b], 0)),
            pl.BlockSpec((None, d, tf), lambda b, s, be, bc, rb: (be[b], 0, fi(s, bc, b))),
            pl.BlockSpec((None, d, tf), lambda b, s, be, bc, rb: (be[b], 0, nf + fi(s, bc, b))),
            pl.BlockSpec((None, d_ff, tn), lambda b, s, be, bc, rb: (be[b], 0, ni(s, bc, b))),
            pl.BlockSpec((None, d_ff, tn), lambda b, s, be, bc, rb: (be[b], 0, nh + ni(s, bc, b))),
            pl.BlockSpec((None, 1, tf), lambda b, s, be, bc, rb: (be[b], 0, fi(s, bc, b))),
            pl.BlockSpec((None, 1, tf), lambda b, s, be, bc, rb: (be[b], 0, nf + fi(s, bc, b))),
            pl.BlockSpec((None, 1, tn), lambda b, s, be, bc, rb: (be[b], 0, ni(s, bc, b))),
            pl.BlockSpec((None, 1, tn), lambda b, s, be, bc, rb: (be[b], 0, nh + ni(s, bc, b))),
        ],
        out_specs=pl.BlockSpec((tm, tn), lambda b, s, be, bc, rb: (rb[b], ni(s, bc, b))),
        scratch_shapes=[pltpu.VMEM((nf, tm, tf), BF16)],
    )
    return pl.pallas_call(
        functools.partial(_moe_kernel, nf=nf, ts=ts),
        grid_spec=grid_spec,
        out_shape=jax.ShapeDtypeStruct((n_rows, d // 2), jnp.int32),
        compiler_params=_cparams(("arbitrary", "arbitrary")),
        name="moe_ffn",
    )(blk_e, blk_cnt, blk_row, xs, w_gu, w_gu, w_dn, w_dn, b_gu3, b_gu3, b_dn3, b_dn3)


def _final_kernel(h_ref, y0_ref, y1_ref, y2_ref, y3_ref, p_ref, g_ref, o_ref):
    p = p_ref[...]
    y = None
    for k, y_ref in enumerate((y0_ref, y1_ref, y2_ref, y3_ref)):
        t = _unpack_bf16_pairs(y_ref[...]).astype(F32) * p[:, k:k + 1]
        y = t if y is None else y + t
    h = h_ref[...] + y
    ms = jnp.mean(h * h, axis=-1, keepdims=True)
    o_ref[...] = (h * lax.rsqrt(ms + RMS_EPS)) * g_ref[...]


def _final(h1, y_km, gates, g, *, tm):
    m, d = h1.shape
    nb = m // tm
    assert TOP_K == 4
    y_specs = [pl.BlockSpec((tm, d // 2), functools.partial(lambda i, k: (k * nb + i, 0), k=k))
               for k in range(TOP_K)]
    return pl.pallas_call(
        _final_kernel,
        grid=(nb,),
        in_specs=([pl.BlockSpec((tm, d), lambda i: (i, 0))] + y_specs
                  + [pl.BlockSpec((tm, TOP_K), lambda i: (i, 0)), pl.BlockSpec((1, d), lambda i: (0, 0))]),
        out_specs=pl.BlockSpec((tm, d), lambda i: (i, 0)),
        out_shape=jax.ShapeDtypeStruct((m, d), F32),
        compiler_params=_cparams(("parallel",)),
        name="final_norm",
    )(h1, y_km, y_km, y_km, y_km, gates, g)


def _routing(top_e, counts, tm):
    n_exp = counts.shape[0]
    n_assign = top_e.shape[0] * TOP_K
    flat_e = top_e.reshape(-1)
    order = jnp.argsort(flat_e, stable=True).astype(jnp.int32)
    rank = jnp.argsort(order).astype(jnp.int32)
    padded = (counts + tm - 1) // tm * tm
    pad_end = jnp.cumsum(padded).astype(jnp.int32)
    pad_start = pad_end - padded
    grp_start = jnp.cumsum(counts).astype(jnp.int32) - counts
    n_blk = -(-n_assign // tm) + n_exp
    blk_start = jnp.arange(n_blk, dtype=jnp.int32) * tm
    blk_e = jnp.minimum(jnp.sum((blk_start[:, None] >= pad_end[None, :]).astype(jnp.int32), axis=1), n_exp - 1)
    blk_off = blk_start - pad_start[blk_e]
    blk_cnt = jnp.clip(counts[blk_e] - blk_off, 0, tm).astype(jnp.int32)
    n_tok = top_e.shape[0]
    tok_sorted = jnp.concatenate([order // TOP_K, jnp.arange(tm, dtype=jnp.int32) % n_tok])
    win_start = jnp.clip(grp_start[blk_e] + blk_off, 0, n_assign)
    row_tok = jax.vmap(lambda st: lax.dynamic_slice(tok_sorted, (st,), (tm,)))(win_start)
    spread = (blk_start[:, None] + jnp.arange(tm, dtype=jnp.int32)[None, :]) % n_tok
    row_tok = jnp.where(blk_cnt[:, None] > 0, row_tok, spread).reshape(-1)
    delta = pad_start - grp_start
    e2d = flat_e.reshape(-1, LANES)
    hit = e2d[None] == jnp.arange(n_exp, dtype=jnp.int32)[:, None, None]
    dest_flat = rank + jnp.sum(jnp.where(hit, delta[:, None, None], 0), axis=0).reshape(-1)
    last_used = jnp.maximum(pad_end[-1] // tm - 1, 0)
    blk_e = jnp.where(blk_cnt > 0, blk_e, blk_e[last_used]).astype(jnp.int32)
    blk_row = jnp.where(blk_cnt > 0, jnp.arange(n_blk, dtype=jnp.int32), last_used).astype(jnp.int32)
    return row_tok, dest_flat, blk_e, blk_cnt, blk_row


def kernel(x, meta_tokens, g_mix, w_in, lam_q1, lam_k1, lam_q2, lam_k2, g_sb_out, g_da_out, w_out, g_ffn,
           w_router, b_router, w_gate_up, b_gate_up, w_down, b_down, g_final):
    b, s, d = x.shape
    depth = w_in.shape[0]
    assert depth == 1, "single-layer trunk"
    layer = 0
    sb_heads = (d // 2) // SB_HEAD_DIM
    da_heads = (d // 2) // DA_V_DIM
    sb_w = sb_heads * SB_HEAD_DIM
    da_col0 = 3 * sb_w
    da_qk_w = da_heads * 2 * DA_HEAD_DIM
    n_exp = w_router.shape[-1]
    lam_init = 0.8 - 0.6 * math.exp(-0.3 * layer)

    tm_proj, tn_proj = TM_PROJ, TN_PROJ
    tq, tk_sb, ts_da = TQ_ATTN, TK_SB, TS_DA
    tm_out = TM_OUT
    tm_moe, tf_moe = TM_MOE, TF_MOE
    tm_fin = TM_FINAL

    x2d = x.reshape(b * s, d)
    w_in_bf = w_in.reshape(w_in.shape[1:]).astype(BF16)
    g_mix2 = g_mix[layer].reshape(1, d)
    rope_cols = (da_col0, da_col0 + 2 * da_qk_w)

    pos_f = N_META + jnp.arange(s, dtype=jnp.int32)
    proj = _in_proj(x2d, g_mix2, w_in_bf, _rope_tables(pos_f), tm=tm_proj, tn=tn_proj,
                    rope_cols=rope_cols, pos_blocks=s // tm_proj)
    meta_pad = jnp.zeros((META_PAD, d), x.dtype).at[:N_META].set(meta_tokens.astype(x.dtype))
    pos_m = jnp.arange(META_PAD, dtype=jnp.int32)
    proj_meta = _in_proj(meta_pad, g_mix2, w_in_bf, _rope_tables(pos_m), tm=META_PAD, tn=tn_proj,
                         rope_cols=rope_cols, pos_blocks=1)
    proj3 = proj.reshape(b, s, -1)

    mix_sb = _sb_attention(proj3, proj_meta, g_sb_out[layer].reshape(1, -1), batch=b, seq=s, heads=sb_heads,
                           tq=tq, tk=tk_sb)
    lams = tuple(t[layer].reshape(1, DA_HEAD_DIM).astype(F32) for t in (lam_q1, lam_k1, lam_q2, lam_k2))
    mix_da = _da_attention(proj3, proj_meta, lams, g_da_out[layer].reshape(1, -1), batch=b, seq=s, heads=da_heads,
                           tq=TQ_DA, tkv=TKV_DA, ts=ts_da, col0=da_col0, lam_init=lam_init)

    w_out_bf = w_out[layer].astype(BF16)
    wr = w_router[layer]
    wr_hi = wr.astype(BF16)
    wr_lo = (wr - wr_hi.astype(F32)).astype(BF16)
    h1, u_packed, top_e, gates, tile_counts = _out_proj(
        mix_sb.reshape(b * s, -1), mix_da.reshape(b * s, -1), w_out_bf[:sb_w], w_out_bf[sb_w:], x2d,
        g_ffn[layer].reshape(1, d), wr_hi, wr_lo, b_router[layer].reshape(1, n_exp), tm=tm_out)

    counts = jnp.sum(tile_counts, axis=(0, 1)).astype(jnp.int32)
    row_tok, dest_flat, blk_e, blk_cnt, blk_row = _routing(top_e, counts, tm_moe)
    xs = _sc_gather_rows(u_packed, row_tok)
    rows = _moe_ffn(blk_e, blk_cnt, blk_row, xs, w_gate_up.reshape(w_gate_up.shape[1:]),
                    w_down.reshape(w_down.shape[1:]), b_gate_up[layer], b_down[layer],
                    tm=tm_moe, ts=TS_MOE, tf=tf_moe, tn=TN_MOE)
    dest_km = dest_flat.reshape(b * s, TOP_K).T.reshape(-1)
    y_km = _sc_gather_rows(rows, dest_km)

    out = _final(h1, y_km, gates, g_final.reshape(1, d), tm=tm_fin)
    return out.reshape(b, s, d)
```

```python
import functools
import math

import jax
import jax.numpy as jnp
from jax import lax
from jax.experimental import pallas as pl
from jax.experimental.pallas import tpu as pltpu
from jax.experimental.pallas import tpu_sc as plsc

F32 = jnp.float32
BF16 = jnp.bfloat16

CHUNK = 64
N_META = 16
RMS_EPS = 1e-5
SB_HEAD_DIM = 128
DA_HEAD_DIM = 64
DA_V_DIM = 2 * DA_HEAD_DIM
ROPE_THETA = 500000.0
ROPE_DIM = DA_HEAD_DIM // 4
TOP_K = 4
SWIGLU_LIMIT = 7.0
SWIGLU_ALPHA = 1.702

LANES = 128
META_PAD = 128
NEG_BIG = -1e30
SB_DEAD_RUN = 110.0
VMEM_LIMIT = 56 * 1024 * 1024
SC_CORES, SC_SUBCORES = 2, 16
SC_GATHER_ROWS = 64

TM_PROJ, TN_PROJ = 1024, 1024
TQ_ATTN = 512
TQ_DA, TKV_DA = 1024, 512
TK_SB = 256
TS_DA = 256
TM_OUT = 512
TM_MOE, TS_MOE = 2048, 512
TF_MOE, TN_MOE = 256, 256
TM_FINAL = 512


def _cparams(sem):
    return pltpu.CompilerParams(dimension_semantics=sem, vmem_limit_bytes=VMEM_LIMIT)


def _in_proj_kernel(x_ref, g_ref, w_ref, c_ref, sa_ref, sb_ref, o_ref, u_scr, *, tn, rope_lo, rope_hi):
    n = pl.program_id(1)

    @pl.when(n == 0)
    def _():
        x = x_ref[...]
        ms = jnp.mean(x * x, axis=-1, keepdims=True)
        u_scr[...] = ((x * lax.rsqrt(ms + RMS_EPS)) * g_ref[...]).astype(BF16)

    acc = jnp.dot(u_scr[...], w_ref[...], preferred_element_type=F32)
    is_rope = jnp.logical_and(n >= rope_lo, n < rope_hi)

    @pl.when(is_rope)
    def _():
        for c in range(tn // LANES):
            xc = acc[:, c * LANES:(c + 1) * LANES]
            r = (xc * c_ref[...] + pltpu.roll(xc, LANES - ROPE_DIM // 2, 1) * sa_ref[...]
                 + pltpu.roll(xc, ROPE_DIM // 2, 1) * sb_ref[...])
            o_ref[:, c * LANES:(c + 1) * LANES] = r.astype(BF16)

    @pl.when(jnp.logical_not(is_rope))
    def _():
        o_ref[...] = acc.astype(BF16)


def _rope_tables(pos):
    half = ROPE_DIM // 2
    inv_freq = ROPE_THETA ** (-(jnp.arange(half, dtype=F32) * 2.0 / ROPE_DIM))
    ang = pos.astype(F32)[:, None] * inv_freq[None, :]
    cos, sin = jnp.cos(ang), jnp.sin(ang)
    p = pos.shape[0]
    ones = jnp.ones((p, DA_HEAD_DIM - ROPE_DIM), F32)
    zeros8 = jnp.zeros((p, half), F32)
    zeros48 = jnp.zeros((p, DA_HEAD_DIM - ROPE_DIM), F32)
    c64 = jnp.concatenate([cos, cos, ones], axis=1)
    sa64 = jnp.concatenate([-sin, zeros8, zeros48], axis=1)
    sb64 = jnp.concatenate([zeros8, sin, zeros48], axis=1)
    tile2 = lambda t: jnp.concatenate([t, t], axis=1)
    return tile2(c64), tile2(sa64), tile2(sb64)


def _in_proj(x2d, g, w_bf, tables, *, tm, tn, rope_cols, pos_blocks):
    m, d = x2d.shape
    n_cols = w_bf.shape[1]
    c_t, sa_t, sb_t = tables
    kern = functools.partial(_in_proj_kernel, tn=tn, rope_lo=rope_cols[0] // tn, rope_hi=rope_cols[1] // tn)
    tab_spec = pl.BlockSpec((tm, LANES), lambda i, n: (i % pos_blocks, 0))
    return pl.pallas_call(
        kern,
        grid=(m // tm, n_cols // tn),
        in_specs=[
            pl.BlockSpec((tm, d), lambda i, n: (i, 0)),
            pl.BlockSpec((1, d), lambda i, n: (0, 0)),
            pl.BlockSpec((d, tn), lambda i, n: (0, n)),
            tab_spec, tab_spec, tab_spec,
        ],
        out_specs=pl.BlockSpec((tm, tn), lambda i, n: (i, n)),
        out_shape=jax.ShapeDtypeStruct((m, n_cols), BF16),
        scratch_shapes=[pltpu.VMEM((tm, d), BF16)],
        compiler_params=_cparams(("parallel", "arbitrary")),
        name="in_proj",
    )(x2d, g, w_bf, c_t, sa_t, sb_t)


def _dot_nt(a, b):
    return lax.dot_general(a, b, (((1,), (1,)), ((), ())), preferred_element_type=F32)


def _suffix_sum_matrix(n):
    j = lax.broadcasted_iota(jnp.int32, (2 * n, n), 0)
    s = lax.broadcasted_iota(jnp.int32, (2 * n, n), 1)
    return jnp.where(jnp.where(j >= n, j - n, j) > s, 1.0, 0.0).astype(BF16)


def _sb_block(q, kb, vb, u2, run, acc, mask, scale):
    z = _dot_nt(q, kb) * scale
    sp = jnp.maximum(z, 0.0) + jnp.log(1.0 + jnp.exp(-jnp.abs(z)))
    if mask is not None:
        sp = jnp.where(mask, sp, 0.0)
    hi = sp.astype(BF16)
    lo = (sp - hi.astype(F32)).astype(BF16)
    cs = jnp.dot(jnp.concatenate([hi, lo], axis=1), u2, preferred_element_type=F32)
    w = jnp.exp(z - sp - cs - run)
    if mask is not None:
        w = jnp.where(mask, w, 0.0)
    acc = acc + jnp.dot(w.astype(BF16), vb, preferred_element_type=F32)
    run = run + (cs[:, :1] + sp[:, :1])
    return run, acc


def _sb_kernel(q_ref, k_ref, v_ref, km_ref, vm_ref, g_ref, o_ref, *, tq, tk, scale):
    i = pl.program_id(2)
    nsub = tq // tk
    u_blk = _suffix_sum_matrix(tk)
    row = lax.broadcasted_iota(jnp.int32, (tk, tk), 0)
    col = lax.broadcasted_iota(jnp.int32, (tk, tk), 1)
    diag_mask = col < row

    def kv(blk):
        s0 = pl.multiple_of(blk * tk, tk)
        return k_ref[pl.ds(s0, tk), :], v_ref[pl.ds(s0, tk), :]

    runs, accs = [], []
    for h in range(nsub):
        qh = q_ref[h * tk:(h + 1) * tk, :]
        run = jnp.zeros((tk, 1), F32)
        acc = jnp.zeros((tk, SB_HEAD_DIM), F32)
        for c in range(h, -1, -1):
            kb, vb = kv(i * nsub + c)
            run, acc = _sb_block(qh, kb, vb, u_blk, run, acc, diag_mask if c == h else None, scale)
        runs.append(run)
        accs.append(acc)
    run = jnp.concatenate(runs, axis=0)
    acc = jnp.concatenate(accs, axis=0)
    q = q_ref[...]

    def alive_flag(r):
        return (jnp.min(r) < SB_DEAD_RUN).astype(jnp.int32)

    def cond(carry):
        jb, alive, _, _ = carry
        return jnp.logical_and(jb >= 0, alive > 0)

    def body(carry):
        jb, _, r, a = carry
        kb, vb = kv(jb)
        r, a = _sb_block(q, kb, vb, u_blk, r, a, None, scale)
        return jb - 1, alive_flag(r), r, a

    _, alive, run, acc = lax.while_loop(cond, body, (i * nsub - 1, alive_flag(run), run, acc))

    def meta_block():
        mcol = lax.broadcasted_iota(jnp.int32, (tq, META_PAD), 1)
        return _sb_block(q, km_ref[...], vm_ref[...], _suffix_sum_matrix(META_PAD), run, acc,
                         mcol < N_META, scale)[1]

    acc = lax.cond(alive > 0, meta_block, lambda: acc)

    ms = jnp.mean(acc * acc, axis=-1, keepdims=True)
    o_ref[...] = ((acc * lax.rsqrt(ms + RMS_EPS)) * g_ref[...]).astype(BF16)


def _sb_attention(proj, proj_meta, g_sb, *, batch, seq, heads, tq, tk):
    hd = SB_HEAD_DIM
    kern = functools.partial(_sb_kernel, tq=tq, tk=tk, scale=hd ** -0.5)
    return pl.pallas_call(
        kern,
        grid=(batch, heads, seq // tq),
        in_specs=[
            pl.BlockSpec((None, tq, hd), lambda b, h, i: (b, i, h)),
            pl.BlockSpec((None, seq, hd), lambda b, h, i: (b, 0, heads + h)),
            pl.BlockSpec((None, seq, hd), lambda b, h, i: (b, 0, 2 * heads + h)),
            pl.BlockSpec((META_PAD, hd), lambda b, h, i: (0, heads + h)),
            pl.BlockSpec((META_PAD, hd), lambda b, h, i: (0, 2 * heads + h)),
            pl.BlockSpec((1, hd), lambda b, h, i: (0, h)),
        ],
        out_specs=pl.BlockSpec((None, tq, hd), lambda b, h, i: (b, i, h)),
        out_shape=jax.ShapeDtypeStruct((batch, seq, heads * hd), BF16),
        compiler_params=_cparams(("parallel", "parallel", "arbitrary")),
        name="sb_attn",
    )(proj, proj, proj, proj_meta, proj_meta, g_sb)


def _with_ones(vb):
    return jnp.concatenate([vb, jnp.ones_like(vb)], axis=1)


def _da_block(q1, q2, kb, vb1, st, mask):
    m1, a1, m2, a2 = st

    def one(qc, m, a):
        s = _dot_nt(qc, kb)
        if mask is not None:
            s = jnp.where(mask, s, NEG_BIG)
        mn = jnp.maximum(m, jnp.max(s, axis=-1, keepdims=True))
        p = jnp.exp(s - mn)
        a = jnp.exp(m - mn) * a + jnp.dot(p.astype(BF16), vb1, preferred_element_type=F32)
        return mn, a

    m1, a1 = one(q1, m1, a1)
    m2, a2 = one(q2, m2, a2)
    return m1, a1, m2, a2


def _da_kernel(lq1_ref, lk1_ref, lq2_ref, lk2_ref, q_ref, k_ref, v_ref, km_ref, vm_ref, g_ref, o_ref,
               *, tq, tkv, ts, scale, lam_init):
    i = pl.program_id(2)
    nsub = tq // ts
    lam = (jnp.exp(jnp.sum(lq1_ref[...] * lk1_ref[...], axis=-1, keepdims=True))
           - jnp.exp(jnp.sum(lq2_ref[...] * lk2_ref[...], axis=-1, keepdims=True)) + lam_init)

    q = q_ref[...]
    lane = lax.broadcasted_iota(jnp.int32, q.shape, 1)
    qs = q * jnp.asarray(scale, BF16)
    zero = jnp.zeros_like(qs)
    q1 = jnp.where(lane < DA_HEAD_DIM, qs, zero)
    q2 = jnp.where(lane >= DA_HEAD_DIM, qs, zero)

    neg = jnp.full((tq, 1), NEG_BIG, F32)
    za = jnp.zeros((tq, 2 * DA_V_DIM), F32)
    st = (neg, za, neg, za)

    def body(j, carry):
        s0 = pl.multiple_of(j * tkv, tkv)
        return _da_block(q1, q2, k_ref[pl.ds(s0, tkv), :], _with_ones(v_ref[pl.ds(s0, tkv), :]), carry, None)

    st = lax.fori_loop(0, i * (tq // tkv), body, st)

    shift = CHUNK.bit_length() - 1
    s0 = pl.multiple_of(i * tq, tq)
    outs = []
    for h in range(nsub):
        rows = slice(h * ts, (h + 1) * ts)
        width = (h + 1) * ts
        kb = jnp.concatenate([km_ref[...], k_ref[pl.ds(s0, width), :]], axis=0)
        vb = jnp.concatenate([vm_ref[...], v_ref[pl.ds(s0, width), :]], axis=0)
        col = lax.broadcasted_iota(jnp.int32, (ts, META_PAD + width), 1)
        row = lax.broadcasted_iota(jnp.int32, (ts, META_PAD + width), 0) + h * ts
        frame_vis = jnp.logical_and(col >= META_PAD,
                                    jnp.right_shift(col - META_PAD, shift) <= jnp.right_shift(row, shift))
        vis = jnp.logical_or(col < N_META, frame_vis)
        sth = _da_block(q1[rows], q2[rows], kb, _with_ones(vb), tuple(t[rows] for t in st), vis)
        _, a1, _, a2 = sth
        outs.append(a1[:, :DA_V_DIM] / a1[:, DA_V_DIM:] - lam * (a2[:, :DA_V_DIM] / a2[:, DA_V_DIM:]))
    o = jnp.concatenate(outs, axis=0)
    ms = jnp.mean(o * o, axis=-1, keepdims=True)
    o_ref[...] = (((o * lax.rsqrt(ms + RMS_EPS)) * g_ref[...]) * (1.0 - lam_init)).astype(BF16)


def _da_attention(proj, proj_meta, lams, g_da, *, batch, seq, heads, tq, tkv, ts, col0, lam_init):
    hd = DA_V_DIM
    qb, kb, vb = col0 // hd, col0 // hd + heads, col0 // hd + 2 * heads
    kern = functools.partial(_da_kernel, tq=tq, tkv=tkv, ts=ts, scale=DA_HEAD_DIM ** -0.5, lam_init=lam_init)
    lam_spec = pl.BlockSpec((1, DA_HEAD_DIM), lambda b, h, i: (0, 0))
    return pl.pallas_call(
        kern,
        grid=(batch, heads, seq // tq),
        in_specs=[
            lam_spec, lam_spec, lam_spec, lam_spec,
            pl.BlockSpec((None, tq, hd), lambda b, h, i: (b, i, qb + h)),
            pl.BlockSpec((None, seq, hd), lambda b, h, i: (b, 0, kb + h)),
            pl.BlockSpec((None, seq, hd), lambda b, h, i: (b, 0, vb + h)),
            pl.BlockSpec((META_PAD, hd), lambda b, h, i: (0, kb + h)),
            pl.BlockSpec((META_PAD, hd), lambda b, h, i: (0, vb + h)),
            pl.BlockSpec((1, hd), lambda b, h, i: (0, h)),
        ],
        out_specs=pl.BlockSpec((None, tq, hd), lambda b, h, i: (b, i, h)),
        out_shape=jax.ShapeDtypeStruct((batch, seq, heads * hd), BF16),
        compiler_params=_cparams(("parallel", "parallel", "arbitrary")),
        name="da_attn",
    )(*lams, proj, proj, proj, proj_meta, proj_meta, g_da)


def _pack_bf16_pairs(x):
    n = x.shape[1] // 2
    lo = lax.bitcast_convert_type(x[:, :n].astype(BF16).astype(F32), jnp.int32)
    hi = lax.bitcast_convert_type(x[:, n:].astype(BF16).astype(F32), jnp.int32)
    return jnp.bitwise_or(lax.shift_right_logical(lo, 16), jnp.bitwise_and(hi, jnp.int32(-65536)))


def _unpack_bf16_pairs(w):
    lo = lax.bitcast_convert_type(lax.shift_left(w, 16), F32)
    hi = lax.bitcast_convert_type(jnp.bitwise_and(w, jnp.int32(-65536)), F32)
    return jnp.concatenate([lo.astype(BF16), hi.astype(BF16)], axis=1)


def _sc_gather_rows(table, idx):
    n_idx = idx.shape[0]
    width = table.shape[1]
    n_workers = SC_CORES * SC_SUBCORES
    per_worker = n_idx // n_workers
    win = SC_GATHER_ROWS
    assert n_idx % (n_workers * win) == 0
    mesh = plsc.VectorSubcoreMesh(core_axis_name="c", subcore_axis_name="s")

    @functools.partial(
        pl.kernel, mesh=mesh,
        out_type=jax.ShapeDtypeStruct((n_idx, width), table.dtype),
        scratch_types=[pltpu.VMEM((win,), jnp.int32), pltpu.VMEM((win, width), table.dtype),
                       pltpu.SemaphoreType.DMA],
    )
    def gather(table_hbm, idx_hbm, out_hbm, idx_v, rows_v, sem):
        worker = lax.axis_index("s") * SC_CORES + lax.axis_index("c")
        base = worker * per_worker

        @pl.loop(0, per_worker // win)
        def _(t):
            off = pl.multiple_of(base + t * win, win)
            pltpu.sync_copy(idx_hbm.at[pl.ds(off, win)], idx_v)
            pltpu.async_copy(table_hbm.at[idx_v], rows_v, sem).wait()
            pltpu.sync_copy(rows_v, out_hbm.at[pl.ds(off, win)])

    return gather(table, idx)


def _split3_dot(a, b_hi, b_lo):
    a_hi = a.astype(BF16)
    a_lo = (a - a_hi.astype(F32)).astype(BF16)
    return (jnp.dot(a_hi, b_hi, preferred_element_type=F32)
            + jnp.dot(a_hi, b_lo, preferred_element_type=F32)
            + jnp.dot(a_lo, b_hi, preferred_element_type=F32))


def _out_proj_kernel(ms_ref, md_ref, ws_ref, wd_ref, x_ref, g_ref, wr_hi_ref, wr_lo_ref, br_ref,
                     h_ref, u_ref, e_ref, p_ref, c_ref, *, n_exp):
    h = (x_ref[...]
         + jnp.dot(ms_ref[...], ws_ref[...], preferred_element_type=F32)
         + jnp.dot(md_ref[...], wd_ref[...], preferred_element_type=F32))
    h_ref[...] = h
    msq = jnp.mean(h * h, axis=-1, keepdims=True)
    u = (h * lax.rsqrt(msq + RMS_EPS)) * g_ref[...]
    u_ref[...] = _pack_bf16_pairs(u)

    logits = _split3_dot(u, wr_hi_ref[...], wr_lo_ref[...]) + br_ref[...]
    lane = lax.broadcasted_iota(jnp.int32, logits.shape, 1).astype(F32)
    work = logits
    tops, idxs = [], []
    for _ in range(TOP_K):
        mx = jnp.max(work, axis=-1, keepdims=True)
        ix = jnp.min(jnp.where(work == mx, lane, float(n_exp)), axis=-1, keepdims=True)
        tops.append(mx)
        idxs.append(ix)
        work = jnp.where(lane == ix, -jnp.inf, work)
    ex = [jnp.exp(t - tops[0]) for t in tops]
    den = ex[0] + ex[1] + ex[2] + ex[3]
    kl = lax.broadcasted_iota(jnp.int32, (logits.shape[0], TOP_K), 1)
    e_out = jnp.zeros((logits.shape[0], TOP_K), F32)
    p_out = jnp.zeros((logits.shape[0], TOP_K), F32)
    for k in range(TOP_K):
        e_out = jnp.where(kl == k, idxs[k], e_out)
        p_out = jnp.where(kl == k, ex[k] / den, p_out)
    e_ref[...] = e_out.astype(jnp.int32)
    p_ref[...] = p_out
    hits = jnp.zeros(logits.shape, F32)
    for k in range(TOP_K):
        hits = hits + jnp.where(lane == idxs[k], 1.0, 0.0)
    c_ref[...] = jnp.sum(hits, axis=0, keepdims=True)


def _out_proj(mix_sb, mix_da, w_sb, w_da, x2d, g_ffn, wr_hi, wr_lo, b_r, *, tm):
    m, d = x2d.shape
    ks, kd = mix_sb.shape[1], mix_da.shape[1]
    n_exp = wr_hi.shape[1]
    kern = functools.partial(_out_proj_kernel, n_exp=n_exp)
    const = lambda shape: pl.BlockSpec(shape, lambda i: (0, 0))
    return pl.pallas_call(
        kern,
        grid=(m // tm,),
        in_specs=[
            pl.BlockSpec((tm, ks), lambda i: (i, 0)),
            pl.BlockSpec((tm, kd), lambda i: (i, 0)),
            const((ks, d)), const((kd, d)),
            pl.BlockSpec((tm, d), lambda i: (i, 0)),
            const((1, d)), const((d, n_exp)), const((d, n_exp)), const((1, n_exp)),
        ],
        out_specs=[
            pl.BlockSpec((tm, d), lambda i: (i, 0)),
            pl.BlockSpec((tm, d // 2), lambda i: (i, 0)),
            pl.BlockSpec((tm, TOP_K), lambda i: (i, 0)),
            pl.BlockSpec((tm, TOP_K), lambda i: (i, 0)),
            pl.BlockSpec((None, 1, n_exp), lambda i: (i, 0, 0)),
        ],
        out_shape=[
            jax.ShapeDtypeStruct((m, d), F32),
            jax.ShapeDtypeStruct((m, d // 2), jnp.int32),
            jax.ShapeDtypeStruct((m, TOP_K), jnp.int32),
            jax.ShapeDtypeStruct((m, TOP_K), F32),
            jax.ShapeDtypeStruct((m // tm, 1, n_exp), F32),
        ],
        compiler_params=_cparams(("parallel",)),
        name="out_proj_router",
    )(mix_sb, mix_da, w_sb, w_da, x2d, g_ffn, wr_hi, wr_lo, b_r)


def _moe_kernel(be_ref, bc_ref, rb_ref, x_ref, wg_ref, wu_ref, wda_ref, wdb_ref, bg_ref, bup_ref, bda_ref, bdb_ref,
                o_ref, act_ref, *, nf, ts):
    blk = pl.program_id(0)
    s = pl.program_id(1)
    cnt = bc_ref[blk]
    nsub = x_ref.shape[0] // ts

    @pl.when(jnp.logical_and(cnt > 0, s < nf))
    def _():
        wg = wg_ref[...].astype(BF16)
        wu = wu_ref[...].astype(BF16)
        slot = jnp.minimum(s, nf - 1)
        for j in range(nsub):
            @pl.when(cnt > j * ts)
            def _():
                x = _unpack_bf16_pairs(x_ref[j * ts:(j + 1) * ts, :])
                g = jnp.dot(x, wg, preferred_element_type=F32) + bg_ref[...]
                u = jnp.dot(x, wu, preferred_element_type=F32) + bup_ref[...]
                gate = jnp.minimum(g, SWIGLU_LIMIT)
                up = jnp.clip(u, -SWIGLU_LIMIT, SWIGLU_LIMIT)
                act = (up + 1.0) * gate * jax.nn.sigmoid(SWIGLU_ALPHA * gate)
                act_ref[slot, j * ts:(j + 1) * ts, :] = act.astype(BF16)

    @pl.when(jnp.logical_and(cnt > 0, s >= nf))
    def _():
        tf = act_ref.shape[2]
        wda = wda_ref[...].astype(BF16)
        wdb = wdb_ref[...].astype(BF16)
        for j in range(nsub):
            @pl.when(cnt > j * ts)
            def _():
                ya = bda_ref[...]
                yb = bdb_ref[...]
                for f in range(nf):
                    a = act_ref[f, j * ts:(j + 1) * ts, :]
                    ya = ya + jnp.dot(a, wda[f * tf:(f + 1) * tf, :], preferred_element_type=F32)
                    yb = yb + jnp.dot(a, wdb[f * tf:(f + 1) * tf, :], preferred_element_type=F32)
                o_ref[j * ts:(j + 1) * ts, :] = _pack_bf16_pairs(jnp.concatenate([ya, yb], axis=1))

            @pl.when(cnt <= j * ts)
            def _():
                o_ref[j * ts:(j + 1) * ts, :] = jnp.zeros((ts, o_ref.shape[1]), o_ref.dtype)


def _moe_ffn(blk_e, blk_cnt, blk_row, xs, w_gu, w_dn, b_gu, b_dn, *, tm, ts, tf, tn):
    n_rows = xs.shape[0]
    n_exp, d, f2 = w_gu.shape
    d_ff = f2 // 2
    nf = d_ff // tf
    nn = d // tn
    n_blk = n_rows // tm
    b_gu3 = b_gu.reshape(n_exp, 1, f2)
    b_dn3 = b_dn.reshape(n_exp, 1, d)

    def fi(s, bc, b):
        return jnp.where(bc[b] > 0, jnp.minimum(s, nf - 1), nf - 1)

    nh = nn // 2

    def ni(s, bc, b):
        return jnp.where(bc[b] > 0, jnp.clip(s - nf, 0, nh - 1), nh - 1)

    grid_spec = pltpu.PrefetchScalarGridSpec(
        num_scalar_prefetch=3,
        grid=(n_blk, nf + nh),
        in_specs=[
            pl.BlockSpec((tm, d // 2), lambda b, s, be, bc, rb: (rb[b], 0)),
            pl.BlockSpec((None, d, tf), lambda b, s, be, bc, rb: (be[b], 0, fi(s, bc, b))),
            pl.BlockSpec((None, d, tf), lambda b, s, be, bc, rb: (be[b], 0, nf + fi(s, bc, b))),
            pl.BlockSpec((None, d_ff, tn), lambda b, s, be, bc, rb: (be[b], 0, ni(s, bc, b))),
            pl.BlockSpec((None, d_ff, tn), lambda b, s, be, bc, rb: (be[b], 0, nh + ni(s, bc, b))),
            pl.BlockSpec((None, 1, tf), lambda b, s, be, bc, rb: (be[b], 0, fi(s, bc, b))),
            pl.BlockSpec((None, 1, tf), lambda b, s, be, bc, rb: (be[b], 0, nf + fi(s, bc, b))),
            pl.BlockSpec((None, 1, tn), lambda b, s, be, bc, rb: (be[b], 0, ni(s, bc, b))),
            pl.BlockSpec((None, 1, tn), lambda b, s, be, bc, rb: (be[b], 0, nh + ni(s, bc, b))),
        ],
        out_specs=pl.BlockSpec((tm, tn), lambda b, s, be, bc, rb: (rb[b], ni(s, bc, b))),
        scratch_shapes=[pltpu.VMEM((nf, tm, tf), BF16)],
    )
    return pl.pallas_call(
        functools.partial(_moe_kernel, nf=nf, ts=ts),
        grid_spec=grid_spec,
        out_shape=jax.ShapeDtypeStruct((n_rows, d // 2), jnp.int32),
        compiler_params=_cparams(("arbitrary", "arbitrary")),
        name="moe_ffn",
    )(blk_e, blk_cnt, blk_row, xs, w_gu, w_gu, w_dn, w_dn, b_gu3, b_gu3, b_dn3, b_dn3)


def _final_kernel(h_ref, y0_ref, y1_ref, y2_ref, y3_ref, p_ref, g_ref, o_ref):
    p = p_ref[...]
    y = None
    for k, y_ref in enumerate((y0_ref, y1_ref, y2_ref, y3_ref)):
        t = _unpack_bf16_pairs(y_ref[...]).astype(F32) * p[:, k:k + 1]
        y = t if y is None else y + t
    h = h_ref[...] + y
    ms = jnp.mean(h * h, axis=-1, keepdims=True)
    o_ref[...] = (h * lax.rsqrt(ms + RMS_EPS)) * g_ref[...]


def _final(h1, y_km, gates, g, *, tm):
    m, d = h1.shape
    nb = m // tm
    assert TOP_K == 4
    y_specs = [pl.BlockSpec((tm, d // 2), functools.partial(lambda i, k: (k * nb + i, 0), k=k))
               for k in range(TOP_K)]
    return pl.pallas_call(
        _final_kernel,
        grid=(nb,),
        in_specs=([pl.BlockSpec((tm, d), lambda i: (i, 0))] + y_specs
                  + [pl.BlockSpec((tm, TOP_K), lambda i: (i, 0)), pl.BlockSpec((1, d), lambda i: (0, 0))]),
        out_specs=pl.BlockSpec((tm, d), lambda i: (i, 0)),
        out_shape=jax.ShapeDtypeStruct((m, d), F32),
        compiler_params=_cparams(("parallel",)),
        name="final_norm",
    )(h1, y_km, y_km, y_km, y_km, gates, g)


def _routing(top_e, counts, tm):
    n_exp = counts.shape[0]
    n_assign = top_e.shape[0] * TOP_K
    flat_e = top_e.reshape(-1)
    order = jnp.argsort(flat_e, stable=True).astype(jnp.int32)
    rank = jnp.argsort(order).astype(jnp.int32)
    padded = (counts + tm - 1) // tm * tm
    pad_end = jnp.cumsum(padded).astype(jnp.int32)
    pad_start = pad_end - padded
    grp_start = jnp.cumsum(counts).astype(jnp.int32) - counts
    n_blk = -(-n_assign // tm) + n_exp
    blk_start = jnp.arange(n_blk, dtype=jnp.int32) * tm
    blk_e = jnp.minimum(jnp.sum((blk_start[:, None] >= pad_end[None, :]).astype(jnp.int32), axis=1), n_exp - 1)
    blk_off = blk_start - pad_start[blk_e]
    blk_cnt = jnp.clip(counts[blk_e] - blk_off, 0, tm).astype(jnp.int32)
    n_tok = top_e.shape[0]
    tok_sorted = jnp.concatenate([order // TOP_K, jnp.arange(tm, dtype=jnp.int32) % n_tok])
    win_start = jnp.clip(grp_start[blk_e] + blk_off, 0, n_assign)
    row_tok = jax.vmap(lambda st: lax.dynamic_slice(tok_sorted, (st,), (tm,)))(win_start)
    spread = (blk_start[:, None] + jnp.arange(tm, dtype=jnp.int32)[None, :]) % n_tok
    row_tok = jnp.where(blk_cnt[:, None] > 0, row_tok, spread).reshape(-1)
    delta = pad_start - grp_start
    e2d = flat_e.reshape(-1, LANES)
    hit = e2d[None] == jnp.arange(n_exp, dtype=jnp.int32)[:, None, None]
    dest_flat = rank + jnp.sum(jnp.where(hit, delta[:, None, None], 0), axis=0).reshape(-1)
    last_used = jnp.maximum(pad_end[-1] // tm - 1, 0)
    blk_e = jnp.where(blk_cnt > 0, blk_e, blk_e[last_used]).astype(jnp.int32)
    blk_row = jnp.where(blk_cnt > 0, jnp.arange(n_blk, dtype=jnp.int32), last_used).astype(jnp.int32)
    return row_tok, dest_flat, blk_e, blk_cnt, blk_row


def kernel(x, meta_tokens, g_mix, w_in, lam_q1, lam_k1, lam_q2, lam_k2, g_sb_out, g_da_out, w_out, g_ffn,
           w_router, b_router, w_gate_up, b_gate_up, w_down, b_down, g_final):
    b, s, d = x.shape
    depth = w_in.shape[0]
    assert depth == 1, "single-layer trunk"
    layer = 0
    sb_heads = (d // 2) // SB_HEAD_DIM
    da_heads = (d // 2) // DA_V_DIM
    sb_w = sb_heads * SB_HEAD_DIM
    da_col0 = 3 * sb_w
    da_qk_w = da_heads * 2 * DA_HEAD_DIM
    n_exp = w_router.shape[-1]
    lam_init = 0.8 - 0.6 * math.exp(-0.3 * layer)

    tm_proj, tn_proj = TM_PROJ, TN_PROJ
    tq, tk_sb, ts_da = TQ_ATTN, TK_SB, TS_DA
    tm_out = TM_OUT
    tm_moe, tf_moe = TM_MOE, TF_MOE
    tm_fin = TM_FINAL

    x2d = x.reshape(b * s, d)
    w_in_bf = w_in.reshape(w_in.shape[1:]).astype(BF16)
    g_mix2 = g_mix[layer].reshape(1, d)
    rope_cols = (da_col0, da_col0 + 2 * da_qk_w)

    pos_f = N_META + jnp.arange(s, dtype=jnp.int32)
    proj = _in_proj(x2d, g_mix2, w_in_bf, _rope_tables(pos_f), tm=tm_proj, tn=tn_proj,
                    rope_cols=rope_cols, pos_blocks=s // tm_proj)
    meta_pad = jnp.zeros((META_PAD, d), x.dtype).at[:N_META].set(meta_tokens.astype(x.dtype))
    pos_m = jnp.arange(META_PAD, dtype=jnp.int32)
    proj_meta = _in_proj(meta_pad, g_mix2, w_in_bf, _rope_tables(pos_m), tm=META_PAD, tn=tn_proj,
                         rope_cols=rope_cols, pos_blocks=1)
    proj3 = proj.reshape(b, s, -1)

    mix_sb = _sb_attention(proj3, proj_meta, g_sb_out[layer].reshape(1, -1), batch=b, seq=s, heads=sb_heads,
                           tq=tq, tk=tk_sb)
    lams = tuple(t[layer].reshape(1, DA_HEAD_DIM).astype(F32) for t in (lam_q1, lam_k1, lam_q2, lam_k2))
    mix_da = _da_attention(proj3, proj_meta, lams, g_da_out[layer].reshape(1, -1), batch=b, seq=s, heads=da_heads,
                           tq=TQ_DA, tkv=TKV_DA, ts=ts_da, col0=da_col0, lam_init=lam_init)

    w_out_bf = w_out[layer].astype(BF16)
    wr = w_router[layer]
    wr_hi = wr.astype(BF16)
    wr_lo = (wr - wr_hi.astype(F32)).astype(BF16)
    h1, u_packed, top_e, gates, tile_counts = _out_proj(
        mix_sb.reshape(b * s, -1), mix_da.reshape(b * s, -1), w_out_bf[:sb_w], w_out_bf[sb_w:], x2d,
        g_ffn[layer].reshape(1, d), wr_hi, wr_lo, b_router[layer].reshape(1, n_exp), tm=tm_out)

    counts = jnp.sum(tile_counts, axis=(0, 1)).astype(jnp.int32)
    row_tok, dest_flat, blk_e, blk_cnt, blk_row = _routing(top_e, counts, tm_moe)
    xs = _sc_gather_rows(u_packed, row_tok)
    rows = _moe_ffn(blk_e, blk_cnt, blk_row, xs, w_gate_up.reshape(w_gate_up.shape[1:]),
                    w_down.reshape(w_down.shape[1:]), b_gate_up[layer], b_down[layer],
                    tm=tm_moe, ts=TS_MOE, tf=tf_moe, tn=TN_MOE)
    dest_km = dest_flat.reshape(b * s, TOP_K).T.reshape(-1)
    y_km = _sc_gather_rows(rows, dest_km)

    out = _final(h1, y_km, gates, g_final.reshape(1, d), tm=tm_fin)
    return out.reshape(b, s, d)
```

```python
import functools
import math

import jax
import jax.numpy as jnp
from jax import lax
from jax.experimental import pallas as pl
from jax.experimental.pallas import tpu as pltpu
from jax.experimental.pallas import tpu_sc as plsc

F32 = jnp.float32
BF16 = jnp.bfloat16

CHUNK = 64
N_META = 16
RMS_EPS = 1e-5
SB_HEAD_DIM = 128
DA_HEAD_DIM = 64
DA_V_DIM = 2 * DA_HEAD_DIM
ROPE_THETA = 500000.0
ROPE_DIM = DA_HEAD_DIM // 4
TOP_K = 4
SWIGLU_LIMIT = 7.0
SWIGLU_ALPHA = 1.702

LANES = 128
META_PAD = 128
NEG_BIG = -1e30
SB_DEAD_RUN = 110.0
VMEM_LIMIT = 56 * 1024 * 1024
SC_CORES, SC_SUBCORES = 2, 16
SC_GATHER_ROWS = 64

TM_PROJ, TN_PROJ = 1024, 1024
TQ_ATTN = 512
HEADS_PER_STEP_SB = 2
TQ_DA, TKV_DA = 1024, 512
TK_SB = 256
TS_DA = 256
TM_OUT = 512
TM_MOE, TS_MOE = 2048, 512
TF_MOE, TN_MOE = 256, 256
TM_FINAL = 512


def _cparams(sem):
    return pltpu.CompilerParams(dimension_semantics=sem, vmem_limit_bytes=VMEM_LIMIT)


def _in_proj_kernel(x_ref, g_ref, w_ref, c_ref, sa_ref, sb_ref, o_ref, u_scr, *, tn, rope_lo, rope_hi):
    n = pl.program_id(1)

    @pl.when(n == 0)
    def _():
        x = x_ref[...]
        ms = jnp.mean(x * x, axis=-1, keepdims=True)
        u_scr[...] = ((x * lax.rsqrt(ms + RMS_EPS)) * g_ref[...]).astype(BF16)

    acc = jnp.dot(u_scr[...], w_ref[...], preferred_element_type=F32)
    is_rope = jnp.logical_and(n >= rope_lo, n < rope_hi)

    @pl.when(is_rope)
    def _():
        for c in range(tn // LANES):
            xc = acc[:, c * LANES:(c + 1) * LANES]
            r = (xc * c_ref[...] + pltpu.roll(xc, LANES - ROPE_DIM // 2, 1) * sa_ref[...]
                 + pltpu.roll(xc, ROPE_DIM // 2, 1) * sb_ref[...])
            o_ref[:, c * LANES:(c + 1) * LANES] = r.astype(BF16)

    @pl.when(jnp.logical_not(is_rope))
    def _():
        o_ref[...] = acc.astype(BF16)


def _rope_tables(pos):
    half = ROPE_DIM // 2
    inv_freq = ROPE_THETA ** (-(jnp.arange(half, dtype=F32) * 2.0 / ROPE_DIM))
    ang = pos.astype(F32)[:, None] * inv_freq[None, :]
    cos, sin = jnp.cos(ang), jnp.sin(ang)
    p = pos.shape[0]
    ones = jnp.ones((p, DA_HEAD_DIM - ROPE_DIM), F32)
    zeros8 = jnp.zeros((p, half), F32)
    zeros48 = jnp.zeros((p, DA_HEAD_DIM - ROPE_DIM), F32)
    c64 = jnp.concatenate([cos, cos, ones], axis=1)
    sa64 = jnp.concatenate([-sin, zeros8, zeros48], axis=1)
    sb64 = jnp.concatenate([zeros8, sin, zeros48], axis=1)
    tile2 = lambda t: jnp.concatenate([t, t], axis=1)
    return tile2(c64), tile2(sa64), tile2(sb64)


def _in_proj(x2d, g, w_bf, tables, *, tm, tn, rope_cols, pos_blocks):
    m, d = x2d.shape
    n_cols = w_bf.shape[1]
    c_t, sa_t, sb_t = tables
    kern = functools.partial(_in_proj_kernel, tn=tn, rope_lo=rope_cols[0] // tn, rope_hi=rope_cols[1] // tn)
    tab_spec = pl.BlockSpec((tm, LANES), lambda i, n: (i % pos_blocks, 0))
    return pl.pallas_call(
        kern,
        grid=(m // tm, n_cols // tn),
        in_specs=[
            pl.BlockSpec((tm, d), lambda i, n: (i, 0)),
            pl.BlockSpec((1, d), lambda i, n: (0, 0)),
            pl.BlockSpec((d, tn), lambda i, n: (0, n)),
            tab_spec, tab_spec, tab_spec,
        ],
        out_specs=pl.BlockSpec((tm, tn), lambda i, n: (i, n)),
        out_shape=jax.ShapeDtypeStruct((m, n_cols), BF16),
        scratch_shapes=[pltpu.VMEM((tm, d), BF16)],
        compiler_params=_cparams(("parallel", "arbitrary")),
        name="in_proj",
    )(x2d, g, w_bf, c_t, sa_t, sb_t)


def _dot_nt(a, b):
    return lax.dot_general(a, b, (((1,), (1,)), ((), ())), preferred_element_type=F32)


def _suffix_sum_matrix(n):
    j = lax.broadcasted_iota(jnp.int32, (2 * n, n), 0)
    s = lax.broadcasted_iota(jnp.int32, (2 * n, n), 1)
    return jnp.where(jnp.where(j >= n, j - n, j) > s, 1.0, 0.0).astype(BF16)


def _sb_block(q, kb, vb, u2, run, acc, mask, scale):
    z = _dot_nt(q, kb) * scale
    sp = jnp.maximum(z, 0.0) + jnp.log(1.0 + jnp.exp(-jnp.abs(z)))
    if mask is not None:
        sp = jnp.where(mask, sp, 0.0)
    hi = sp.astype(BF16)
    lo = (sp - hi.astype(F32)).astype(BF16)
    cs = jnp.dot(jnp.concatenate([hi, lo], axis=1), u2, preferred_element_type=F32)
    w = jnp.exp(z - sp - cs - run)
    if mask is not None:
        w = jnp.where(mask, w, 0.0)
    acc = acc + jnp.dot(w.astype(BF16), vb, preferred_element_type=F32)
    run = run + (cs[:, :1] + sp[:, :1])
    return run, acc


def _sb_kernel(q_ref, k_ref, v_ref, km_ref, vm_ref, g_ref, o_ref, *, tq, tk, hp, scale):
    i = pl.program_id(2)
    nsub = tq // tk
    hd = SB_HEAD_DIM
    u_blk = _suffix_sum_matrix(tk)
    row = lax.broadcasted_iota(jnp.int32, (tk, tk), 0)
    col = lax.broadcasted_iota(jnp.int32, (tk, tk), 1)
    diag_mask = col < row
    heads = range(hp)

    def kv(blk, h):
        s0 = pl.multiple_of(blk * tk, tk)
        return k_ref[pl.ds(s0, tk), h * hd:(h + 1) * hd], v_ref[pl.ds(s0, tk), h * hd:(h + 1) * hd]

    runs, accs = [], []
    for h in heads:
        slab_runs, slab_accs = [], []
        for sl in range(nsub):
            qs = q_ref[sl * tk:(sl + 1) * tk, h * hd:(h + 1) * hd]
            run = jnp.zeros((tk, 1), F32)
            acc = jnp.zeros((tk, hd), F32)
            for c in range(sl, -1, -1):
                kb, vb = kv(i * nsub + c, h)
                run, acc = _sb_block(qs, kb, vb, u_blk, run, acc, diag_mask if c == sl else None, scale)
            slab_runs.append(run)
            slab_accs.append(acc)
        runs.append(jnp.concatenate(slab_runs, axis=0))
        accs.append(jnp.concatenate(slab_accs, axis=0))
    q = [q_ref[:, h * hd:(h + 1) * hd] for h in heads]

    def alive_flag(rs):
        lowest = functools.reduce(jnp.minimum, [jnp.min(r) for r in rs])
        return (lowest < SB_DEAD_RUN).astype(jnp.int32)

    def cond(carry):
        jb, alive, _, _ = carry
        return jnp.logical_and(jb >= 0, alive > 0)

    def body(carry):
        jb, _, rs, as_ = carry
        out = [_sb_block(q[h], *kv(jb, h), u_blk, rs[h], as_[h], None, scale) for h in heads]
        rs = tuple(o[0] for o in out)
        return jb - 1, alive_flag(rs), rs, tuple(o[1] for o in out)

    runs, accs = tuple(runs), tuple(accs)
    _, alive, runs, accs = lax.while_loop(cond, body, (i * nsub - 1, alive_flag(runs), runs, accs))

    def meta_block():
        mcol = lax.broadcasted_iota(jnp.int32, (tq, META_PAD), 1)
        u_meta = _suffix_sum_matrix(META_PAD)
        return tuple(_sb_block(q[h], km_ref[:, h * hd:(h + 1) * hd], vm_ref[:, h * hd:(h + 1) * hd], u_meta,
                               runs[h], accs[h], mcol < N_META, scale)[1] for h in heads)

    accs = lax.cond(alive > 0, meta_block, lambda: accs)

    for h in heads:
        acc = accs[h]
        ms = jnp.mean(acc * acc, axis=-1, keepdims=True)
        gain = g_ref[:, h * hd:(h + 1) * hd]
        o_ref[:, h * hd:(h + 1) * hd] = ((acc * lax.rsqrt(ms + RMS_EPS)) * gain).astype(BF16)


def _sb_attention(proj, proj_meta, g_sb, *, batch, seq, heads, tq, tk, hp):
    hd = SB_HEAD_DIM
    wd = hp * hd
    ng = heads // hp
    kern = functools.partial(_sb_kernel, tq=tq, tk=tk, hp=hp, scale=hd ** -0.5)
    return pl.pallas_call(
        kern,
        grid=(batch, ng, seq // tq),
        in_specs=[
            pl.BlockSpec((None, tq, wd), lambda b, h, i: (b, i, h)),
            pl.BlockSpec((None, seq, wd), lambda b, h, i: (b, 0, ng + h)),
            pl.BlockSpec((None, seq, wd), lambda b, h, i: (b, 0, 2 * ng + h)),
            pl.BlockSpec((META_PAD, wd), lambda b, h, i: (0, ng + h)),
            pl.BlockSpec((META_PAD, wd), lambda b, h, i: (0, 2 * ng + h)),
            pl.BlockSpec((1, wd), lambda b, h, i: (0, h)),
        ],
        out_specs=pl.BlockSpec((None, tq, wd), lambda b, h, i: (b, i, h)),
        out_shape=jax.ShapeDtypeStruct((batch, seq, heads * hd), BF16),
        compiler_params=_cparams(("parallel", "parallel", "arbitrary")),
        name="sb_attn",
    )(proj, proj, proj, proj_meta, proj_meta, g_sb)


def _with_ones(vb):
    return jnp.concatenate([vb, jnp.ones_like(vb)], axis=1)


def _da_block(q1, q2, kb, vb1, st, mask):
    m1, a1, m2, a2 = st

    def one(qc, m, a):
        s = _dot_nt(qc, kb)
        if mask is not None:
            s = jnp.where(mask, s, NEG_BIG)
        mn = jnp.maximum(m, jnp.max(s, axis=-1, keepdims=True))
        p = jnp.exp(s - mn)
        a = jnp.exp(m - mn) * a + jnp.dot(p.astype(BF16), vb1, preferred_element_type=F32)
        return mn, a

    m1, a1 = one(q1, m1, a1)
    m2, a2 = one(q2, m2, a2)
    return m1, a1, m2, a2


def _da_kernel(lq1_ref, lk1_ref, lq2_ref, lk2_ref, q_ref, k_ref, v_ref, km_ref, vm_ref, g_ref, o_ref,
               *, tq, tkv, ts, scale, lam_init):
    i = pl.program_id(2)
    nsub = tq // ts
    lam = (jnp.exp(jnp.sum(lq1_ref[...] * lk1_ref[...], axis=-1, keepdims=True))
           - jnp.exp(jnp.sum(lq2_ref[...] * lk2_ref[...], axis=-1, keepdims=True)) + lam_init)

    q = q_ref[...]
    lane = lax.broadcasted_iota(jnp.int32, q.shape, 1)
    qs = q * jnp.asarray(scale, BF16)
    zero = jnp.zeros_like(qs)
    q1 = jnp.where(lane < DA_HEAD_DIM, qs, zero)
    q2 = jnp.where(lane >= DA_HEAD_DIM, qs, zero)

    neg = jnp.full((tq, 1), NEG_BIG, F32)
    za = jnp.zeros((tq, 2 * DA_V_DIM), F32)
    st = (neg, za, neg, za)

    def body(j, carry):
        for c in range(tq // tkv):
            s0 = pl.multiple_of(j * tq + c * tkv, tkv)
            carry = _da_block(q1, q2, k_ref[pl.ds(s0, tkv), :], _with_ones(v_ref[pl.ds(s0, tkv), :]), carry, None)
        return carry

    st = lax.fori_loop(0, i, body, st)

    shift = CHUNK.bit_length() - 1
    s0 = pl.multiple_of(i * tq, tq)
    outs = []
    for h in range(nsub):
        rows = slice(h * ts, (h + 1) * ts)
        width = (h + 1) * ts
        kb = jnp.concatenate([km_ref[...], k_ref[pl.ds(s0, width), :]], axis=0)
        vb = jnp.concatenate([vm_ref[...], v_ref[pl.ds(s0, width), :]], axis=0)
        col = lax.broadcasted_iota(jnp.int32, (ts, META_PAD + width), 1)
        row = lax.broadcasted_iota(jnp.int32, (ts, META_PAD + width), 0) + h * ts
        frame_vis = jnp.logical_and(col >= META_PAD,
                                    jnp.right_shift(col - META_PAD, shift) <= jnp.right_shift(row, shift))
        vis = jnp.logical_or(col < N_META, frame_vis)
        sth = _da_block(q1[rows], q2[rows], kb, _with_ones(vb), tuple(t[rows] for t in st), vis)
        _, a1, _, a2 = sth
        outs.append(a1[:, :DA_V_DIM] / a1[:, DA_V_DIM:] - lam * (a2[:, :DA_V_DIM] / a2[:, DA_V_DIM:]))
    o = jnp.concatenate(outs, axis=0)
    ms = jnp.mean(o * o, axis=-1, keepdims=True)
    o_ref[...] = (((o * lax.rsqrt(ms + RMS_EPS)) * g_ref[...]) * (1.0 - lam_init)).astype(BF16)


def _da_attention(proj, proj_meta, lams, g_da, *, batch, seq, heads, tq, tkv, ts, col0, lam_init):
    hd = DA_V_DIM
    qb, kb, vb = col0 // hd, col0 // hd + heads, col0 // hd + 2 * heads
    kern = functools.partial(_da_kernel, tq=tq, tkv=tkv, ts=ts, scale=DA_HEAD_DIM ** -0.5, lam_init=lam_init)
    lam_spec = pl.BlockSpec((1, DA_HEAD_DIM), lambda b, h, i: (0, 0))
    return pl.pallas_call(
        kern,
        grid=(batch, heads, seq // tq),
        in_specs=[
            lam_spec, lam_spec, lam_spec, lam_spec,
            pl.BlockSpec((None, tq, hd), lambda b, h, i: (b, i, qb + h)),
            pl.BlockSpec((None, seq, hd), lambda b, h, i: (b, 0, kb + h)),
            pl.BlockSpec((None, seq, hd), lambda b, h, i: (b, 0, vb + h)),
            pl.BlockSpec((META_PAD, hd), lambda b, h, i: (0, kb + h)),
            pl.BlockSpec((META_PAD, hd), lambda b, h, i: (0, vb + h)),
            pl.BlockSpec((1, hd), lambda b, h, i: (0, h)),
        ],
        out_specs=pl.BlockSpec((None, tq, hd), lambda b, h, i: (b, i, h)),
        out_shape=jax.ShapeDtypeStruct((batch, seq, heads * hd), BF16),
        compiler_params=_cparams(("parallel", "parallel", "arbitrary")),
        name="da_attn",
    )(*lams, proj, proj, proj, proj_meta, proj_meta, g_da)


def _pack_bf16_pairs(x):
    n = x.shape[1] // 2
    lo = lax.bitcast_convert_type(x[:, :n].astype(BF16).astype(F32), jnp.int32)
    hi = lax.bitcast_convert_type(x[:, n:].astype(BF16).astype(F32), jnp.int32)
    return jnp.bitwise_or(lax.shift_right_logical(lo, 16), jnp.bitwise_and(hi, jnp.int32(-65536)))


def _unpack_bf16_pairs(w):
    lo = lax.bitcast_convert_type(lax.shift_left(w, 16), F32)
    hi = lax.bitcast_convert_type(jnp.bitwise_and(w, jnp.int32(-65536)), F32)
    return jnp.concatenate([lo.astype(BF16), hi.astype(BF16)], axis=1)


def _sc_gather_rows(table, idx):
    n_idx = idx.shape[0]
    width = table.shape[1]
    n_workers = SC_CORES * SC_SUBCORES
    per_worker = n_idx // n_workers
    win = SC_GATHER_ROWS
    assert n_idx % (n_workers * win) == 0
    mesh = plsc.VectorSubcoreMesh(core_axis_name="c", subcore_axis_name="s")

    @functools.partial(
        pl.kernel, mesh=mesh,
        out_type=jax.ShapeDtypeStruct((n_idx, width), table.dtype),
        scratch_types=[pltpu.VMEM((win,), jnp.int32), pltpu.VMEM((win, width), table.dtype),
                       pltpu.SemaphoreType.DMA],
    )
    def gather(table_hbm, idx_hbm, out_hbm, idx_v, rows_v, sem):
        worker = lax.axis_index("s") * SC_CORES + lax.axis_index("c")
        base = worker * per_worker

        @pl.loop(0, per_worker // win)
        def _(t):
            off = pl.multiple_of(base + t * win, win)
            pltpu.sync_copy(idx_hbm.at[pl.ds(off, win)], idx_v)
            pltpu.async_copy(table_hbm.at[idx_v], rows_v, sem).wait()
            pltpu.sync_copy(rows_v, out_hbm.at[pl.ds(off, win)])

    return gather(table, idx)


def _split3_dot(a, b_hi, b_lo):
    a_hi = a.astype(BF16)
    a_lo = (a - a_hi.astype(F32)).astype(BF16)
    return (jnp.dot(a_hi, b_hi, preferred_element_type=F32)
            + jnp.dot(a_hi, b_lo, preferred_element_type=F32)
            + jnp.dot(a_lo, b_hi, preferred_element_type=F32))


def _out_proj_kernel(ms_ref, md_ref, ws_ref, wd_ref, x_ref, g_ref, wr_hi_ref, wr_lo_ref, br_ref,
                     h_ref, u_ref, e_ref, p_ref, c_ref, *, n_exp):
    h = (x_ref[...]
         + jnp.dot(ms_ref[...], ws_ref[...], preferred_element_type=F32)
         + jnp.dot(md_ref[...], wd_ref[...], preferred_element_type=F32))
    h_ref[...] = h
    msq = jnp.mean(h * h, axis=-1, keepdims=True)
    u = (h * lax.rsqrt(msq + RMS_EPS)) * g_ref[...]
    u_ref[...] = _pack_bf16_pairs(u)

    logits = _split3_dot(u, wr_hi_ref[...], wr_lo_ref[...]) + br_ref[...]
    lane = lax.broadcasted_iota(jnp.int32, logits.shape, 1).astype(F32)
    work = logits
    tops, idxs = [], []
    for _ in range(TOP_K):
        mx = jnp.max(work, axis=-1, keepdims=True)
        ix = jnp.min(jnp.where(work == mx, lane, float(n_exp)), axis=-1, keepdims=True)
        tops.append(mx)
        idxs.append(ix)
        work = jnp.where(lane == ix, -jnp.inf, work)
    ex = [jnp.exp(t - tops[0]) for t in tops]
    den = ex[0] + ex[1] + ex[2] + ex[3]
    kl = lax.broadcasted_iota(jnp.int32, (logits.shape[0], TOP_K), 1)
    e_out = jnp.zeros((logits.shape[0], TOP_K), F32)
    p_out = jnp.zeros((logits.shape[0], TOP_K), F32)
    for k in range(TOP_K):
        e_out = jnp.where(kl == k, idxs[k], e_out)
        p_out = jnp.where(kl == k, ex[k] / den, p_out)
    e_ref[...] = e_out.astype(jnp.int32)
    p_ref[...] = p_out
    hits = jnp.zeros(logits.shape, F32)
    for k in range(TOP_K):
        hits = hits + jnp.where(lane == idxs[k], 1.0, 0.0)
    c_ref[...] = jnp.sum(hits, axis=0, keepdims=True)


def _out_proj(mix_sb, mix_da, w_sb, w_da, x2d, g_ffn, wr_hi, wr_lo, b_r, *, tm):
    m, d = x2d.shape
    ks, kd = mix_sb.shape[1], mix_da.shape[1]
    n_exp = wr_hi.shape[1]
    kern = functools.partial(_out_proj_kernel, n_exp=n_exp)
    const = lambda shape: pl.BlockSpec(shape, lambda i: (0, 0))
    return pl.pallas_call(
        kern,
        grid=(m // tm,),
        in_specs=[
            pl.BlockSpec((tm, ks), lambda i: (i, 0)),
            pl.BlockSpec((tm, kd), lambda i: (i, 0)),
            const((ks, d)), const((kd, d)),
            pl.BlockSpec((tm, d), lambda i: (i, 0)),
            const((1, d)), const((d, n_exp)), const((d, n_exp)), const((1, n_exp)),
        ],
        out_specs=[
            pl.BlockSpec((tm, d), lambda i: (i, 0)),
            pl.BlockSpec((tm, d // 2), lambda i: (i, 0)),
            pl.BlockSpec((tm, TOP_K), lambda i: (i, 0)),
            pl.BlockSpec((tm, TOP_K), lambda i: (i, 0)),
            pl.BlockSpec((None, 1, n_exp), lambda i: (i, 0, 0)),
        ],
        out_shape=[
            jax.ShapeDtypeStruct((m, d), F32),
            jax.ShapeDtypeStruct((m, d // 2), jnp.int32),
            jax.ShapeDtypeStruct((m, TOP_K), jnp.int32),
            jax.ShapeDtypeStruct((m, TOP_K), F32),
            jax.ShapeDtypeStruct((m // tm, 1, n_exp), F32),
        ],
        compiler_params=_cparams(("parallel",)),
        name="out_proj_router",
    )(mix_sb, mix_da, w_sb, w_da, x2d, g_ffn, wr_hi, wr_lo, b_r)


def _moe_kernel(be_ref, bc_ref, rb_ref, x_ref, wg_ref, wu_ref, wda_ref, wdb_ref, bg_ref, bup_ref, bda_ref, bdb_ref,
                o_ref, act_ref, *, nf, ts):
    blk = pl.program_id(0)
    s = pl.program_id(1)
    cnt = bc_ref[blk]
    tg = 2 * ts
    ngrp = x_ref.shape[0] // tg

    def row_groups(run, skip):
        for g in range(ngrp):
            lo = g * tg

            @pl.when(cnt > lo + ts)
            def _():
                run(lo, tg)

            @pl.when(jnp.logical_and(cnt > lo, cnt <= lo + ts))
            def _():
                run(lo, ts)
                skip(lo + ts, ts)

            @pl.when(cnt <= lo)
            def _():
                skip(lo, tg)

    @pl.when(jnp.logical_and(cnt > 0, s < nf))
    def _():
        wg = wg_ref[...].astype(BF16)
        wu = wu_ref[...].astype(BF16)
        slot = jnp.minimum(s, nf - 1)

        def up_proj(r0, nrows):
            x = _unpack_bf16_pairs(x_ref[r0:r0 + nrows, :])
            g = jnp.dot(x, wg, preferred_element_type=F32) + bg_ref[...]
            u = jnp.dot(x, wu, preferred_element_type=F32) + bup_ref[...]
            gate = jnp.minimum(g, SWIGLU_LIMIT)
            up = jnp.clip(u, -SWIGLU_LIMIT, SWIGLU_LIMIT)
            act = (up + 1.0) * gate * jax.nn.sigmoid(SWIGLU_ALPHA * gate)
            act_ref[slot, r0:r0 + nrows, :] = act.astype(BF16)

        row_groups(up_proj, lambda r0, nrows: None)

    @pl.when(jnp.logical_and(cnt > 0, s >= nf))
    def _():
        tf = act_ref.shape[2]
        wda = wda_ref[...].astype(BF16)
        wdb = wdb_ref[...].astype(BF16)

        def down_proj(r0, nrows):
            ya = bda_ref[...]
            yb = bdb_ref[...]
            for f in range(nf):
                a = act_ref[f, r0:r0 + nrows, :]
                ya = ya + jnp.dot(a, wda[f * tf:(f + 1) * tf, :], preferred_element_type=F32)
                yb = yb + jnp.dot(a, wdb[f * tf:(f + 1) * tf, :], preferred_element_type=F32)
            o_ref[r0:r0 + nrows, :] = _pack_bf16_pairs(jnp.concatenate([ya, yb], axis=1))

        def zero_fill(r0, nrows):
            o_ref[r0:r0 + nrows, :] = jnp.zeros((nrows, o_ref.shape[1]), o_ref.dtype)

        row_groups(down_proj, zero_fill)


def _moe_ffn(blk_e, blk_cnt, blk_row, xs, w_gu, w_dn, b_gu, b_dn, *, tm, ts, tf, tn):
    n_rows = xs.shape[0]
    n_exp, d, f2 = w_gu.shape
    d_ff = f2 // 2
    nf = d_ff // tf
    nn = d // tn
    n_blk = n_rows // tm
    b_gu3 = b_gu.reshape(n_exp, 1, f2)
    b_dn3 = b_dn.reshape(n_exp, 1, d)

    def fi(s, bc, b):
        return jnp.where(bc[b] > 0, jnp.minimum(s, nf - 1), nf - 1)

    nh = nn // 2

    def ni(s, bc, b):
        return jnp.where(bc[b] > 0, jnp.clip(s - nf, 0, nh - 1), nh - 1)

    grid_spec = pltpu.PrefetchScalarGridSpec(
        num_scalar_prefetch=3,
        grid=(n_blk, nf + nh),
        in_specs=[
            pl.BlockSpec((tm, d // 2), lambda b, s, be, bc, rb: (rb[b], 0)),
            pl.BlockSpec((None, d, tf), lambda b, s, be, bc, rb: (be[b], 0, fi(s, bc, b))),
            pl.BlockSpec((None, d, tf), lambda b, s, be, bc, rb: (be[b], 0, nf + fi(s, bc, b))),
            pl.BlockSpec((None, d_ff, tn), lambda b, s, be, bc, rb: (be[b], 0, ni(s, bc, b))),
            pl.BlockSpec((None, d_ff, tn), lambda b, s, be, bc, rb: (be[b], 0, nh + ni(s, bc, b))),
            pl.BlockSpec((None, 1, tf), lambda b, s, be, bc, rb: (be[b], 0, fi(s, bc, b))),
            pl.BlockSpec((None, 1, tf), lambda b, s, be, bc, rb: (be[b], 0, nf + fi(s, bc, b))),
            pl.BlockSpec((None, 1, tn), lambda b, s, be, bc, rb: (be[b], 0, ni(s, bc, b))),
            pl.BlockSpec((None, 1, tn), lambda b, s, be, bc, rb: (be[b], 0, nh + ni(s, bc, b))),
        ],
        out_specs=pl.BlockSpec((tm, tn), lambda b, s, be, bc, rb: (rb[b], ni(s, bc, b))),
        scratch_shapes=[pltpu.VMEM((nf, tm, tf), BF16)],
    )
    return pl.pallas_call(
        functools.partial(_moe_kernel, nf=nf, ts=ts),
        grid_spec=grid_spec,
        out_shape=jax.ShapeDtypeStruct((n_rows, d // 2), jnp.int32),
        compiler_params=_cparams(("arbitrary", "arbitrary")),
        name="moe_ffn",
    )(blk_e, blk_cnt, blk_row, xs, w_gu, w_gu, w_dn, w_dn, b_gu3, b_gu3, b_dn3, b_dn3)


def _final_kernel(h_ref, y0_ref, y1_ref, y2_ref, y3_ref, p_ref, g_ref, o_ref):
    p = p_ref[...]
    y = None
    for k, y_ref in enumerate((y0_ref, y1_ref, y2_ref, y3_ref)):
        t = _unpack_bf16_pairs(y_ref[...]).astype(F32) * p[:, k:k + 1]
        y = t if y is None else y + t
    h = h_ref[...] + y
    ms = jnp.mean(h * h, axis=-1, keepdims=True)
    o_ref[...] = (h * lax.rsqrt(ms + RMS_EPS)) * g_ref[...]


def _final(h1, y_km, gates, g, *, tm):
    m, d = h1.shape
    nb = m // tm
    assert TOP_K == 4
    y_specs = [pl.BlockSpec((tm, d // 2), functools.partial(lambda i, k: (k * nb + i, 0), k=k))
               for k in range(TOP_K)]
    return pl.pallas_call(
        _final_kernel,
        grid=(nb,),
        in_specs=([pl.BlockSpec((tm, d), lambda i: (i, 0))] + y_specs
                  + [pl.BlockSpec((tm, TOP_K), lambda i: (i, 0)), pl.BlockSpec((1, d), lambda i: (0, 0))]),
        out_specs=pl.BlockSpec((tm, d), lambda i: (i, 0)),
        out_shape=jax.ShapeDtypeStruct((m, d), F32),
        compiler_params=_cparams(("parallel",)),
        name="final_norm",
    )(h1, y_km, y_km, y_km, y_km, gates, g)


def _routing(top_e, counts, tm):
    n_exp = counts.shape[0]
    n_assign = top_e.shape[0] * TOP_K
    flat_e = top_e.reshape(-1)
    order = jnp.argsort(flat_e, stable=True).astype(jnp.int32)
    rank = jnp.argsort(order).astype(jnp.int32)
    padded = (counts + tm - 1) // tm * tm
    pad_end = jnp.cumsum(padded).astype(jnp.int32)
    pad_start = pad_end - padded
    grp_start = jnp.cumsum(counts).astype(jnp.int32) - counts
    n_blk = -(-n_assign // tm) + n_exp
    blk_start = jnp.arange(n_blk, dtype=jnp.int32) * tm
    blk_e = jnp.minimum(jnp.sum((blk_start[:, None] >= pad_end[None, :]).astype(jnp.int32), axis=1), n_exp - 1)
    blk_off = blk_start - pad_start[blk_e]
    blk_cnt = jnp.clip(counts[blk_e] - blk_off, 0, tm).astype(jnp.int32)
    n_tok = top_e.shape[0]
    tok_sorted = jnp.concatenate([order // TOP_K, jnp.arange(tm, dtype=jnp.int32) % n_tok])
    win_start = jnp.clip(grp_start[blk_e] + blk_off, 0, n_assign)
    row_tok = jax.vmap(lambda st: lax.dynamic_slice(tok_sorted, (st,), (tm,)))(win_start)
    spread = (blk_start[:, None] + jnp.arange(tm, dtype=jnp.int32)[None, :]) % n_tok
    row_tok = jnp.where(blk_cnt[:, None] > 0, row_tok, spread).reshape(-1)
    delta = pad_start - grp_start
    e2d = flat_e.reshape(-1, LANES)
    hit = e2d[None] == jnp.arange(n_exp, dtype=jnp.int32)[:, None, None]
    dest_flat = rank + jnp.sum(jnp.where(hit, delta[:, None, None], 0), axis=0).reshape(-1)
    last_used = jnp.maximum(pad_end[-1] // tm - 1, 0)
    blk_e = jnp.where(blk_cnt > 0, blk_e, blk_e[last_used]).astype(jnp.int32)
    blk_row = jnp.where(blk_cnt > 0, jnp.arange(n_blk, dtype=jnp.int32), last_used).astype(jnp.int32)
    return row_tok, dest_flat, blk_e, blk_cnt, blk_row


def kernel(x, meta_tokens, g_mix, w_in, lam_q1, lam_k1, lam_q2, lam_k2, g_sb_out, g_da_out, w_out, g_ffn,
           w_router, b_router, w_gate_up, b_gate_up, w_down, b_down, g_final):
    b, s, d = x.shape
    depth = w_in.shape[0]
    assert depth == 1, "single-layer trunk"
    layer = 0
    sb_heads = (d // 2) // SB_HEAD_DIM
    da_heads = (d // 2) // DA_V_DIM
    sb_w = sb_heads * SB_HEAD_DIM
    da_col0 = 3 * sb_w
    da_qk_w = da_heads * 2 * DA_HEAD_DIM
    n_exp = w_router.shape[-1]
    lam_init = 0.8 - 0.6 * math.exp(-0.3 * layer)

    tm_proj, tn_proj = TM_PROJ, TN_PROJ
    tq, tk_sb, ts_da = TQ_ATTN, TK_SB, TS_DA
    tm_out = TM_OUT
    tm_moe, tf_moe = TM_MOE, TF_MOE
    tm_fin = TM_FINAL

    x2d = x.reshape(b * s, d)
    w_in_bf = w_in.reshape(w_in.shape[1:]).astype(BF16)
    g_mix2 = g_mix[layer].reshape(1, d)
    rope_cols = (da_col0, da_col0 + 2 * da_qk_w)

    pos_f = N_META + jnp.arange(s, dtype=jnp.int32)
    proj = _in_proj(x2d, g_mix2, w_in_bf, _rope_tables(pos_f), tm=tm_proj, tn=tn_proj,
                    rope_cols=rope_cols, pos_blocks=s // tm_proj)
    meta_pad = jnp.zeros((META_PAD, d), x.dtype).at[:N_META].set(meta_tokens.astype(x.dtype))
    pos_m = jnp.arange(META_PAD, dtype=jnp.int32)
    proj_meta = _in_proj(meta_pad, g_mix2, w_in_bf, _rope_tables(pos_m), tm=META_PAD, tn=tn_proj,
                         rope_cols=rope_cols, pos_blocks=1)
    proj3 = proj.reshape(b, s, -1)

    mix_sb = _sb_attention(proj3, proj_meta, g_sb_out[layer].reshape(1, -1), batch=b, seq=s, heads=sb_heads,
                           tq=tq, tk=tk_sb, hp=HEADS_PER_STEP_SB)
    lams = tuple(t[layer].reshape(1, DA_HEAD_DIM).astype(F32) for t in (lam_q1, lam_k1, lam_q2, lam_k2))
    mix_da = _da_attention(proj3, proj_meta, lams, g_da_out[layer].reshape(1, -1), batch=b, seq=s, heads=da_heads,
                           tq=TQ_DA, tkv=TKV_DA, ts=ts_da, col0=da_col0, lam_init=lam_init)

    w_out_bf = w_out[layer].astype(BF16)
    wr = w_router[layer]
    wr_hi = wr.astype(BF16)
    wr_lo = (wr - wr_hi.astype(F32)).astype(BF16)
    h1, u_packed, top_e, gates, tile_counts = _out_proj(
        mix_sb.reshape(b * s, -1), mix_da.reshape(b * s, -1), w_out_bf[:sb_w], w_out_bf[sb_w:], x2d,
        g_ffn[layer].reshape(1, d), wr_hi, wr_lo, b_router[layer].reshape(1, n_exp), tm=tm_out)

    counts = jnp.sum(tile_counts, axis=(0, 1)).astype(jnp.int32)
    row_tok, dest_flat, blk_e, blk_cnt, blk_row = _routing(top_e, counts, tm_moe)
    xs = _sc_gather_rows(u_packed, row_tok)
    rows = _moe_ffn(blk_e, blk_cnt, blk_row, xs, w_gate_up.reshape(w_gate_up.shape[1:]),
                    w_down.reshape(w_down.shape[1:]), b_gate_up[layer], b_down[layer],
                    tm=tm_moe, ts=TS_MOE, tf=tf_moe, tn=TN_MOE)
    dest_km = dest_flat.reshape(b * s, TOP_K).T.reshape(-1)
    y_km = _sc_gather_rows(rows, dest_km)

    out = _final(h1, y_km, gates, g_final.reshape(1, d), tm=tm_fin)
    return out.reshape(b, s, d)
```

```python
import functools
import math

import jax
import jax.numpy as jnp
from jax import lax
from jax.experimental import pallas as pl
from jax.experimental.pallas import tpu as pltpu
from jax.experimental.pallas import tpu_sc as plsc

F32 = jnp.float32
BF16 = jnp.bfloat16

CHUNK = 64
N_META = 16
RMS_EPS = 1e-5
SB_HEAD_DIM = 128
DA_HEAD_DIM = 64
DA_V_DIM = 2 * DA_HEAD_DIM
ROPE_THETA = 500000.0
ROPE_DIM = DA_HEAD_DIM // 4
TOP_K = 4
SWIGLU_LIMIT = 7.0
SWIGLU_ALPHA = 1.702

LANES = 128
META_PAD = 128
NEG_BIG = -1e30
SB_DEAD_RUN = 110.0
VMEM_LIMIT = 56 * 1024 * 1024
SC_CORES, SC_SUBCORES = 2, 16
SC_GATHER_ROWS = 64

TM_PROJ, TN_PROJ = 1024, 1024
TQ_ATTN = 512
HEADS_PER_STEP_SB = 2
TQ_DA, TKV_DA = 1024, 512
TK_SB = 256
TS_DA = 256
TM_OUT = 512
TM_MOE, TS_MOE = 2560, 512
TF_MOE, TN_MOE = 256, 256
TM_FINAL = 512


def _cparams(sem):
    return pltpu.CompilerParams(dimension_semantics=sem, vmem_limit_bytes=VMEM_LIMIT)


def _in_proj_kernel(x_ref, g_ref, w_ref, c_ref, sa_ref, sb_ref, o_ref, u_scr, *, tn, rope_lo, rope_hi):
    n = pl.program_id(1)

    @pl.when(n == 0)
    def _():
        x = x_ref[...]
        ms = jnp.mean(x * x, axis=-1, keepdims=True)
        u_scr[...] = ((x * lax.rsqrt(ms + RMS_EPS)) * g_ref[...]).astype(BF16)

    acc = jnp.dot(u_scr[...], w_ref[...], preferred_element_type=F32)
    is_rope = jnp.logical_and(n >= rope_lo, n < rope_hi)

    @pl.when(is_rope)
    def _():
        for c in range(tn // LANES):
            xc = acc[:, c * LANES:(c + 1) * LANES]
            r = (xc * c_ref[...] + pltpu.roll(xc, LANES - ROPE_DIM // 2, 1) * sa_ref[...]
                 + pltpu.roll(xc, ROPE_DIM // 2, 1) * sb_ref[...])
            o_ref[:, c * LANES:(c + 1) * LANES] = r.astype(BF16)

    @pl.when(jnp.logical_not(is_rope))
    def _():
        o_ref[...] = acc.astype(BF16)


def _rope_tables(pos):
    half = ROPE_DIM // 2
    inv_freq = ROPE_THETA ** (-(jnp.arange(half, dtype=F32) * 2.0 / ROPE_DIM))
    ang = pos.astype(F32)[:, None] * inv_freq[None, :]
    cos, sin = jnp.cos(ang), jnp.sin(ang)
    p = pos.shape[0]
    ones = jnp.ones((p, DA_HEAD_DIM - ROPE_DIM), F32)
    zeros8 = jnp.zeros((p, half), F32)
    zeros48 = jnp.zeros((p, DA_HEAD_DIM - ROPE_DIM), F32)
    c64 = jnp.concatenate([cos, cos, ones], axis=1)
    sa64 = jnp.concatenate([-sin, zeros8, zeros48], axis=1)
    sb64 = jnp.concatenate([zeros8, sin, zeros48], axis=1)
    tile2 = lambda t: jnp.concatenate([t, t], axis=1)
    return tile2(c64), tile2(sa64), tile2(sb64)


def _in_proj(x2d, g, w_bf, tables, *, tm, tn, rope_cols, pos_blocks):
    m, d = x2d.shape
    n_cols = w_bf.shape[1]
    c_t, sa_t, sb_t = tables
    kern = functools.partial(_in_proj_kernel, tn=tn, rope_lo=rope_cols[0] // tn, rope_hi=rope_cols[1] // tn)
    tab_spec = pl.BlockSpec((tm, LANES), lambda i, n: (i % pos_blocks, 0))
    return pl.pallas_call(
        kern,
        grid=(m // tm, n_cols // tn),
        in_specs=[
            pl.BlockSpec((tm, d), lambda i, n: (i, 0)),
            pl.BlockSpec((1, d), lambda i, n: (0, 0)),
            pl.BlockSpec((d, tn), lambda i, n: (0, n)),
            tab_spec, tab_spec, tab_spec,
        ],
        out_specs=pl.BlockSpec((tm, tn), lambda i, n: (i, n)),
        out_shape=jax.ShapeDtypeStruct((m, n_cols), BF16),
        scratch_shapes=[pltpu.VMEM((tm, d), BF16)],
        compiler_params=_cparams(("parallel", "arbitrary")),
        name="in_proj",
    )(x2d, g, w_bf, c_t, sa_t, sb_t)


def _dot_nt(a, b):
    return lax.dot_general(a, b, (((1,), (1,)), ((), ())), preferred_element_type=F32)


def _suffix_sum_matrix(n):
    j = lax.broadcasted_iota(jnp.int32, (2 * n, n), 0)
    s = lax.broadcasted_iota(jnp.int32, (2 * n, n), 1)
    return jnp.where(jnp.where(j >= n, j - n, j) > s, 1.0, 0.0).astype(BF16)


def _sb_block(q, kb, vb, u2, run, acc, mask, scale):
    z = _dot_nt(q, kb) * scale
    sp = jnp.maximum(z, 0.0) + jnp.log(1.0 + jnp.exp(-jnp.abs(z)))
    if mask is not None:
        sp = jnp.where(mask, sp, 0.0)
    hi = sp.astype(BF16)
    lo = (sp - hi.astype(F32)).astype(BF16)
    cs = jnp.dot(jnp.concatenate([hi, lo], axis=1), u2, preferred_element_type=F32)
    w = jnp.exp(z - sp - cs - run)
    if mask is not None:
        w = jnp.where(mask, w, 0.0)
    acc = acc + jnp.dot(w.astype(BF16), vb, preferred_element_type=F32)
    run = run + (cs[:, :1] + sp[:, :1])
    return run, acc


def _sb_kernel(q_ref, k_ref, v_ref, km_ref, vm_ref, g_ref, o_ref, *, tq, tk, hp, scale):
    i = pl.program_id(2)
    nsub = tq // tk
    hd = SB_HEAD_DIM
    u_blk = _suffix_sum_matrix(tk)
    row = lax.broadcasted_iota(jnp.int32, (tk, tk), 0)
    col = lax.broadcasted_iota(jnp.int32, (tk, tk), 1)
    diag_mask = col < row
    heads = range(hp)

    def kv(blk, h):
        s0 = pl.multiple_of(blk * tk, tk)
        return k_ref[pl.ds(s0, tk), h * hd:(h + 1) * hd], v_ref[pl.ds(s0, tk), h * hd:(h + 1) * hd]

    runs, accs = [], []
    for h in heads:
        slab_runs, slab_accs = [], []
        for sl in range(nsub):
            qs = q_ref[sl * tk:(sl + 1) * tk, h * hd:(h + 1) * hd]
            run = jnp.zeros((tk, 1), F32)
            acc = jnp.zeros((tk, hd), F32)
            for c in range(sl, -1, -1):
                kb, vb = kv(i * nsub + c, h)
                run, acc = _sb_block(qs, kb, vb, u_blk, run, acc, diag_mask if c == sl else None, scale)
            slab_runs.append(run)
            slab_accs.append(acc)
        runs.append(jnp.concatenate(slab_runs, axis=0))
        accs.append(jnp.concatenate(slab_accs, axis=0))
    q = [q_ref[:, h * hd:(h + 1) * hd] for h in heads]

    def alive_flag(rs):
        lowest = functools.reduce(jnp.minimum, [jnp.min(r) for r in rs])
        return (lowest < SB_DEAD_RUN).astype(jnp.int32)

    def cond(carry):
        jb, alive, _, _ = carry
        return jnp.logical_and(jb >= 0, alive > 0)

    def body(carry):
        jb, _, rs, as_ = carry
        out = [_sb_block(q[h], *kv(jb, h), u_blk, rs[h], as_[h], None, scale) for h in heads]
        rs = tuple(o[0] for o in out)
        return jb - 1, alive_flag(rs), rs, tuple(o[1] for o in out)

    runs, accs = tuple(runs), tuple(accs)
    _, alive, runs, accs = lax.while_loop(cond, body, (i * nsub - 1, alive_flag(runs), runs, accs))

    def meta_block():
        mcol = lax.broadcasted_iota(jnp.int32, (tq, META_PAD), 1)
        u_meta = _suffix_sum_matrix(META_PAD)
        return tuple(_sb_block(q[h], km_ref[:, h * hd:(h + 1) * hd], vm_ref[:, h * hd:(h + 1) * hd], u_meta,
                               runs[h], accs[h], mcol < N_META, scale)[1] for h in heads)

    accs = lax.cond(alive > 0, meta_block, lambda: accs)

    for h in heads:
        acc = accs[h]
        ms = jnp.mean(acc * acc, axis=-1, keepdims=True)
        gain = g_ref[:, h * hd:(h + 1) * hd]
        o_ref[:, h * hd:(h + 1) * hd] = ((acc * lax.rsqrt(ms + RMS_EPS)) * gain).astype(BF16)


def _sb_attention(proj, proj_meta, g_sb, *, batch, seq, heads, tq, tk, hp):
    hd = SB_HEAD_DIM
    wd = hp * hd
    ng = heads // hp
    kern = functools.partial(_sb_kernel, tq=tq, tk=tk, hp=hp, scale=hd ** -0.5)
    return pl.pallas_call(
        kern,
        grid=(batch, ng, seq // tq),
        in_specs=[
            pl.BlockSpec((None, tq, wd), lambda b, h, i: (b, i, h)),
            pl.BlockSpec((None, seq, wd), lambda b, h, i: (b, 0, ng + h)),
            pl.BlockSpec((None, seq, wd), lambda b, h, i: (b, 0, 2 * ng + h)),
            pl.BlockSpec((META_PAD, wd), lambda b, h, i: (0, ng + h)),
            pl.BlockSpec((META_PAD, wd), lambda b, h, i: (0, 2 * ng + h)),
            pl.BlockSpec((1, wd), lambda b, h, i: (0, h)),
        ],
        out_specs=pl.BlockSpec((None, tq, wd), lambda b, h, i: (b, i, h)),
        out_shape=jax.ShapeDtypeStruct((batch, seq, heads * hd), BF16),
        compiler_params=_cparams(("parallel", "parallel", "arbitrary")),
        name="sb_attn",
    )(proj, proj, proj, proj_meta, proj_meta, g_sb)


def _with_ones(vb):
    return jnp.concatenate([vb, jnp.ones_like(vb)], axis=1)


def _da_block(q1, q2, kb, vb1, st, mask):
    m1, a1, m2, a2 = st

    def one(qc, m, a):
        s = _dot_nt(qc, kb)
        if mask is not None:
            s = jnp.where(mask, s, NEG_BIG)
        mn = jnp.maximum(m, jnp.max(s, axis=-1, keepdims=True))
        p = jnp.exp(s - mn)
        a = jnp.exp(m - mn) * a + jnp.dot(p.astype(BF16), vb1, preferred_element_type=F32)
        return mn, a

    m1, a1 = one(q1, m1, a1)
    m2, a2 = one(q2, m2, a2)
    return m1, a1, m2, a2


def _da_kernel(lq1_ref, lk1_ref, lq2_ref, lk2_ref, q_ref, k_ref, v_ref, km_ref, vm_ref, g_ref, o_ref,
               *, tq, tkv, ts, scale, lam_init):
    i = pl.program_id(2)
    nsub = tq // ts
    lam = (jnp.exp(jnp.sum(lq1_ref[...] * lk1_ref[...], axis=-1, keepdims=True))
           - jnp.exp(jnp.sum(lq2_ref[...] * lk2_ref[...], axis=-1, keepdims=True)) + lam_init)

    q = q_ref[...]
    lane = lax.broadcasted_iota(jnp.int32, q.shape, 1)
    qs = q * jnp.asarray(scale, BF16)
    zero = jnp.zeros_like(qs)
    q1 = jnp.where(lane < DA_HEAD_DIM, qs, zero)
    q2 = jnp.where(lane >= DA_HEAD_DIM, qs, zero)

    neg = jnp.full((tq, 1), NEG_BIG, F32)
    za = jnp.zeros((tq, 2 * DA_V_DIM), F32)
    st = (neg, za, neg, za)

    def body(j, carry):
        for c in range(tq // tkv):
            s0 = pl.multiple_of(j * tq + c * tkv, tkv)
            carry = _da_block(q1, q2, k_ref[pl.ds(s0, tkv), :], _with_ones(v_ref[pl.ds(s0, tkv), :]), carry, None)
        return carry

    st = lax.fori_loop(0, i, body, st)

    shift = CHUNK.bit_length() - 1
    s0 = pl.multiple_of(i * tq, tq)
    outs = []
    for h in range(nsub):
        rows = slice(h * ts, (h + 1) * ts)
        width = (h + 1) * ts
        kb = jnp.concatenate([km_ref[...], k_ref[pl.ds(s0, width), :]], axis=0)
        vb = jnp.concatenate([vm_ref[...], v_ref[pl.ds(s0, width), :]], axis=0)
        col = lax.broadcasted_iota(jnp.int32, (ts, META_PAD + width), 1)
        row = lax.broadcasted_iota(jnp.int32, (ts, META_PAD + width), 0) + h * ts
        frame_vis = jnp.logical_and(col >= META_PAD,
                                    jnp.right_shift(col - META_PAD, shift) <= jnp.right_shift(row, shift))
        vis = jnp.logical_or(col < N_META, frame_vis)
        sth = _da_block(q1[rows], q2[rows], kb, _with_ones(vb), tuple(t[rows] for t in st), vis)
        _, a1, _, a2 = sth
        outs.append(a1[:, :DA_V_DIM] / a1[:, DA_V_DIM:] - lam * (a2[:, :DA_V_DIM] / a2[:, DA_V_DIM:]))
    o = jnp.concatenate(outs, axis=0)
    ms = jnp.mean(o * o, axis=-1, keepdims=True)
    o_ref[...] = (((o * lax.rsqrt(ms + RMS_EPS)) * g_ref[...]) * (1.0 - lam_init)).astype(BF16)


def _da_attention(proj, proj_meta, lams, g_da, *, batch, seq, heads, tq, tkv, ts, col0, lam_init):
    hd = DA_V_DIM
    qb, kb, vb = col0 // hd, col0 // hd + heads, col0 // hd + 2 * heads
    kern = functools.partial(_da_kernel, tq=tq, tkv=tkv, ts=ts, scale=DA_HEAD_DIM ** -0.5, lam_init=lam_init)
    lam_spec = pl.BlockSpec((1, DA_HEAD_DIM), lambda b, h, i: (0, 0))
    return pl.pallas_call(
        kern,
        grid=(batch, heads, seq // tq),
        in_specs=[
            lam_spec, lam_spec, lam_spec, lam_spec,
            pl.BlockSpec((None, tq, hd), lambda b, h, i: (b, i, qb + h)),
            pl.BlockSpec((None, seq, hd), lambda b, h, i: (b, 0, kb + h)),
            pl.BlockSpec((None, seq, hd), lambda b, h, i: (b, 0, vb + h)),
            pl.BlockSpec((META_PAD, hd), lambda b, h, i: (0, kb + h)),
            pl.BlockSpec((META_PAD, hd), lambda b, h, i: (0, vb + h)),
            pl.BlockSpec((1, hd), lambda b, h, i: (0, h)),
        ],
        out_specs=pl.BlockSpec((None, tq, hd), lambda b, h, i: (b, i, h)),
        out_shape=jax.ShapeDtypeStruct((batch, seq, heads * hd), BF16),
        compiler_params=_cparams(("parallel", "parallel", "arbitrary")),
        name="da_attn",
    )(*lams, proj, proj, proj, proj_meta, proj_meta, g_da)


def _pack_bf16_pairs(x):
    n = x.shape[1] // 2
    lo = lax.bitcast_convert_type(x[:, :n].astype(BF16).astype(F32), jnp.int32)
    hi = lax.bitcast_convert_type(x[:, n:].astype(BF16).astype(F32), jnp.int32)
    return jnp.bitwise_or(lax.shift_right_logical(lo, 16), jnp.bitwise_and(hi, jnp.int32(-65536)))


def _unpack_bf16_pairs(w):
    lo = lax.bitcast_convert_type(lax.shift_left(w, 16), F32)
    hi = lax.bitcast_convert_type(jnp.bitwise_and(w, jnp.int32(-65536)), F32)
    return jnp.concatenate([lo.astype(BF16), hi.astype(BF16)], axis=1)


def _sc_gather_rows(table, idx):
    n_idx = idx.shape[0]
    width = table.shape[1]
    n_workers = SC_CORES * SC_SUBCORES
    per_worker = n_idx // n_workers
    win = SC_GATHER_ROWS
    assert n_idx % (n_workers * win) == 0
    mesh = plsc.VectorSubcoreMesh(core_axis_name="c", subcore_axis_name="s")

    @functools.partial(
        pl.kernel, mesh=mesh,
        out_type=jax.ShapeDtypeStruct((n_idx, width), table.dtype),
        scratch_types=[pltpu.VMEM((win,), jnp.int32), pltpu.VMEM((win, width), table.dtype),
                       pltpu.SemaphoreType.DMA],
    )
    def gather(table_hbm, idx_hbm, out_hbm, idx_v, rows_v, sem):
        worker = lax.axis_index("s") * SC_CORES + lax.axis_index("c")
        base = worker * per_worker

        @pl.loop(0, per_worker // win)
        def _(t):
            off = pl.multiple_of(base + t * win, win)
            pltpu.sync_copy(idx_hbm.at[pl.ds(off, win)], idx_v)
            pltpu.async_copy(table_hbm.at[idx_v], rows_v, sem).wait()
            pltpu.sync_copy(rows_v, out_hbm.at[pl.ds(off, win)])

    return gather(table, idx)


def _split3_dot(a, b_hi, b_lo):
    a_hi = a.astype(BF16)
    a_lo = (a - a_hi.astype(F32)).astype(BF16)
    return (jnp.dot(a_hi, b_hi, preferred_element_type=F32)
            + jnp.dot(a_hi, b_lo, preferred_element_type=F32)
            + jnp.dot(a_lo, b_hi, preferred_element_type=F32))


def _out_proj_kernel(ms_ref, md_ref, ws_ref, wd_ref, x_ref, g_ref, wr_hi_ref, wr_lo_ref, br_ref,
                     h_ref, u_ref, e_ref, p_ref, c_ref, *, n_exp):
    h = (x_ref[...]
         + jnp.dot(ms_ref[...], ws_ref[...], preferred_element_type=F32)
         + jnp.dot(md_ref[...], wd_ref[...], preferred_element_type=F32))
    h_ref[...] = h
    msq = jnp.mean(h * h, axis=-1, keepdims=True)
    u = (h * lax.rsqrt(msq + RMS_EPS)) * g_ref[...]
    u_ref[...] = _pack_bf16_pairs(u)

    logits = _split3_dot(u, wr_hi_ref[...], wr_lo_ref[...]) + br_ref[...]
    lane = lax.broadcasted_iota(jnp.int32, logits.shape, 1).astype(F32)
    work = logits
    tops, idxs = [], []
    for _ in range(TOP_K):
        mx = jnp.max(work, axis=-1, keepdims=True)
        ix = jnp.min(jnp.where(work == mx, lane, float(n_exp)), axis=-1, keepdims=True)
        tops.append(mx)
        idxs.append(ix)
        work = jnp.where(lane == ix, -jnp.inf, work)
    ex = [jnp.exp(t - tops[0]) for t in tops]
    den = ex[0] + ex[1] + ex[2] + ex[3]
    kl = lax.broadcasted_iota(jnp.int32, (logits.shape[0], TOP_K), 1)
    e_out = jnp.zeros((logits.shape[0], TOP_K), F32)
    p_out = jnp.zeros((logits.shape[0], TOP_K), F32)
    for k in range(TOP_K):
        e_out = jnp.where(kl == k, idxs[k], e_out)
        p_out = jnp.where(kl == k, ex[k] / den, p_out)
    e_ref[...] = e_out.astype(jnp.int32)
    p_ref[...] = p_out
    hits = jnp.zeros(logits.shape, F32)
    for k in range(TOP_K):
        hits = hits + jnp.where(lane == idxs[k], 1.0, 0.0)
    c_ref[...] = jnp.sum(hits, axis=0, keepdims=True)


def _out_proj(mix_sb, mix_da, w_sb, w_da, x2d, g_ffn, wr_hi, wr_lo, b_r, *, tm):
    m, d = x2d.shape
    ks, kd = mix_sb.shape[1], mix_da.shape[1]
    n_exp = wr_hi.shape[1]
    kern = functools.partial(_out_proj_kernel, n_exp=n_exp)
    const = lambda shape: pl.BlockSpec(shape, lambda i: (0, 0))
    return pl.pallas_call(
        kern,
        grid=(m // tm,),
        in_specs=[
            pl.BlockSpec((tm, ks), lambda i: (i, 0)),
            pl.BlockSpec((tm, kd), lambda i: (i, 0)),
            const((ks, d)), const((kd, d)),
            pl.BlockSpec((tm, d), lambda i: (i, 0)),
            const((1, d)), const((d, n_exp)), const((d, n_exp)), const((1, n_exp)),
        ],
        out_specs=[
            pl.BlockSpec((tm, d), lambda i: (i, 0)),
            pl.BlockSpec((tm, d // 2), lambda i: (i, 0)),
            pl.BlockSpec((tm, TOP_K), lambda i: (i, 0)),
            pl.BlockSpec((tm, TOP_K), lambda i: (i, 0)),
            pl.BlockSpec((None, 1, n_exp), lambda i: (i, 0, 0)),
        ],
        out_shape=[
            jax.ShapeDtypeStruct((m, d), F32),
            jax.ShapeDtypeStruct((m, d // 2), jnp.int32),
            jax.ShapeDtypeStruct((m, TOP_K), jnp.int32),
            jax.ShapeDtypeStruct((m, TOP_K), F32),
            jax.ShapeDtypeStruct((m // tm, 1, n_exp), F32),
        ],
        compiler_params=_cparams(("parallel",)),
        name="out_proj_router",
    )(mix_sb, mix_da, w_sb, w_da, x2d, g_ffn, wr_hi, wr_lo, b_r)


def _moe_kernel(be_ref, bc_ref, rb_ref, x_ref, wg_ref, wu_ref, wda_ref, wdb_ref, bg_ref, bup_ref, bda_ref, bdb_ref,
                o_ref, act_ref, *, nf, ts):
    blk = pl.program_id(0)
    s = pl.program_id(1)
    cnt = bc_ref[blk]
    tg = 2 * ts
    tm = x_ref.shape[0]

    def row_groups(run, skip):
        for lo in range(0, tm, tg):
            size = min(tg, tm - lo)

            @pl.when(cnt > lo + size - ts)
            def _():
                run(lo, size)

            if size > ts:
                @pl.when(jnp.logical_and(cnt > lo, cnt <= lo + ts))
                def _():
                    run(lo, ts)
                    skip(lo + ts, ts)

            @pl.when(cnt <= lo)
            def _():
                skip(lo, size)

    @pl.when(jnp.logical_and(cnt > 0, s < nf))
    def _():
        wg = wg_ref[...].astype(BF16)
        wu = wu_ref[...].astype(BF16)
        slot = jnp.minimum(s, nf - 1)

        def up_proj(r0, nrows):
            x = _unpack_bf16_pairs(x_ref[r0:r0 + nrows, :])
            g = jnp.dot(x, wg, preferred_element_type=F32) + bg_ref[...]
            u = jnp.dot(x, wu, preferred_element_type=F32) + bup_ref[...]
            gate = jnp.minimum(g, SWIGLU_LIMIT)
            up = jnp.clip(u, -SWIGLU_LIMIT, SWIGLU_LIMIT)
            act = (up + 1.0) * gate * jax.nn.sigmoid(SWIGLU_ALPHA * gate)
            act_ref[slot, r0:r0 + nrows, :] = act.astype(BF16)

        row_groups(up_proj, lambda r0, nrows: None)

    @pl.when(jnp.logical_and(cnt > 0, s >= nf))
    def _():
        tf = act_ref.shape[2]
        wda = wda_ref[...].astype(BF16)
        wdb = wdb_ref[...].astype(BF16)

        def down_proj(r0, nrows):
            ya = bda_ref[...]
            yb = bdb_ref[...]
            for f in range(nf):
                a = act_ref[f, r0:r0 + nrows, :]
                ya = ya + jnp.dot(a, wda[f * tf:(f + 1) * tf, :], preferred_element_type=F32)
                yb = yb + jnp.dot(a, wdb[f * tf:(f + 1) * tf, :], preferred_element_type=F32)
            o_ref[r0:r0 + nrows, :] = _pack_bf16_pairs(jnp.concatenate([ya, yb], axis=1))

        def zero_fill(r0, nrows):
            o_ref[r0:r0 + nrows, :] = jnp.zeros((nrows, o_ref.shape[1]), o_ref.dtype)

        row_groups(down_proj, zero_fill)


def _moe_ffn(blk_e, blk_cnt, blk_row, xs, w_gu, w_dn, b_gu, b_dn, *, tm, ts, tf, tn):
    n_rows = xs.shape[0]
    n_exp, d, f2 = w_gu.shape
    d_ff = f2 // 2
    nf = d_ff // tf
    nn = d // tn
    n_blk = n_rows // tm
    b_gu3 = b_gu.reshape(n_exp, 1, f2)
    b_dn3 = b_dn.reshape(n_exp, 1, d)

    def fi(s, bc, b):
        return jnp.where(bc[b] > 0, jnp.minimum(s, nf - 1), nf - 1)

    nh = nn // 2

    def ni(s, bc, b):
        return jnp.where(bc[b] > 0, jnp.clip(s - nf, 0, nh - 1), nh - 1)

    grid_spec = pltpu.PrefetchScalarGridSpec(
        num_scalar_prefetch=3,
        grid=(n_blk, nf + nh),
        in_specs=[
            pl.BlockSpec((tm, d // 2), lambda b, s, be, bc, rb: (rb[b], 0), pipeline_mode=pl.Buffered(1)),
            pl.BlockSpec((None, d, tf), lambda b, s, be, bc, rb: (be[b], 0, fi(s, bc, b))),
            pl.BlockSpec((None, d, tf), lambda b, s, be, bc, rb: (be[b], 0, nf + fi(s, bc, b))),
            pl.BlockSpec((None, d_ff, tn), lambda b, s, be, bc, rb: (be[b], 0, ni(s, bc, b))),
            pl.BlockSpec((None, d_ff, tn), lambda b, s, be, bc, rb: (be[b], 0, nh + ni(s, bc, b))),
            pl.BlockSpec((None, 1, tf), lambda b, s, be, bc, rb: (be[b], 0, fi(s, bc, b))),
            pl.BlockSpec((None, 1, tf), lambda b, s, be, bc, rb: (be[b], 0, nf + fi(s, bc, b))),
            pl.BlockSpec((None, 1, tn), lambda b, s, be, bc, rb: (be[b], 0, ni(s, bc, b))),
            pl.BlockSpec((None, 1, tn), lambda b, s, be, bc, rb: (be[b], 0, nh + ni(s, bc, b))),
        ],
        out_specs=pl.BlockSpec((tm, tn), lambda b, s, be, bc, rb: (rb[b], ni(s, bc, b))),
        scratch_shapes=[pltpu.VMEM((nf, tm, tf), BF16)],
    )
    return pl.pallas_call(
        functools.partial(_moe_kernel, nf=nf, ts=ts),
        grid_spec=grid_spec,
        out_shape=jax.ShapeDtypeStruct((n_rows, d // 2), jnp.int32),
        compiler_params=_cparams(("arbitrary", "arbitrary")),
        name="moe_ffn",
    )(blk_e, blk_cnt, blk_row, xs, w_gu, w_gu, w_dn, w_dn, b_gu3, b_gu3, b_dn3, b_dn3)


def _final_kernel(h_ref, y0_ref, y1_ref, y2_ref, y3_ref, p_ref, g_ref, o_ref):
    p = p_ref[...]
    y = None
    for k, y_ref in enumerate((y0_ref, y1_ref, y2_ref, y3_ref)):
        t = _unpack_bf16_pairs(y_ref[...]).astype(F32) * p[:, k:k + 1]
        y = t if y is None else y + t
    h = h_ref[...] + y
    ms = jnp.mean(h * h, axis=-1, keepdims=True)
    o_ref[...] = (h * lax.rsqrt(ms + RMS_EPS)) * g_ref[...]


def _final(h1, y_km, gates, g, *, tm):
    m, d = h1.shape
    nb = m // tm
    assert TOP_K == 4
    y_specs = [pl.BlockSpec((tm, d // 2), functools.partial(lambda i, k: (k * nb + i, 0), k=k))
               for k in range(TOP_K)]
    return pl.pallas_call(
        _final_kernel,
        grid=(nb,),
        in_specs=([pl.BlockSpec((tm, d), lambda i: (i, 0))] + y_specs
                  + [pl.BlockSpec((tm, TOP_K), lambda i: (i, 0)), pl.BlockSpec((1, d), lambda i: (0, 0))]),
        out_specs=pl.BlockSpec((tm, d), lambda i: (i, 0)),
        out_shape=jax.ShapeDtypeStruct((m, d), F32),
        compiler_params=_cparams(("parallel",)),
        name="final_norm",
    )(h1, y_km, y_km, y_km, y_km, gates, g)


def _moe_blocks(n_assign, n_exp, tm):
    n_blk = -(-n_assign // tm) + n_exp
    unit = SC_CORES * SC_SUBCORES * SC_GATHER_ROWS
    while (n_blk * tm) % unit:
        n_blk += 1
    return n_blk


def _routing(top_e, counts, tm):
    n_exp = counts.shape[0]
    n_tok = top_e.shape[0]
    n_assign = n_tok * TOP_K
    flat_e = top_e.reshape(-1)
    padded = (counts + tm - 1) // tm * tm
    pad_end = jnp.cumsum(padded).astype(jnp.int32)
    pad_start = pad_end - padded
    n_blk = _moe_blocks(n_assign, n_exp, tm)
    n_rows = n_blk * tm
    blk_start = jnp.arange(n_blk, dtype=jnp.int32) * tm
    blk_e = jnp.minimum(jnp.sum((blk_start[:, None] >= pad_end[None, :]).astype(jnp.int32), axis=1), n_exp - 1)
    blk_off = blk_start - pad_start[blk_e]
    blk_cnt = jnp.clip(counts[blk_e] - blk_off, 0, tm).astype(jnp.int32)
    n_fill = n_rows - n_assign
    j = jnp.arange(tm, dtype=jnp.int32)[None, :]
    e_ids = jnp.arange(n_exp, dtype=jnp.int32)[:, None]
    fill_keys = jnp.where(j < (padded - counts)[:, None], e_ids, n_exp).reshape(-1)
    fill_keys = jnp.concatenate([fill_keys, jnp.full((n_fill - n_exp * tm,), n_exp, jnp.int32)])
    keys = jnp.concatenate([flat_e, fill_keys])
    toks = jnp.concatenate([jnp.arange(n_assign, dtype=jnp.int32) // TOP_K,
                            jnp.arange(n_fill, dtype=jnp.int32) % n_tok])
    _, row_tok, entry = lax.sort((keys, toks, jnp.arange(n_rows, dtype=jnp.int32)), num_keys=1, is_stable=True)
    dest_flat = jnp.argsort(entry).astype(jnp.int32)[:n_assign]
    last_used = jnp.maximum(pad_end[-1] // tm - 1, 0)
    blk_e = jnp.where(blk_cnt > 0, blk_e, blk_e[last_used]).astype(jnp.int32)
    blk_row = jnp.where(blk_cnt > 0, jnp.arange(n_blk, dtype=jnp.int32), last_used).astype(jnp.int32)
    return row_tok, dest_flat, blk_e, blk_cnt, blk_row


def kernel(x, meta_tokens, g_mix, w_in, lam_q1, lam_k1, lam_q2, lam_k2, g_sb_out, g_da_out, w_out, g_ffn,
           w_router, b_router, w_gate_up, b_gate_up, w_down, b_down, g_final):
    b, s, d = x.shape
    depth = w_in.shape[0]
    assert depth == 1, "single-layer trunk"
    layer = 0
    sb_heads = (d // 2) // SB_HEAD_DIM
    da_heads = (d // 2) // DA_V_DIM
    sb_w = sb_heads * SB_HEAD_DIM
    da_col0 = 3 * sb_w
    da_qk_w = da_heads * 2 * DA_HEAD_DIM
    n_exp = w_router.shape[-1]
    lam_init = 0.8 - 0.6 * math.exp(-0.3 * layer)

    tm_proj, tn_proj = TM_PROJ, TN_PROJ
    tq, tk_sb, ts_da = TQ_ATTN, TK_SB, TS_DA
    tm_out = TM_OUT
    tm_moe, tf_moe = TM_MOE, TF_MOE
    tm_fin = TM_FINAL

    x2d = x.reshape(b * s, d)
    w_in_bf = w_in.reshape(w_in.shape[1:]).astype(BF16)
    g_mix2 = g_mix[layer].reshape(1, d)
    rope_cols = (da_col0, da_col0 + 2 * da_qk_w)

    pos_f = N_META + jnp.arange(s, dtype=jnp.int32)
    proj = _in_proj(x2d, g_mix2, w_in_bf, _rope_tables(pos_f), tm=tm_proj, tn=tn_proj,
                    rope_cols=rope_cols, pos_blocks=s // tm_proj)
    meta_pad = jnp.zeros((META_PAD, d), x.dtype).at[:N_META].set(meta_tokens.astype(x.dtype))
    pos_m = jnp.arange(META_PAD, dtype=jnp.int32)
    proj_meta = _in_proj(meta_pad, g_mix2, w_in_bf, _rope_tables(pos_m), tm=META_PAD, tn=tn_proj,
                         rope_cols=rope_cols, pos_blocks=1)
    proj3 = proj.reshape(b, s, -1)

    mix_sb = _sb_attention(proj3, proj_meta, g_sb_out[layer].reshape(1, -1), batch=b, seq=s, heads=sb_heads,
                           tq=tq, tk=tk_sb, hp=HEADS_PER_STEP_SB)
    lams = tuple(t[layer].reshape(1, DA_HEAD_DIM).astype(F32) for t in (lam_q1, lam_k1, lam_q2, lam_k2))
    mix_da = _da_attention(proj3, proj_meta, lams, g_da_out[layer].reshape(1, -1), batch=b, seq=s, heads=da_heads,
                           tq=TQ_DA, tkv=TKV_DA, ts=ts_da, col0=da_col0, lam_init=lam_init)

    w_out_bf = w_out[layer].astype(BF16)
    wr = w_router[layer]
    wr_hi = wr.astype(BF16)
    wr_lo = (wr - wr_hi.astype(F32)).astype(BF16)
    h1, u_packed, top_e, gates, tile_counts = _out_proj(
        mix_sb.reshape(b * s, -1), mix_da.reshape(b * s, -1), w_out_bf[:sb_w], w_out_bf[sb_w:], x2d,
        g_ffn[layer].reshape(1, d), wr_hi, wr_lo, b_router[layer].reshape(1, n_exp), tm=tm_out)

    counts = jnp.sum(tile_counts, axis=(0, 1)).astype(jnp.int32)
    row_tok, dest_flat, blk_e, blk_cnt, blk_row = _routing(top_e, counts, tm_moe)
    xs = _sc_gather_rows(u_packed, row_tok)
    rows = _moe_ffn(blk_e, blk_cnt, blk_row, xs, w_gate_up.reshape(w_gate_up.shape[1:]),
                    w_down.reshape(w_down.shape[1:]), b_gate_up[layer], b_down[layer],
                    tm=tm_moe, ts=TS_MOE, tf=tf_moe, tn=TN_MOE)
    dest_km = dest_flat.reshape(b * s, TOP_K).T.reshape(-1)
    y_km = _sc_gather_rows(rows, dest_km)

    out = _final(h1, y_km, gates, g_final.reshape(1, d), tm=tm_fin)
    return out.reshape(b, s, d)
```

```python
import functools
import math

import jax
import jax.numpy as jnp
from jax import lax
from jax.experimental import pallas as pl
from jax.experimental.pallas import tpu as pltpu
from jax.experimental.pallas import tpu_sc as plsc

F32 = jnp.float32
BF16 = jnp.bfloat16

CHUNK = 64
N_META = 16
RMS_EPS = 1e-5
SB_HEAD_DIM = 128
DA_HEAD_DIM = 64
DA_V_DIM = 2 * DA_HEAD_DIM
ROPE_THETA = 500000.0
ROPE_DIM = DA_HEAD_DIM // 4
TOP_K = 4
SWIGLU_LIMIT = 7.0
SWIGLU_ALPHA = 1.702

LANES = 128
META_PAD = 128
NEG_BIG = -1e30
SB_DEAD_RUN = 110.0
VMEM_LIMIT = 56 * 1024 * 1024
SC_CORES, SC_SUBCORES = 2, 16
SC_GATHER_ROWS = 32

TM_PROJ, TN_PROJ = 1024, 1024
TQ_ATTN = 512
HEADS_PER_STEP_SB = 2
TQ_DA, TKV_DA = 1024, 512
TK_SB = 256
TS_DA = 256
TM_OUT = 512
TM_MOE, TS_MOE = 2048, 512
TF_MOE, TN_MOE = 256, 256
TM_FINAL = 512


def _cparams(sem):
    return pltpu.CompilerParams(dimension_semantics=sem, vmem_limit_bytes=VMEM_LIMIT)


def _in_proj_kernel(x_ref, g_ref, w_ref, c_ref, sa_ref, sb_ref, o_ref, u_scr, *, tn, rope_lo, rope_hi):
    n = pl.program_id(1)

    @pl.when(n == 0)
    def _():
        x = x_ref[...]
        ms = jnp.mean(x * x, axis=-1, keepdims=True)
        u_scr[...] = ((x * lax.rsqrt(ms + RMS_EPS)) * g_ref[...]).astype(BF16)

    acc = jnp.dot(u_scr[...], w_ref[...], preferred_element_type=F32)
    is_rope = jnp.logical_and(n >= rope_lo, n < rope_hi)

    @pl.when(is_rope)
    def _():
        for c in range(tn // LANES):
            xc = acc[:, c * LANES:(c + 1) * LANES]
            r = (xc * c_ref[...] + pltpu.roll(xc, LANES - ROPE_DIM // 2, 1) * sa_ref[...]
                 + pltpu.roll(xc, ROPE_DIM // 2, 1) * sb_ref[...])
            o_ref[:, c * LANES:(c + 1) * LANES] = r.astype(BF16)

    @pl.when(jnp.logical_not(is_rope))
    def _():
        o_ref[...] = acc.astype(BF16)


def _rope_tables(pos):
    half = ROPE_DIM // 2
    inv_freq = ROPE_THETA ** (-(jnp.arange(half, dtype=F32) * 2.0 / ROPE_DIM))
    ang = pos.astype(F32)[:, None] * inv_freq[None, :]
    cos, sin = jnp.cos(ang), jnp.sin(ang)
    p = pos.shape[0]
    ones = jnp.ones((p, DA_HEAD_DIM - ROPE_DIM), F32)
    zeros8 = jnp.zeros((p, half), F32)
    zeros48 = jnp.zeros((p, DA_HEAD_DIM - ROPE_DIM), F32)
    c64 = jnp.concatenate([cos, cos, ones], axis=1)
    sa64 = jnp.concatenate([-sin, zeros8, zeros48], axis=1)
    sb64 = jnp.concatenate([zeros8, sin, zeros48], axis=1)
    tile2 = lambda t: jnp.concatenate([t, t], axis=1)
    return tile2(c64), tile2(sa64), tile2(sb64)


def _in_proj(x2d, g, w_bf, tables, *, tm, tn, rope_cols, pos_blocks):
    m, d = x2d.shape
    n_cols = w_bf.shape[1]
    c_t, sa_t, sb_t = tables
    kern = functools.partial(_in_proj_kernel, tn=tn, rope_lo=rope_cols[0] // tn, rope_hi=rope_cols[1] // tn)
    tab_spec = pl.BlockSpec((tm, LANES), lambda i, n: (i % pos_blocks, 0))
    return pl.pallas_call(
        kern,
        grid=(m // tm, n_cols // tn),
        in_specs=[
            pl.BlockSpec((tm, d), lambda i, n: (i, 0)),
            pl.BlockSpec((1, d), lambda i, n: (0, 0)),
            pl.BlockSpec((d, tn), lambda i, n: (0, n)),
            tab_spec, tab_spec, tab_spec,
        ],
        out_specs=pl.BlockSpec((tm, tn), lambda i, n: (i, n)),
        out_shape=jax.ShapeDtypeStruct((m, n_cols), BF16),
        scratch_shapes=[pltpu.VMEM((tm, d), BF16)],
        compiler_params=_cparams(("parallel", "arbitrary")),
        name="in_proj",
    )(x2d, g, w_bf, c_t, sa_t, sb_t)


def _dot_nt(a, b):
    return lax.dot_general(a, b, (((1,), (1,)), ((), ())), preferred_element_type=F32)


def _suffix_sum_matrix(n):
    j = lax.broadcasted_iota(jnp.int32, (2 * n, n), 0)
    s = lax.broadcasted_iota(jnp.int32, (2 * n, n), 1)
    return jnp.where(jnp.where(j >= n, j - n, j) > s, 1.0, 0.0).astype(BF16)


def _sb_block(q, kb, vb, u2, run, acc, mask, scale):
    z = _dot_nt(q, kb) * scale
    sp = jnp.maximum(z, 0.0) + jnp.log(1.0 + jnp.exp(-jnp.abs(z)))
    if mask is not None:
        sp = jnp.where(mask, sp, 0.0)
    hi = sp.astype(BF16)
    lo = (sp - hi.astype(F32)).astype(BF16)
    cs = jnp.dot(jnp.concatenate([hi, lo], axis=1), u2, preferred_element_type=F32)
    w = jnp.exp(z - sp - cs - run)
    if mask is not None:
        w = jnp.where(mask, w, 0.0)
    acc = acc + jnp.dot(w.astype(BF16), vb, preferred_element_type=F32)
    run = run + (cs[:, :1] + sp[:, :1])
    return run, acc


def _sb_kernel(q_ref, k_ref, v_ref, km_ref, vm_ref, g_ref, o_ref, *, tq, tk, hp, scale):
    i = pl.program_id(2)
    nsub = tq // tk
    hd = SB_HEAD_DIM
    u_blk = _suffix_sum_matrix(tk)
    row = lax.broadcasted_iota(jnp.int32, (tk, tk), 0)
    col = lax.broadcasted_iota(jnp.int32, (tk, tk), 1)
    diag_mask = col < row
    heads = range(hp)

    def kv(blk, h):
        s0 = pl.multiple_of(blk * tk, tk)
        return k_ref[pl.ds(s0, tk), h * hd:(h + 1) * hd], v_ref[pl.ds(s0, tk), h * hd:(h + 1) * hd]

    runs, accs = [], []
    for h in heads:
        slab_runs, slab_accs = [], []
        for sl in range(nsub):
            qs = q_ref[sl * tk:(sl + 1) * tk, h * hd:(h + 1) * hd]
            run = jnp.zeros((tk, 1), F32)
            acc = jnp.zeros((tk, hd), F32)
            for c in range(sl, -1, -1):
                kb, vb = kv(i * nsub + c, h)
                run, acc = _sb_block(qs, kb, vb, u_blk, run, acc, diag_mask if c == sl else None, scale)
            slab_runs.append(run)
            slab_accs.append(acc)
        runs.append(jnp.concatenate(slab_runs, axis=0))
        accs.append(jnp.concatenate(slab_accs, axis=0))
    q = [q_ref[:, h * hd:(h + 1) * hd] for h in heads]

    def alive_flag(rs):
        lowest = functools.reduce(jnp.minimum, [jnp.min(r) for r in rs])
        return (lowest < SB_DEAD_RUN).astype(jnp.int32)

    def cond(carry):
        jb, alive, _, _ = carry
        return jnp.logical_and(jb >= 0, alive > 0)

    def body(carry):
        jb, _, rs, as_ = carry
        out = [_sb_block(q[h], *kv(jb, h), u_blk, rs[h], as_[h], None, scale) for h in heads]
        rs = tuple(o[0] for o in out)
        return jb - 1, alive_flag(rs), rs, tuple(o[1] for o in out)

    runs, accs = tuple(runs), tuple(accs)
    _, alive, runs, accs = lax.while_loop(cond, body, (i * nsub - 1, alive_flag(runs), runs, accs))

    def meta_block():
        mcol = lax.broadcasted_iota(jnp.int32, (tq, META_PAD), 1)
        u_meta = _suffix_sum_matrix(META_PAD)
        return tuple(_sb_block(q[h], km_ref[:, h * hd:(h + 1) * hd], vm_ref[:, h * hd:(h + 1) * hd], u_meta,
                               runs[h], accs[h], mcol < N_META, scale)[1] for h in heads)

    accs = lax.cond(alive > 0, meta_block, lambda: accs)

    for h in heads:
        acc = accs[h]
        ms = jnp.mean(acc * acc, axis=-1, keepdims=True)
        gain = g_ref[:, h * hd:(h + 1) * hd]
        o_ref[:, h * hd:(h + 1) * hd] = ((acc * lax.rsqrt(ms + RMS_EPS)) * gain).astype(BF16)


def _sb_attention(proj, proj_meta, g_sb, *, batch, seq, heads, tq, tk, hp):
    hd = SB_HEAD_DIM
    wd = hp * hd
    ng = heads // hp
    kern = functools.partial(_sb_kernel, tq=tq, tk=tk, hp=hp, scale=hd ** -0.5)
    return pl.pallas_call(
        kern,
        grid=(batch, ng, seq // tq),
        in_specs=[
            pl.BlockSpec((None, tq, wd), lambda b, h, i: (b, i, h)),
            pl.BlockSpec((None, seq, wd), lambda b, h, i: (b, 0, ng + h)),
            pl.BlockSpec((None, seq, wd), lambda b, h, i: (b, 0, 2 * ng + h)),
            pl.BlockSpec((META_PAD, wd), lambda b, h, i: (0, ng + h)),
            pl.BlockSpec((META_PAD, wd), lambda b, h, i: (0, 2 * ng + h)),
            pl.BlockSpec((1, wd), lambda b, h, i: (0, h)),
        ],
        out_specs=pl.BlockSpec((None, tq, wd), lambda b, h, i: (b, i, h)),
        out_shape=jax.ShapeDtypeStruct((batch, seq, heads * hd), BF16),
        compiler_params=_cparams(("parallel", "parallel", "arbitrary")),
        name="sb_attn",
    )(proj, proj, proj, proj_meta, proj_meta, g_sb)


def _with_ones(vb):
    return jnp.concatenate([vb, jnp.ones_like(vb)], axis=1)


def _da_block(q1, q2, kb, vb1, st, mask):
    m1, a1, m2, a2 = st

    def one(qc, m, a):
        s = _dot_nt(qc, kb)
        if mask is not None:
            s = jnp.where(mask, s, NEG_BIG)
        mn = jnp.maximum(m, jnp.max(s, axis=-1, keepdims=True))
        p = jnp.exp(s - mn)
        a = jnp.exp(m - mn) * a + jnp.dot(p.astype(BF16), vb1, preferred_element_type=F32)
        return mn, a

    m1, a1 = one(q1, m1, a1)
    m2, a2 = one(q2, m2, a2)
    return m1, a1, m2, a2


def _da_kernel(lq1_ref, lk1_ref, lq2_ref, lk2_ref, q_ref, k_ref, v_ref, km_ref, vm_ref, g_ref, o_ref,
               *, tq, tkv, ts, scale, lam_init):
    i = pl.program_id(2)
    nsub = tq // ts
    lam = (jnp.exp(jnp.sum(lq1_ref[...] * lk1_ref[...], axis=-1, keepdims=True))
           - jnp.exp(jnp.sum(lq2_ref[...] * lk2_ref[...], axis=-1, keepdims=True)) + lam_init)

    q = q_ref[...]
    lane = lax.broadcasted_iota(jnp.int32, q.shape, 1)
    qs = q * jnp.asarray(scale, BF16)
    zero = jnp.zeros_like(qs)
    q1 = jnp.where(lane < DA_HEAD_DIM, qs, zero)
    q2 = jnp.where(lane >= DA_HEAD_DIM, qs, zero)

    neg = jnp.full((tq, 1), NEG_BIG, F32)
    za = jnp.zeros((tq, 2 * DA_V_DIM), F32)
    st = (neg, za, neg, za)

    def body(j, carry):
        for c in range(tq // tkv):
            s0 = pl.multiple_of(j * tq + c * tkv, tkv)
            carry = _da_block(q1, q2, k_ref[pl.ds(s0, tkv), :], _with_ones(v_ref[pl.ds(s0, tkv), :]), carry, None)
        return carry

    st = lax.fori_loop(0, i, body, st)

    shift = CHUNK.bit_length() - 1
    s0 = pl.multiple_of(i * tq, tq)
    outs = []
    for h in range(nsub):
        rows = slice(h * ts, (h + 1) * ts)
        width = (h + 1) * ts
        kb = jnp.concatenate([km_ref[...], k_ref[pl.ds(s0, width), :]], axis=0)
        vb = jnp.concatenate([vm_ref[...], v_ref[pl.ds(s0, width), :]], axis=0)
        col = lax.broadcasted_iota(jnp.int32, (ts, META_PAD + width), 1)
        row = lax.broadcasted_iota(jnp.int32, (ts, META_PAD + width), 0) + h * ts
        frame_vis = jnp.logical_and(col >= META_PAD,
                                    jnp.right_shift(col - META_PAD, shift) <= jnp.right_shift(row, shift))
        vis = jnp.logical_or(col < N_META, frame_vis)
        sth = _da_block(q1[rows], q2[rows], kb, _with_ones(vb), tuple(t[rows] for t in st), vis)
        _, a1, _, a2 = sth
        outs.append(a1[:, :DA_V_DIM] / a1[:, DA_V_DIM:] - lam * (a2[:, :DA_V_DIM] / a2[:, DA_V_DIM:]))
    o = jnp.concatenate(outs, axis=0)
    ms = jnp.mean(o * o, axis=-1, keepdims=True)
    o_ref[...] = (((o * lax.rsqrt(ms + RMS_EPS)) * g_ref[...]) * (1.0 - lam_init)).astype(BF16)


def _da_attention(proj, proj_meta, lams, g_da, *, batch, seq, heads, tq, tkv, ts, col0, lam_init):
    hd = DA_V_DIM
    qb, kb, vb = col0 // hd, col0 // hd + heads, col0 // hd + 2 * heads
    kern = functools.partial(_da_kernel, tq=tq, tkv=tkv, ts=ts, scale=DA_HEAD_DIM ** -0.5, lam_init=lam_init)
    lam_spec = pl.BlockSpec((1, DA_HEAD_DIM), lambda b, h, i: (0, 0))
    return pl.pallas_call(
        kern,
        grid=(batch, heads, seq // tq),
        in_specs=[
            lam_spec, lam_spec, lam_spec, lam_spec,
            pl.BlockSpec((None, tq, hd), lambda b, h, i: (b, i, qb + h)),
            pl.BlockSpec((None, seq, hd), lambda b, h, i: (b, 0, kb + h)),
            pl.BlockSpec((None, seq, hd), lambda b, h, i: (b, 0, vb + h)),
            pl.BlockSpec((META_PAD, hd), lambda b, h, i: (0, kb + h)),
            pl.BlockSpec((META_PAD, hd), lambda b, h, i: (0, vb + h)),
            pl.BlockSpec((1, hd), lambda b, h, i: (0, h)),
        ],
        out_specs=pl.BlockSpec((None, tq, hd), lambda b, h, i: (b, i, h)),
        out_shape=jax.ShapeDtypeStruct((batch, seq, heads * hd), BF16),
        compiler_params=_cparams(("parallel", "parallel", "arbitrary")),
        name="da_attn",
    )(*lams, proj, proj, proj, proj_meta, proj_meta, g_da)


def _pack_bf16_pairs(x):
    n = x.shape[1] // 2
    lo = lax.bitcast_convert_type(x[:, :n].astype(BF16).astype(F32), jnp.int32)
    hi = lax.bitcast_convert_type(x[:, n:].astype(BF16).astype(F32), jnp.int32)
    return jnp.bitwise_or(lax.shift_right_logical(lo, 16), jnp.bitwise_and(hi, jnp.int32(-65536)))


def _unpack_bf16_pairs(w):
    lo = lax.bitcast_convert_type(lax.shift_left(w, 16), F32)
    hi = lax.bitcast_convert_type(jnp.bitwise_and(w, jnp.int32(-65536)), F32)
    return jnp.concatenate([lo.astype(BF16), hi.astype(BF16)], axis=1)


def _sc_gather_rows(table, idx):
    n_idx = idx.shape[0]
    width = table.shape[1]
    n_workers = SC_CORES * SC_SUBCORES
    per_worker = n_idx // n_workers
    win = SC_GATHER_ROWS
    assert n_idx % (n_workers * win) == 0
    mesh = plsc.VectorSubcoreMesh(core_axis_name="c", subcore_axis_name="s")

    n_win = per_worker // win
    assert n_win % 2 == 0
    buf = lambda: pltpu.VMEM((win, width), table.dtype)
    ivec = lambda: pltpu.VMEM((win,), jnp.int32)
    dma = pltpu.SemaphoreType.DMA

    @functools.partial(
        pl.kernel, mesh=mesh,
        out_type=jax.ShapeDtypeStruct((n_idx, width), table.dtype),
        scratch_types=[ivec(), ivec(), buf(), buf(), dma, dma, dma, dma],
    )
    def gather(table_hbm, idx_hbm, out_hbm, idx0, idx1, rows0, rows1, gsem0, gsem1, wsem0, wsem1):
        worker = lax.axis_index("s") * SC_CORES + lax.axis_index("c")
        base = worker * per_worker

        def off(t):
            return pl.multiple_of(base + t * win, win)

        def start_gather(t, idx_v, rows_v, sem):
            pltpu.sync_copy(idx_hbm.at[pl.ds(off(t), win)], idx_v)
            pltpu.make_async_copy(table_hbm.at[idx_v], rows_v, sem).start()

        def wait_gather(idx_v, rows_v, sem):
            pltpu.make_async_copy(table_hbm.at[idx_v], rows_v, sem).wait()

        def writeback(t, rows_v, sem):
            return pltpu.make_async_copy(rows_v, out_hbm.at[pl.ds(off(t), win)], sem)

        start_gather(0, idx0, rows0, gsem0)

        @pl.loop(0, n_win, step=2)
        def _(t):
            wait_gather(idx0, rows0, gsem0)
            writeback(t, rows0, wsem0).start()

            @pl.when(t > 0)
            def _():
                writeback(t - 1, rows1, wsem1).wait()

            start_gather(t + 1, idx1, rows1, gsem1)
            wait_gather(idx1, rows1, gsem1)
            writeback(t + 1, rows1, wsem1).start()
            writeback(t, rows0, wsem0).wait()

            @pl.when(t + 2 < n_win)
            def _():
                start_gather(t + 2, idx0, rows0, gsem0)

        writeback(n_win - 1, rows1, wsem1).wait()

    return gather(table, idx)


def _split3_dot(a, b_hi, b_lo):
    a_hi = a.astype(BF16)
    a_lo = (a - a_hi.astype(F32)).astype(BF16)
    return (jnp.dot(a_hi, b_hi, preferred_element_type=F32)
            + jnp.dot(a_hi, b_lo, preferred_element_type=F32)
            + jnp.dot(a_lo, b_hi, preferred_element_type=F32))


def _out_proj_kernel(ms_ref, md_ref, ws_ref, wd_ref, x_ref, g_ref, wr_hi_ref, wr_lo_ref, br_ref,
                     h_ref, u_ref, e_ref, p_ref, c_ref, *, n_exp):
    h = (x_ref[...]
         + jnp.dot(ms_ref[...], ws_ref[...], preferred_element_type=F32)
         + jnp.dot(md_ref[...], wd_ref[...], preferred_element_type=F32))
    h_ref[...] = h
    msq = jnp.mean(h * h, axis=-1, keepdims=True)
    u = (h * lax.rsqrt(msq + RMS_EPS)) * g_ref[...]
    u_ref[...] = _pack_bf16_pairs(u)

    logits = _split3_dot(u, wr_hi_ref[...], wr_lo_ref[...]) + br_ref[...]
    lane = lax.broadcasted_iota(jnp.int32, logits.shape, 1).astype(F32)
    work = logits
    tops, idxs = [], []
    for _ in range(TOP_K):
        mx = jnp.max(work, axis=-1, keepdims=True)
        ix = jnp.min(jnp.where(work == mx, lane, float(n_exp)), axis=-1, keepdims=True)
        tops.append(mx)
        idxs.append(ix)
        work = jnp.where(lane == ix, -jnp.inf, work)
    ex = [jnp.exp(t - tops[0]) for t in tops]
    den = ex[0] + ex[1] + ex[2] + ex[3]
    kl = lax.broadcasted_iota(jnp.int32, (logits.shape[0], TOP_K), 1)
    e_out = jnp.zeros((logits.shape[0], TOP_K), F32)
    p_out = jnp.zeros((logits.shape[0], TOP_K), F32)
    for k in range(TOP_K):
        e_out = jnp.where(kl == k, idxs[k], e_out)
        p_out = jnp.where(kl == k, ex[k] / den, p_out)
    e_ref[...] = e_out.astype(jnp.int32)
    p_ref[...] = p_out
    hits = jnp.zeros(logits.shape, F32)
    for k in range(TOP_K):
        hits = hits + jnp.where(lane == idxs[k], 1.0, 0.0)
    c_ref[...] = jnp.sum(hits, axis=0, keepdims=True)


def _out_proj(mix_sb, mix_da, w_sb, w_da, x2d, g_ffn, wr_hi, wr_lo, b_r, *, tm):
    m, d = x2d.shape
    ks, kd = mix_sb.shape[1], mix_da.shape[1]
    n_exp = wr_hi.shape[1]
    kern = functools.partial(_out_proj_kernel, n_exp=n_exp)
    const = lambda shape: pl.BlockSpec(shape, lambda i: (0, 0))
    return pl.pallas_call(
        kern,
        grid=(m // tm,),
        in_specs=[
            pl.BlockSpec((tm, ks), lambda i: (i, 0)),
            pl.BlockSpec((tm, kd), lambda i: (i, 0)),
            const((ks, d)), const((kd, d)),
            pl.BlockSpec((tm, d), lambda i: (i, 0)),
            const((1, d)), const((d, n_exp)), const((d, n_exp)), const((1, n_exp)),
        ],
        out_specs=[
            pl.BlockSpec((tm, d), lambda i: (i, 0)),
            pl.BlockSpec((tm, d // 2), lambda i: (i, 0)),
            pl.BlockSpec((tm, TOP_K), lambda i: (i, 0)),
            pl.BlockSpec((tm, TOP_K), lambda i: (i, 0)),
            pl.BlockSpec((None, 1, n_exp), lambda i: (i, 0, 0)),
        ],
        out_shape=[
            jax.ShapeDtypeStruct((m, d), F32),
            jax.ShapeDtypeStruct((m, d // 2), jnp.int32),
            jax.ShapeDtypeStruct((m, TOP_K), jnp.int32),
            jax.ShapeDtypeStruct((m, TOP_K), F32),
            jax.ShapeDtypeStruct((m // tm, 1, n_exp), F32),
        ],
        compiler_params=_cparams(("parallel",)),
        name="out_proj_router",
    )(mix_sb, mix_da, w_sb, w_da, x2d, g_ffn, wr_hi, wr_lo, b_r)


def _moe_kernel(be_ref, bc_ref, rb_ref, x_ref, wg_ref, wu_ref, wda_ref, wdb_ref, bg_ref, bup_ref, bda_ref, bdb_ref,
                o_ref, act_ref, *, nf, ts):
    blk = pl.program_id(0)
    s = pl.program_id(1)
    cnt = bc_ref[blk]
    tg = 2 * ts
    tm = x_ref.shape[0]

    def row_groups(run, skip):
        for lo in range(0, tm, tg):
            size = min(tg, tm - lo)

            @pl.when(cnt > lo + size - ts)
            def _():
                run(lo, size)

            if size > ts:
                @pl.when(jnp.logical_and(cnt > lo, cnt <= lo + ts))
                def _():
                    run(lo, ts)
                    skip(lo + ts, ts)

            @pl.when(cnt <= lo)
            def _():
                skip(lo, size)

    @pl.when(jnp.logical_and(cnt > 0, s < nf))
    def _():
        wg = wg_ref[...].astype(BF16)
        wu = wu_ref[...].astype(BF16)
        slot = jnp.minimum(s, nf - 1)

        def up_proj(r0, nrows):
            x = _unpack_bf16_pairs(x_ref[r0:r0 + nrows, :])
            g = jnp.dot(x, wg, preferred_element_type=F32) + bg_ref[...]
            u = jnp.dot(x, wu, preferred_element_type=F32) + bup_ref[...]
            gate = jnp.minimum(g, SWIGLU_LIMIT)
            up = jnp.clip(u, -SWIGLU_LIMIT, SWIGLU_LIMIT)
            act = (up + 1.0) * gate * jax.nn.sigmoid(SWIGLU_ALPHA * gate)
            act_ref[slot, r0:r0 + nrows, :] = act.astype(BF16)

        row_groups(up_proj, lambda r0, nrows: None)

    @pl.when(jnp.logical_and(cnt > 0, s >= nf))
    def _():
        tf = act_ref.shape[2]
        wda = wda_ref[...].astype(BF16)
        wdb = wdb_ref[...].astype(BF16)

        def down_proj(r0, nrows):
            ya = bda_ref[...]
            yb = bdb_ref[...]
            for f in range(nf):
                a = act_ref[f, r0:r0 + nrows, :]
                ya = ya + jnp.dot(a, wda[f * tf:(f + 1) * tf, :], preferred_element_type=F32)
                yb = yb + jnp.dot(a, wdb[f * tf:(f + 1) * tf, :], preferred_element_type=F32)
            o_ref[r0:r0 + nrows, :] = _pack_bf16_pairs(jnp.concatenate([ya, yb], axis=1))

        def zero_fill(r0, nrows):
            o_ref[r0:r0 + nrows, :] = jnp.zeros((nrows, o_ref.shape[1]), o_ref.dtype)

        row_groups(down_proj, zero_fill)


def _moe_ffn(blk_e, blk_cnt, blk_row, xs, w_gu, w_dn, b_gu, b_dn, *, tm, ts, tf, tn):
    n_rows = xs.shape[0]
    n_exp, d, f2 = w_gu.shape
    d_ff = f2 // 2
    nf = d_ff // tf
    nn = d // tn
    n_blk = n_rows // tm
    b_gu3 = b_gu.reshape(n_exp, 1, f2)
    b_dn3 = b_dn.reshape(n_exp, 1, d)

    def fi(s, bc, b):
        return jnp.where(bc[b] > 0, jnp.minimum(s, nf - 1), nf - 1)

    nh = nn // 2

    def ni(s, bc, b):
        return jnp.where(bc[b] > 0, jnp.clip(s - nf, 0, nh - 1), nh - 1)

    grid_spec = pltpu.PrefetchScalarGridSpec(
        num_scalar_prefetch=3,
        grid=(n_blk, nf + nh),
        in_specs=[
            pl.BlockSpec((tm, d // 2), lambda b, s, be, bc, rb: (rb[b], 0)),
            pl.BlockSpec((None, d, tf), lambda b, s, be, bc, rb: (be[b], 0, fi(s, bc, b))),
            pl.BlockSpec((None, d, tf), lambda b, s, be, bc, rb: (be[b], 0, nf + fi(s, bc, b))),
            pl.BlockSpec((None, d_ff, tn), lambda b, s, be, bc, rb: (be[b], 0, ni(s, bc, b))),
            pl.BlockSpec((None, d_ff, tn), lambda b, s, be, bc, rb: (be[b], 0, nh + ni(s, bc, b))),
            pl.BlockSpec((None, 1, tf), lambda b, s, be, bc, rb: (be[b], 0, fi(s, bc, b))),
            pl.BlockSpec((None, 1, tf), lambda b, s, be, bc, rb: (be[b], 0, nf + fi(s, bc, b))),
            pl.BlockSpec((None, 1, tn), lambda b, s, be, bc, rb: (be[b], 0, ni(s, bc, b))),
            pl.BlockSpec((None, 1, tn), lambda b, s, be, bc, rb: (be[b], 0, nh + ni(s, bc, b))),
        ],
        out_specs=pl.BlockSpec((tm, tn), lambda b, s, be, bc, rb: (rb[b], ni(s, bc, b))),
        scratch_shapes=[pltpu.VMEM((nf, tm, tf), BF16)],
    )
    return pl.pallas_call(
        functools.partial(_moe_kernel, nf=nf, ts=ts),
        grid_spec=grid_spec,
        out_shape=jax.ShapeDtypeStruct((n_rows, d // 2), jnp.int32),
        compiler_params=_cparams(("arbitrary", "arbitrary")),
        name="moe_ffn",
    )(blk_e, blk_cnt, blk_row, xs, w_gu, w_gu, w_dn, w_dn, b_gu3, b_gu3, b_dn3, b_dn3)


def _final_kernel(h_ref, y0_ref, y1_ref, y2_ref, y3_ref, p_ref, g_ref, o_ref):
    p = p_ref[...]
    y = None
    for k, y_ref in enumerate((y0_ref, y1_ref, y2_ref, y3_ref)):
        t = _unpack_bf16_pairs(y_ref[...]).astype(F32) * p[:, k:k + 1]
        y = t if y is None else y + t
    h = h_ref[...] + y
    ms = jnp.mean(h * h, axis=-1, keepdims=True)
    o_ref[...] = (h * lax.rsqrt(ms + RMS_EPS)) * g_ref[...]


def _final(h1, y_km, gates, g, *, tm):
    m, d = h1.shape
    nb = m // tm
    assert TOP_K == 4
    y_specs = [pl.BlockSpec((tm, d // 2), functools.partial(lambda i, k: (k * nb + i, 0), k=k))
               for k in range(TOP_K)]
    return pl.pallas_call(
        _final_kernel,
        grid=(nb,),
        in_specs=([pl.BlockSpec((tm, d), lambda i: (i, 0))] + y_specs
                  + [pl.BlockSpec((tm, TOP_K), lambda i: (i, 0)), pl.BlockSpec((1, d), lambda i: (0, 0))]),
        out_specs=pl.BlockSpec((tm, d), lambda i: (i, 0)),
        out_shape=jax.ShapeDtypeStruct((m, d), F32),
        compiler_params=_cparams(("parallel",)),
        name="final_norm",
    )(h1, y_km, y_km, y_km, y_km, gates, g)


def _moe_blocks(n_assign, n_exp, tm):
    n_blk = -(-n_assign // tm) + n_exp
    unit = SC_CORES * SC_SUBCORES * SC_GATHER_ROWS
    while (n_blk * tm) % unit:
        n_blk += 1
    return n_blk


def _routing(top_e, counts, tm):
    n_exp = counts.shape[0]
    n_tok = top_e.shape[0]
    n_assign = n_tok * TOP_K
    flat_e = top_e.reshape(-1)
    padded = (counts + tm - 1) // tm * tm
    pad_end = jnp.cumsum(padded).astype(jnp.int32)
    pad_start = pad_end - padded
    n_blk = _moe_blocks(n_assign, n_exp, tm)
    n_rows = n_blk * tm
    blk_start = jnp.arange(n_blk, dtype=jnp.int32) * tm
    blk_e = jnp.minimum(jnp.sum((blk_start[:, None] >= pad_end[None, :]).astype(jnp.int32), axis=1), n_exp - 1)
    blk_off = blk_start - pad_start[blk_e]
    blk_cnt = jnp.clip(counts[blk_e] - blk_off, 0, tm).astype(jnp.int32)
    n_fill = n_rows - n_assign
    j = jnp.arange(tm, dtype=jnp.int32)[None, :]
    e_ids = jnp.arange(n_exp, dtype=jnp.int32)[:, None]
    fill_keys = jnp.where(j < (padded - counts)[:, None], e_ids, n_exp).reshape(-1)
    fill_keys = jnp.concatenate([fill_keys, jnp.full((n_fill - n_exp * tm,), n_exp, jnp.int32)])
    keys = jnp.concatenate([flat_e, fill_keys])
    toks = jnp.concatenate([jnp.arange(n_assign, dtype=jnp.int32) // TOP_K,
                            jnp.arange(n_fill, dtype=jnp.int32) % n_tok])
    _, row_tok, entry = lax.sort((keys, toks, jnp.arange(n_rows, dtype=jnp.int32)), num_keys=1, is_stable=True)
    dest_flat = jnp.argsort(entry).astype(jnp.int32)[:n_assign]
    last_used = jnp.maximum(pad_end[-1] // tm - 1, 0)
    blk_e = jnp.where(blk_cnt > 0, blk_e, blk_e[last_used]).astype(jnp.int32)
    blk_row = jnp.where(blk_cnt > 0, jnp.arange(n_blk, dtype=jnp.int32), last_used).astype(jnp.int32)
    return row_tok, dest_flat, blk_e, blk_cnt, blk_row


def kernel(x, meta_tokens, g_mix, w_in, lam_q1, lam_k1, lam_q2, lam_k2, g_sb_out, g_da_out, w_out, g_ffn,
           w_router, b_router, w_gate_up, b_gate_up, w_down, b_down, g_final):
    b, s, d = x.shape
    depth = w_in.shape[0]
    assert depth == 1, "single-layer trunk"
    layer = 0
    sb_heads = (d // 2) // SB_HEAD_DIM
    da_heads = (d // 2) // DA_V_DIM
    sb_w = sb_heads * SB_HEAD_DIM
    da_col0 = 3 * sb_w
    da_qk_w = da_heads * 2 * DA_HEAD_DIM
    n_exp = w_router.shape[-1]
    lam_init = 0.8 - 0.6 * math.exp(-0.3 * layer)

    tm_proj, tn_proj = TM_PROJ, TN_PROJ
    tq, tk_sb, ts_da = TQ_ATTN, TK_SB, TS_DA
    tm_out = TM_OUT
    tm_moe, tf_moe = TM_MOE, TF_MOE
    tm_fin = TM_FINAL

    x2d = x.reshape(b * s, d)
    w_in_bf = w_in.reshape(w_in.shape[1:]).astype(BF16)
    g_mix2 = g_mix[layer].reshape(1, d)
    rope_cols = (da_col0, da_col0 + 2 * da_qk_w)

    pos_f = N_META + jnp.arange(s, dtype=jnp.int32)
    proj = _in_proj(x2d, g_mix2, w_in_bf, _rope_tables(pos_f), tm=tm_proj, tn=tn_proj,
                    rope_cols=rope_cols, pos_blocks=s // tm_proj)
    meta_pad = jnp.zeros((META_PAD, d), x.dtype).at[:N_META].set(meta_tokens.astype(x.dtype))
    pos_m = jnp.arange(META_PAD, dtype=jnp.int32)
    proj_meta = _in_proj(meta_pad, g_mix2, w_in_bf, _rope_tables(pos_m), tm=META_PAD, tn=tn_proj,
                         rope_cols=rope_cols, pos_blocks=1)
    proj3 = proj.reshape(b, s, -1)

    mix_sb = _sb_attention(proj3, proj_meta, g_sb_out[layer].reshape(1, -1), batch=b, seq=s, heads=sb_heads,
                           tq=tq, tk=tk_sb, hp=HEADS_PER_STEP_SB)
    lams = tuple(t[layer].reshape(1, DA_HEAD_DIM).astype(F32) for t in (lam_q1, lam_k1, lam_q2, lam_k2))
    mix_da = _da_attention(proj3, proj_meta, lams, g_da_out[layer].reshape(1, -1), batch=b, seq=s, heads=da_heads,
                           tq=TQ_DA, tkv=TKV_DA, ts=ts_da, col0=da_col0, lam_init=lam_init)

    w_out_bf = w_out[layer].astype(BF16)
    wr = w_router[layer]
    wr_hi = wr.astype(BF16)
    wr_lo = (wr - wr_hi.astype(F32)).astype(BF16)
    h1, u_packed, top_e, gates, tile_counts = _out_proj(
        mix_sb.reshape(b * s, -1), mix_da.reshape(b * s, -1), w_out_bf[:sb_w], w_out_bf[sb_w:], x2d,
        g_ffn[layer].reshape(1, d), wr_hi, wr_lo, b_router[layer].reshape(1, n_exp), tm=tm_out)

    counts = jnp.sum(tile_counts, axis=(0, 1)).astype(jnp.int32)
    row_tok, dest_flat, blk_e, blk_cnt, blk_row = _routing(top_e, counts, tm_moe)
    xs = _sc_gather_rows(u_packed, row_tok)
    rows = _moe_ffn(blk_e, blk_cnt, blk_row, xs, w_gate_up.reshape(w_gate_up.shape[1:]),
                    w_down.reshape(w_down.shape[1:]), b_gate_up[layer], b_down[layer],
                    tm=tm_moe, ts=TS_MOE, tf=tf_moe, tn=TN_MOE)
    dest_km = dest_flat.reshape(b * s, TOP_K).T.reshape(-1)
    y_km = _sc_gather_rows(rows, dest_km)

    out = _final(h1, y_km, gates, g_final.reshape(1, d), tm=tm_fin)
    return out.reshape(b, s, d)
```

```python
import functools
import math

import jax
import jax.numpy as jnp
from jax import lax
from jax.experimental import pallas as pl
from jax.experimental.pallas import tpu as pltpu
from jax.experimental.pallas import tpu_sc as plsc

F32 = jnp.float32
BF16 = jnp.bfloat16

CHUNK = 64
N_META = 16
RMS_EPS = 1e-5
SB_HEAD_DIM = 128
DA_HEAD_DIM = 64
DA_V_DIM = 2 * DA_HEAD_DIM
ROPE_THETA = 500000.0
ROPE_DIM = DA_HEAD_DIM // 4
TOP_K = 4
SWIGLU_LIMIT = 7.0
SWIGLU_ALPHA = 1.702

LANES = 128
META_PAD = 128
NEG_BIG = -1e30
SB_DEAD_RUN = 110.0
VMEM_LIMIT = 56 * 1024 * 1024
SC_CORES, SC_SUBCORES = 2, 16
SC_GATHER_ROWS = 32

TM_PROJ, TN_PROJ = 1024, 1024
TQ_ATTN = 512
HEADS_PER_STEP_SB = 2
TQ_DA, TKV_DA = 1024, 512
TK_SB = 256
TS_DA = 256
TM_OUT = 512
TM_MOE, TS_MOE = 2048, 512
TF_MOE, TN_MOE = 256, 256
TM_FINAL = 512


def _cparams(sem):
    return pltpu.CompilerParams(dimension_semantics=sem, vmem_limit_bytes=VMEM_LIMIT)


def _in_proj_kernel(x_ref, g_ref, w_ref, c_ref, sa_ref, sb_ref, o_ref, u_scr, *, tn, rope_lo, rope_hi):
    n = pl.program_id(1)

    @pl.when(n == 0)
    def _():
        x = x_ref[...]
        ms = jnp.mean(x * x, axis=-1, keepdims=True)
        u_scr[...] = ((x * lax.rsqrt(ms + RMS_EPS)) * g_ref[...]).astype(BF16)

    acc = jnp.dot(u_scr[...], w_ref[...], preferred_element_type=F32)
    is_rope = jnp.logical_and(n >= rope_lo, n < rope_hi)
    o_ref[...] = acc.astype(BF16)

    @pl.when(is_rope)
    def _():
        for c in range(tn // LANES):
            xc = acc[:, c * LANES:(c + 1) * LANES]
            r = (xc * c_ref[...] + pltpu.roll(xc, LANES - ROPE_DIM // 2, 1) * sa_ref[...]
                 + pltpu.roll(xc, ROPE_DIM // 2, 1) * sb_ref[...])
            o_ref[:, c * LANES:(c + 1) * LANES] = r.astype(BF16)


def _rope_tables(pos):
    half = ROPE_DIM // 2
    inv_freq = ROPE_THETA ** (-(jnp.arange(half, dtype=F32) * 2.0 / ROPE_DIM))
    ang = pos.astype(F32)[:, None] * inv_freq[None, :]
    cos, sin = jnp.cos(ang), jnp.sin(ang)
    p = pos.shape[0]
    ones = jnp.ones((p, DA_HEAD_DIM - ROPE_DIM), F32)
    zeros8 = jnp.zeros((p, half), F32)
    zeros48 = jnp.zeros((p, DA_HEAD_DIM - ROPE_DIM), F32)
    c64 = jnp.concatenate([cos, cos, ones], axis=1)
    sa64 = jnp.concatenate([-sin, zeros8, zeros48], axis=1)
    sb64 = jnp.concatenate([zeros8, sin, zeros48], axis=1)
    tile2 = lambda t: jnp.concatenate([t, t], axis=1)
    return tile2(c64), tile2(sa64), tile2(sb64)


def _in_proj(x2d, g, w_bf, tables, *, tm, tn, rope_cols, pos_blocks):
    m, d = x2d.shape
    n_cols = w_bf.shape[1]
    c_t, sa_t, sb_t = tables
    kern = functools.partial(_in_proj_kernel, tn=tn, rope_lo=rope_cols[0] // tn, rope_hi=rope_cols[1] // tn)
    tab_spec = pl.BlockSpec((tm, LANES), lambda i, n: (i % pos_blocks, 0))
    return pl.pallas_call(
        kern,
        grid=(m // tm, n_cols // tn),
        in_specs=[
            pl.BlockSpec((tm, d), lambda i, n: (i, 0)),
            pl.BlockSpec((1, d), lambda i, n: (0, 0)),
            pl.BlockSpec((d, tn), lambda i, n: (0, n)),
            tab_spec, tab_spec, tab_spec,
        ],
        out_specs=pl.BlockSpec((tm, tn), lambda i, n: (i, n)),
        out_shape=jax.ShapeDtypeStruct((m, n_cols), BF16),
        scratch_shapes=[pltpu.VMEM((tm, d), BF16)],
        compiler_params=_cparams(("parallel", "arbitrary")),
        name="in_proj",
    )(x2d, g, w_bf, c_t, sa_t, sb_t)


def _dot_nt(a, b):
    return lax.dot_general(a, b, (((1,), (1,)), ((), ())), preferred_element_type=F32)


def _suffix_sum_matrix(n):
    j = lax.broadcasted_iota(jnp.int32, (2 * n, n), 0)
    s = lax.broadcasted_iota(jnp.int32, (2 * n, n), 1)
    return jnp.where(jnp.where(j >= n, j - n, j) > s, 1.0, 0.0).astype(BF16)


def _sb_block(q, kb, vb, u2, run, acc, mask, scale):
    z = _dot_nt(q, kb) * scale
    sp = jnp.maximum(z, 0.0) + jnp.log(1.0 + jnp.exp(-jnp.abs(z)))
    if mask is not None:
        sp = jnp.where(mask, sp, 0.0)
    hi = sp.astype(BF16)
    lo = (sp - hi.astype(F32)).astype(BF16)
    cs = jnp.dot(jnp.concatenate([hi, lo], axis=1), u2, preferred_element_type=F32)
    w = jnp.exp(z - sp - cs - run)
    if mask is not None:
        w = jnp.where(mask, w, 0.0)
    acc = acc + jnp.dot(w.astype(BF16), vb, preferred_element_type=F32)
    run = run + (cs[:, :1] + sp[:, :1])
    return run, acc


def _sb_kernel(q_ref, k_ref, v_ref, km_ref, vm_ref, g_ref, o_ref, *, tq, tk, hp, scale):
    i = pl.program_id(2)
    nsub = tq // tk
    hd = SB_HEAD_DIM
    u_blk = _suffix_sum_matrix(tk)
    row = lax.broadcasted_iota(jnp.int32, (tk, tk), 0)
    col = lax.broadcasted_iota(jnp.int32, (tk, tk), 1)
    diag_mask = col < row
    heads = range(hp)

    def kv(blk, h):
        s0 = pl.multiple_of(blk * tk, tk)
        return k_ref[pl.ds(s0, tk), h * hd:(h + 1) * hd], v_ref[pl.ds(s0, tk), h * hd:(h + 1) * hd]

    runs, accs = [], []
    for h in heads:
        slab_runs, slab_accs = [], []
        for sl in range(nsub):
            qs = q_ref[sl * tk:(sl + 1) * tk, h * hd:(h + 1) * hd]
            run = jnp.zeros((tk, 1), F32)
            acc = jnp.zeros((tk, hd), F32)
            for c in range(sl, -1, -1):
                kb, vb = kv(i * nsub + c, h)
                run, acc = _sb_block(qs, kb, vb, u_blk, run, acc, diag_mask if c == sl else None, scale)
            slab_runs.append(run)
            slab_accs.append(acc)
        runs.append(jnp.concatenate(slab_runs, axis=0))
        accs.append(jnp.concatenate(slab_accs, axis=0))
    q = [q_ref[:, h * hd:(h + 1) * hd] for h in heads]

    def alive_flag(rs):
        lowest = functools.reduce(jnp.minimum, [jnp.min(r) for r in rs])
        return (lowest < SB_DEAD_RUN).astype(jnp.int32)

    def cond(carry):
        jb, alive, _, _ = carry
        return jnp.logical_and(jb >= 0, alive > 0)

    def body(carry):
        jb, _, rs, as_ = carry
        out = [_sb_block(q[h], *kv(jb, h), u_blk, rs[h], as_[h], None, scale) for h in heads]
        rs = tuple(o[0] for o in out)
        return jb - 1, alive_flag(rs), rs, tuple(o[1] for o in out)

    runs, accs = tuple(runs), tuple(accs)
    _, alive, runs, accs = lax.while_loop(cond, body, (i * nsub - 1, alive_flag(runs), runs, accs))

    def meta_block():
        mcol = lax.broadcasted_iota(jnp.int32, (tq, META_PAD), 1)
        u_meta = _suffix_sum_matrix(META_PAD)
        return tuple(_sb_block(q[h], km_ref[:, h * hd:(h + 1) * hd], vm_ref[:, h * hd:(h + 1) * hd], u_meta,
                               runs[h], accs[h], mcol < N_META, scale)[1] for h in heads)

    accs = lax.cond(alive > 0, meta_block, lambda: accs)

    for h in heads:
        acc = accs[h]
        ms = jnp.mean(acc * acc, axis=-1, keepdims=True)
        gain = g_ref[:, h * hd:(h + 1) * hd]
        o_ref[:, h * hd:(h + 1) * hd] = ((acc * lax.rsqrt(ms + RMS_EPS)) * gain).astype(BF16)


def _sb_attention(proj, proj_meta, g_sb, *, batch, seq, heads, tq, tk, hp):
    hd = SB_HEAD_DIM
    wd = hp * hd
    ng = heads // hp
    kern = functools.partial(_sb_kernel, tq=tq, tk=tk, hp=hp, scale=hd ** -0.5)
    return pl.pallas_call(
        kern,
        grid=(batch, ng, seq // tq),
        in_specs=[
            pl.BlockSpec((None, tq, wd), lambda b, h, i: (b, i, h)),
            pl.BlockSpec((None, seq, wd), lambda b, h, i: (b, 0, ng + h)),
            pl.BlockSpec((None, seq, wd), lambda b, h, i: (b, 0, 2 * ng + h)),
            pl.BlockSpec((META_PAD, wd), lambda b, h, i: (0, ng + h)),
            pl.BlockSpec((META_PAD, wd), lambda b, h, i: (0, 2 * ng + h)),
            pl.BlockSpec((1, wd), lambda b, h, i: (0, h)),
        ],
        out_specs=pl.BlockSpec((None, tq, wd), lambda b, h, i: (b, i, h)),
        out_shape=jax.ShapeDtypeStruct((batch, seq, heads * hd), BF16),
        compiler_params=_cparams(("parallel", "parallel", "arbitrary")),
        name="sb_attn",
    )(proj, proj, proj, proj_meta, proj_meta, g_sb)


def _with_ones(vb):
    return jnp.concatenate([vb, jnp.ones_like(vb)], axis=1)


def _da_block(q1, q2, kb, vb1, st, mask):
    m1, a1, m2, a2 = st

    def one(qc, m, a):
        s = _dot_nt(qc, kb)
        if mask is not None:
            s = jnp.where(mask, s, NEG_BIG)
        mn = jnp.maximum(m, jnp.max(s, axis=-1, keepdims=True))
        p = jnp.exp(s - mn)
        a = jnp.exp(m - mn) * a + jnp.dot(p.astype(BF16), vb1, preferred_element_type=F32)
        return mn, a

    m1, a1 = one(q1, m1, a1)
    m2, a2 = one(q2, m2, a2)
    return m1, a1, m2, a2


def _da_kernel(lq1_ref, lk1_ref, lq2_ref, lk2_ref, q_ref, k_ref, v_ref, km_ref, vm_ref, g_ref, o_ref,
               *, tq, tkv, ts, scale, lam_init):
    i = pl.program_id(2)
    nsub = tq // ts
    lam = (jnp.exp(jnp.sum(lq1_ref[...] * lk1_ref[...], axis=-1, keepdims=True))
           - jnp.exp(jnp.sum(lq2_ref[...] * lk2_ref[...], axis=-1, keepdims=True)) + lam_init)

    q = q_ref[...]
    lane = lax.broadcasted_iota(jnp.int32, q.shape, 1)
    qs = q * jnp.asarray(scale, BF16)
    zero = jnp.zeros_like(qs)
    q1 = jnp.where(lane < DA_HEAD_DIM, qs, zero)
    q2 = jnp.where(lane >= DA_HEAD_DIM, qs, zero)

    neg = jnp.full((tq, 1), NEG_BIG, F32)
    za = jnp.zeros((tq, 2 * DA_V_DIM), F32)
    st = (neg, za, neg, za)

    def body(j, carry):
        for c in range(tq // tkv):
            s0 = pl.multiple_of(j * tq + c * tkv, tkv)
            carry = _da_block(q1, q2, k_ref[pl.ds(s0, tkv), :], _with_ones(v_ref[pl.ds(s0, tkv), :]), carry, None)
        return carry

    st = lax.fori_loop(0, i, body, st)

    shift = CHUNK.bit_length() - 1
    s0 = pl.multiple_of(i * tq, tq)
    outs = []
    for h in range(nsub):
        rows = slice(h * ts, (h + 1) * ts)
        width = (h + 1) * ts
        kb = jnp.concatenate([km_ref[...], k_ref[pl.ds(s0, width), :]], axis=0)
        vb = jnp.concatenate([vm_ref[...], v_ref[pl.ds(s0, width), :]], axis=0)
        col = lax.broadcasted_iota(jnp.int32, (ts, META_PAD + width), 1)
        row = lax.broadcasted_iota(jnp.int32, (ts, META_PAD + width), 0) + h * ts
        frame_vis = jnp.logical_and(col >= META_PAD,
                                    jnp.right_shift(col - META_PAD, shift) <= jnp.right_shift(row, shift))
        vis = jnp.logical_or(col < N_META, frame_vis)
        sth = _da_block(q1[rows], q2[rows], kb, _with_ones(vb), tuple(t[rows] for t in st), vis)
        _, a1, _, a2 = sth
        outs.append(a1[:, :DA_V_DIM] / a1[:, DA_V_DIM:] - lam * (a2[:, :DA_V_DIM] / a2[:, DA_V_DIM:]))
    o = jnp.concatenate(outs, axis=0)
    ms = jnp.mean(o * o, axis=-1, keepdims=True)
    o_ref[...] = (((o * lax.rsqrt(ms + RMS_EPS)) * g_ref[...]) * (1.0 - lam_init)).astype(BF16)


def _da_attention(proj, proj_meta, lams, g_da, *, batch, seq, heads, tq, tkv, ts, col0, lam_init):
    hd = DA_V_DIM
    qb, kb, vb = col0 // hd, col0 // hd + heads, col0 // hd + 2 * heads
    kern = functools.partial(_da_kernel, tq=tq, tkv=tkv, ts=ts, scale=DA_HEAD_DIM ** -0.5, lam_init=lam_init)
    lam_spec = pl.BlockSpec((1, DA_HEAD_DIM), lambda b, h, i: (0, 0))
    return pl.pallas_call(
        kern,
        grid=(batch, heads, seq // tq),
        in_specs=[
            lam_spec, lam_spec, lam_spec, lam_spec,
            pl.BlockSpec((None, tq, hd), lambda b, h, i: (b, i, qb + h)),
            pl.BlockSpec((None, seq, hd), lambda b, h, i: (b, 0, kb + h)),
            pl.BlockSpec((None, seq, hd), lambda b, h, i: (b, 0, vb + h)),
            pl.BlockSpec((META_PAD, hd), lambda b, h, i: (0, kb + h)),
            pl.BlockSpec((META_PAD, hd), lambda b, h, i: (0, vb + h)),
            pl.BlockSpec((1, hd), lambda b, h, i: (0, h)),
        ],
        out_specs=pl.BlockSpec((None, tq, hd), lambda b, h, i: (b, i, h)),
        out_shape=jax.ShapeDtypeStruct((batch, seq, heads * hd), BF16),
        compiler_params=_cparams(("parallel", "parallel", "arbitrary")),
        name="da_attn",
    )(*lams, proj, proj, proj, proj_meta, proj_meta, g_da)


def _pack_bf16_pairs(x):
    n = x.shape[1] // 2
    lo = lax.bitcast_convert_type(x[:, :n].astype(BF16).astype(F32), jnp.int32)
    hi = lax.bitcast_convert_type(x[:, n:].astype(BF16).astype(F32), jnp.int32)
    return jnp.bitwise_or(lax.shift_right_logical(lo, 16), jnp.bitwise_and(hi, jnp.int32(-65536)))


def _unpack_bf16_pairs(w):
    lo = lax.bitcast_convert_type(lax.shift_left(w, 16), F32)
    hi = lax.bitcast_convert_type(jnp.bitwise_and(w, jnp.int32(-65536)), F32)
    return jnp.concatenate([lo.astype(BF16), hi.astype(BF16)], axis=1)


def _sc_gather_rows(table, idx):
    n_idx = idx.shape[0]
    width = table.shape[1]
    n_workers = SC_CORES * SC_SUBCORES
    per_worker = n_idx // n_workers
    win = SC_GATHER_ROWS
    assert n_idx % (n_workers * win) == 0
    mesh = plsc.VectorSubcoreMesh(core_axis_name="c", subcore_axis_name="s")

    n_win = per_worker // win
    assert n_win % 2 == 0
    buf = lambda: pltpu.VMEM((win, width), table.dtype)
    dma = pltpu.SemaphoreType.DMA

    @functools.partial(
        pl.kernel, mesh=mesh,
        out_type=jax.ShapeDtypeStruct((n_idx, width), table.dtype),
        scratch_types=[pltpu.VMEM((n_win, win), jnp.int32), buf(), buf(), dma, dma, dma, dma],
    )
    def gather(table_hbm, idx_hbm, out_hbm, idx_v, rows0, rows1, gsem0, gsem1, wsem0, wsem1):
        worker = lax.axis_index("s") * SC_CORES + lax.axis_index("c")
        base = worker * per_worker
        pltpu.sync_copy(idx_hbm.at[pl.ds(worker * n_win, n_win)], idx_v)

        def gather_copy(t, rows_v, sem):
            return pltpu.make_async_copy(table_hbm.at[idx_v.at[t]], rows_v, sem)

        def writeback(t, rows_v, sem):
            return pltpu.make_async_copy(rows_v, out_hbm.at[pl.ds(pl.multiple_of(base + t * win, win), win)], sem)

        gather_copy(0, rows0, gsem0).start()

        @pl.loop(0, n_win, step=2)
        def _(t):
            gather_copy(t, rows0, gsem0).wait()
            writeback(t, rows0, wsem0).start()

            @pl.when(t > 0)
            def _():
                writeback(t - 1, rows1, wsem1).wait()

            gather_copy(t + 1, rows1, gsem1).start()
            gather_copy(t + 1, rows1, gsem1).wait()
            writeback(t + 1, rows1, wsem1).start()
            writeback(t, rows0, wsem0).wait()

            @pl.when(t + 2 < n_win)
            def _():
                gather_copy(t + 2, rows0, gsem0).start()

        writeback(n_win - 1, rows1, wsem1).wait()

    return gather(table, idx.reshape(n_idx // win, win))


def _split3_dot(a, b_hi, b_lo):
    a_hi = a.astype(BF16)
    a_lo = (a - a_hi.astype(F32)).astype(BF16)
    return (jnp.dot(a_hi, b_hi, preferred_element_type=F32)
            + jnp.dot(a_hi, b_lo, preferred_element_type=F32)
            + jnp.dot(a_lo, b_hi, preferred_element_type=F32))


def _out_proj_kernel(ms_ref, md_ref, ws_ref, wd_ref, x_ref, g_ref, wr_hi_ref, wr_lo_ref, br_ref,
                     h_ref, u_ref, e_ref, p_ref, c_ref, *, n_exp):
    h = (x_ref[...]
         + jnp.dot(ms_ref[...], ws_ref[...], preferred_element_type=F32)
         + jnp.dot(md_ref[...], wd_ref[...], preferred_element_type=F32))
    h_ref[...] = h
    msq = jnp.mean(h * h, axis=-1, keepdims=True)
    u = (h * lax.rsqrt(msq + RMS_EPS)) * g_ref[...]
    u_ref[...] = _pack_bf16_pairs(u)

    logits = _split3_dot(u, wr_hi_ref[...], wr_lo_ref[...]) + br_ref[...]
    lane = lax.broadcasted_iota(jnp.int32, logits.shape, 1).astype(F32)
    work = logits
    tops, idxs = [], []
    for _ in range(TOP_K):
        mx = jnp.max(work, axis=-1, keepdims=True)
        ix = jnp.min(jnp.where(work == mx, lane, float(n_exp)), axis=-1, keepdims=True)
        tops.append(mx)
        idxs.append(ix)
        work = jnp.where(lane == ix, -jnp.inf, work)
    ex = [jnp.exp(t - tops[0]) for t in tops]
    den = ex[0] + ex[1] + ex[2] + ex[3]
    kl = lax.broadcasted_iota(jnp.int32, (logits.shape[0], TOP_K), 1)
    e_out = jnp.zeros((logits.shape[0], TOP_K), F32)
    p_out = jnp.zeros((logits.shape[0], TOP_K), F32)
    for k in range(TOP_K):
        e_out = jnp.where(kl == k, idxs[k], e_out)
        p_out = jnp.where(kl == k, ex[k] / den, p_out)
    e_ref[...] = e_out.astype(jnp.int32)
    p_ref[...] = p_out
    hits = jnp.zeros(logits.shape, F32)
    for k in range(TOP_K):
        hits = hits + jnp.where(lane == idxs[k], 1.0, 0.0)
    c_ref[...] = jnp.sum(hits, axis=0, keepdims=True)


def _out_proj(mix_sb, mix_da, w_sb, w_da, x2d, g_ffn, wr_hi, wr_lo, b_r, *, tm):
    m, d = x2d.shape
    ks, kd = mix_sb.shape[1], mix_da.shape[1]
    n_exp = wr_hi.shape[1]
    kern = functools.partial(_out_proj_kernel, n_exp=n_exp)
    const = lambda shape: pl.BlockSpec(shape, lambda i: (0, 0))
    return pl.pallas_call(
        kern,
        grid=(m // tm,),
        in_specs=[
            pl.BlockSpec((tm, ks), lambda i: (i, 0)),
            pl.BlockSpec((tm, kd), lambda i: (i, 0)),
            const((ks, d)), const((kd, d)),
            pl.BlockSpec((tm, d), lambda i: (i, 0)),
            const((1, d)), const((d, n_exp)), const((d, n_exp)), const((1, n_exp)),
        ],
        out_specs=[
            pl.BlockSpec((tm, d), lambda i: (i, 0)),
            pl.BlockSpec((tm, d // 2), lambda i: (i, 0)),
            pl.BlockSpec((tm, TOP_K), lambda i: (i, 0)),
            pl.BlockSpec((tm, TOP_K), lambda i: (i, 0)),
            pl.BlockSpec((None, 1, n_exp), lambda i: (i, 0, 0)),
        ],
        out_shape=[
            jax.ShapeDtypeStruct((m, d), F32),
            jax.ShapeDtypeStruct((m, d // 2), jnp.int32),
            jax.ShapeDtypeStruct((m, TOP_K), jnp.int32),
            jax.ShapeDtypeStruct((m, TOP_K), F32),
            jax.ShapeDtypeStruct((m // tm, 1, n_exp), F32),
        ],
        compiler_params=_cparams(("parallel",)),
        name="out_proj_router",
    )(mix_sb, mix_da, w_sb, w_da, x2d, g_ffn, wr_hi, wr_lo, b_r)


def _moe_kernel(be_ref, bc_ref, rb_ref, x_ref, wg_ref, wu_ref, wda_ref, wdb_ref, bg_ref, bup_ref, bda_ref, bdb_ref,
                o_ref, act_ref, *, nf, ts):
    blk = pl.program_id(0)
    s = pl.program_id(1)
    cnt = bc_ref[blk]
    tg = 2 * ts
    tm = x_ref.shape[0]

    def row_groups(run, skip):
        for lo in range(0, tm, tg):
            size = min(tg, tm - lo)

            @pl.when(cnt > lo + size - ts)
            def _():
                run(lo, size)

            if size > ts:
                @pl.when(jnp.logical_and(cnt > lo, cnt <= lo + ts))
                def _():
                    run(lo, ts)
                    skip(lo + ts, ts)

            @pl.when(cnt <= lo)
            def _():
                skip(lo, size)

    @pl.when(jnp.logical_and(cnt > 0, s < nf))
    def _():
        wg = wg_ref[...].astype(BF16)
        wu = wu_ref[...].astype(BF16)
        slot = jnp.minimum(s, nf - 1)

        def up_proj(r0, nrows):
            x = _unpack_bf16_pairs(x_ref[r0:r0 + nrows, :])
            g = jnp.dot(x, wg, preferred_element_type=F32) + bg_ref[...]
            u = jnp.dot(x, wu, preferred_element_type=F32) + bup_ref[...]
            gate = jnp.minimum(g, SWIGLU_LIMIT)
            up = jnp.clip(u, -SWIGLU_LIMIT, SWIGLU_LIMIT)
            act = (up + 1.0) * gate * jax.nn.sigmoid(SWIGLU_ALPHA * gate)
            act_ref[slot, r0:r0 + nrows, :] = act.astype(BF16)

        row_groups(up_proj, lambda r0, nrows: None)

    @pl.when(jnp.logical_and(cnt > 0, s >= nf))
    def _():
        tf = act_ref.shape[2]
        wda = wda_ref[...].astype(BF16)
        wdb = wdb_ref[...].astype(BF16)

        def down_proj(r0, nrows):
            ya = bda_ref[...]
            yb = bdb_ref[...]
            for f in range(nf):
                a = act_ref[f, r0:r0 + nrows, :]
                ya = ya + jnp.dot(a, wda[f * tf:(f + 1) * tf, :], preferred_element_type=F32)
                yb = yb + jnp.dot(a, wdb[f * tf:(f + 1) * tf, :], preferred_element_type=F32)
            o_ref[r0:r0 + nrows, :] = _pack_bf16_pairs(jnp.concatenate([ya, yb], axis=1))

        def zero_fill(r0, nrows):
            o_ref[r0:r0 + nrows, :] = jnp.zeros((nrows, o_ref.shape[1]), o_ref.dtype)

        row_groups(down_proj, zero_fill)


def _moe_ffn(blk_e, blk_cnt, blk_row, xs, w_gu, w_dn, b_gu, b_dn, *, tm, ts, tf, tn):
    n_rows = xs.shape[0]
    n_exp, d, f2 = w_gu.shape
    d_ff = f2 // 2
    nf = d_ff // tf
    nn = d // tn
    n_blk = n_rows // tm
    b_gu3 = b_gu.reshape(n_exp, 1, f2)
    b_dn3 = b_dn.reshape(n_exp, 1, d)

    def fi(s, bc, b):
        return jnp.where(bc[b] > 0, jnp.minimum(s, nf - 1), nf - 1)

    nh = nn // 2

    def ni(s, bc, b):
        return jnp.where(bc[b] > 0, jnp.clip(s - nf, 0, nh - 1), nh - 1)

    grid_spec = pltpu.PrefetchScalarGridSpec(
        num_scalar_prefetch=3,
        grid=(n_blk, nf + nh),
        in_specs=[
            pl.BlockSpec((tm, d // 2), lambda b, s, be, bc, rb: (rb[b], 0)),
            pl.BlockSpec((None, d, tf), lambda b, s, be, bc, rb: (be[b], 0, fi(s, bc, b))),
            pl.BlockSpec((None, d, tf), lambda b, s, be, bc, rb: (be[b], 0, nf + fi(s, bc, b))),
            pl.BlockSpec((None, d_ff, tn), lambda b, s, be, bc, rb: (be[b], 0, ni(s, bc, b))),
            pl.BlockSpec((None, d_ff, tn), lambda b, s, be, bc, rb: (be[b], 0, nh + ni(s, bc, b))),
            pl.BlockSpec((None, 1, tf), lambda b, s, be, bc, rb: (be[b], 0, fi(s, bc, b))),
            pl.BlockSpec((None, 1, tf), lambda b, s, be, bc, rb: (be[b], 0, nf + fi(s, bc, b))),
            pl.BlockSpec((None, 1, tn), lambda b, s, be, bc, rb: (be[b], 0, ni(s, bc, b))),
            pl.BlockSpec((None, 1, tn), lambda b, s, be, bc, rb: (be[b], 0, nh + ni(s, bc, b))),
        ],
        out_specs=pl.BlockSpec((tm, tn), lambda b, s, be, bc, rb: (rb[b], ni(s, bc, b))),
        scratch_shapes=[pltpu.VMEM((nf, tm, tf), BF16)],
    )
    return pl.pallas_call(
        functools.partial(_moe_kernel, nf=nf, ts=ts),
        grid_spec=grid_spec,
        out_shape=jax.ShapeDtypeStruct((n_rows, d // 2), jnp.int32),
        compiler_params=_cparams(("arbitrary", "arbitrary")),
        name="moe_ffn",
    )(blk_e, blk_cnt, blk_row, xs, w_gu, w_gu, w_dn, w_dn, b_gu3, b_gu3, b_dn3, b_dn3)


def _final_kernel(h_ref, y0_ref, y1_ref, y2_ref, y3_ref, p_ref, g_ref, o_ref):
    p = p_ref[...]
    y = None
    for k, y_ref in enumerate((y0_ref, y1_ref, y2_ref, y3_ref)):
        t = _unpack_bf16_pairs(y_ref[...]).astype(F32) * p[:, k:k + 1]
        y = t if y is None else y + t
    h = h_ref[...] + y
    ms = jnp.mean(h * h, axis=-1, keepdims=True)
    o_ref[...] = (h * lax.rsqrt(ms + RMS_EPS)) * g_ref[...]


def _final(h1, y_km, gates, g, *, tm):
    m, d = h1.shape
    nb = m // tm
    assert TOP_K == 4
    y_specs = [pl.BlockSpec((tm, d // 2), functools.partial(lambda i, k: (k * nb + i, 0), k=k))
               for k in range(TOP_K)]
    return pl.pallas_call(
        _final_kernel,
        grid=(nb,),
        in_specs=([pl.BlockSpec((tm, d), lambda i: (i, 0))] + y_specs
                  + [pl.BlockSpec((tm, TOP_K), lambda i: (i, 0)), pl.BlockSpec((1, d), lambda i: (0, 0))]),
        out_specs=pl.BlockSpec((tm, d), lambda i: (i, 0)),
        out_shape=jax.ShapeDtypeStruct((m, d), F32),
        compiler_params=_cparams(("parallel",)),
        name="final_norm",
    )(h1, y_km, y_km, y_km, y_km, gates, g)


def _moe_blocks(n_assign, n_exp, tm):
    n_blk = -(-n_assign // tm) + n_exp
    unit = SC_CORES * SC_SUBCORES * SC_GATHER_ROWS
    while (n_blk * tm) % unit:
        n_blk += 1
    return n_blk


def _routing(top_e, counts, tm):
    n_exp = counts.shape[0]
    n_tok = top_e.shape[0]
    n_assign = n_tok * TOP_K
    flat_e = top_e.reshape(-1)
    padded = (counts + tm - 1) // tm * tm
    pad_end = jnp.cumsum(padded).astype(jnp.int32)
    pad_start = pad_end - padded
    n_blk = _moe_blocks(n_assign, n_exp, tm)
    n_rows = n_blk * tm
    blk_start = jnp.arange(n_blk, dtype=jnp.int32) * tm
    blk_e = jnp.minimum(jnp.sum((blk_start[:, None] >= pad_end[None, :]).astype(jnp.int32), axis=1), n_exp - 1)
    blk_off = blk_start - pad_start[blk_e]
    blk_cnt = jnp.clip(counts[blk_e] - blk_off, 0, tm).astype(jnp.int32)
    n_fill = n_rows - n_assign
    j = jnp.arange(tm, dtype=jnp.int32)[None, :]
    e_ids = jnp.arange(n_exp, dtype=jnp.int32)[:, None]
    fill_keys = jnp.where(j < (padded - counts)[:, None], e_ids, n_exp).reshape(-1)
    fill_keys = jnp.concatenate([fill_keys, jnp.full((n_fill - n_exp * tm,), n_exp, jnp.int32)])
    keys = jnp.concatenate([flat_e, fill_keys])
    toks = jnp.concatenate([jnp.arange(n_assign, dtype=jnp.int32) // TOP_K,
                            jnp.arange(n_fill, dtype=jnp.int32) % n_tok])
    _, row_tok, entry = lax.sort((keys, toks, jnp.arange(n_rows, dtype=jnp.int32)), num_keys=1, is_stable=True)
    dest_flat = jnp.argsort(entry).astype(jnp.int32)[:n_assign]
    last_used = jnp.maximum(pad_end[-1] // tm - 1, 0)
    blk_e = jnp.where(blk_cnt > 0, blk_e, blk_e[last_used]).astype(jnp.int32)
    blk_row = jnp.where(blk_cnt > 0, jnp.arange(n_blk, dtype=jnp.int32), last_used).astype(jnp.int32)
    return row_tok, dest_flat, blk_e, blk_cnt, blk_row


def kernel(x, meta_tokens, g_mix, w_in, lam_q1, lam_k1, lam_q2, lam_k2, g_sb_out, g_da_out, w_out, g_ffn,
           w_router, b_router, w_gate_up, b_gate_up, w_down, b_down, g_final):
    b, s, d = x.shape
    depth = w_in.shape[0]
    assert depth == 1, "single-layer trunk"
    layer = 0
    sb_heads = (d // 2) // SB_HEAD_DIM
    da_heads = (d // 2) // DA_V_DIM
    sb_w = sb_heads * SB_HEAD_DIM
    da_col0 = 3 * sb_w
    da_qk_w = da_heads * 2 * DA_HEAD_DIM
    n_exp = w_router.shape[-1]
    lam_init = 0.8 - 0.6 * math.exp(-0.3 * layer)

    tm_proj, tn_proj = TM_PROJ, TN_PROJ
    tq, tk_sb, ts_da = TQ_ATTN, TK_SB, TS_DA
    tm_out = TM_OUT
    tm_moe, tf_moe = TM_MOE, TF_MOE
    tm_fin = TM_FINAL

    x2d = x.reshape(b * s, d)
    w_in_bf = w_in.reshape(w_in.shape[1:]).astype(BF16)
    g_mix2 = g_mix[layer].reshape(1, d)
    rope_cols = (da_col0, da_col0 + 2 * da_qk_w)

    pos_f = N_META + jnp.arange(s, dtype=jnp.int32)
    proj = _in_proj(x2d, g_mix2, w_in_bf, _rope_tables(pos_f), tm=tm_proj, tn=tn_proj,
                    rope_cols=rope_cols, pos_blocks=s // tm_proj)
    meta_pad = jnp.zeros((META_PAD, d), x.dtype).at[:N_META].set(meta_tokens.astype(x.dtype))
    pos_m = jnp.arange(META_PAD, dtype=jnp.int32)
    proj_meta = _in_proj(meta_pad, g_mix2, w_in_bf, _rope_tables(pos_m), tm=META_PAD, tn=tn_proj,
                         rope_cols=rope_cols, pos_blocks=1)
    proj3 = proj.reshape(b, s, -1)

    mix_sb = _sb_attention(proj3, proj_meta, g_sb_out[layer].reshape(1, -1), batch=b, seq=s, heads=sb_heads,
                           tq=tq, tk=tk_sb, hp=HEADS_PER_STEP_SB)
    lams = tuple(t[layer].reshape(1, DA_HEAD_DIM).astype(F32) for t in (lam_q1, lam_k1, lam_q2, lam_k2))
    mix_da = _da_attention(proj3, proj_meta, lams, g_da_out[layer].reshape(1, -1), batch=b, seq=s, heads=da_heads,
                           tq=TQ_DA, tkv=TKV_DA, ts=ts_da, col0=da_col0, lam_init=lam_init)

    w_out_bf = w_out[layer].astype(BF16)
    wr = w_router[layer]
    wr_hi = wr.astype(BF16)
    wr_lo = (wr - wr_hi.astype(F32)).astype(BF16)
    h1, u_packed, top_e, gates, tile_counts = _out_proj(
        mix_sb.reshape(b * s, -1), mix_da.reshape(b * s, -1), w_out_bf[:sb_w], w_out_bf[sb_w:], x2d,
        g_ffn[layer].reshape(1, d), wr_hi, wr_lo, b_router[layer].reshape(1, n_exp), tm=tm_out)

    counts = jnp.sum(tile_counts, axis=(0, 1)).astype(jnp.int32)
    row_tok, dest_flat, blk_e, blk_cnt, blk_row = _routing(top_e, counts, tm_moe)
    xs = _sc_gather_rows(u_packed, row_tok)
    rows = _moe_ffn(blk_e, blk_cnt, blk_row, xs, w_gate_up.reshape(w_gate_up.shape[1:]),
                    w_down.reshape(w_down.shape[1:]), b_gate_up[layer], b_down[layer],
                    tm=tm_moe, ts=TS_MOE, tf=tf_moe, tn=TN_MOE)
    dest_km = dest_flat.reshape(b * s, TOP_K).T.reshape(-1)
    y_km = _sc_gather_rows(rows, dest_km)

    out = _final(h1, y_km, gates, g_final.reshape(1, d), tm=tm_fin)
    return out.reshape(b, s, d)
```

```python
import functools
import math

import jax
import jax.numpy as jnp
from jax import lax
from jax.experimental import pallas as pl
from jax.experimental.pallas import tpu as pltpu
from jax.experimental.pallas import tpu_sc as plsc

F32 = jnp.float32
BF16 = jnp.bfloat16

CHUNK = 64
N_META = 16
RMS_EPS = 1e-5
SB_HEAD_DIM = 128
DA_HEAD_DIM = 64
DA_V_DIM = 2 * DA_HEAD_DIM
ROPE_THETA = 500000.0
ROPE_DIM = DA_HEAD_DIM // 4
TOP_K = 4
SWIGLU_LIMIT = 7.0
SWIGLU_ALPHA = 1.702

LANES = 128
META_PAD = 128
NEG_BIG = -1e30
SB_DEAD_RUN = 110.0
VMEM_LIMIT = 56 * 1024 * 1024
SC_CORES, SC_SUBCORES = 2, 16
SC_GATHER_ROWS = 32

TM_PROJ, TN_PROJ = 1024, 1024
TQ_ATTN = 512
HEADS_PER_STEP_SB = 2
TQ_DA, TKV_DA = 1024, 512
TK_SB = 256
TS_DA = 256
TM_OUT = 512
TM_MOE, TS_MOE = 2048, 512
TF_MOE, TN_MOE = 256, 256
TM_FINAL = 512


def _cparams(sem):
    return pltpu.CompilerParams(dimension_semantics=sem, vmem_limit_bytes=VMEM_LIMIT)


def _in_proj_kernel(x_ref, g_ref, w_ref, c_ref, sa_ref, sb_ref, o_ref, u_scr, *, tn, rope_lo, rope_hi):
    n = pl.program_id(1)

    @pl.when(n == 0)
    def _():
        x = x_ref[...]
        ms = jnp.mean(x * x, axis=-1, keepdims=True)
        u_scr[...] = ((x * lax.rsqrt(ms + RMS_EPS)) * g_ref[...]).astype(BF16)

    acc = jnp.dot(u_scr[...], w_ref[...], preferred_element_type=F32)
    is_rope = jnp.logical_and(n >= rope_lo, n < rope_hi)
    o_ref[...] = acc.astype(BF16)

    @pl.when(is_rope)
    def _():
        for c in range(tn // LANES):
            xc = acc[:, c * LANES:(c + 1) * LANES]
            r = (xc * c_ref[...] + pltpu.roll(xc, LANES - ROPE_DIM // 2, 1) * sa_ref[...]
                 + pltpu.roll(xc, ROPE_DIM // 2, 1) * sb_ref[...])
            o_ref[:, c * LANES:(c + 1) * LANES] = r.astype(BF16)


def _rope_tables(pos):
    half = ROPE_DIM // 2
    inv_freq = ROPE_THETA ** (-(jnp.arange(half, dtype=F32) * 2.0 / ROPE_DIM))
    ang = pos.astype(F32)[:, None] * inv_freq[None, :]
    cos, sin = jnp.cos(ang), jnp.sin(ang)
    p = pos.shape[0]
    ones = jnp.ones((p, DA_HEAD_DIM - ROPE_DIM), F32)
    zeros8 = jnp.zeros((p, half), F32)
    zeros48 = jnp.zeros((p, DA_HEAD_DIM - ROPE_DIM), F32)
    c64 = jnp.concatenate([cos, cos, ones], axis=1)
    sa64 = jnp.concatenate([-sin, zeros8, zeros48], axis=1)
    sb64 = jnp.concatenate([zeros8, sin, zeros48], axis=1)
    tile2 = lambda t: jnp.concatenate([t, t], axis=1)
    return tile2(c64), tile2(sa64), tile2(sb64)


def _in_proj(x2d, g, w_bf, tables, *, tm, tn, rope_cols, pos_blocks):
    m, d = x2d.shape
    n_cols = w_bf.shape[1]
    c_t, sa_t, sb_t = tables
    kern = functools.partial(_in_proj_kernel, tn=tn, rope_lo=rope_cols[0] // tn, rope_hi=rope_cols[1] // tn)
    tab_spec = pl.BlockSpec((tm, LANES), lambda i, n: (i % pos_blocks, 0))
    return pl.pallas_call(
        kern,
        grid=(m // tm, n_cols // tn),
        in_specs=[
            pl.BlockSpec((tm, d), lambda i, n: (i, 0)),
            pl.BlockSpec((1, d), lambda i, n: (0, 0)),
            pl.BlockSpec((d, tn), lambda i, n: (0, n)),
            tab_spec, tab_spec, tab_spec,
        ],
        out_specs=pl.BlockSpec((tm, tn), lambda i, n: (i, n)),
        out_shape=jax.ShapeDtypeStruct((m, n_cols), BF16),
        scratch_shapes=[pltpu.VMEM((tm, d), BF16)],
        compiler_params=_cparams(("parallel", "arbitrary")),
        name="in_proj",
    )(x2d, g, w_bf, c_t, sa_t, sb_t)


def _dot_nt(a, b):
    return lax.dot_general(a, b, (((1,), (1,)), ((), ())), preferred_element_type=F32)


def _suffix_sum_matrix(n):
    j = lax.broadcasted_iota(jnp.int32, (2 * n, n), 0)
    s = lax.broadcasted_iota(jnp.int32, (2 * n, n), 1)
    return jnp.where(jnp.where(j >= n, j - n, j) > s, 1.0, 0.0).astype(BF16)


def _sb_block(q, kb, vb, u2, run, acc, mask, scale):
    z = _dot_nt(q, kb) * scale
    sp = jnp.maximum(z, 0.0) + jnp.log(1.0 + jnp.exp(-jnp.abs(z)))
    if mask is not None:
        sp = jnp.where(mask, sp, 0.0)
    hi = sp.astype(BF16)
    lo = (sp - hi.astype(F32)).astype(BF16)
    cs = jnp.dot(jnp.concatenate([hi, lo], axis=1), u2, preferred_element_type=F32)
    w = jnp.exp(z - sp - cs - run)
    if mask is not None:
        w = jnp.where(mask, w, 0.0)
    acc = acc + jnp.dot(w.astype(BF16), vb, preferred_element_type=F32)
    run = run + (cs[:, :1] + sp[:, :1])
    return run, acc


def _sb_kernel(q_ref, k_ref, v_ref, km_ref, vm_ref, g_ref, o_ref, *, tq, tk, hp, scale):
    i = pl.program_id(2)
    nsub = tq // tk
    hd = SB_HEAD_DIM
    u_blk = _suffix_sum_matrix(tk)
    row = lax.broadcasted_iota(jnp.int32, (tk, tk), 0)
    col = lax.broadcasted_iota(jnp.int32, (tk, tk), 1)
    diag_mask = col < row
    heads = range(hp)

    def kv(blk, h):
        s0 = pl.multiple_of(blk * tk, tk)
        return k_ref[pl.ds(s0, tk), h * hd:(h + 1) * hd], v_ref[pl.ds(s0, tk), h * hd:(h + 1) * hd]

    runs, accs = [], []
    for h in heads:
        slab_runs, slab_accs = [], []
        for sl in range(nsub):
            qs = q_ref[sl * tk:(sl + 1) * tk, h * hd:(h + 1) * hd]
            run = jnp.zeros((tk, 1), F32)
            acc = jnp.zeros((tk, hd), F32)
            for c in range(sl, -1, -1):
                kb, vb = kv(i * nsub + c, h)
                run, acc = _sb_block(qs, kb, vb, u_blk, run, acc, diag_mask if c == sl else None, scale)
            slab_runs.append(run)
            slab_accs.append(acc)
        runs.append(jnp.concatenate(slab_runs, axis=0))
        accs.append(jnp.concatenate(slab_accs, axis=0))
    q = [q_ref[:, h * hd:(h + 1) * hd] for h in heads]

    def alive_flag(rs):
        lowest = functools.reduce(jnp.minimum, [jnp.min(r) for r in rs])
        return (lowest < SB_DEAD_RUN).astype(jnp.int32)

    def cond(carry):
        jb, alive, _, _ = carry
        return jnp.logical_and(jb >= 0, alive > 0)

    def body(carry):
        jb, _, rs, as_ = carry
        out = [_sb_block(q[h], *kv(jb, h), u_blk, rs[h], as_[h], None, scale) for h in heads]
        rs = tuple(o[0] for o in out)
        return jb - 1, alive_flag(rs), rs, tuple(o[1] for o in out)

    runs, accs = tuple(runs), tuple(accs)
    _, alive, runs, accs = lax.while_loop(cond, body, (i * nsub - 1, alive_flag(runs), runs, accs))

    def meta_block():
        mcol = lax.broadcasted_iota(jnp.int32, (tq, META_PAD), 1)
        u_meta = _suffix_sum_matrix(META_PAD)
        return tuple(_sb_block(q[h], km_ref[:, h * hd:(h + 1) * hd], vm_ref[:, h * hd:(h + 1) * hd], u_meta,
                               runs[h], accs[h], mcol < N_META, scale)[1] for h in heads)

    accs = lax.cond(alive > 0, meta_block, lambda: accs)

    for h in heads:
        acc = accs[h]
        ms = jnp.mean(acc * acc, axis=-1, keepdims=True)
        gain = g_ref[:, h * hd:(h + 1) * hd]
        o_ref[:, h * hd:(h + 1) * hd] = ((acc * lax.rsqrt(ms + RMS_EPS)) * gain).astype(BF16)


def _sb_attention(proj, proj_meta, g_sb, *, batch, seq, heads, tq, tk, hp):
    hd = SB_HEAD_DIM
    wd = hp * hd
    ng = heads // hp
    kern = functools.partial(_sb_kernel, tq=tq, tk=tk, hp=hp, scale=hd ** -0.5)
    return pl.pallas_call(
        kern,
        grid=(batch, ng, seq // tq),
        in_specs=[
            pl.BlockSpec((None, tq, wd), lambda b, h, i: (b, i, h)),
            pl.BlockSpec((None, seq, wd), lambda b, h, i: (b, 0, ng + h)),
            pl.BlockSpec((None, seq, wd), lambda b, h, i: (b, 0, 2 * ng + h)),
            pl.BlockSpec((META_PAD, wd), lambda b, h, i: (0, ng + h)),
            pl.BlockSpec((META_PAD, wd), lambda b, h, i: (0, 2 * ng + h)),
            pl.BlockSpec((1, wd), lambda b, h, i: (0, h)),
        ],
        out_specs=pl.BlockSpec((None, tq, wd), lambda b, h, i: (b, i, h)),
        out_shape=jax.ShapeDtypeStruct((batch, seq, heads * hd), BF16),
        compiler_params=_cparams(("parallel", "parallel", "arbitrary")),
        name="sb_attn",
    )(proj, proj, proj, proj_meta, proj_meta, g_sb)


def _with_ones(vb):
    return jnp.concatenate([vb, jnp.ones_like(vb)], axis=1)


def _da_block(q1, q2, kb, vb1, st, mask):
    m1, a1, m2, a2 = st

    def one(qc, m, a):
        s = _dot_nt(qc, kb)
        if mask is not None:
            s = jnp.where(mask, s, NEG_BIG)
        mn = jnp.maximum(m, jnp.max(s, axis=-1, keepdims=True))
        p = jnp.exp(s - mn)
        a = jnp.exp(m - mn) * a + jnp.dot(p.astype(BF16), vb1, preferred_element_type=F32)
        return mn, a

    m1, a1 = one(q1, m1, a1)
    m2, a2 = one(q2, m2, a2)
    return m1, a1, m2, a2


def _da_kernel(lq1_ref, lk1_ref, lq2_ref, lk2_ref, q_ref, k_ref, v_ref, km_ref, vm_ref, g_ref, o_ref,
               *, tq, tkv, ts, scale, lam_init):
    i = pl.program_id(2)
    nsub = tq // ts
    lam = (jnp.exp(jnp.sum(lq1_ref[...] * lk1_ref[...], axis=-1, keepdims=True))
           - jnp.exp(jnp.sum(lq2_ref[...] * lk2_ref[...], axis=-1, keepdims=True)) + lam_init)

    q = q_ref[...]
    lane = lax.broadcasted_iota(jnp.int32, q.shape, 1)
    qs = q * jnp.asarray(scale, BF16)
    zero = jnp.zeros_like(qs)
    q1 = jnp.where(lane < DA_HEAD_DIM, qs, zero)
    q2 = jnp.where(lane >= DA_HEAD_DIM, qs, zero)

    neg = jnp.full((tq, 1), NEG_BIG, F32)
    za = jnp.zeros((tq, 2 * DA_V_DIM), F32)
    st = (neg, za, neg, za)

    def body(j, carry):
        for c in range(tq // tkv):
            s0 = pl.multiple_of(j * tq + c * tkv, tkv)
            carry = _da_block(q1, q2, k_ref[pl.ds(s0, tkv), :], _with_ones(v_ref[pl.ds(s0, tkv), :]), carry, None)
        return carry

    st = lax.fori_loop(0, i, body, st)

    shift = CHUNK.bit_length() - 1
    s0 = pl.multiple_of(i * tq, tq)
    outs = []
    for h in range(nsub):
        rows = slice(h * ts, (h + 1) * ts)
        width = (h + 1) * ts
        kb = jnp.concatenate([km_ref[...], k_ref[pl.ds(s0, width), :]], axis=0)
        vb = jnp.concatenate([vm_ref[...], v_ref[pl.ds(s0, width), :]], axis=0)
        col = lax.broadcasted_iota(jnp.int32, (ts, META_PAD + width), 1)
        row = lax.broadcasted_iota(jnp.int32, (ts, META_PAD + width), 0) + h * ts
        frame_vis = jnp.logical_and(col >= META_PAD,
                                    jnp.right_shift(col - META_PAD, shift) <= jnp.right_shift(row, shift))
        vis = jnp.logical_or(col < N_META, frame_vis)
        sth = _da_block(q1[rows], q2[rows], kb, _with_ones(vb), tuple(t[rows] for t in st), vis)
        _, a1, _, a2 = sth
        outs.append(a1[:, :DA_V_DIM] / a1[:, DA_V_DIM:] - lam * (a2[:, :DA_V_DIM] / a2[:, DA_V_DIM:]))
    o = jnp.concatenate(outs, axis=0)
    ms = jnp.mean(o * o, axis=-1, keepdims=True)
    o_ref[...] = (((o * lax.rsqrt(ms + RMS_EPS)) * g_ref[...]) * (1.0 - lam_init)).astype(BF16)


def _da_attention(proj, proj_meta, lams, g_da, *, batch, seq, heads, tq, tkv, ts, col0, lam_init):
    hd = DA_V_DIM
    qb, kb, vb = col0 // hd, col0 // hd + heads, col0 // hd + 2 * heads
    kern = functools.partial(_da_kernel, tq=tq, tkv=tkv, ts=ts, scale=DA_HEAD_DIM ** -0.5, lam_init=lam_init)
    lam_spec = pl.BlockSpec((1, DA_HEAD_DIM), lambda b, h, i: (0, 0))
    return pl.pallas_call(
        kern,
        grid=(batch, heads, seq // tq),
        in_specs=[
            lam_spec, lam_spec, lam_spec, lam_spec,
            pl.BlockSpec((None, tq, hd), lambda b, h, i: (b, i, qb + h)),
            pl.BlockSpec((None, seq, hd), lambda b, h, i: (b, 0, kb + h)),
            pl.BlockSpec((None, seq, hd), lambda b, h, i: (b, 0, vb + h)),
            pl.BlockSpec((META_PAD, hd), lambda b, h, i: (0, kb + h)),
            pl.BlockSpec((META_PAD, hd), lambda b, h, i: (0, vb + h)),
            pl.BlockSpec((1, hd), lambda b, h, i: (0, h)),
        ],
        out_specs=pl.BlockSpec((None, tq, hd), lambda b, h, i: (b, i, h)),
        out_shape=jax.ShapeDtypeStruct((batch, seq, heads * hd), BF16),
        compiler_params=_cparams(("parallel", "parallel", "arbitrary")),
        name="da_attn",
    )(*lams, proj, proj, proj, proj_meta, proj_meta, g_da)


def _pack_bf16_pairs(x):
    n = x.shape[1] // 2
    lo = lax.bitcast_convert_type(x[:, :n].astype(BF16).astype(F32), jnp.int32)
    hi = lax.bitcast_convert_type(x[:, n:].astype(BF16).astype(F32), jnp.int32)
    return jnp.bitwise_or(lax.shift_right_logical(lo, 16), jnp.bitwise_and(hi, jnp.int32(-65536)))


def _unpack_bf16_pairs(w):
    lo = lax.bitcast_convert_type(lax.shift_left(w, 16), F32)
    hi = lax.bitcast_convert_type(jnp.bitwise_and(w, jnp.int32(-65536)), F32)
    return jnp.concatenate([lo.astype(BF16), hi.astype(BF16)], axis=1)


def _sc_gather_rows(table, idx):
    n_idx = idx.shape[0]
    width = table.shape[1]
    n_workers = SC_CORES * SC_SUBCORES
    per_worker = n_idx // n_workers
    win = SC_GATHER_ROWS
    assert n_idx % (n_workers * win) == 0
    mesh = plsc.VectorSubcoreMesh(core_axis_name="c", subcore_axis_name="s")

    n_win = per_worker // win
    assert n_win % 2 == 0
    buf = lambda: pltpu.VMEM((win, width), table.dtype)
    dma = pltpu.SemaphoreType.DMA

    @functools.partial(
        pl.kernel, mesh=mesh,
        out_type=jax.ShapeDtypeStruct((n_idx, width), table.dtype),
        scratch_types=[pltpu.VMEM((n_win, win), jnp.int32), buf(), buf(), dma, dma, dma, dma],
    )
    def gather(table_hbm, idx_hbm, out_hbm, idx_v, rows0, rows1, gsem0, gsem1, wsem0, wsem1):
        worker = lax.axis_index("s") * SC_CORES + lax.axis_index("c")
        base = worker * per_worker
        pltpu.sync_copy(idx_hbm.at[pl.ds(worker * n_win, n_win)], idx_v)

        def gather_copy(t, rows_v, sem):
            return pltpu.make_async_copy(table_hbm.at[idx_v.at[t]], rows_v, sem)

        def writeback(t, rows_v, sem):
            return pltpu.make_async_copy(rows_v, out_hbm.at[pl.ds(pl.multiple_of(base + t * win, win), win)], sem)

        gather_copy(0, rows0, gsem0).start()

        @pl.loop(0, n_win, step=2)
        def _(t):
            gather_copy(t, rows0, gsem0).wait()
            writeback(t, rows0, wsem0).start()

            @pl.when(t > 0)
            def _():
                writeback(t - 1, rows1, wsem1).wait()

            gather_copy(t + 1, rows1, gsem1).start()
            gather_copy(t + 1, rows1, gsem1).wait()
            writeback(t + 1, rows1, wsem1).start()
            writeback(t, rows0, wsem0).wait()

            @pl.when(t + 2 < n_win)
            def _():
                gather_copy(t + 2, rows0, gsem0).start()

        writeback(n_win - 1, rows1, wsem1).wait()

    return gather(table, idx.reshape(n_idx // win, win))


def _sc_scatter_rows(src, dest_km, n_rows):
    n_tok, width = src.shape
    n_slots = dest_km.shape[0]
    n_workers = SC_CORES * SC_SUBCORES
    win = SC_GATHER_ROWS
    per_worker = n_tok // n_workers
    n_win = per_worker // win
    assert n_tok % (n_workers * win) == 0 and n_win % 2 == 0
    mesh = plsc.VectorSubcoreMesh(core_axis_name="c", subcore_axis_name="s")
    buf = lambda: pltpu.VMEM((win, width), src.dtype)
    dma = pltpu.SemaphoreType.DMA

    @functools.partial(
        pl.kernel, mesh=mesh,
        out_type=jax.ShapeDtypeStruct((n_rows, width), src.dtype),
        scratch_types=[pltpu.VMEM((n_slots, n_win, win), jnp.int32), buf(), buf(), dma, dma, dma, dma],
    )
    def scatter(src_hbm, idx_hbm, out_hbm, idx_v, rows0, rows1, lsem0, lsem1, ssem0, ssem1):
        worker = lax.axis_index("s") * SC_CORES + lax.axis_index("c")
        base = worker * per_worker
        for k in range(n_slots):
            pltpu.sync_copy(idx_hbm.at[k, pl.ds(worker * n_win, n_win)], idx_v.at[k])

        def load(t, rows_v, sem):
            return pltpu.make_async_copy(src_hbm.at[pl.ds(pl.multiple_of(base + t * win, win), win)], rows_v, sem)

        def start_scatters(t, rows_v, sem):
            for k in range(n_slots):
                pltpu.make_async_copy(rows_v, out_hbm.at[idx_v.at[k, t]], sem).start()

        def wait_scatters(t, rows_v, sem):
            for k in range(n_slots):
                pltpu.make_async_copy(rows_v, out_hbm.at[idx_v.at[k, t]], sem).wait()

        load(0, rows0, lsem0).start()

        @pl.loop(0, n_win, step=2)
        def _(t):
            load(t, rows0, lsem0).wait()
            start_scatters(t, rows0, ssem0)

            @pl.when(t > 0)
            def _():
                wait_scatters(t - 1, rows1, ssem1)

            load(t + 1, rows1, lsem1).start()
            load(t + 1, rows1, lsem1).wait()
            start_scatters(t + 1, rows1, ssem1)
            wait_scatters(t, rows0, ssem0)

            @pl.when(t + 2 < n_win)
            def _():
                load(t + 2, rows0, lsem0).start()

        wait_scatters(n_win - 1, rows1, ssem1)

    return scatter(src, dest_km.reshape(n_slots, n_tok // win, win))


def _split3_dot(a, b_hi, b_lo):
    a_hi = a.astype(BF16)
    a_lo = (a - a_hi.astype(F32)).astype(BF16)
    return (jnp.dot(a_hi, b_hi, preferred_element_type=F32)
            + jnp.dot(a_hi, b_lo, preferred_element_type=F32)
            + jnp.dot(a_lo, b_hi, preferred_element_type=F32))


def _out_proj_kernel(ms_ref, md_ref, ws_ref, wd_ref, x_ref, g_ref, wr_hi_ref, wr_lo_ref, br_ref,
                     h_ref, u_ref, e_ref, p_ref, c_ref, *, n_exp):
    h = (x_ref[...]
         + jnp.dot(ms_ref[...], ws_ref[...], preferred_element_type=F32)
         + jnp.dot(md_ref[...], wd_ref[...], preferred_element_type=F32))
    h_ref[...] = h
    msq = jnp.mean(h * h, axis=-1, keepdims=True)
    u = (h * lax.rsqrt(msq + RMS_EPS)) * g_ref[...]
    u_ref[...] = _pack_bf16_pairs(u)

    logits = _split3_dot(u, wr_hi_ref[...], wr_lo_ref[...]) + br_ref[...]
    lane = lax.broadcasted_iota(jnp.int32, logits.shape, 1).astype(F32)
    work = logits
    tops, idxs = [], []
    for _ in range(TOP_K):
        mx = jnp.max(work, axis=-1, keepdims=True)
        ix = jnp.min(jnp.where(work == mx, lane, float(n_exp)), axis=-1, keepdims=True)
        tops.append(mx)
        idxs.append(ix)
        work = jnp.where(lane == ix, -jnp.inf, work)
    ex = [jnp.exp(t - tops[0]) for t in tops]
    den = ex[0] + ex[1] + ex[2] + ex[3]
    kl = lax.broadcasted_iota(jnp.int32, (logits.shape[0], TOP_K), 1)
    e_out = jnp.zeros((logits.shape[0], TOP_K), F32)
    p_out = jnp.zeros((logits.shape[0], TOP_K), F32)
    for k in range(TOP_K):
        e_out = jnp.where(kl == k, idxs[k], e_out)
        p_out = jnp.where(kl == k, ex[k] / den, p_out)
    e_ref[...] = e_out.astype(jnp.int32)
    p_ref[...] = p_out
    hits = jnp.zeros(logits.shape, F32)
    for k in range(TOP_K):
        hits = hits + jnp.where(lane == idxs[k], 1.0, 0.0)
    c_ref[...] = jnp.sum(hits, axis=0, keepdims=True)


def _out_proj(mix_sb, mix_da, w_sb, w_da, x2d, g_ffn, wr_hi, wr_lo, b_r, *, tm):
    m, d = x2d.shape
    ks, kd = mix_sb.shape[1], mix_da.shape[1]
    n_exp = wr_hi.shape[1]
    kern = functools.partial(_out_proj_kernel, n_exp=n_exp)
    const = lambda shape: pl.BlockSpec(shape, lambda i: (0, 0))
    return pl.pallas_call(
        kern,
        grid=(m // tm,),
        in_specs=[
            pl.BlockSpec((tm, ks), lambda i: (i, 0)),
            pl.BlockSpec((tm, kd), lambda i: (i, 0)),
            const((ks, d)), const((kd, d)),
            pl.BlockSpec((tm, d), lambda i: (i, 0)),
            const((1, d)), const((d, n_exp)), const((d, n_exp)), const((1, n_exp)),
        ],
        out_specs=[
            pl.BlockSpec((tm, d), lambda i: (i, 0)),
            pl.BlockSpec((tm, d // 2), lambda i: (i, 0)),
            pl.BlockSpec((tm, TOP_K), lambda i: (i, 0)),
            pl.BlockSpec((tm, TOP_K), lambda i: (i, 0)),
            pl.BlockSpec((None, 1, n_exp), lambda i: (i, 0, 0)),
        ],
        out_shape=[
            jax.ShapeDtypeStruct((m, d), F32),
            jax.ShapeDtypeStruct((m, d // 2), jnp.int32),
            jax.ShapeDtypeStruct((m, TOP_K), jnp.int32),
            jax.ShapeDtypeStruct((m, TOP_K), F32),
            jax.ShapeDtypeStruct((m // tm, 1, n_exp), F32),
        ],
        compiler_params=_cparams(("parallel",)),
        name="out_proj_router",
    )(mix_sb, mix_da, w_sb, w_da, x2d, g_ffn, wr_hi, wr_lo, b_r)


def _moe_kernel(be_ref, bc_ref, rb_ref, x_ref, wg_ref, wu_ref, wda_ref, wdb_ref, bg_ref, bup_ref, bda_ref, bdb_ref,
                o_ref, act_ref, *, nf, ts):
    blk = pl.program_id(0)
    s = pl.program_id(1)
    cnt = bc_ref[blk]
    tg = 2 * ts
    tm = x_ref.shape[0]

    def row_groups(run, skip):
        for lo in range(0, tm, tg):
            size = min(tg, tm - lo)

            @pl.when(cnt > lo + size - ts)
            def _():
                run(lo, size)

            if size > ts:
                @pl.when(jnp.logical_and(cnt > lo, cnt <= lo + ts))
                def _():
                    run(lo, ts)
                    skip(lo + ts, ts)

            @pl.when(cnt <= lo)
            def _():
                skip(lo, size)

    @pl.when(jnp.logical_and(cnt > 0, s < nf))
    def _():
        wg = wg_ref[...].astype(BF16)
        wu = wu_ref[...].astype(BF16)
        slot = jnp.minimum(s, nf - 1)

        def up_proj(r0, nrows):
            x = _unpack_bf16_pairs(x_ref[r0:r0 + nrows, :])
            g = jnp.dot(x, wg, preferred_element_type=F32) + bg_ref[...]
            u = jnp.dot(x, wu, preferred_element_type=F32) + bup_ref[...]
            gate = jnp.minimum(g, SWIGLU_LIMIT)
            up = jnp.clip(u, -SWIGLU_LIMIT, SWIGLU_LIMIT)
            act = (up + 1.0) * gate * jax.nn.sigmoid(SWIGLU_ALPHA * gate)
            act_ref[slot, r0:r0 + nrows, :] = act.astype(BF16)

        row_groups(up_proj, lambda r0, nrows: None)

    @pl.when(jnp.logical_and(cnt > 0, s >= nf))
    def _():
        tf = act_ref.shape[2]
        wda = wda_ref[...].astype(BF16)
        wdb = wdb_ref[...].astype(BF16)

        def down_proj(r0, nrows):
            ya = bda_ref[...]
            yb = bdb_ref[...]
            for f in range(nf):
                a = act_ref[f, r0:r0 + nrows, :]
                ya = ya + jnp.dot(a, wda[f * tf:(f + 1) * tf, :], preferred_element_type=F32)
                yb = yb + jnp.dot(a, wdb[f * tf:(f + 1) * tf, :], preferred_element_type=F32)
            o_ref[r0:r0 + nrows, :] = _pack_bf16_pairs(jnp.concatenate([ya, yb], axis=1))

        def zero_fill(r0, nrows):
            o_ref[r0:r0 + nrows, :] = jnp.zeros((nrows, o_ref.shape[1]), o_ref.dtype)

        row_groups(down_proj, zero_fill)


def _moe_ffn(blk_e, blk_cnt, blk_row, xs, w_gu, w_dn, b_gu, b_dn, *, tm, ts, tf, tn):
    n_rows = xs.shape[0]
    n_exp, d, f2 = w_gu.shape
    d_ff = f2 // 2
    nf = d_ff // tf
    nn = d // tn
    n_blk = n_rows // tm
    b_gu3 = b_gu.reshape(n_exp, 1, f2)
    b_dn3 = b_dn.reshape(n_exp, 1, d)

    def fi(s, bc, b):
        return jnp.where(bc[b] > 0, jnp.minimum(s, nf - 1), nf - 1)

    nh = nn // 2

    def ni(s, bc, b):
        return jnp.where(bc[b] > 0, jnp.clip(s - nf, 0, nh - 1), nh - 1)

    grid_spec = pltpu.PrefetchScalarGridSpec(
        num_scalar_prefetch=3,
        grid=(n_blk, nf + nh),
        in_specs=[
            pl.BlockSpec((tm, d // 2), lambda b, s, be, bc, rb: (rb[b], 0)),
            pl.BlockSpec((None, d, tf), lambda b, s, be, bc, rb: (be[b], 0, fi(s, bc, b))),
            pl.BlockSpec((None, d, tf), lambda b, s, be, bc, rb: (be[b], 0, nf + fi(s, bc, b))),
            pl.BlockSpec((None, d_ff, tn), lambda b, s, be, bc, rb: (be[b], 0, ni(s, bc, b))),
            pl.BlockSpec((None, d_ff, tn), lambda b, s, be, bc, rb: (be[b], 0, nh + ni(s, bc, b))),
            pl.BlockSpec((None, 1, tf), lambda b, s, be, bc, rb: (be[b], 0, fi(s, bc, b))),
            pl.BlockSpec((None, 1, tf), lambda b, s, be, bc, rb: (be[b], 0, nf + fi(s, bc, b))),
            pl.BlockSpec((None, 1, tn), lambda b, s, be, bc, rb: (be[b], 0, ni(s, bc, b))),
            pl.BlockSpec((None, 1, tn), lambda b, s, be, bc, rb: (be[b], 0, nh + ni(s, bc, b))),
        ],
        out_specs=pl.BlockSpec((tm, tn), lambda b, s, be, bc, rb: (rb[b], ni(s, bc, b))),
        scratch_shapes=[pltpu.VMEM((nf, tm, tf), BF16)],
    )
    return pl.pallas_call(
        functools.partial(_moe_kernel, nf=nf, ts=ts),
        grid_spec=grid_spec,
        out_shape=jax.ShapeDtypeStruct((n_rows, d // 2), jnp.int32),
        compiler_params=_cparams(("arbitrary", "arbitrary")),
        name="moe_ffn",
    )(blk_e, blk_cnt, blk_row, xs, w_gu, w_gu, w_dn, w_dn, b_gu3, b_gu3, b_dn3, b_dn3)


def _final_kernel(h_ref, y0_ref, y1_ref, y2_ref, y3_ref, p_ref, g_ref, o_ref):
    p = p_ref[...]
    y = None
    for k, y_ref in enumerate((y0_ref, y1_ref, y2_ref, y3_ref)):
        t = _unpack_bf16_pairs(y_ref[...]).astype(F32) * p[:, k:k + 1]
        y = t if y is None else y + t
    h = h_ref[...] + y
    ms = jnp.mean(h * h, axis=-1, keepdims=True)
    o_ref[...] = (h * lax.rsqrt(ms + RMS_EPS)) * g_ref[...]


def _final(h1, y_km, gates, g, *, tm):
    m, d = h1.shape
    nb = m // tm
    assert TOP_K == 4
    y_specs = [pl.BlockSpec((tm, d // 2), functools.partial(lambda i, k: (k * nb + i, 0), k=k))
               for k in range(TOP_K)]
    return pl.pallas_call(
        _final_kernel,
        grid=(nb,),
        in_specs=([pl.BlockSpec((tm, d), lambda i: (i, 0))] + y_specs
                  + [pl.BlockSpec((tm, TOP_K), lambda i: (i, 0)), pl.BlockSpec((1, d), lambda i: (0, 0))]),
        out_specs=pl.BlockSpec((tm, d), lambda i: (i, 0)),
        out_shape=jax.ShapeDtypeStruct((m, d), F32),
        compiler_params=_cparams(("parallel",)),
        name="final_norm",
    )(h1, y_km, y_km, y_km, y_km, gates, g)


def _moe_blocks(n_assign, n_exp, tm):
    n_blk = -(-n_assign // tm) + n_exp
    unit = SC_CORES * SC_SUBCORES * SC_GATHER_ROWS
    while (n_blk * tm) % unit:
        n_blk += 1
    return n_blk


def _routing(top_e, counts, tm):
    n_exp = counts.shape[0]
    n_tok = top_e.shape[0]
    n_assign = n_tok * TOP_K
    flat_e = top_e.reshape(-1)
    padded = (counts + tm - 1) // tm * tm
    pad_end = jnp.cumsum(padded).astype(jnp.int32)
    pad_start = pad_end - padded
    grp_start = jnp.cumsum(counts).astype(jnp.int32) - counts
    n_blk = _moe_blocks(n_assign, n_exp, tm)
    blk_start = jnp.arange(n_blk, dtype=jnp.int32) * tm
    blk_e = jnp.minimum(jnp.sum((blk_start[:, None] >= pad_end[None, :]).astype(jnp.int32), axis=1), n_exp - 1)
    blk_off = blk_start - pad_start[blk_e]
    blk_cnt = jnp.clip(counts[blk_e] - blk_off, 0, tm).astype(jnp.int32)
    order = jnp.argsort(flat_e, stable=True).astype(jnp.int32)
    rank = jnp.argsort(order).astype(jnp.int32)
    delta = pad_start - grp_start
    e2d = flat_e.reshape(-1, LANES)
    hit = e2d[None] == jnp.arange(n_exp, dtype=jnp.int32)[:, None, None]
    dest_flat = rank + jnp.sum(jnp.where(hit, delta[:, None, None], 0), axis=0).reshape(-1)
    last_used = jnp.maximum(pad_end[-1] // tm - 1, 0)
    blk_e = jnp.where(blk_cnt > 0, blk_e, blk_e[last_used]).astype(jnp.int32)
    blk_row = jnp.where(blk_cnt > 0, jnp.arange(n_blk, dtype=jnp.int32), last_used).astype(jnp.int32)
    return dest_flat, blk_e, blk_cnt, blk_row, n_blk * tm


def kernel(x, meta_tokens, g_mix, w_in, lam_q1, lam_k1, lam_q2, lam_k2, g_sb_out, g_da_out, w_out, g_ffn,
           w_router, b_router, w_gate_up, b_gate_up, w_down, b_down, g_final):
    b, s, d = x.shape
    depth = w_in.shape[0]
    assert depth == 1, "single-layer trunk"
    layer = 0
    sb_heads = (d // 2) // SB_HEAD_DIM
    da_heads = (d // 2) // DA_V_DIM
    sb_w = sb_heads * SB_HEAD_DIM
    da_col0 = 3 * sb_w
    da_qk_w = da_heads * 2 * DA_HEAD_DIM
    n_exp = w_router.shape[-1]
    lam_init = 0.8 - 0.6 * math.exp(-0.3 * layer)

    tm_proj, tn_proj = TM_PROJ, TN_PROJ
    tq, tk_sb, ts_da = TQ_ATTN, TK_SB, TS_DA
    tm_out = TM_OUT
    tm_moe, tf_moe = TM_MOE, TF_MOE
    tm_fin = TM_FINAL

    x2d = x.reshape(b * s, d)
    w_in_bf = w_in.reshape(w_in.shape[1:]).astype(BF16)
    g_mix2 = g_mix[layer].reshape(1, d)
    rope_cols = (da_col0, da_col0 + 2 * da_qk_w)

    pos_f = N_META + jnp.arange(s, dtype=jnp.int32)
    proj = _in_proj(x2d, g_mix2, w_in_bf, _rope_tables(pos_f), tm=tm_proj, tn=tn_proj,
                    rope_cols=rope_cols, pos_blocks=s // tm_proj)
    meta_pad = jnp.zeros((META_PAD, d), x.dtype).at[:N_META].set(meta_tokens.astype(x.dtype))
    pos_m = jnp.arange(META_PAD, dtype=jnp.int32)
    proj_meta = _in_proj(meta_pad, g_mix2, w_in_bf, _rope_tables(pos_m), tm=META_PAD, tn=tn_proj,
                         rope_cols=rope_cols, pos_blocks=1)
    proj3 = proj.reshape(b, s, -1)

    mix_sb = _sb_attention(proj3, proj_meta, g_sb_out[layer].reshape(1, -1), batch=b, seq=s, heads=sb_heads,
                           tq=tq, tk=tk_sb, hp=HEADS_PER_STEP_SB)
    lams = tuple(t[layer].reshape(1, DA_HEAD_DIM).astype(F32) for t in (lam_q1, lam_k1, lam_q2, lam_k2))
    mix_da = _da_attention(proj3, proj_meta, lams, g_da_out[layer].reshape(1, -1), batch=b, seq=s, heads=da_heads,
                           tq=TQ_DA, tkv=TKV_DA, ts=ts_da, col0=da_col0, lam_init=lam_init)

    w_out_bf = w_out[layer].astype(BF16)
    wr = w_router[layer]
    wr_hi = wr.astype(BF16)
    wr_lo = (wr - wr_hi.astype(F32)).astype(BF16)
    h1, u_packed, top_e, gates, tile_counts = _out_proj(
        mix_sb.reshape(b * s, -1), mix_da.reshape(b * s, -1), w_out_bf[:sb_w], w_out_bf[sb_w:], x2d,
        g_ffn[layer].reshape(1, d), wr_hi, wr_lo, b_router[layer].reshape(1, n_exp), tm=tm_out)

    counts = jnp.sum(tile_counts, axis=(0, 1)).astype(jnp.int32)
    dest_flat, blk_e, blk_cnt, blk_row, n_rows = _routing(top_e, counts, tm_moe)
    dest_km = dest_flat.reshape(b * s, TOP_K).T
    xs = _sc_scatter_rows(u_packed, dest_km, n_rows)
    rows = _moe_ffn(blk_e, blk_cnt, blk_row, xs, w_gate_up.reshape(w_gate_up.shape[1:]),
                    w_down.reshape(w_down.shape[1:]), b_gate_up[layer], b_down[layer],
                    tm=tm_moe, ts=TS_MOE, tf=tf_moe, tn=TN_MOE)
    y_km = _sc_gather_rows(rows, dest_km.reshape(-1))

    out = _final(h1, y_km, gates, g_final.reshape(1, d), tm=tm_fin)
    return out.reshape(b, s, d)
```

```python
import functools
import math

import jax
import jax.numpy as jnp
from jax import lax
from jax.experimental import pallas as pl
from jax.experimental.pallas import tpu as pltpu
from jax.experimental.pallas import tpu_sc as plsc

F32 = jnp.float32
BF16 = jnp.bfloat16

CHUNK = 64
N_META = 16
RMS_EPS = 1e-5
SB_HEAD_DIM = 128
DA_HEAD_DIM = 64
DA_V_DIM = 2 * DA_HEAD_DIM
ROPE_THETA = 500000.0
ROPE_DIM = DA_HEAD_DIM // 4
TOP_K = 4
SWIGLU_LIMIT = 7.0
SWIGLU_ALPHA = 1.702

LANES = 128
META_PAD = 128
NEG_BIG = -1e30
SB_DEAD_RUN = 110.0
VMEM_LIMIT = 56 * 1024 * 1024
SC_CORES, SC_SUBCORES = 2, 16
SC_GATHER_ROWS = 32

TM_PROJ, TN_PROJ = 1024, 1024
TQ_ATTN = 512
HEADS_PER_STEP_SB = 2
TQ_DA, TKV_DA = 1024, 512
TK_SB = 256
TS_DA = 256
TM_OUT = 512
TM_MOE, TS_MOE = 2176, 544
TF_MOE, TN_MOE = 256, 256
TM_FINAL = 512


def _cparams(sem):
    return pltpu.CompilerParams(dimension_semantics=sem, vmem_limit_bytes=VMEM_LIMIT)


def _in_proj_kernel(x_ref, g_ref, w_ref, c_ref, sa_ref, sb_ref, o_ref, u_scr, *, tn, rope_lo, rope_hi):
    n = pl.program_id(1)

    @pl.when(n == 0)
    def _():
        x = x_ref[...]
        ms = jnp.mean(x * x, axis=-1, keepdims=True)
        u_scr[...] = ((x * lax.rsqrt(ms + RMS_EPS)) * g_ref[...]).astype(BF16)

    acc = jnp.dot(u_scr[...], w_ref[...], preferred_element_type=F32)
    is_rope = jnp.logical_and(n >= rope_lo, n < rope_hi)
    o_ref[...] = acc.astype(BF16)

    @pl.when(is_rope)
    def _():
        for c in range(tn // LANES):
            xc = acc[:, c * LANES:(c + 1) * LANES]
            r = (xc * c_ref[...] + pltpu.roll(xc, LANES - ROPE_DIM // 2, 1) * sa_ref[...]
                 + pltpu.roll(xc, ROPE_DIM // 2, 1) * sb_ref[...])
            o_ref[:, c * LANES:(c + 1) * LANES] = r.astype(BF16)


def _rope_tables(pos):
    half = ROPE_DIM // 2
    inv_freq = ROPE_THETA ** (-(jnp.arange(half, dtype=F32) * 2.0 / ROPE_DIM))
    ang = pos.astype(F32)[:, None] * inv_freq[None, :]
    cos, sin = jnp.cos(ang), jnp.sin(ang)
    p = pos.shape[0]
    ones = jnp.ones((p, DA_HEAD_DIM - ROPE_DIM), F32)
    zeros8 = jnp.zeros((p, half), F32)
    zeros48 = jnp.zeros((p, DA_HEAD_DIM - ROPE_DIM), F32)
    c64 = jnp.concatenate([cos, cos, ones], axis=1)
    sa64 = jnp.concatenate([-sin, zeros8, zeros48], axis=1)
    sb64 = jnp.concatenate([zeros8, sin, zeros48], axis=1)
    tile2 = lambda t: jnp.concatenate([t, t], axis=1)
    return tile2(c64), tile2(sa64), tile2(sb64)


def _in_proj(x2d, g, w_bf, tables, *, tm, tn, rope_cols, pos_blocks):
    m, d = x2d.shape
    n_cols = w_bf.shape[1]
    c_t, sa_t, sb_t = tables
    kern = functools.partial(_in_proj_kernel, tn=tn, rope_lo=rope_cols[0] // tn, rope_hi=rope_cols[1] // tn)
    tab_spec = pl.BlockSpec((tm, LANES), lambda i, n: (i % pos_blocks, 0))
    return pl.pallas_call(
        kern,
        grid=(m // tm, n_cols // tn),
        in_specs=[
            pl.BlockSpec((tm, d), lambda i, n: (i, 0)),
            pl.BlockSpec((1, d), lambda i, n: (0, 0)),
            pl.BlockSpec((d, tn), lambda i, n: (0, n)),
            tab_spec, tab_spec, tab_spec,
        ],
        out_specs=pl.BlockSpec((tm, tn), lambda i, n: (i, n)),
        out_shape=jax.ShapeDtypeStruct((m, n_cols), BF16),
        scratch_shapes=[pltpu.VMEM((tm, d), BF16)],
        compiler_params=_cparams(("parallel", "arbitrary")),
        name="in_proj",
    )(x2d, g, w_bf, c_t, sa_t, sb_t)


def _dot_nt(a, b):
    return lax.dot_general(a, b, (((1,), (1,)), ((), ())), preferred_element_type=F32)


def _suffix_sum_matrix(n):
    j = lax.broadcasted_iota(jnp.int32, (2 * n, n), 0)
    s = lax.broadcasted_iota(jnp.int32, (2 * n, n), 1)
    return jnp.where(jnp.where(j >= n, j - n, j) > s, 1.0, 0.0).astype(BF16)


def _sb_block(q, kb, vb, u2, run, acc, mask, scale):
    z = _dot_nt(q, kb) * scale
    sp = jnp.maximum(z, 0.0) + jnp.log(1.0 + jnp.exp(-jnp.abs(z)))
    if mask is not None:
        sp = jnp.where(mask, sp, 0.0)
    hi = sp.astype(BF16)
    lo = (sp - hi.astype(F32)).astype(BF16)
    cs = jnp.dot(jnp.concatenate([hi, lo], axis=1), u2, preferred_element_type=F32)
    w = jnp.exp(z - sp - cs - run)
    if mask is not None:
        w = jnp.where(mask, w, 0.0)
    acc = acc + jnp.dot(w.astype(BF16), vb, preferred_element_type=F32)
    run = run + (cs[:, :1] + sp[:, :1])
    return run, acc


def _sb_kernel(q_ref, k_ref, v_ref, km_ref, vm_ref, g_ref, o_ref, *, tq, tk, hp, scale):
    i = pl.program_id(2)
    nsub = tq // tk
    hd = SB_HEAD_DIM
    u_blk = _suffix_sum_matrix(tk)
    row = lax.broadcasted_iota(jnp.int32, (tk, tk), 0)
    col = lax.broadcasted_iota(jnp.int32, (tk, tk), 1)
    diag_mask = col < row
    heads = range(hp)

    def kv(blk, h):
        s0 = pl.multiple_of(blk * tk, tk)
        return k_ref[pl.ds(s0, tk), h * hd:(h + 1) * hd], v_ref[pl.ds(s0, tk), h * hd:(h + 1) * hd]

    runs, accs = [], []
    for h in heads:
        slab_runs, slab_accs = [], []
        for sl in range(nsub):
            qs = q_ref[sl * tk:(sl + 1) * tk, h * hd:(h + 1) * hd]
            run = jnp.zeros((tk, 1), F32)
            acc = jnp.zeros((tk, hd), F32)
            for c in range(sl, -1, -1):
                kb, vb = kv(i * nsub + c, h)
                run, acc = _sb_block(qs, kb, vb, u_blk, run, acc, diag_mask if c == sl else None, scale)
            slab_runs.append(run)
            slab_accs.append(acc)
        runs.append(jnp.concatenate(slab_runs, axis=0))
        accs.append(jnp.concatenate(slab_accs, axis=0))
    q = [q_ref[:, h * hd:(h + 1) * hd] for h in heads]

    def alive_flag(rs):
        lowest = functools.reduce(jnp.minimum, [jnp.min(r) for r in rs])
        return (lowest < SB_DEAD_RUN).astype(jnp.int32)

    def cond(carry):
        jb, alive, _, _ = carry
        return jnp.logical_and(jb >= 0, alive > 0)

    def body(carry):
        jb, _, rs, as_ = carry
        out = [_sb_block(q[h], *kv(jb, h), u_blk, rs[h], as_[h], None, scale) for h in heads]
        rs = tuple(o[0] for o in out)
        return jb - 1, alive_flag(rs), rs, tuple(o[1] for o in out)

    runs, accs = tuple(runs), tuple(accs)
    _, alive, runs, accs = lax.while_loop(cond, body, (i * nsub - 1, alive_flag(runs), runs, accs))

    def meta_block():
        mcol = lax.broadcasted_iota(jnp.int32, (tq, META_PAD), 1)
        u_meta = _suffix_sum_matrix(META_PAD)
        return tuple(_sb_block(q[h], km_ref[:, h * hd:(h + 1) * hd], vm_ref[:, h * hd:(h + 1) * hd], u_meta,
                               runs[h], accs[h], mcol < N_META, scale)[1] for h in heads)

    accs = lax.cond(alive > 0, meta_block, lambda: accs)

    for h in heads:
        acc = accs[h]
        ms = jnp.mean(acc * acc, axis=-1, keepdims=True)
        gain = g_ref[:, h * hd:(h + 1) * hd]
        o_ref[:, h * hd:(h + 1) * hd] = ((acc * lax.rsqrt(ms + RMS_EPS)) * gain).astype(BF16)


def _sb_attention(proj, proj_meta, g_sb, *, batch, seq, heads, tq, tk, hp):
    hd = SB_HEAD_DIM
    wd = hp * hd
    ng = heads // hp
    kern = functools.partial(_sb_kernel, tq=tq, tk=tk, hp=hp, scale=hd ** -0.5)
    return pl.pallas_call(
        kern,
        grid=(batch, ng, seq // tq),
        in_specs=[
            pl.BlockSpec((None, tq, wd), lambda b, h, i: (b, i, h)),
            pl.BlockSpec((None, seq, wd), lambda b, h, i: (b, 0, ng + h)),
            pl.BlockSpec((None, seq, wd), lambda b, h, i: (b, 0, 2 * ng + h)),
            pl.BlockSpec((META_PAD, wd), lambda b, h, i: (0, ng + h)),
            pl.BlockSpec((META_PAD, wd), lambda b, h, i: (0, 2 * ng + h)),
            pl.BlockSpec((1, wd), lambda b, h, i: (0, h)),
        ],
        out_specs=pl.BlockSpec((None, tq, wd), lambda b, h, i: (b, i, h)),
        out_shape=jax.ShapeDtypeStruct((batch, seq, heads * hd), BF16),
        compiler_params=_cparams(("parallel", "parallel", "arbitrary")),
        name="sb_attn",
    )(proj, proj, proj, proj_meta, proj_meta, g_sb)


def _with_ones(vb):
    return jnp.concatenate([vb, jnp.ones_like(vb)], axis=1)


def _da_block(q1, q2, kb, vb1, st, mask):
    m1, a1, m2, a2 = st

    def one(qc, m, a):
        s = _dot_nt(qc, kb)
        if mask is not None:
            s = jnp.where(mask, s, NEG_BIG)
        mn = jnp.maximum(m, jnp.max(s, axis=-1, keepdims=True))
        p = jnp.exp(s - mn)
        a = jnp.exp(m - mn) * a + jnp.dot(p.astype(BF16), vb1, preferred_element_type=F32)
        return mn, a

    m1, a1 = one(q1, m1, a1)
    m2, a2 = one(q2, m2, a2)
    return m1, a1, m2, a2


def _da_kernel(lq1_ref, lk1_ref, lq2_ref, lk2_ref, q_ref, k_ref, v_ref, km_ref, vm_ref, g_ref, o_ref,
               *, tq, tkv, ts, scale, lam_init):
    i = pl.program_id(2)
    nsub = tq // ts
    lam = (jnp.exp(jnp.sum(lq1_ref[...] * lk1_ref[...], axis=-1, keepdims=True))
           - jnp.exp(jnp.sum(lq2_ref[...] * lk2_ref[...], axis=-1, keepdims=True)) + lam_init)

    q = q_ref[...]
    lane = lax.broadcasted_iota(jnp.int32, q.shape, 1)
    qs = q * jnp.asarray(scale, BF16)
    zero = jnp.zeros_like(qs)
    q1 = jnp.where(lane < DA_HEAD_DIM, qs, zero)
    q2 = jnp.where(lane >= DA_HEAD_DIM, qs, zero)

    neg = jnp.full((tq, 1), NEG_BIG, F32)
    za = jnp.zeros((tq, 2 * DA_V_DIM), F32)
    st = (neg, za, neg, za)

    def body(j, carry):
        for c in range(tq // tkv):
            s0 = pl.multiple_of(j * tq + c * tkv, tkv)
            carry = _da_block(q1, q2, k_ref[pl.ds(s0, tkv), :], _with_ones(v_ref[pl.ds(s0, tkv), :]), carry, None)
        return carry

    st = lax.fori_loop(0, i, body, st)

    shift = CHUNK.bit_length() - 1
    s0 = pl.multiple_of(i * tq, tq)
    outs = []
    for h in range(nsub):
        rows = slice(h * ts, (h + 1) * ts)
        width = (h + 1) * ts
        kb = jnp.concatenate([km_ref[...], k_ref[pl.ds(s0, width), :]], axis=0)
        vb = jnp.concatenate([vm_ref[...], v_ref[pl.ds(s0, width), :]], axis=0)
        col = lax.broadcasted_iota(jnp.int32, (ts, META_PAD + width), 1)
        row = lax.broadcasted_iota(jnp.int32, (ts, META_PAD + width), 0) + h * ts
        frame_vis = jnp.logical_and(col >= META_PAD,
                                    jnp.right_shift(col - META_PAD, shift) <= jnp.right_shift(row, shift))
        vis = jnp.logical_or(col < N_META, frame_vis)
        sth = _da_block(q1[rows], q2[rows], kb, _with_ones(vb), tuple(t[rows] for t in st), vis)
        _, a1, _, a2 = sth
        outs.append(a1[:, :DA_V_DIM] / a1[:, DA_V_DIM:] - lam * (a2[:, :DA_V_DIM] / a2[:, DA_V_DIM:]))
    o = jnp.concatenate(outs, axis=0)
    ms = jnp.mean(o * o, axis=-1, keepdims=True)
    o_ref[...] = (((o * lax.rsqrt(ms + RMS_EPS)) * g_ref[...]) * (1.0 - lam_init)).astype(BF16)


def _da_attention(proj, proj_meta, lams, g_da, *, batch, seq, heads, tq, tkv, ts, col0, lam_init):
    hd = DA_V_DIM
    qb, kb, vb = col0 // hd, col0 // hd + heads, col0 // hd + 2 * heads
    kern = functools.partial(_da_kernel, tq=tq, tkv=tkv, ts=ts, scale=DA_HEAD_DIM ** -0.5, lam_init=lam_init)
    lam_spec = pl.BlockSpec((1, DA_HEAD_DIM), lambda b, h, i: (0, 0))
    return pl.pallas_call(
        kern,
        grid=(batch, heads, seq // tq),
        in_specs=[
            lam_spec, lam_spec, lam_spec, lam_spec,
            pl.BlockSpec((None, tq, hd), lambda b, h, i: (b, i, qb + h)),
            pl.BlockSpec((None, seq, hd), lambda b, h, i: (b, 0, kb + h)),
            pl.BlockSpec((None, seq, hd), lambda b, h, i: (b, 0, vb + h)),
            pl.BlockSpec((META_PAD, hd), lambda b, h, i: (0, kb + h)),
            pl.BlockSpec((META_PAD, hd), lambda b, h, i: (0, vb + h)),
            pl.BlockSpec((1, hd), lambda b, h, i: (0, h)),
        ],
        out_specs=pl.BlockSpec((None, tq, hd), lambda b, h, i: (b, i, h)),
        out_shape=jax.ShapeDtypeStruct((batch, seq, heads * hd), BF16),
        compiler_params=_cparams(("parallel", "parallel", "arbitrary")),
        name="da_attn",
    )(*lams, proj, proj, proj, proj_meta, proj_meta, g_da)


def _pack_bf16_pairs(x):
    n = x.shape[1] // 2
    lo = lax.bitcast_convert_type(x[:, :n].astype(BF16).astype(F32), jnp.int32)
    hi = lax.bitcast_convert_type(x[:, n:].astype(BF16).astype(F32), jnp.int32)
    return jnp.bitwise_or(lax.shift_right_logical(lo, 16), jnp.bitwise_and(hi, jnp.int32(-65536)))


def _unpack_bf16_pairs(w):
    lo = lax.bitcast_convert_type(lax.shift_left(w, 16), F32)
    hi = lax.bitcast_convert_type(jnp.bitwise_and(w, jnp.int32(-65536)), F32)
    return jnp.concatenate([lo.astype(BF16), hi.astype(BF16)], axis=1)


def _sc_gather_rows(table, idx):
    n_idx = idx.shape[0]
    width = table.shape[1]
    n_workers = SC_CORES * SC_SUBCORES
    per_worker = n_idx // n_workers
    win = SC_GATHER_ROWS
    assert n_idx % (n_workers * win) == 0
    mesh = plsc.VectorSubcoreMesh(core_axis_name="c", subcore_axis_name="s")

    n_win = per_worker // win
    assert n_win % 2 == 0
    buf = lambda: pltpu.VMEM((win, width), table.dtype)
    dma = pltpu.SemaphoreType.DMA

    @functools.partial(
        pl.kernel, mesh=mesh,
        out_type=jax.ShapeDtypeStruct((n_idx, width), table.dtype),
        scratch_types=[pltpu.VMEM((n_win, win), jnp.int32), buf(), buf(), dma, dma, dma, dma],
    )
    def gather(table_hbm, idx_hbm, out_hbm, idx_v, rows0, rows1, gsem0, gsem1, wsem0, wsem1):
        worker = lax.axis_index("s") * SC_CORES + lax.axis_index("c")
        base = worker * per_worker
        pltpu.sync_copy(idx_hbm.at[pl.ds(worker * n_win, n_win)], idx_v)

        def gather_copy(t, rows_v, sem):
            return pltpu.make_async_copy(table_hbm.at[idx_v.at[t]], rows_v, sem)

        def writeback(t, rows_v, sem):
            return pltpu.make_async_copy(rows_v, out_hbm.at[pl.ds(pl.multiple_of(base + t * win, win), win)], sem)

        gather_copy(0, rows0, gsem0).start()

        @pl.loop(0, n_win, step=2)
        def _(t):
            gather_copy(t, rows0, gsem0).wait()
            writeback(t, rows0, wsem0).start()

            @pl.when(t > 0)
            def _():
                writeback(t - 1, rows1, wsem1).wait()

            gather_copy(t + 1, rows1, gsem1).start()
            gather_copy(t + 1, rows1, gsem1).wait()
            writeback(t + 1, rows1, wsem1).start()
            writeback(t, rows0, wsem0).wait()

            @pl.when(t + 2 < n_win)
            def _():
                gather_copy(t + 2, rows0, gsem0).start()

        writeback(n_win - 1, rows1, wsem1).wait()

    return gather(table, idx.reshape(n_idx // win, win))


def _sc_scatter_rows(src, dest_km, n_rows):
    n_tok, width = src.shape
    n_slots = dest_km.shape[0]
    n_workers = SC_CORES * SC_SUBCORES
    win = SC_GATHER_ROWS
    per_worker = n_tok // n_workers
    n_win = per_worker // win
    assert n_tok % (n_workers * win) == 0 and n_win % 2 == 0
    mesh = plsc.VectorSubcoreMesh(core_axis_name="c", subcore_axis_name="s")
    buf = lambda: pltpu.VMEM((win, width), src.dtype)
    dma = pltpu.SemaphoreType.DMA

    @functools.partial(
        pl.kernel, mesh=mesh,
        out_type=jax.ShapeDtypeStruct((n_rows, width), src.dtype),
        scratch_types=[pltpu.VMEM((n_slots, n_win, win), jnp.int32), buf(), buf(), dma, dma, dma, dma],
    )
    def scatter(src_hbm, idx_hbm, out_hbm, idx_v, rows0, rows1, lsem0, lsem1, ssem0, ssem1):
        worker = lax.axis_index("s") * SC_CORES + lax.axis_index("c")
        base = worker * per_worker
        for k in range(n_slots):
            pltpu.sync_copy(idx_hbm.at[k, pl.ds(worker * n_win, n_win)], idx_v.at[k])

        def load(t, rows_v, sem):
            return pltpu.make_async_copy(src_hbm.at[pl.ds(pl.multiple_of(base + t * win, win), win)], rows_v, sem)

        def start_scatters(t, rows_v, sem):
            for k in range(n_slots):
                pltpu.make_async_copy(rows_v, out_hbm.at[idx_v.at[k, t]], sem).start()

        def wait_scatters(t, rows_v, sem):
            for k in range(n_slots):
                pltpu.make_async_copy(rows_v, out_hbm.at[idx_v.at[k, t]], sem).wait()

        load(0, rows0, lsem0).start()

        @pl.loop(0, n_win, step=2)
        def _(t):
            load(t, rows0, lsem0).wait()
            start_scatters(t, rows0, ssem0)

            @pl.when(t > 0)
            def _():
                wait_scatters(t - 1, rows1, ssem1)

            load(t + 1, rows1, lsem1).start()
            load(t + 1, rows1, lsem1).wait()
            start_scatters(t + 1, rows1, ssem1)
            wait_scatters(t, rows0, ssem0)

            @pl.when(t + 2 < n_win)
            def _():
                load(t + 2, rows0, lsem0).start()

        wait_scatters(n_win - 1, rows1, ssem1)

    return scatter(src, dest_km.reshape(n_slots, n_tok // win, win))


def _split3_dot(a, b_hi, b_lo):
    a_hi = a.astype(BF16)
    a_lo = (a - a_hi.astype(F32)).astype(BF16)
    return (jnp.dot(a_hi, b_hi, preferred_element_type=F32)
            + jnp.dot(a_hi, b_lo, preferred_element_type=F32)
            + jnp.dot(a_lo, b_hi, preferred_element_type=F32))


def _out_proj_kernel(ms_ref, md_ref, ws_ref, wd_ref, x_ref, g_ref, wr_hi_ref, wr_lo_ref, br_ref,
                     h_ref, u_ref, e_ref, p_ref, c_ref, *, n_exp):
    h = (x_ref[...]
         + jnp.dot(ms_ref[...], ws_ref[...], preferred_element_type=F32)
         + jnp.dot(md_ref[...], wd_ref[...], preferred_element_type=F32))
    h_ref[...] = h
    msq = jnp.mean(h * h, axis=-1, keepdims=True)
    u = (h * lax.rsqrt(msq + RMS_EPS)) * g_ref[...]
    u_ref[...] = _pack_bf16_pairs(u)

    logits = _split3_dot(u, wr_hi_ref[...], wr_lo_ref[...]) + br_ref[...]
    lane = lax.broadcasted_iota(jnp.int32, logits.shape, 1).astype(F32)
    work = logits
    tops, idxs = [], []
    for _ in range(TOP_K):
        mx = jnp.max(work, axis=-1, keepdims=True)
        ix = jnp.min(jnp.where(work == mx, lane, float(n_exp)), axis=-1, keepdims=True)
        tops.append(mx)
        idxs.append(ix)
        work = jnp.where(lane == ix, -jnp.inf, work)
    ex = [jnp.exp(t - tops[0]) for t in tops]
    den = ex[0] + ex[1] + ex[2] + ex[3]
    kl = lax.broadcasted_iota(jnp.int32, (logits.shape[0], TOP_K), 1)
    e_out = jnp.zeros((logits.shape[0], TOP_K), F32)
    p_out = jnp.zeros((logits.shape[0], TOP_K), F32)
    for k in range(TOP_K):
        e_out = jnp.where(kl == k, idxs[k], e_out)
        p_out = jnp.where(kl == k, ex[k] / den, p_out)
    e_ref[...] = e_out.astype(jnp.int32)
    p_ref[...] = p_out
    hits = jnp.zeros(logits.shape, F32)
    for k in range(TOP_K):
        hits = hits + jnp.where(lane == idxs[k], 1.0, 0.0)
    c_ref[...] = jnp.sum(hits, axis=0, keepdims=True)


def _out_proj(mix_sb, mix_da, w_sb, w_da, x2d, g_ffn, wr_hi, wr_lo, b_r, *, tm):
    m, d = x2d.shape
    ks, kd = mix_sb.shape[1], mix_da.shape[1]
    n_exp = wr_hi.shape[1]
    kern = functools.partial(_out_proj_kernel, n_exp=n_exp)
    const = lambda shape: pl.BlockSpec(shape, lambda i: (0, 0))
    return pl.pallas_call(
        kern,
        grid=(m // tm,),
        in_specs=[
            pl.BlockSpec((tm, ks), lambda i: (i, 0)),
            pl.BlockSpec((tm, kd), lambda i: (i, 0)),
            const((ks, d)), const((kd, d)),
            pl.BlockSpec((tm, d), lambda i: (i, 0)),
            const((1, d)), const((d, n_exp)), const((d, n_exp)), const((1, n_exp)),
        ],
        out_specs=[
            pl.BlockSpec((tm, d), lambda i: (i, 0)),
            pl.BlockSpec((tm, d // 2), lambda i: (i, 0)),
            pl.BlockSpec((tm, TOP_K), lambda i: (i, 0)),
            pl.BlockSpec((tm, TOP_K), lambda i: (i, 0)),
            pl.BlockSpec((None, 1, n_exp), lambda i: (i, 0, 0)),
        ],
        out_shape=[
            jax.ShapeDtypeStruct((m, d), F32),
            jax.ShapeDtypeStruct((m, d // 2), jnp.int32),
            jax.ShapeDtypeStruct((m, TOP_K), jnp.int32),
            jax.ShapeDtypeStruct((m, TOP_K), F32),
            jax.ShapeDtypeStruct((m // tm, 1, n_exp), F32),
        ],
        compiler_params=_cparams(("parallel",)),
        name="out_proj_router",
    )(mix_sb, mix_da, w_sb, w_da, x2d, g_ffn, wr_hi, wr_lo, b_r)


def _moe_kernel(be_ref, bc_ref, rb_ref, x_ref, wg_ref, wu_ref, wda_ref, wdb_ref, bg_ref, bup_ref, bda_ref, bdb_ref,
                o_ref, act_ref, *, nf, ts):
    blk = pl.program_id(0)
    s = pl.program_id(1)
    cnt = bc_ref[blk]
    tg = 2 * ts
    tm = x_ref.shape[0]

    def row_groups(run, skip):
        for lo in range(0, tm, tg):
            size = min(tg, tm - lo)

            @pl.when(cnt > lo + size - ts)
            def _():
                run(lo, size)

            if size > ts:
                @pl.when(jnp.logical_and(cnt > lo, cnt <= lo + ts))
                def _():
                    run(lo, ts)
                    skip(lo + ts, ts)

            @pl.when(cnt <= lo)
            def _():
                skip(lo, size)

    @pl.when(jnp.logical_and(cnt > 0, s < nf))
    def _():
        slot = jnp.minimum(s, nf - 1)

        def up_proj(r0, nrows):
            x = _unpack_bf16_pairs(x_ref[r0:r0 + nrows, :])
            g = jnp.dot(x, wg_ref[...].astype(BF16), preferred_element_type=F32) + bg_ref[...]
            u = jnp.dot(x, wu_ref[...].astype(BF16), preferred_element_type=F32) + bup_ref[...]
            gate = jnp.minimum(g, SWIGLU_LIMIT)
            up = jnp.clip(u, -SWIGLU_LIMIT, SWIGLU_LIMIT)
            act = (up + 1.0) * gate * jax.nn.sigmoid(SWIGLU_ALPHA * gate)
            act_ref[slot, r0:r0 + nrows, :] = act.astype(BF16)

        row_groups(up_proj, lambda r0, nrows: None)

    @pl.when(jnp.logical_and(cnt > 0, s >= nf))
    def _():
        tf = act_ref.shape[2]

        def down_proj(r0, nrows):
            ya = bda_ref[...]
            yb = bdb_ref[...]
            for f in range(nf):
                a = act_ref[f, r0:r0 + nrows, :]
                wda = wda_ref[f * tf:(f + 1) * tf, :].astype(BF16)
                wdb = wdb_ref[f * tf:(f + 1) * tf, :].astype(BF16)
                ya = ya + jnp.dot(a, wda, preferred_element_type=F32)
                yb = yb + jnp.dot(a, wdb, preferred_element_type=F32)
            o_ref[r0:r0 + nrows, :] = _pack_bf16_pairs(jnp.concatenate([ya, yb], axis=1))

        def zero_fill(r0, nrows):
            o_ref[r0:r0 + nrows, :] = jnp.zeros((nrows, o_ref.shape[1]), o_ref.dtype)

        row_groups(down_proj, zero_fill)


def _moe_ffn(blk_e, blk_cnt, blk_row, xs, w_gu, w_dn, b_gu, b_dn, *, tm, ts, tf, tn):
    n_rows = xs.shape[0]
    n_exp, d, f2 = w_gu.shape
    d_ff = f2 // 2
    nf = d_ff // tf
    nn = d // tn
    n_blk = n_rows // tm
    b_gu3 = b_gu.reshape(n_exp, 1, f2)
    b_dn3 = b_dn.reshape(n_exp, 1, d)

    def fi(s, bc, b):
        return jnp.where(bc[b] > 0, jnp.minimum(s, nf - 1), nf - 1)

    nh = nn // 2

    def ni(s, bc, b):
        return jnp.where(bc[b] > 0, jnp.clip(s - nf, 0, nh - 1), nh - 1)

    grid_spec = pltpu.PrefetchScalarGridSpec(
        num_scalar_prefetch=3,
        grid=(n_blk, nf + nh),
        in_specs=[
            pl.BlockSpec((tm, d // 2), lambda b, s, be, bc, rb: (rb[b], 0)),
            pl.BlockSpec((None, d, tf), lambda b, s, be, bc, rb: (be[b], 0, fi(s, bc, b))),
            pl.BlockSpec((None, d, tf), lambda b, s, be, bc, rb: (be[b], 0, nf + fi(s, bc, b))),
            pl.BlockSpec((None, d_ff, tn), lambda b, s, be, bc, rb: (be[b], 0, ni(s, bc, b))),
            pl.BlockSpec((None, d_ff, tn), lambda b, s, be, bc, rb: (be[b], 0, nh + ni(s, bc, b))),
            pl.BlockSpec((None, 1, tf), lambda b, s, be, bc, rb: (be[b], 0, fi(s, bc, b))),
            pl.BlockSpec((None, 1, tf), lambda b, s, be, bc, rb: (be[b], 0, nf + fi(s, bc, b))),
            pl.BlockSpec((None, 1, tn), lambda b, s, be, bc, rb: (be[b], 0, ni(s, bc, b))),
            pl.BlockSpec((None, 1, tn), lambda b, s, be, bc, rb: (be[b], 0, nh + ni(s, bc, b))),
        ],
        out_specs=pl.BlockSpec((tm, tn), lambda b, s, be, bc, rb: (rb[b], ni(s, bc, b))),
        scratch_shapes=[pltpu.VMEM((nf, tm, tf), BF16)],
    )
    return pl.pallas_call(
        functools.partial(_moe_kernel, nf=nf, ts=ts),
        grid_spec=grid_spec,
        out_shape=jax.ShapeDtypeStruct((n_rows, d // 2), jnp.int32),
        compiler_params=_cparams(("arbitrary", "arbitrary")),
        name="moe_ffn",
    )(blk_e, blk_cnt, blk_row, xs, w_gu, w_gu, w_dn, w_dn, b_gu3, b_gu3, b_dn3, b_dn3)


def _final_kernel(h_ref, y0_ref, y1_ref, y2_ref, y3_ref, p_ref, g_ref, o_ref):
    p = p_ref[...]
    y = None
    for k, y_ref in enumerate((y0_ref, y1_ref, y2_ref, y3_ref)):
        t = _unpack_bf16_pairs(y_ref[...]).astype(F32) * p[:, k:k + 1]
        y = t if y is None else y + t
    h = h_ref[...] + y
    ms = jnp.mean(h * h, axis=-1, keepdims=True)
    o_ref[...] = (h * lax.rsqrt(ms + RMS_EPS)) * g_ref[...]


def _final(h1, y_km, gates, g, *, tm):
    m, d = h1.shape
    nb = m // tm
    assert TOP_K == 4
    y_specs = [pl.BlockSpec((tm, d // 2), functools.partial(lambda i, k: (k * nb + i, 0), k=k))
               for k in range(TOP_K)]
    return pl.pallas_call(
        _final_kernel,
        grid=(nb,),
        in_specs=([pl.BlockSpec((tm, d), lambda i: (i, 0))] + y_specs
                  + [pl.BlockSpec((tm, TOP_K), lambda i: (i, 0)), pl.BlockSpec((1, d), lambda i: (0, 0))]),
        out_specs=pl.BlockSpec((tm, d), lambda i: (i, 0)),
        out_shape=jax.ShapeDtypeStruct((m, d), F32),
        compiler_params=_cparams(("parallel",)),
        name="final_norm",
    )(h1, y_km, y_km, y_km, y_km, gates, g)


def _moe_blocks(n_assign, n_exp, tm):
    n_blk = -(-n_assign // tm) + n_exp
    unit = SC_CORES * SC_SUBCORES * SC_GATHER_ROWS
    while (n_blk * tm) % unit:
        n_blk += 1
    return n_blk


def _routing(top_e, counts, tm):
    n_exp = counts.shape[0]
    n_tok = top_e.shape[0]
    n_assign = n_tok * TOP_K
    flat_e = top_e.reshape(-1)
    padded = (counts + tm - 1) // tm * tm
    pad_end = jnp.cumsum(padded).astype(jnp.int32)
    pad_start = pad_end - padded
    grp_start = jnp.cumsum(counts).astype(jnp.int32) - counts
    n_blk = _moe_blocks(n_assign, n_exp, tm)
    blk_start = jnp.arange(n_blk, dtype=jnp.int32) * tm
    blk_e = jnp.minimum(jnp.sum((blk_start[:, None] >= pad_end[None, :]).astype(jnp.int32), axis=1), n_exp - 1)
    blk_off = blk_start - pad_start[blk_e]
    blk_cnt = jnp.clip(counts[blk_e] - blk_off, 0, tm).astype(jnp.int32)
    order = jnp.argsort(flat_e, stable=True).astype(jnp.int32)
    rank = jnp.argsort(order).astype(jnp.int32)
    delta = pad_start - grp_start
    e2d = flat_e.reshape(-1, LANES)
    hit = e2d[None] == jnp.arange(n_exp, dtype=jnp.int32)[:, None, None]
    dest_flat = rank + jnp.sum(jnp.where(hit, delta[:, None, None], 0), axis=0).reshape(-1)
    last_used = jnp.maximum(pad_end[-1] // tm - 1, 0)
    blk_e = jnp.where(blk_cnt > 0, blk_e, blk_e[last_used]).astype(jnp.int32)
    blk_row = jnp.where(blk_cnt > 0, jnp.arange(n_blk, dtype=jnp.int32), last_used).astype(jnp.int32)
    return dest_flat, blk_e, blk_cnt, blk_row, n_blk * tm


def kernel(x, meta_tokens, g_mix, w_in, lam_q1, lam_k1, lam_q2, lam_k2, g_sb_out, g_da_out, w_out, g_ffn,
           w_router, b_router, w_gate_up, b_gate_up, w_down, b_down, g_final):
    b, s, d = x.shape
    depth = w_in.shape[0]
    assert depth == 1, "single-layer trunk"
    layer = 0
    sb_heads = (d // 2) // SB_HEAD_DIM
    da_heads = (d // 2) // DA_V_DIM
    sb_w = sb_heads * SB_HEAD_DIM
    da_col0 = 3 * sb_w
    da_qk_w = da_heads * 2 * DA_HEAD_DIM
    n_exp = w_router.shape[-1]
    lam_init = 0.8 - 0.6 * math.exp(-0.3 * layer)

    tm_proj, tn_proj = TM_PROJ, TN_PROJ
    tq, tk_sb, ts_da = TQ_ATTN, TK_SB, TS_DA
    tm_out = TM_OUT
    tm_moe, tf_moe = TM_MOE, TF_MOE
    tm_fin = TM_FINAL

    x2d = x.reshape(b * s, d)
    w_in_bf = w_in.reshape(w_in.shape[1:]).astype(BF16)
    g_mix2 = g_mix[layer].reshape(1, d)
    rope_cols = (da_col0, da_col0 + 2 * da_qk_w)

    pos_f = N_META + jnp.arange(s, dtype=jnp.int32)
    proj = _in_proj(x2d, g_mix2, w_in_bf, _rope_tables(pos_f), tm=tm_proj, tn=tn_proj,
                    rope_cols=rope_cols, pos_blocks=s // tm_proj)
    meta_pad = jnp.zeros((META_PAD, d), x.dtype).at[:N_META].set(meta_tokens.astype(x.dtype))
    pos_m = jnp.arange(META_PAD, dtype=jnp.int32)
    proj_meta = _in_proj(meta_pad, g_mix2, w_in_bf, _rope_tables(pos_m), tm=META_PAD, tn=tn_proj,
                         rope_cols=rope_cols, pos_blocks=1)
    proj3 = proj.reshape(b, s, -1)

    mix_sb = _sb_attention(proj3, proj_meta, g_sb_out[layer].reshape(1, -1), batch=b, seq=s, heads=sb_heads,
                           tq=tq, tk=tk_sb, hp=HEADS_PER_STEP_SB)
    lams = tuple(t[layer].reshape(1, DA_HEAD_DIM).astype(F32) for t in (lam_q1, lam_k1, lam_q2, lam_k2))
    mix_da = _da_attention(proj3, proj_meta, lams, g_da_out[layer].reshape(1, -1), batch=b, seq=s, heads=da_heads,
                           tq=TQ_DA, tkv=TKV_DA, ts=ts_da, col0=da_col0, lam_init=lam_init)

    w_out_bf = w_out[layer].astype(BF16)
    wr = w_router[layer]
    wr_hi = wr.astype(BF16)
    wr_lo = (wr - wr_hi.astype(F32)).astype(BF16)
    h1, u_packed, top_e, gates, tile_counts = _out_proj(
        mix_sb.reshape(b * s, -1), mix_da.reshape(b * s, -1), w_out_bf[:sb_w], w_out_bf[sb_w:], x2d,
        g_ffn[layer].reshape(1, d), wr_hi, wr_lo, b_router[layer].reshape(1, n_exp), tm=tm_out)

    counts = jnp.sum(tile_counts, axis=(0, 1)).astype(jnp.int32)
    dest_flat, blk_e, blk_cnt, blk_row, n_rows = _routing(top_e, counts, tm_moe)
    dest_km = dest_flat.reshape(b * s, TOP_K).T
    xs = _sc_scatter_rows(u_packed, dest_km, n_rows)
    rows = _moe_ffn(blk_e, blk_cnt, blk_row, xs, w_gate_up.reshape(w_gate_up.shape[1:]),
                    w_down.reshape(w_down.shape[1:]), b_gate_up[layer], b_down[layer],
                    tm=tm_moe, ts=TS_MOE, tf=tf_moe, tn=TN_MOE)
    y_km = _sc_gather_rows(rows, dest_km.reshape(-1))

    out = _final(h1, y_km, gates, g_final.reshape(1, d), tm=tm_fin)
    return out.reshape(b, s, d)
```

```python
import functools
import math

import jax
import jax.numpy as jnp
from jax import lax
from jax.experimental import pallas as pl
from jax.experimental.pallas import tpu as pltpu
from jax.experimental.pallas import tpu_sc as plsc

F32 = jnp.float32
BF16 = jnp.bfloat16

CHUNK = 64
N_META = 16
RMS_EPS = 1e-5
SB_HEAD_DIM = 128
DA_HEAD_DIM = 64
DA_V_DIM = 2 * DA_HEAD_DIM
ROPE_THETA = 500000.0
ROPE_DIM = DA_HEAD_DIM // 4
TOP_K = 4
SWIGLU_LIMIT = 7.0
SWIGLU_ALPHA = 1.702

LANES = 128
META_PAD = 128
NEG_BIG = -1e30
SB_DEAD_RUN = 110.0
VMEM_LIMIT = 56 * 1024 * 1024
SC_CORES, SC_SUBCORES = 2, 16
SC_GATHER_ROWS = 32

TM_PROJ, TN_PROJ = 1024, 1024
TQ_ATTN = 512
HEADS_PER_STEP_SB = 4
TQ_DA, TKV_DA = 1024, 512
TK_SB = 256
TS_DA = 512
TM_OUT = 512
TM_MOE, TS_MOE = 2176, 544
TF_MOE, TN_MOE = 256, 256
TM_FINAL = 512


def _cparams(sem):
    return pltpu.CompilerParams(dimension_semantics=sem, vmem_limit_bytes=VMEM_LIMIT)


def _in_proj_kernel(x_ref, g_ref, w_ref, c_ref, sa_ref, sb_ref, o_ref, u_scr, *, tn, rope_lo, rope_hi):
    n = pl.program_id(1)

    @pl.when(n == 0)
    def _():
        x = x_ref[...]
        ms = jnp.mean(x * x, axis=-1, keepdims=True)
        u_scr[...] = ((x * lax.rsqrt(ms + RMS_EPS)) * g_ref[...]).astype(BF16)

    acc = jnp.dot(u_scr[...], w_ref[...], preferred_element_type=F32)
    is_rope = jnp.logical_and(n >= rope_lo, n < rope_hi)
    o_ref[...] = acc.astype(BF16)

    @pl.when(is_rope)
    def _():
        for c in range(tn // LANES):
            xc = acc[:, c * LANES:(c + 1) * LANES]
            r = (xc * c_ref[...] + pltpu.roll(xc, LANES - ROPE_DIM // 2, 1) * sa_ref[...]
                 + pltpu.roll(xc, ROPE_DIM // 2, 1) * sb_ref[...])
            o_ref[:, c * LANES:(c + 1) * LANES] = r.astype(BF16)


def _rope_tables(pos):
    half = ROPE_DIM // 2
    inv_freq = ROPE_THETA ** (-(jnp.arange(half, dtype=F32) * 2.0 / ROPE_DIM))
    ang = pos.astype(F32)[:, None] * inv_freq[None, :]
    cos, sin = jnp.cos(ang), jnp.sin(ang)
    p = pos.shape[0]
    ones = jnp.ones((p, DA_HEAD_DIM - ROPE_DIM), F32)
    zeros8 = jnp.zeros((p, half), F32)
    zeros48 = jnp.zeros((p, DA_HEAD_DIM - ROPE_DIM), F32)
    c64 = jnp.concatenate([cos, cos, ones], axis=1)
    sa64 = jnp.concatenate([-sin, zeros8, zeros48], axis=1)
    sb64 = jnp.concatenate([zeros8, sin, zeros48], axis=1)
    tile2 = lambda t: jnp.concatenate([t, t], axis=1)
    return tile2(c64), tile2(sa64), tile2(sb64)


def _in_proj(x2d, g, w_bf, tables, *, tm, tn, rope_cols, pos_blocks):
    m, d = x2d.shape
    n_cols = w_bf.shape[1]
    c_t, sa_t, sb_t = tables
    kern = functools.partial(_in_proj_kernel, tn=tn, rope_lo=rope_cols[0] // tn, rope_hi=rope_cols[1] // tn)
    tab_spec = pl.BlockSpec((tm, LANES), lambda i, n: (i % pos_blocks, 0))
    return pl.pallas_call(
        kern,
        grid=(m // tm, n_cols // tn),
        in_specs=[
            pl.BlockSpec((tm, d), lambda i, n: (i, 0)),
            pl.BlockSpec((1, d), lambda i, n: (0, 0)),
            pl.BlockSpec((d, tn), lambda i, n: (0, n)),
            tab_spec, tab_spec, tab_spec,
        ],
        out_specs=pl.BlockSpec((tm, tn), lambda i, n: (i, n)),
        out_shape=jax.ShapeDtypeStruct((m, n_cols), BF16),
        scratch_shapes=[pltpu.VMEM((tm, d), BF16)],
        compiler_params=_cparams(("parallel", "arbitrary")),
        name="in_proj",
    )(x2d, g, w_bf, c_t, sa_t, sb_t)


def _dot_nt(a, b):
    return lax.dot_general(a, b, (((1,), (1,)), ((), ())), preferred_element_type=F32)


def _suffix_sum_matrix(n):
    j = lax.broadcasted_iota(jnp.int32, (2 * n, n), 0)
    s = lax.broadcasted_iota(jnp.int32, (2 * n, n), 1)
    return jnp.where(jnp.where(j >= n, j - n, j) > s, 1.0, 0.0).astype(BF16)


def _sb_block(q, kb, vb, u2, run, acc, mask, scale):
    z = _dot_nt(q, kb) * scale
    sp = jnp.maximum(z, 0.0) + jnp.log(1.0 + jnp.exp(-jnp.abs(z)))
    if mask is not None:
        sp = jnp.where(mask, sp, 0.0)
    hi = sp.astype(BF16)
    lo = (sp - hi.astype(F32)).astype(BF16)
    cs = jnp.dot(jnp.concatenate([hi, lo], axis=1), u2, preferred_element_type=F32)
    w = jnp.exp(z - sp - cs - run)
    if mask is not None:
        w = jnp.where(mask, w, 0.0)
    acc = acc + jnp.dot(w.astype(BF16), vb, preferred_element_type=F32)
    run = run + (cs[:, :1] + sp[:, :1])
    return run, acc


def _sb_kernel(q_ref, k_ref, v_ref, km_ref, vm_ref, g_ref, o_ref, *, tq, tk, hp, scale):
    i = pl.program_id(2)
    nsub = tq // tk
    hd = SB_HEAD_DIM
    u_blk = _suffix_sum_matrix(tk)
    row = lax.broadcasted_iota(jnp.int32, (tk, tk), 0)
    col = lax.broadcasted_iota(jnp.int32, (tk, tk), 1)
    diag_mask = col < row
    heads = range(hp)

    def kv(blk, h):
        s0 = pl.multiple_of(blk * tk, tk)
        return k_ref[pl.ds(s0, tk), h * hd:(h + 1) * hd], v_ref[pl.ds(s0, tk), h * hd:(h + 1) * hd]

    runs, accs = [], []
    for h in heads:
        slab_runs, slab_accs = [], []
        for sl in range(nsub):
            qs = q_ref[sl * tk:(sl + 1) * tk, h * hd:(h + 1) * hd]
            run = jnp.zeros((tk, 1), F32)
            acc = jnp.zeros((tk, hd), F32)
            for c in range(sl, -1, -1):
                kb, vb = kv(i * nsub + c, h)
                run, acc = _sb_block(qs, kb, vb, u_blk, run, acc, diag_mask if c == sl else None, scale)
            slab_runs.append(run)
            slab_accs.append(acc)
        runs.append(jnp.concatenate(slab_runs, axis=0))
        accs.append(jnp.concatenate(slab_accs, axis=0))
    q = [q_ref[:, h * hd:(h + 1) * hd] for h in heads]

    def alive_flag(rs):
        lowest = functools.reduce(jnp.minimum, [jnp.min(r) for r in rs])
        return (lowest < SB_DEAD_RUN).astype(jnp.int32)

    def cond(carry):
        jb, alive, _, _ = carry
        return jnp.logical_and(jb >= 0, alive > 0)

    def body(carry):
        jb, _, rs, as_ = carry
        out = [_sb_block(q[h], *kv(jb, h), u_blk, rs[h], as_[h], None, scale) for h in heads]
        rs = tuple(o[0] for o in out)
        return jb - 1, alive_flag(rs), rs, tuple(o[1] for o in out)

    runs, accs = tuple(runs), tuple(accs)
    _, alive, runs, accs = lax.while_loop(cond, body, (i * nsub - 1, alive_flag(runs), runs, accs))

    def meta_block():
        mcol = lax.broadcasted_iota(jnp.int32, (tq, META_PAD), 1)
        u_meta = _suffix_sum_matrix(META_PAD)
        return tuple(_sb_block(q[h], km_ref[:, h * hd:(h + 1) * hd], vm_ref[:, h * hd:(h + 1) * hd], u_meta,
                               runs[h], accs[h], mcol < N_META, scale)[1] for h in heads)

    accs = lax.cond(alive > 0, meta_block, lambda: accs)

    for h in heads:
        acc = accs[h]
        ms = jnp.mean(acc * acc, axis=-1, keepdims=True)
        gain = g_ref[:, h * hd:(h + 1) * hd]
        o_ref[:, h * hd:(h + 1) * hd] = ((acc * lax.rsqrt(ms + RMS_EPS)) * gain).astype(BF16)


def _sb_attention(proj, proj_meta, g_sb, *, batch, seq, heads, tq, tk, hp):
    hd = SB_HEAD_DIM
    wd = hp * hd
    ng = heads // hp
    kern = functools.partial(_sb_kernel, tq=tq, tk=tk, hp=hp, scale=hd ** -0.5)
    return pl.pallas_call(
        kern,
        grid=(batch, ng, seq // tq),
        in_specs=[
            pl.BlockSpec((None, tq, wd), lambda b, h, i: (b, i, h)),
            pl.BlockSpec((None, seq, wd), lambda b, h, i: (b, 0, ng + h)),
            pl.BlockSpec((None, seq, wd), lambda b, h, i: (b, 0, 2 * ng + h)),
            pl.BlockSpec((META_PAD, wd), lambda b, h, i: (0, ng + h)),
            pl.BlockSpec((META_PAD, wd), lambda b, h, i: (0, 2 * ng + h)),
            pl.BlockSpec((1, wd), lambda b, h, i: (0, h)),
        ],
        out_specs=pl.BlockSpec((None, tq, wd), lambda b, h, i: (b, i, h)),
        out_shape=jax.ShapeDtypeStruct((batch, seq, heads * hd), BF16),
        compiler_params=_cparams(("parallel", "parallel", "arbitrary")),
        name="sb_attn",
    )(proj, proj, proj, proj_meta, proj_meta, g_sb)


def _with_ones(vb):
    return jnp.concatenate([vb, jnp.ones_like(vb)], axis=1)


def _da_block(q1, q2, kb, vb1, st, mask):
    m1, a1, m2, a2 = st

    def one(qc, m, a):
        s = _dot_nt(qc, kb)
        if mask is not None:
            s = jnp.where(mask, s, NEG_BIG)
        mn = jnp.maximum(m, jnp.max(s, axis=-1, keepdims=True))
        p = jnp.exp(s - mn)
        a = jnp.exp(m - mn) * a + jnp.dot(p.astype(BF16), vb1, preferred_element_type=F32)
        return mn, a

    m1, a1 = one(q1, m1, a1)
    m2, a2 = one(q2, m2, a2)
    return m1, a1, m2, a2


def _da_kernel(lq1_ref, lk1_ref, lq2_ref, lk2_ref, q_ref, k_ref, v_ref, km_ref, vm_ref, g_ref, o_ref,
               *, tq, tkv, ts, scale, lam_init):
    i = pl.program_id(2)
    nsub = tq // ts
    lam = (jnp.exp(jnp.sum(lq1_ref[...] * lk1_ref[...], axis=-1, keepdims=True))
           - jnp.exp(jnp.sum(lq2_ref[...] * lk2_ref[...], axis=-1, keepdims=True)) + lam_init)

    q = q_ref[...]
    lane = lax.broadcasted_iota(jnp.int32, q.shape, 1)
    qs = q * jnp.asarray(scale, BF16)
    zero = jnp.zeros_like(qs)
    q1 = jnp.where(lane < DA_HEAD_DIM, qs, zero)
    q2 = jnp.where(lane >= DA_HEAD_DIM, qs, zero)

    neg = jnp.full((tq, 1), NEG_BIG, F32)
    za = jnp.zeros((tq, 2 * DA_V_DIM), F32)
    st = (neg, za, neg, za)

    def body(j, carry):
        for c in range(tq // tkv):
            s0 = pl.multiple_of(j * tq + c * tkv, tkv)
            carry = _da_block(q1, q2, k_ref[pl.ds(s0, tkv), :], _with_ones(v_ref[pl.ds(s0, tkv), :]), carry, None)
        return carry

    st = lax.fori_loop(0, i, body, st)

    shift = CHUNK.bit_length() - 1
    s0 = pl.multiple_of(i * tq, tq)
    outs = []
    for h in range(nsub):
        rows = slice(h * ts, (h + 1) * ts)
        width = (h + 1) * ts
        kb = jnp.concatenate([km_ref[...], k_ref[pl.ds(s0, width), :]], axis=0)
        vb = jnp.concatenate([vm_ref[...], v_ref[pl.ds(s0, width), :]], axis=0)
        col = lax.broadcasted_iota(jnp.int32, (ts, META_PAD + width), 1)
        row = lax.broadcasted_iota(jnp.int32, (ts, META_PAD + width), 0) + h * ts
        frame_vis = jnp.logical_and(col >= META_PAD,
                                    jnp.right_shift(col - META_PAD, shift) <= jnp.right_shift(row, shift))
        vis = jnp.logical_or(col < N_META, frame_vis)
        sth = _da_block(q1[rows], q2[rows], kb, _with_ones(vb), tuple(t[rows] for t in st), vis)
        _, a1, _, a2 = sth
        outs.append(a1[:, :DA_V_DIM] / a1[:, DA_V_DIM:] - lam * (a2[:, :DA_V_DIM] / a2[:, DA_V_DIM:]))
    o = jnp.concatenate(outs, axis=0)
    ms = jnp.mean(o * o, axis=-1, keepdims=True)
    o_ref[...] = (((o * lax.rsqrt(ms + RMS_EPS)) * g_ref[...]) * (1.0 - lam_init)).astype(BF16)


def _da_attention(proj, proj_meta, lams, g_da, *, batch, seq, heads, tq, tkv, ts, col0, lam_init):
    hd = DA_V_DIM
    qb, kb, vb = col0 // hd, col0 // hd + heads, col0 // hd + 2 * heads
    kern = functools.partial(_da_kernel, tq=tq, tkv=tkv, ts=ts, scale=DA_HEAD_DIM ** -0.5, lam_init=lam_init)
    lam_spec = pl.BlockSpec((1, DA_HEAD_DIM), lambda b, h, i: (0, 0))
    return pl.pallas_call(
        kern,
        grid=(batch, heads, seq // tq),
        in_specs=[
            lam_spec, lam_spec, lam_spec, lam_spec,
            pl.BlockSpec((None, tq, hd), lambda b, h, i: (b, i, qb + h)),
            pl.BlockSpec((None, seq, hd), lambda b, h, i: (b, 0, kb + h)),
            pl.BlockSpec((None, seq, hd), lambda b, h, i: (b, 0, vb + h)),
            pl.BlockSpec((META_PAD, hd), lambda b, h, i: (0, kb + h)),
            pl.BlockSpec((META_PAD, hd), lambda b, h, i: (0, vb + h)),
            pl.BlockSpec((1, hd), lambda b, h, i: (0, h)),
        ],
        out_specs=pl.BlockSpec((None, tq, hd), lambda b, h, i: (b, i, h)),
        out_shape=jax.ShapeDtypeStruct((batch, seq, heads * hd), BF16),
        compiler_params=_cparams(("parallel", "parallel", "arbitrary")),
        name="da_attn",
    )(*lams, proj, proj, proj, proj_meta, proj_meta, g_da)


def _pack_bf16_pairs(x):
    n = x.shape[1] // 2
    lo = lax.bitcast_convert_type(x[:, :n].astype(BF16).astype(F32), jnp.int32)
    hi = lax.bitcast_convert_type(x[:, n:].astype(BF16).astype(F32), jnp.int32)
    return jnp.bitwise_or(lax.shift_right_logical(lo, 16), jnp.bitwise_and(hi, jnp.int32(-65536)))


def _unpack_bf16_pairs(w):
    lo = lax.bitcast_convert_type(lax.shift_left(w, 16), F32)
    hi = lax.bitcast_convert_type(jnp.bitwise_and(w, jnp.int32(-65536)), F32)
    return jnp.concatenate([lo.astype(BF16), hi.astype(BF16)], axis=1)


def _sc_gather_rows(table, idx):
    n_idx = idx.shape[0]
    width = table.shape[1]
    n_workers = SC_CORES * SC_SUBCORES
    per_worker = n_idx // n_workers
    win = SC_GATHER_ROWS
    assert n_idx % (n_workers * win) == 0
    mesh = plsc.VectorSubcoreMesh(core_axis_name="c", subcore_axis_name="s")

    n_win = per_worker // win
    assert n_win % 2 == 0
    buf = lambda: pltpu.VMEM((win, width), table.dtype)
    dma = pltpu.SemaphoreType.DMA

    @functools.partial(
        pl.kernel, mesh=mesh,
        out_type=jax.ShapeDtypeStruct((n_idx, width), table.dtype),
        scratch_types=[pltpu.VMEM((n_win, win), jnp.int32), buf(), buf(), dma, dma, dma, dma],
    )
    def gather(table_hbm, idx_hbm, out_hbm, idx_v, rows0, rows1, gsem0, gsem1, wsem0, wsem1):
        worker = lax.axis_index("s") * SC_CORES + lax.axis_index("c")
        base = worker * per_worker
        pltpu.sync_copy(idx_hbm.at[pl.ds(worker * n_win, n_win)], idx_v)

        def gather_copy(t, rows_v, sem):
            return pltpu.make_async_copy(table_hbm.at[idx_v.at[t]], rows_v, sem)

        def writeback(t, rows_v, sem):
            return pltpu.make_async_copy(rows_v, out_hbm.at[pl.ds(pl.multiple_of(base + t * win, win), win)], sem)

        gather_copy(0, rows0, gsem0).start()

        @pl.loop(0, n_win, step=2)
        def _(t):
            gather_copy(t, rows0, gsem0).wait()
            writeback(t, rows0, wsem0).start()

            @pl.when(t > 0)
            def _():
                writeback(t - 1, rows1, wsem1).wait()

            gather_copy(t + 1, rows1, gsem1).start()
            gather_copy(t + 1, rows1, gsem1).wait()
            writeback(t + 1, rows1, wsem1).start()
            writeback(t, rows0, wsem0).wait()

            @pl.when(t + 2 < n_win)
            def _():
                gather_copy(t + 2, rows0, gsem0).start()

        writeback(n_win - 1, rows1, wsem1).wait()

    return gather(table, idx.reshape(n_idx // win, win))


def _sc_scatter_rows(src, dest_km, n_rows):
    n_tok, width = src.shape
    n_slots = dest_km.shape[0]
    n_workers = SC_CORES * SC_SUBCORES
    win = SC_GATHER_ROWS
    per_worker = n_tok // n_workers
    n_win = per_worker // win
    assert n_tok % (n_workers * win) == 0 and n_win % 2 == 0
    mesh = plsc.VectorSubcoreMesh(core_axis_name="c", subcore_axis_name="s")
    buf = lambda: pltpu.VMEM((win, width), src.dtype)
    dma = pltpu.SemaphoreType.DMA

    @functools.partial(
        pl.kernel, mesh=mesh,
        out_type=jax.ShapeDtypeStruct((n_rows, width), src.dtype),
        scratch_types=[pltpu.VMEM((n_slots, n_win, win), jnp.int32), buf(), buf(), dma, dma, dma, dma],
    )
    def scatter(src_hbm, idx_hbm, out_hbm, idx_v, rows0, rows1, lsem0, lsem1, ssem0, ssem1):
        worker = lax.axis_index("s") * SC_CORES + lax.axis_index("c")
        base = worker * per_worker
        for k in range(n_slots):
            pltpu.sync_copy(idx_hbm.at[k, pl.ds(worker * n_win, n_win)], idx_v.at[k])

        def load(t, rows_v, sem):
            return pltpu.make_async_copy(src_hbm.at[pl.ds(pl.multiple_of(base + t * win, win), win)], rows_v, sem)

        def start_scatters(t, rows_v, sem):
            for k in range(n_slots):
                pltpu.make_async_copy(rows_v, out_hbm.at[idx_v.at[k, t]], sem).start()

        def wait_scatters(t, rows_v, sem):
            for k in range(n_slots):
                pltpu.make_async_copy(rows_v, out_hbm.at[idx_v.at[k, t]], sem).wait()

        load(0, rows0, lsem0).start()

        @pl.loop(0, n_win, step=2)
        def _(t):
            load(t, rows0, lsem0).wait()
            start_scatters(t, rows0, ssem0)

            @pl.when(t > 0)
            def _():
                wait_scatters(t - 1, rows1, ssem1)

            load(t + 1, rows1, lsem1).start()
            load(t + 1, rows1, lsem1).wait()
            start_scatters(t + 1, rows1, ssem1)
            wait_scatters(t, rows0, ssem0)

            @pl.when(t + 2 < n_win)
            def _():
                load(t + 2, rows0, lsem0).start()

        wait_scatters(n_win - 1, rows1, ssem1)

    return scatter(src, dest_km.reshape(n_slots, n_tok // win, win))


def _split3_dot(a, b_hi, b_lo):
    a_hi = a.astype(BF16)
    a_lo = (a - a_hi.astype(F32)).astype(BF16)
    return (jnp.dot(a_hi, b_hi, preferred_element_type=F32)
            + jnp.dot(a_hi, b_lo, preferred_element_type=F32)
            + jnp.dot(a_lo, b_hi, preferred_element_type=F32))


def _out_proj_kernel(ms_ref, md_ref, ws_ref, wd_ref, x_ref, g_ref, wr_hi_ref, wr_lo_ref, br_ref,
                     h_ref, u_ref, e_ref, p_ref, c_ref, *, n_exp):
    h = (x_ref[...]
         + jnp.dot(ms_ref[...], ws_ref[...], preferred_element_type=F32)
         + jnp.dot(md_ref[...], wd_ref[...], preferred_element_type=F32))
    h_ref[...] = h
    msq = jnp.mean(h * h, axis=-1, keepdims=True)
    u = (h * lax.rsqrt(msq + RMS_EPS)) * g_ref[...]
    u_ref[...] = _pack_bf16_pairs(u)

    logits = _split3_dot(u, wr_hi_ref[...], wr_lo_ref[...]) + br_ref[...]
    lane = lax.broadcasted_iota(jnp.int32, logits.shape, 1).astype(F32)
    work = logits
    tops, idxs = [], []
    for _ in range(TOP_K):
        mx = jnp.max(work, axis=-1, keepdims=True)
        ix = jnp.min(jnp.where(work == mx, lane, float(n_exp)), axis=-1, keepdims=True)
        tops.append(mx)
        idxs.append(ix)
        work = jnp.where(lane == ix, -jnp.inf, work)
    ex = [jnp.exp(t - tops[0]) for t in tops]
    den = ex[0] + ex[1] + ex[2] + ex[3]
    kl = lax.broadcasted_iota(jnp.int32, (logits.shape[0], TOP_K), 1)
    e_out = jnp.zeros((logits.shape[0], TOP_K), F32)
    p_out = jnp.zeros((logits.shape[0], TOP_K), F32)
    for k in range(TOP_K):
        e_out = jnp.where(kl == k, idxs[k], e_out)
        p_out = jnp.where(kl == k, ex[k] / den, p_out)
    e_ref[...] = e_out.astype(jnp.int32)
    p_ref[...] = p_out
    hits = jnp.zeros(logits.shape, F32)
    for k in range(TOP_K):
        hits = hits + jnp.where(lane == idxs[k], 1.0, 0.0)
    c_ref[...] = jnp.sum(hits, axis=0, keepdims=True)


def _out_proj(mix_sb, mix_da, w_sb, w_da, x2d, g_ffn, wr_hi, wr_lo, b_r, *, tm):
    m, d = x2d.shape
    ks, kd = mix_sb.shape[1], mix_da.shape[1]
    n_exp = wr_hi.shape[1]
    kern = functools.partial(_out_proj_kernel, n_exp=n_exp)
    const = lambda shape: pl.BlockSpec(shape, lambda i: (0, 0))
    return pl.pallas_call(
        kern,
        grid=(m // tm,),
        in_specs=[
            pl.BlockSpec((tm, ks), lambda i: (i, 0)),
            pl.BlockSpec((tm, kd), lambda i: (i, 0)),
            const((ks, d)), const((kd, d)),
            pl.BlockSpec((tm, d), lambda i: (i, 0)),
            const((1, d)), const((d, n_exp)), const((d, n_exp)), const((1, n_exp)),
        ],
        out_specs=[
            pl.BlockSpec((tm, d), lambda i: (i, 0)),
            pl.BlockSpec((tm, d // 2), lambda i: (i, 0)),
            pl.BlockSpec((tm, TOP_K), lambda i: (i, 0)),
            pl.BlockSpec((tm, TOP_K), lambda i: (i, 0)),
            pl.BlockSpec((None, 1, n_exp), lambda i: (i, 0, 0)),
        ],
        out_shape=[
            jax.ShapeDtypeStruct((m, d), F32),
            jax.ShapeDtypeStruct((m, d // 2), jnp.int32),
            jax.ShapeDtypeStruct((m, TOP_K), jnp.int32),
            jax.ShapeDtypeStruct((m, TOP_K), F32),
            jax.ShapeDtypeStruct((m // tm, 1, n_exp), F32),
        ],
        compiler_params=_cparams(("parallel",)),
        name="out_proj_router",
    )(mix_sb, mix_da, w_sb, w_da, x2d, g_ffn, wr_hi, wr_lo, b_r)


def _moe_kernel(be_ref, bc_ref, rb_ref, x_ref, wg_ref, wu_ref, wda_ref, wdb_ref, bg_ref, bup_ref, bda_ref, bdb_ref,
                o_ref, act_ref, *, nf, ts):
    blk = pl.program_id(0)
    s = pl.program_id(1)
    cnt = bc_ref[blk]
    tg = 2 * ts
    tm = x_ref.shape[0]

    def row_groups(run, skip):
        for lo in range(0, tm, tg):
            size = min(tg, tm - lo)

            @pl.when(cnt > lo + size - ts)
            def _():
                run(lo, size)

            if size > ts:
                @pl.when(jnp.logical_and(cnt > lo, cnt <= lo + ts))
                def _():
                    run(lo, ts)
                    skip(lo + ts, ts)

            @pl.when(cnt <= lo)
            def _():
                skip(lo, size)

    @pl.when(jnp.logical_and(cnt > 0, s < nf))
    def _():
        slot = jnp.minimum(s, nf - 1)

        def up_proj(r0, nrows):
            x = _unpack_bf16_pairs(x_ref[r0:r0 + nrows, :])
            g = jnp.dot(x, wg_ref[...].astype(BF16), preferred_element_type=F32) + bg_ref[...]
            u = jnp.dot(x, wu_ref[...].astype(BF16), preferred_element_type=F32) + bup_ref[...]
            gate = jnp.minimum(g, SWIGLU_LIMIT)
            up = jnp.clip(u, -SWIGLU_LIMIT, SWIGLU_LIMIT)
            act = (up + 1.0) * gate * jax.nn.sigmoid(SWIGLU_ALPHA * gate)
            act_ref[slot, r0:r0 + nrows, :] = act.astype(BF16)

        row_groups(up_proj, lambda r0, nrows: None)

    @pl.when(jnp.logical_and(cnt > 0, s >= nf))
    def _():
        tf = act_ref.shape[2]

        def down_proj(r0, nrows):
            ya = bda_ref[...]
            yb = bdb_ref[...]
            for f in range(nf):
                a = act_ref[f, r0:r0 + nrows, :]
                wda = wda_ref[f * tf:(f + 1) * tf, :].astype(BF16)
                wdb = wdb_ref[f * tf:(f + 1) * tf, :].astype(BF16)
                ya = ya + jnp.dot(a, wda, preferred_element_type=F32)
                yb = yb + jnp.dot(a, wdb, preferred_element_type=F32)
            o_ref[r0:r0 + nrows, :] = _pack_bf16_pairs(jnp.concatenate([ya, yb], axis=1))

        def zero_fill(r0, nrows):
            o_ref[r0:r0 + nrows, :] = jnp.zeros((nrows, o_ref.shape[1]), o_ref.dtype)

        row_groups(down_proj, zero_fill)


def _moe_ffn(blk_e, blk_cnt, blk_row, xs, w_gu, w_dn, b_gu, b_dn, *, tm, ts, tf, tn):
    n_rows = xs.shape[0]
    n_exp, d, f2 = w_gu.shape
    d_ff = f2 // 2
    nf = d_ff // tf
    nn = d // tn
    n_blk = n_rows // tm
    b_gu3 = b_gu.reshape(n_exp, 1, f2)
    b_dn3 = b_dn.reshape(n_exp, 1, d)

    def fi(s, bc, b):
        return jnp.where(bc[b] > 0, jnp.minimum(s, nf - 1), nf - 1)

    nh = nn // 2

    def ni(s, bc, b):
        return jnp.where(bc[b] > 0, jnp.clip(s - nf, 0, nh - 1), nh - 1)

    grid_spec = pltpu.PrefetchScalarGridSpec(
        num_scalar_prefetch=3,
        grid=(n_blk, nf + nh),
        in_specs=[
            pl.BlockSpec((tm, d // 2), lambda b, s, be, bc, rb: (rb[b], 0)),
            pl.BlockSpec((None, d, tf), lambda b, s, be, bc, rb: (be[b], 0, fi(s, bc, b))),
            pl.BlockSpec((None, d, tf), lambda b, s, be, bc, rb: (be[b], 0, nf + fi(s, bc, b))),
            pl.BlockSpec((None, d_ff, tn), lambda b, s, be, bc, rb: (be[b], 0, ni(s, bc, b))),
            pl.BlockSpec((None, d_ff, tn), lambda b, s, be, bc, rb: (be[b], 0, nh + ni(s, bc, b))),
            pl.BlockSpec((None, 1, tf), lambda b, s, be, bc, rb: (be[b], 0, fi(s, bc, b))),
            pl.BlockSpec((None, 1, tf), lambda b, s, be, bc, rb: (be[b], 0, nf + fi(s, bc, b))),
            pl.BlockSpec((None, 1, tn), lambda b, s, be, bc, rb: (be[b], 0, ni(s, bc, b))),
            pl.BlockSpec((None, 1, tn), lambda b, s, be, bc, rb: (be[b], 0, nh + ni(s, bc, b))),
        ],
        out_specs=pl.BlockSpec((tm, tn), lambda b, s, be, bc, rb: (rb[b], ni(s, bc, b))),
        scratch_shapes=[pltpu.VMEM((nf, tm, tf), BF16)],
    )
    return pl.pallas_call(
        functools.partial(_moe_kernel, nf=nf, ts=ts),
        grid_spec=grid_spec,
        out_shape=jax.ShapeDtypeStruct((n_rows, d // 2), jnp.int32),
        compiler_params=_cparams(("arbitrary", "arbitrary")),
        name="moe_ffn",
    )(blk_e, blk_cnt, blk_row, xs, w_gu, w_gu, w_dn, w_dn, b_gu3, b_gu3, b_dn3, b_dn3)


def _final_kernel(h_ref, y0_ref, y1_ref, y2_ref, y3_ref, p_ref, g_ref, o_ref):
    p = p_ref[...]
    y = None
    for k, y_ref in enumerate((y0_ref, y1_ref, y2_ref, y3_ref)):
        t = _unpack_bf16_pairs(y_ref[...]).astype(F32) * p[:, k:k + 1]
        y = t if y is None else y + t
    h = h_ref[...] + y
    ms = jnp.mean(h * h, axis=-1, keepdims=True)
    o_ref[...] = (h * lax.rsqrt(ms + RMS_EPS)) * g_ref[...]


def _final(h1, y_km, gates, g, *, tm):
    m, d = h1.shape
    nb = m // tm
    assert TOP_K == 4
    y_specs = [pl.BlockSpec((tm, d // 2), functools.partial(lambda i, k: (k * nb + i, 0), k=k))
               for k in range(TOP_K)]
    return pl.pallas_call(
        _final_kernel,
        grid=(nb,),
        in_specs=([pl.BlockSpec((tm, d), lambda i: (i, 0))] + y_specs
                  + [pl.BlockSpec((tm, TOP_K), lambda i: (i, 0)), pl.BlockSpec((1, d), lambda i: (0, 0))]),
        out_specs=pl.BlockSpec((tm, d), lambda i: (i, 0)),
        out_shape=jax.ShapeDtypeStruct((m, d), F32),
        compiler_params=_cparams(("parallel",)),
        name="final_norm",
    )(h1, y_km, y_km, y_km, y_km, gates, g)


def _moe_blocks(n_assign, n_exp, tm):
    n_blk = -(-n_assign // tm) + n_exp
    unit = SC_CORES * SC_SUBCORES * SC_GATHER_ROWS
    while (n_blk * tm) % unit:
        n_blk += 1
    return n_blk


def _routing(top_e, counts, tm):
    n_exp = counts.shape[0]
    n_tok = top_e.shape[0]
    n_assign = n_tok * TOP_K
    flat_e = top_e.reshape(-1)
    padded = (counts + tm - 1) // tm * tm
    pad_end = jnp.cumsum(padded).astype(jnp.int32)
    pad_start = pad_end - padded
    grp_start = jnp.cumsum(counts).astype(jnp.int32) - counts
    n_blk = _moe_blocks(n_assign, n_exp, tm)
    blk_start = jnp.arange(n_blk, dtype=jnp.int32) * tm
    blk_e = jnp.minimum(jnp.sum((blk_start[:, None] >= pad_end[None, :]).astype(jnp.int32), axis=1), n_exp - 1)
    blk_off = blk_start - pad_start[blk_e]
    blk_cnt = jnp.clip(counts[blk_e] - blk_off, 0, tm).astype(jnp.int32)
    order = jnp.argsort(flat_e, stable=True).astype(jnp.int32)
    rank = jnp.argsort(order).astype(jnp.int32)
    delta = pad_start - grp_start
    e2d = flat_e.reshape(-1, LANES)
    hit = e2d[None] == jnp.arange(n_exp, dtype=jnp.int32)[:, None, None]
    dest_flat = rank + jnp.sum(jnp.where(hit, delta[:, None, None], 0), axis=0).reshape(-1)
    last_used = jnp.maximum(pad_end[-1] // tm - 1, 0)
    blk_e = jnp.where(blk_cnt > 0, blk_e, blk_e[last_used]).astype(jnp.int32)
    blk_row = jnp.where(blk_cnt > 0, jnp.arange(n_blk, dtype=jnp.int32), last_used).astype(jnp.int32)
    return dest_flat, blk_e, blk_cnt, blk_row, n_blk * tm


def kernel(x, meta_tokens, g_mix, w_in, lam_q1, lam_k1, lam_q2, lam_k2, g_sb_out, g_da_out, w_out, g_ffn,
           w_router, b_router, w_gate_up, b_gate_up, w_down, b_down, g_final):
    b, s, d = x.shape
    depth = w_in.shape[0]
    assert depth == 1, "single-layer trunk"
    layer = 0
    sb_heads = (d // 2) // SB_HEAD_DIM
    da_heads = (d // 2) // DA_V_DIM
    sb_w = sb_heads * SB_HEAD_DIM
    da_col0 = 3 * sb_w
    da_qk_w = da_heads * 2 * DA_HEAD_DIM
    n_exp = w_router.shape[-1]
    lam_init = 0.8 - 0.6 * math.exp(-0.3 * layer)

    tm_proj, tn_proj = TM_PROJ, TN_PROJ
    tq, tk_sb, ts_da = TQ_ATTN, TK_SB, TS_DA
    tm_out = TM_OUT
    tm_moe, tf_moe = TM_MOE, TF_MOE
    tm_fin = TM_FINAL

    x2d = x.reshape(b * s, d)
    w_in_bf = w_in.reshape(w_in.shape[1:]).astype(BF16)
    g_mix2 = g_mix[layer].reshape(1, d)
    rope_cols = (da_col0, da_col0 + 2 * da_qk_w)

    pos_f = N_META + jnp.arange(s, dtype=jnp.int32)
    proj = _in_proj(x2d, g_mix2, w_in_bf, _rope_tables(pos_f), tm=tm_proj, tn=tn_proj,
                    rope_cols=rope_cols, pos_blocks=s // tm_proj)
    meta_pad = jnp.zeros((META_PAD, d), x.dtype).at[:N_META].set(meta_tokens.astype(x.dtype))
    pos_m = jnp.arange(META_PAD, dtype=jnp.int32)
    proj_meta = _in_proj(meta_pad, g_mix2, w_in_bf, _rope_tables(pos_m), tm=META_PAD, tn=tn_proj,
                         rope_cols=rope_cols, pos_blocks=1)
    proj3 = proj.reshape(b, s, -1)

    mix_sb = _sb_attention(proj3, proj_meta, g_sb_out[layer].reshape(1, -1), batch=b, seq=s, heads=sb_heads,
                           tq=tq, tk=tk_sb, hp=HEADS_PER_STEP_SB)
    lams = tuple(t[layer].reshape(1, DA_HEAD_DIM).astype(F32) for t in (lam_q1, lam_k1, lam_q2, lam_k2))
    mix_da = _da_attention(proj3, proj_meta, lams, g_da_out[layer].reshape(1, -1), batch=b, seq=s, heads=da_heads,
                           tq=TQ_DA, tkv=TKV_DA, ts=ts_da, col0=da_col0, lam_init=lam_init)

    w_out_bf = w_out[layer].astype(BF16)
    wr = w_router[layer]
    wr_hi = wr.astype(BF16)
    wr_lo = (wr - wr_hi.astype(F32)).astype(BF16)
    h1, u_packed, top_e, gates, tile_counts = _out_proj(
        mix_sb.reshape(b * s, -1), mix_da.reshape(b * s, -1), w_out_bf[:sb_w], w_out_bf[sb_w:], x2d,
        g_ffn[layer].reshape(1, d), wr_hi, wr_lo, b_router[layer].reshape(1, n_exp), tm=tm_out)

    counts = jnp.sum(tile_counts, axis=(0, 1)).astype(jnp.int32)
    dest_flat, blk_e, blk_cnt, blk_row, n_rows = _routing(top_e, counts, tm_moe)
    dest_km = dest_flat.reshape(b * s, TOP_K).T
    xs = _sc_scatter_rows(u_packed, dest_km, n_rows)
    rows = _moe_ffn(blk_e, blk_cnt, blk_row, xs, w_gate_up.reshape(w_gate_up.shape[1:]),
                    w_down.reshape(w_down.shape[1:]), b_gate_up[layer], b_down[layer],
                    tm=tm_moe, ts=TS_MOE, tf=tf_moe, tn=TN_MOE)
    y_km = _sc_gather_rows(rows, dest_km.reshape(-1))

    out = _final(h1, y_km, gates, g_final.reshape(1, d), tm=tm_fin)
    return out.reshape(b, s, d)
```

```python
import functools
import math

import jax
import jax.numpy as jnp
from jax import lax
from jax.experimental import pallas as pl
from jax.experimental.pallas import tpu as pltpu
from jax.experimental.pallas import tpu_sc as plsc

F32 = jnp.float32
BF16 = jnp.bfloat16

CHUNK = 64
N_META = 16
RMS_EPS = 1e-5
SB_HEAD_DIM = 128
DA_HEAD_DIM = 64
DA_V_DIM = 2 * DA_HEAD_DIM
ROPE_THETA = 500000.0
ROPE_DIM = DA_HEAD_DIM // 4
TOP_K = 4
SWIGLU_LIMIT = 7.0
SWIGLU_ALPHA = 1.702

LANES = 128
META_PAD = 128
NEG_BIG = -1e30
SB_DEAD_RUN = 110.0
VMEM_LIMIT = 56 * 1024 * 1024
SC_CORES, SC_SUBCORES = 2, 16
SC_GATHER_ROWS = 32

TM_PROJ, TN_PROJ = 1024, 1024
TQ_ATTN = 512
HEADS_PER_STEP_SB = 4
TQ_DA, TKV_DA = 1024, 1024
TK_SB = 256
TS_DA = 512
TM_OUT = 512
TM_MOE, TS_MOE = 2176, 544
TF_MOE, TN_MOE = 256, 256
TM_FINAL = 512


def _cparams(sem):
    return pltpu.CompilerParams(dimension_semantics=sem, vmem_limit_bytes=VMEM_LIMIT)


def _in_proj_kernel(x_ref, g_ref, w_ref, c_ref, sa_ref, sb_ref, o_ref, u_scr, *, tn, rope_lo, rope_hi):
    n = pl.program_id(1)

    @pl.when(n == 0)
    def _():
        x = x_ref[...]
        ms = jnp.mean(x * x, axis=-1, keepdims=True)
        u_scr[...] = ((x * lax.rsqrt(ms + RMS_EPS)) * g_ref[...]).astype(BF16)

    acc = jnp.dot(u_scr[...], w_ref[...], preferred_element_type=F32)
    is_rope = jnp.logical_and(n >= rope_lo, n < rope_hi)
    o_ref[...] = acc.astype(BF16)

    @pl.when(is_rope)
    def _():
        for c in range(tn // LANES):
            xc = acc[:, c * LANES:(c + 1) * LANES]
            r = (xc * c_ref[...] + pltpu.roll(xc, LANES - ROPE_DIM // 2, 1) * sa_ref[...]
                 + pltpu.roll(xc, ROPE_DIM // 2, 1) * sb_ref[...])
            o_ref[:, c * LANES:(c + 1) * LANES] = r.astype(BF16)


def _rope_tables(pos):
    half = ROPE_DIM // 2
    inv_freq = ROPE_THETA ** (-(jnp.arange(half, dtype=F32) * 2.0 / ROPE_DIM))
    ang = pos.astype(F32)[:, None] * inv_freq[None, :]
    cos, sin = jnp.cos(ang), jnp.sin(ang)
    p = pos.shape[0]
    ones = jnp.ones((p, DA_HEAD_DIM - ROPE_DIM), F32)
    zeros8 = jnp.zeros((p, half), F32)
    zeros48 = jnp.zeros((p, DA_HEAD_DIM - ROPE_DIM), F32)
    c64 = jnp.concatenate([cos, cos, ones], axis=1)
    sa64 = jnp.concatenate([-sin, zeros8, zeros48], axis=1)
    sb64 = jnp.concatenate([zeros8, sin, zeros48], axis=1)
    tile2 = lambda t: jnp.concatenate([t, t], axis=1)
    return tile2(c64), tile2(sa64), tile2(sb64)


def _in_proj(x2d, g, w_bf, tables, *, tm, tn, rope_cols, pos_blocks):
    m, d = x2d.shape
    n_cols = w_bf.shape[1]
    c_t, sa_t, sb_t = tables
    kern = functools.partial(_in_proj_kernel, tn=tn, rope_lo=rope_cols[0] // tn, rope_hi=rope_cols[1] // tn)
    tab_spec = pl.BlockSpec((tm, LANES), lambda i, n: (i % pos_blocks, 0))
    return pl.pallas_call(
        kern,
        grid=(m // tm, n_cols // tn),
        in_specs=[
            pl.BlockSpec((tm, d), lambda i, n: (i, 0)),
            pl.BlockSpec((1, d), lambda i, n: (0, 0)),
            pl.BlockSpec((d, tn), lambda i, n: (0, n)),
            tab_spec, tab_spec, tab_spec,
        ],
        out_specs=pl.BlockSpec((tm, tn), lambda i, n: (i, n)),
        out_shape=jax.ShapeDtypeStruct((m, n_cols), BF16),
        scratch_shapes=[pltpu.VMEM((tm, d), BF16)],
        compiler_params=_cparams(("parallel", "arbitrary")),
        name="in_proj",
    )(x2d, g, w_bf, c_t, sa_t, sb_t)


def _dot_nt(a, b):
    return lax.dot_general(a, b, (((1,), (1,)), ((), ())), preferred_element_type=F32)


def _suffix_sum_matrix(n):
    j = lax.broadcasted_iota(jnp.int32, (2 * n, n), 0)
    s = lax.broadcasted_iota(jnp.int32, (2 * n, n), 1)
    return jnp.where(jnp.where(j >= n, j - n, j) > s, 1.0, 0.0).astype(BF16)


def _sb_block(q, kb, vb, u2, run, acc, mask, scale):
    z = _dot_nt(q, kb) * scale
    sp = jnp.maximum(z, 0.0) + jnp.log(1.0 + jnp.exp(-jnp.abs(z)))
    if mask is not None:
        sp = jnp.where(mask, sp, 0.0)
    hi = sp.astype(BF16)
    lo = (sp - hi.astype(F32)).astype(BF16)
    cs = jnp.dot(jnp.concatenate([hi, lo], axis=1), u2, preferred_element_type=F32)
    w = jnp.exp(z - sp - cs - run)
    if mask is not None:
        w = jnp.where(mask, w, 0.0)
    acc = acc + jnp.dot(w.astype(BF16), vb, preferred_element_type=F32)
    run = run + (cs[:, :1] + sp[:, :1])
    return run, acc


def _sb_kernel(q_ref, k_ref, v_ref, km_ref, vm_ref, g_ref, o_ref, *, tq, tk, hp, scale):
    i = pl.program_id(2)
    nsub = tq // tk
    hd = SB_HEAD_DIM
    u_blk = _suffix_sum_matrix(tk)
    row = lax.broadcasted_iota(jnp.int32, (tk, tk), 0)
    col = lax.broadcasted_iota(jnp.int32, (tk, tk), 1)
    diag_mask = col < row
    heads = range(hp)

    def kv(blk, h):
        s0 = pl.multiple_of(blk * tk, tk)
        return k_ref[pl.ds(s0, tk), h * hd:(h + 1) * hd], v_ref[pl.ds(s0, tk), h * hd:(h + 1) * hd]

    runs, accs = [], []
    for h in heads:
        slab_runs, slab_accs = [], []
        for sl in range(nsub):
            qs = q_ref[sl * tk:(sl + 1) * tk, h * hd:(h + 1) * hd]
            run = jnp.zeros((tk, 1), F32)
            acc = jnp.zeros((tk, hd), F32)
            for c in range(sl, -1, -1):
                kb, vb = kv(i * nsub + c, h)
                run, acc = _sb_block(qs, kb, vb, u_blk, run, acc, diag_mask if c == sl else None, scale)
            slab_runs.append(run)
            slab_accs.append(acc)
        runs.append(jnp.concatenate(slab_runs, axis=0))
        accs.append(jnp.concatenate(slab_accs, axis=0))
    q = [q_ref[:, h * hd:(h + 1) * hd] for h in heads]

    def alive_flag(rs):
        lowest = functools.reduce(jnp.minimum, [jnp.min(r) for r in rs])
        return (lowest < SB_DEAD_RUN).astype(jnp.int32)

    def cond(carry):
        jb, alive, _, _ = carry
        return jnp.logical_and(jb >= 0, alive > 0)

    def body(carry):
        jb, _, rs, as_ = carry
        out = [_sb_block(q[h], *kv(jb, h), u_blk, rs[h], as_[h], None, scale) for h in heads]
        rs = tuple(o[0] for o in out)
        return jb - 1, alive_flag(rs), rs, tuple(o[1] for o in out)

    runs, accs = tuple(runs), tuple(accs)
    _, alive, runs, accs = lax.while_loop(cond, body, (i * nsub - 1, alive_flag(runs), runs, accs))

    def meta_block():
        mcol = lax.broadcasted_iota(jnp.int32, (tq, META_PAD), 1)
        u_meta = _suffix_sum_matrix(META_PAD)
        return tuple(_sb_block(q[h], km_ref[:, h * hd:(h + 1) * hd], vm_ref[:, h * hd:(h + 1) * hd], u_meta,
                               runs[h], accs[h], mcol < N_META, scale)[1] for h in heads)

    accs = lax.cond(alive > 0, meta_block, lambda: accs)

    for h in heads:
        acc = accs[h]
        ms = jnp.mean(acc * acc, axis=-1, keepdims=True)
        gain = g_ref[:, h * hd:(h + 1) * hd]
        o_ref[:, h * hd:(h + 1) * hd] = ((acc * lax.rsqrt(ms + RMS_EPS)) * gain).astype(BF16)


def _sb_attention(proj, proj_meta, g_sb, *, batch, seq, heads, tq, tk, hp):
    hd = SB_HEAD_DIM
    wd = hp * hd
    ng = heads // hp
    kern = functools.partial(_sb_kernel, tq=tq, tk=tk, hp=hp, scale=hd ** -0.5)
    return pl.pallas_call(
        kern,
        grid=(batch, ng, seq // tq),
        in_specs=[
            pl.BlockSpec((None, tq, wd), lambda b, h, i: (b, i, h)),
            pl.BlockSpec((None, seq, wd), lambda b, h, i: (b, 0, ng + h)),
            pl.BlockSpec((None, seq, wd), lambda b, h, i: (b, 0, 2 * ng + h)),
            pl.BlockSpec((META_PAD, wd), lambda b, h, i: (0, ng + h)),
            pl.BlockSpec((META_PAD, wd), lambda b, h, i: (0, 2 * ng + h)),
            pl.BlockSpec((1, wd), lambda b, h, i: (0, h)),
        ],
        out_specs=pl.BlockSpec((None, tq, wd), lambda b, h, i: (b, i, h)),
        out_shape=jax.ShapeDtypeStruct((batch, seq, heads * hd), BF16),
        compiler_params=_cparams(("parallel", "parallel", "arbitrary")),
        name="sb_attn",
    )(proj, proj, proj, proj_meta, proj_meta, g_sb)


def _with_ones(vb):
    return jnp.concatenate([vb, jnp.ones_like(vb)], axis=1)


def _da_block(q1, q2, kb, vb1, st, mask):
    m1, a1, m2, a2 = st

    def one(qc, m, a):
        s = _dot_nt(qc, kb)
        if mask is not None:
            s = jnp.where(mask, s, NEG_BIG)
        mn = jnp.maximum(m, jnp.max(s, axis=-1, keepdims=True))
        p = jnp.exp(s - mn)
        a = jnp.exp(m - mn) * a + jnp.dot(p.astype(BF16), vb1, preferred_element_type=F32)
        return mn, a

    m1, a1 = one(q1, m1, a1)
    m2, a2 = one(q2, m2, a2)
    return m1, a1, m2, a2


def _da_kernel(lq1_ref, lk1_ref, lq2_ref, lk2_ref, q_ref, k_ref, v_ref, km_ref, vm_ref, g_ref, o_ref,
               *, tq, tkv, ts, scale, lam_init):
    i = pl.program_id(2)
    nsub = tq // ts
    lam = (jnp.exp(jnp.sum(lq1_ref[...] * lk1_ref[...], axis=-1, keepdims=True))
           - jnp.exp(jnp.sum(lq2_ref[...] * lk2_ref[...], axis=-1, keepdims=True)) + lam_init)

    q = q_ref[...]
    lane = lax.broadcasted_iota(jnp.int32, q.shape, 1)
    qs = q * jnp.asarray(scale, BF16)
    zero = jnp.zeros_like(qs)
    q1 = jnp.where(lane < DA_HEAD_DIM, qs, zero)
    q2 = jnp.where(lane >= DA_HEAD_DIM, qs, zero)

    neg = jnp.full((tq, 1), NEG_BIG, F32)
    za = jnp.zeros((tq, 2 * DA_V_DIM), F32)
    st = (neg, za, neg, za)

    def body(j, carry):
        for c in range(tq // tkv):
            s0 = pl.multiple_of(j * tq + c * tkv, tkv)
            carry = _da_block(q1, q2, k_ref[pl.ds(s0, tkv), :], _with_ones(v_ref[pl.ds(s0, tkv), :]), carry, None)
        return carry

    st = lax.fori_loop(0, i, body, st)

    shift = CHUNK.bit_length() - 1
    s0 = pl.multiple_of(i * tq, tq)
    outs = []
    for h in range(nsub):
        rows = slice(h * ts, (h + 1) * ts)
        width = (h + 1) * ts
        kb = jnp.concatenate([km_ref[...], k_ref[pl.ds(s0, width), :]], axis=0)
        vb = jnp.concatenate([vm_ref[...], v_ref[pl.ds(s0, width), :]], axis=0)
        col = lax.broadcasted_iota(jnp.int32, (ts, META_PAD + width), 1)
        row = lax.broadcasted_iota(jnp.int32, (ts, META_PAD + width), 0) + h * ts
        frame_vis = jnp.logical_and(col >= META_PAD,
                                    jnp.right_shift(col - META_PAD, shift) <= jnp.right_shift(row, shift))
        vis = jnp.logical_or(col < N_META, frame_vis)
        sth = _da_block(q1[rows], q2[rows], kb, _with_ones(vb), tuple(t[rows] for t in st), vis)
        _, a1, _, a2 = sth
        outs.append(a1[:, :DA_V_DIM] / a1[:, DA_V_DIM:] - lam * (a2[:, :DA_V_DIM] / a2[:, DA_V_DIM:]))
    o = jnp.concatenate(outs, axis=0)
    ms = jnp.mean(o * o, axis=-1, keepdims=True)
    o_ref[...] = (((o * lax.rsqrt(ms + RMS_EPS)) * g_ref[...]) * (1.0 - lam_init)).astype(BF16)


def _da_attention(proj, proj_meta, lams, g_da, *, batch, seq, heads, tq, tkv, ts, col0, lam_init):
    hd = DA_V_DIM
    qb, kb, vb = col0 // hd, col0 // hd + heads, col0 // hd + 2 * heads
    kern = functools.partial(_da_kernel, tq=tq, tkv=tkv, ts=ts, scale=DA_HEAD_DIM ** -0.5, lam_init=lam_init)
    lam_spec = pl.BlockSpec((1, DA_HEAD_DIM), lambda b, h, i: (0, 0))
    return pl.pallas_call(
        kern,
        grid=(batch, heads, seq // tq),
        in_specs=[
            lam_spec, lam_spec, lam_spec, lam_spec,
            pl.BlockSpec((None, tq, hd), lambda b, h, i: (b, i, qb + h)),
            pl.BlockSpec((None, seq, hd), lambda b, h, i: (b, 0, kb + h)),
            pl.BlockSpec((None, seq, hd), lambda b, h, i: (b, 0, vb + h)),
            pl.BlockSpec((META_PAD, hd), lambda b, h, i: (0, kb + h)),
            pl.BlockSpec((META_PAD, hd), lambda b, h, i: (0, vb + h)),
            pl.BlockSpec((1, hd), lambda b, h, i: (0, h)),
        ],
        out_specs=pl.BlockSpec((None, tq, hd), lambda b, h, i: (b, i, h)),
        out_shape=jax.ShapeDtypeStruct((batch, seq, heads * hd), BF16),
        compiler_params=_cparams(("parallel", "parallel", "arbitrary")),
        name="da_attn",
    )(*lams, proj, proj, proj, proj_meta, proj_meta, g_da)


def _pack_bf16_pairs(x):
    n = x.shape[1] // 2
    lo = lax.bitcast_convert_type(x[:, :n].astype(BF16).astype(F32), jnp.int32)
    hi = lax.bitcast_convert_type(x[:, n:].astype(BF16).astype(F32), jnp.int32)
    return jnp.bitwise_or(lax.shift_right_logical(lo, 16), jnp.bitwise_and(hi, jnp.int32(-65536)))


def _unpack_bf16_pairs(w):
    lo = lax.bitcast_convert_type(lax.shift_left(w, 16), F32)
    hi = lax.bitcast_convert_type(jnp.bitwise_and(w, jnp.int32(-65536)), F32)
    return jnp.concatenate([lo.astype(BF16), hi.astype(BF16)], axis=1)


def _sc_gather_rows(table, idx):
    n_idx = idx.shape[0]
    width = table.shape[1]
    n_workers = SC_CORES * SC_SUBCORES
    per_worker = n_idx // n_workers
    win = SC_GATHER_ROWS
    assert n_idx % (n_workers * win) == 0
    mesh = plsc.VectorSubcoreMesh(core_axis_name="c", subcore_axis_name="s")

    n_win = per_worker // win
    assert n_win % 2 == 0
    buf = lambda: pltpu.VMEM((win, width), table.dtype)
    dma = pltpu.SemaphoreType.DMA

    @functools.partial(
        pl.kernel, mesh=mesh,
        out_type=jax.ShapeDtypeStruct((n_idx, width), table.dtype),
        scratch_types=[pltpu.VMEM((n_win, win), jnp.int32), buf(), buf(), dma, dma, dma, dma],
    )
    def gather(table_hbm, idx_hbm, out_hbm, idx_v, rows0, rows1, gsem0, gsem1, wsem0, wsem1):
        worker = lax.axis_index("s") * SC_CORES + lax.axis_index("c")
        base = worker * per_worker
        pltpu.sync_copy(idx_hbm.at[pl.ds(worker * n_win, n_win)], idx_v)

        def gather_copy(t, rows_v, sem):
            return pltpu.make_async_copy(table_hbm.at[idx_v.at[t]], rows_v, sem)

        def writeback(t, rows_v, sem):
            return pltpu.make_async_copy(rows_v, out_hbm.at[pl.ds(pl.multiple_of(base + t * win, win), win)], sem)

        gather_copy(0, rows0, gsem0).start()

        @pl.loop(0, n_win, step=2)
        def _(t):
            gather_copy(t, rows0, gsem0).wait()
            writeback(t, rows0, wsem0).start()

            @pl.when(t > 0)
            def _():
                writeback(t - 1, rows1, wsem1).wait()

            gather_copy(t + 1, rows1, gsem1).start()
            gather_copy(t + 1, rows1, gsem1).wait()
            writeback(t + 1, rows1, wsem1).start()
            writeback(t, rows0, wsem0).wait()

            @pl.when(t + 2 < n_win)
            def _():
                gather_copy(t + 2, rows0, gsem0).start()

        writeback(n_win - 1, rows1, wsem1).wait()

    return gather(table, idx.reshape(n_idx // win, win))


def _sc_scatter_rows(src, dest_km, n_rows):
    n_tok, width = src.shape
    n_slots = dest_km.shape[0]
    n_workers = SC_CORES * SC_SUBCORES
    win = SC_GATHER_ROWS
    per_worker = n_tok // n_workers
    n_win = per_worker // win
    assert n_tok % (n_workers * win) == 0 and n_win % 2 == 0
    mesh = plsc.VectorSubcoreMesh(core_axis_name="c", subcore_axis_name="s")
    buf = lambda: pltpu.VMEM((win, width), src.dtype)
    dma = pltpu.SemaphoreType.DMA

    @functools.partial(
        pl.kernel, mesh=mesh,
        out_type=jax.ShapeDtypeStruct((n_rows, width), src.dtype),
        scratch_types=[pltpu.VMEM((n_slots, n_win, win), jnp.int32), buf(), buf(), dma, dma, dma, dma],
    )
    def scatter(src_hbm, idx_hbm, out_hbm, idx_v, rows0, rows1, lsem0, lsem1, ssem0, ssem1):
        worker = lax.axis_index("s") * SC_CORES + lax.axis_index("c")
        base = worker * per_worker
        for k in range(n_slots):
            pltpu.sync_copy(idx_hbm.at[k, pl.ds(worker * n_win, n_win)], idx_v.at[k])

        def load(t, rows_v, sem):
            return pltpu.make_async_copy(src_hbm.at[pl.ds(pl.multiple_of(base + t * win, win), win)], rows_v, sem)

        def start_scatters(t, rows_v, sem):
            for k in range(n_slots):
                pltpu.make_async_copy(rows_v, out_hbm.at[idx_v.at[k, t]], sem).start()

        def wait_scatters(t, rows_v, sem):
            for k in range(n_slots):
                pltpu.make_async_copy(rows_v, out_hbm.at[idx_v.at[k, t]], sem).wait()

        load(0, rows0, lsem0).start()

        @pl.loop(0, n_win, step=2)
        def _(t):
            load(t, rows0, lsem0).wait()
            start_scatters(t, rows0, ssem0)

            @pl.when(t > 0)
            def _():
                wait_scatters(t - 1, rows1, ssem1)

            load(t + 1, rows1, lsem1).start()
            load(t + 1, rows1, lsem1).wait()
            start_scatters(t + 1, rows1, ssem1)
            wait_scatters(t, rows0, ssem0)

            @pl.when(t + 2 < n_win)
            def _():
                load(t + 2, rows0, lsem0).start()

        wait_scatters(n_win - 1, rows1, ssem1)

    return scatter(src, dest_km.reshape(n_slots, n_tok // win, win))


def _split3_dot(a, b_hi, b_lo):
    a_hi = a.astype(BF16)
    a_lo = (a - a_hi.astype(F32)).astype(BF16)
    return (jnp.dot(a_hi, b_hi, preferred_element_type=F32)
            + jnp.dot(a_hi, b_lo, preferred_element_type=F32)
            + jnp.dot(a_lo, b_hi, preferred_element_type=F32))


def _out_proj_kernel(ms_ref, md_ref, ws_ref, wd_ref, x_ref, g_ref, wr_hi_ref, wr_lo_ref, br_ref,
                     h_ref, u_ref, e_ref, p_ref, c_ref, *, n_exp):
    h = (x_ref[...]
         + jnp.dot(ms_ref[...], ws_ref[...], preferred_element_type=F32)
         + jnp.dot(md_ref[...], wd_ref[...], preferred_element_type=F32))
    h_ref[...] = h
    msq = jnp.mean(h * h, axis=-1, keepdims=True)
    u = (h * lax.rsqrt(msq + RMS_EPS)) * g_ref[...]
    u_ref[...] = _pack_bf16_pairs(u)

    logits = _split3_dot(u, wr_hi_ref[...], wr_lo_ref[...]) + br_ref[...]
    lane = lax.broadcasted_iota(jnp.int32, logits.shape, 1).astype(F32)
    work = logits
    tops, idxs = [], []
    for _ in range(TOP_K):
        mx = jnp.max(work, axis=-1, keepdims=True)
        ix = jnp.min(jnp.where(work == mx, lane, float(n_exp)), axis=-1, keepdims=True)
        tops.append(mx)
        idxs.append(ix)
        work = jnp.where(lane == ix, -jnp.inf, work)
    ex = [jnp.exp(t - tops[0]) for t in tops]
    den = ex[0] + ex[1] + ex[2] + ex[3]
    kl = lax.broadcasted_iota(jnp.int32, (logits.shape[0], TOP_K), 1)
    e_out = jnp.zeros((logits.shape[0], TOP_K), F32)
    p_out = jnp.zeros((logits.shape[0], TOP_K), F32)
    for k in range(TOP_K):
        e_out = jnp.where(kl == k, idxs[k], e_out)
        p_out = jnp.where(kl == k, ex[k] / den, p_out)
    e_ref[...] = e_out.astype(jnp.int32)
    p_ref[...] = p_out
    hits = jnp.zeros(logits.shape, F32)
    for k in range(TOP_K):
        hits = hits + jnp.where(lane == idxs[k], 1.0, 0.0)
    c_ref[...] = jnp.sum(hits, axis=0, keepdims=True)


def _out_proj(mix_sb, mix_da, w_sb, w_da, x2d, g_ffn, wr_hi, wr_lo, b_r, *, tm):
    m, d = x2d.shape
    ks, kd = mix_sb.shape[1], mix_da.shape[1]
    n_exp = wr_hi.shape[1]
    kern = functools.partial(_out_proj_kernel, n_exp=n_exp)
    const = lambda shape: pl.BlockSpec(shape, lambda i: (0, 0))
    return pl.pallas_call(
        kern,
        grid=(m // tm,),
        in_specs=[
            pl.BlockSpec((tm, ks), lambda i: (i, 0)),
            pl.BlockSpec((tm, kd), lambda i: (i, 0)),
            const((ks, d)), const((kd, d)),
            pl.BlockSpec((tm, d), lambda i: (i, 0)),
            const((1, d)), const((d, n_exp)), const((d, n_exp)), const((1, n_exp)),
        ],
        out_specs=[
            pl.BlockSpec((tm, d), lambda i: (i, 0)),
            pl.BlockSpec((tm, d // 2), lambda i: (i, 0)),
            pl.BlockSpec((tm, TOP_K), lambda i: (i, 0)),
            pl.BlockSpec((tm, TOP_K), lambda i: (i, 0)),
            pl.BlockSpec((None, 1, n_exp), lambda i: (i, 0, 0)),
        ],
        out_shape=[
            jax.ShapeDtypeStruct((m, d), F32),
            jax.ShapeDtypeStruct((m, d // 2), jnp.int32),
            jax.ShapeDtypeStruct((m, TOP_K), jnp.int32),
            jax.ShapeDtypeStruct((m, TOP_K), F32),
            jax.ShapeDtypeStruct((m // tm, 1, n_exp), F32),
        ],
        compiler_params=_cparams(("parallel",)),
        name="out_proj_router",
    )(mix_sb, mix_da, w_sb, w_da, x2d, g_ffn, wr_hi, wr_lo, b_r)


def _moe_kernel(be_ref, bc_ref, rb_ref, x_ref, wg_ref, wu_ref, wda_ref, wdb_ref, bg_ref, bup_ref, bda_ref, bdb_ref,
                o_ref, act_ref, *, nf, ts):
    blk = pl.program_id(0)
    s = pl.program_id(1)
    cnt = bc_ref[blk]
    tg = 2 * ts
    tm = x_ref.shape[0]

    def row_groups(run, skip):
        for lo in range(0, tm, tg):
            size = min(tg, tm - lo)

            @pl.when(cnt > lo + size - ts)
            def _():
                run(lo, size)

            if size > ts:
                @pl.when(jnp.logical_and(cnt > lo, cnt <= lo + ts))
                def _():
                    run(lo, ts)
                    skip(lo + ts, ts)

            @pl.when(cnt <= lo)
            def _():
                skip(lo, size)

    @pl.when(jnp.logical_and(cnt > 0, s < nf))
    def _():
        slot = jnp.minimum(s, nf - 1)

        def up_proj(r0, nrows):
            x = _unpack_bf16_pairs(x_ref[r0:r0 + nrows, :])
            g = jnp.dot(x, wg_ref[...].astype(BF16), preferred_element_type=F32) + bg_ref[...]
            u = jnp.dot(x, wu_ref[...].astype(BF16), preferred_element_type=F32) + bup_ref[...]
            gate = jnp.minimum(g, SWIGLU_LIMIT)
            up = jnp.clip(u, -SWIGLU_LIMIT, SWIGLU_LIMIT)
            act = (up + 1.0) * gate * jax.nn.sigmoid(SWIGLU_ALPHA * gate)
            act_ref[slot, r0:r0 + nrows, :] = act.astype(BF16)

        row_groups(up_proj, lambda r0, nrows: None)

    @pl.when(jnp.logical_and(cnt > 0, s >= nf))
    def _():
        tf = act_ref.shape[2]

        def down_proj(r0, nrows):
            ya = bda_ref[...]
            yb = bdb_ref[...]
            for f in range(nf):
                a = act_ref[f, r0:r0 + nrows, :]
                wda = wda_ref[f * tf:(f + 1) * tf, :].astype(BF16)
                wdb = wdb_ref[f * tf:(f + 1) * tf, :].astype(BF16)
                ya = ya + jnp.dot(a, wda, preferred_element_type=F32)
                yb = yb + jnp.dot(a, wdb, preferred_element_type=F32)
            o_ref[r0:r0 + nrows, :] = _pack_bf16_pairs(jnp.concatenate([ya, yb], axis=1))

        def zero_fill(r0, nrows):
            o_ref[r0:r0 + nrows, :] = jnp.zeros((nrows, o_ref.shape[1]), o_ref.dtype)

        row_groups(down_proj, zero_fill)


def _moe_ffn(blk_e, blk_cnt, blk_row, xs, w_gu, w_dn, b_gu, b_dn, *, tm, ts, tf, tn):
    n_rows = xs.shape[0]
    n_exp, d, f2 = w_gu.shape
    d_ff = f2 // 2
    nf = d_ff // tf
    nn = d // tn
    n_blk = n_rows // tm
    b_gu3 = b_gu.reshape(n_exp, 1, f2)
    b_dn3 = b_dn.reshape(n_exp, 1, d)

    def fi(s, bc, b):
        return jnp.where(bc[b] > 0, jnp.minimum(s, nf - 1), nf - 1)

    nh = nn // 2

    def ni(s, bc, b):
        return jnp.where(bc[b] > 0, jnp.clip(s - nf, 0, nh - 1), nh - 1)

    grid_spec = pltpu.PrefetchScalarGridSpec(
        num_scalar_prefetch=3,
        grid=(n_blk, nf + nh),
        in_specs=[
            pl.BlockSpec((tm, d // 2), lambda b, s, be, bc, rb: (rb[b], 0)),
            pl.BlockSpec((None, d, tf), lambda b, s, be, bc, rb: (be[b], 0, fi(s, bc, b))),
            pl.BlockSpec((None, d, tf), lambda b, s, be, bc, rb: (be[b], 0, nf + fi(s, bc, b))),
            pl.BlockSpec((None, d_ff, tn), lambda b, s, be, bc, rb: (be[b], 0, ni(s, bc, b))),
            pl.BlockSpec((None, d_ff, tn), lambda b, s, be, bc, rb: (be[b], 0, nh + ni(s, bc, b))),
            pl.BlockSpec((None, 1, tf), lambda b, s, be, bc, rb: (be[b], 0, fi(s, bc, b))),
            pl.BlockSpec((None, 1, tf), lambda b, s, be, bc, rb: (be[b], 0, nf + fi(s, bc, b))),
            pl.BlockSpec((None, 1, tn), lambda b, s, be, bc, rb: (be[b], 0, ni(s, bc, b))),
            pl.BlockSpec((None, 1, tn), lambda b, s, be, bc, rb: (be[b], 0, nh + ni(s, bc, b))),
        ],
        out_specs=pl.BlockSpec((tm, tn), lambda b, s, be, bc, rb: (rb[b], ni(s, bc, b))),
        scratch_shapes=[pltpu.VMEM((nf, tm, tf), BF16)],
    )
    return pl.pallas_call(
        functools.partial(_moe_kernel, nf=nf, ts=ts),
        grid_spec=grid_spec,
        out_shape=jax.ShapeDtypeStruct((n_rows, d // 2), jnp.int32),
        compiler_params=_cparams(("arbitrary", "arbitrary")),
        name="moe_ffn",
    )(blk_e, blk_cnt, blk_row, xs, w_gu, w_gu, w_dn, w_dn, b_gu3, b_gu3, b_dn3, b_dn3)


def _final_kernel(h_ref, y0_ref, y1_ref, y2_ref, y3_ref, p_ref, g_ref, o_ref):
    p = p_ref[...]
    y = None
    for k, y_ref in enumerate((y0_ref, y1_ref, y2_ref, y3_ref)):
        t = _unpack_bf16_pairs(y_ref[...]).astype(F32) * p[:, k:k + 1]
        y = t if y is None else y + t
    h = h_ref[...] + y
    ms = jnp.mean(h * h, axis=-1, keepdims=True)
    o_ref[...] = (h * lax.rsqrt(ms + RMS_EPS)) * g_ref[...]


def _final(h1, y_km, gates, g, *, tm):
    m, d = h1.shape
    nb = m // tm
    assert TOP_K == 4
    y_specs = [pl.BlockSpec((tm, d // 2), functools.partial(lambda i, k: (k * nb + i, 0), k=k))
               for k in range(TOP_K)]
    return pl.pallas_call(
        _final_kernel,
        grid=(nb,),
        in_specs=([pl.BlockSpec((tm, d), lambda i: (i, 0))] + y_specs
                  + [pl.BlockSpec((tm, TOP_K), lambda i: (i, 0)), pl.BlockSpec((1, d), lambda i: (0, 0))]),
        out_specs=pl.BlockSpec((tm, d), lambda i: (i, 0)),
        out_shape=jax.ShapeDtypeStruct((m, d), F32),
        compiler_params=_cparams(("parallel",)),
        name="final_norm",
    )(h1, y_km, y_km, y_km, y_km, gates, g)


def _moe_blocks(n_assign, n_exp, tm):
    n_blk = -(-n_assign // tm) + n_exp
    unit = SC_CORES * SC_SUBCORES * SC_GATHER_ROWS
    while (n_blk * tm) % unit:
        n_blk += 1
    return n_blk


def _routing(top_e, counts, tm):
    n_exp = counts.shape[0]
    n_tok = top_e.shape[0]
    n_assign = n_tok * TOP_K
    flat_e = top_e.reshape(-1)
    padded = (counts + tm - 1) // tm * tm
    pad_end = jnp.cumsum(padded).astype(jnp.int32)
    pad_start = pad_end - padded
    grp_start = jnp.cumsum(counts).astype(jnp.int32) - counts
    n_blk = _moe_blocks(n_assign, n_exp, tm)
    blk_start = jnp.arange(n_blk, dtype=jnp.int32) * tm
    blk_e = jnp.minimum(jnp.sum((blk_start[:, None] >= pad_end[None, :]).astype(jnp.int32), axis=1), n_exp - 1)
    blk_off = blk_start - pad_start[blk_e]
    blk_cnt = jnp.clip(counts[blk_e] - blk_off, 0, tm).astype(jnp.int32)
    order = jnp.argsort(flat_e, stable=True).astype(jnp.int32)
    rank = jnp.argsort(order).astype(jnp.int32)
    delta = pad_start - grp_start
    e2d = flat_e.reshape(-1, LANES)
    hit = e2d[None] == jnp.arange(n_exp, dtype=jnp.int32)[:, None, None]
    dest_flat = rank + jnp.sum(jnp.where(hit, delta[:, None, None], 0), axis=0).reshape(-1)
    last_used = jnp.maximum(pad_end[-1] // tm - 1, 0)
    blk_e = jnp.where(blk_cnt > 0, blk_e, blk_e[last_used]).astype(jnp.int32)
    blk_row = jnp.where(blk_cnt > 0, jnp.arange(n_blk, dtype=jnp.int32), last_used).astype(jnp.int32)
    return dest_flat, blk_e, blk_cnt, blk_row, n_blk * tm


def kernel(x, meta_tokens, g_mix, w_in, lam_q1, lam_k1, lam_q2, lam_k2, g_sb_out, g_da_out, w_out, g_ffn,
           w_router, b_router, w_gate_up, b_gate_up, w_down, b_down, g_final):
    b, s, d = x.shape
    depth = w_in.shape[0]
    assert depth == 1, "single-layer trunk"
    layer = 0
    sb_heads = (d // 2) // SB_HEAD_DIM
    da_heads = (d // 2) // DA_V_DIM
    sb_w = sb_heads * SB_HEAD_DIM
    da_col0 = 3 * sb_w
    da_qk_w = da_heads * 2 * DA_HEAD_DIM
    n_exp = w_router.shape[-1]
    lam_init = 0.8 - 0.6 * math.exp(-0.3 * layer)

    tm_proj, tn_proj = TM_PROJ, TN_PROJ
    tq, tk_sb, ts_da = TQ_ATTN, TK_SB, TS_DA
    tm_out = TM_OUT
    tm_moe, tf_moe = TM_MOE, TF_MOE
    tm_fin = TM_FINAL

    x2d = x.reshape(b * s, d)
    w_in_bf = w_in.reshape(w_in.shape[1:]).astype(BF16)
    g_mix2 = g_mix[layer].reshape(1, d)
    rope_cols = (da_col0, da_col0 + 2 * da_qk_w)

    pos_f = N_META + jnp.arange(s, dtype=jnp.int32)
    proj = _in_proj(x2d, g_mix2, w_in_bf, _rope_tables(pos_f), tm=tm_proj, tn=tn_proj,
                    rope_cols=rope_cols, pos_blocks=s // tm_proj)
    meta_pad = jnp.zeros((META_PAD, d), x.dtype).at[:N_META].set(meta_tokens.astype(x.dtype))
    pos_m = jnp.arange(META_PAD, dtype=jnp.int32)
    proj_meta = _in_proj(meta_pad, g_mix2, w_in_bf, _rope_tables(pos_m), tm=META_PAD, tn=tn_proj,
                         rope_cols=rope_cols, pos_blocks=1)
    proj3 = proj.reshape(b, s, -1)

    mix_sb = _sb_attention(proj3, proj_meta, g_sb_out[layer].reshape(1, -1), batch=b, seq=s, heads=sb_heads,
                           tq=tq, tk=tk_sb, hp=HEADS_PER_STEP_SB)
    lams = tuple(t[layer].reshape(1, DA_HEAD_DIM).astype(F32) for t in (lam_q1, lam_k1, lam_q2, lam_k2))
    mix_da = _da_attention(proj3, proj_meta, lams, g_da_out[layer].reshape(1, -1), batch=b, seq=s, heads=da_heads,
                           tq=TQ_DA, tkv=TKV_DA, ts=ts_da, col0=da_col0, lam_init=lam_init)

    w_out_bf = w_out[layer].astype(BF16)
    wr = w_router[layer]
    wr_hi = wr.astype(BF16)
    wr_lo = (wr - wr_hi.astype(F32)).astype(BF16)
    h1, u_packed, top_e, gates, tile_counts = _out_proj(
        mix_sb.reshape(b * s, -1), mix_da.reshape(b * s, -1), w_out_bf[:sb_w], w_out_bf[sb_w:], x2d,
        g_ffn[layer].reshape(1, d), wr_hi, wr_lo, b_router[layer].reshape(1, n_exp), tm=tm_out)

    counts = jnp.sum(tile_counts, axis=(0, 1)).astype(jnp.int32)
    dest_flat, blk_e, blk_cnt, blk_row, n_rows = _routing(top_e, counts, tm_moe)
    dest_km = dest_flat.reshape(b * s, TOP_K).T
    xs = _sc_scatter_rows(u_packed, dest_km, n_rows)
    rows = _moe_ffn(blk_e, blk_cnt, blk_row, xs, w_gate_up.reshape(w_gate_up.shape[1:]),
                    w_down.reshape(w_down.shape[1:]), b_gate_up[layer], b_down[layer],
                    tm=tm_moe, ts=TS_MOE, tf=tf_moe, tn=TN_MOE)
    y_km = _sc_gather_rows(rows, dest_km.reshape(-1))

    out = _final(h1, y_km, gates, g_final.reshape(1, d), tm=tm_fin)
    return out.reshape(b, s, d)
```

```python
import functools
import math

import jax
import jax.numpy as jnp
from jax import lax
from jax.experimental import pallas as pl
from jax.experimental.pallas import tpu as pltpu
from jax.experimental.pallas import tpu_sc as plsc

F32 = jnp.float32
BF16 = jnp.bfloat16

CHUNK = 64
N_META = 16
RMS_EPS = 1e-5
SB_HEAD_DIM = 128
DA_HEAD_DIM = 64
DA_V_DIM = 2 * DA_HEAD_DIM
ROPE_THETA = 500000.0
ROPE_DIM = DA_HEAD_DIM // 4
TOP_K = 4
SWIGLU_LIMIT = 7.0
SWIGLU_ALPHA = 1.702

LANES = 128
META_PAD = 128
NEG_BIG = -1e30
SB_DEAD_RUN = 110.0
VMEM_LIMIT = 56 * 1024 * 1024
SC_CORES, SC_SUBCORES = 2, 16
SC_GATHER_ROWS = 32

TM_PROJ, TN_PROJ = 1024, 1024
TQ_ATTN = 512
HEADS_PER_STEP_SB = 4
TQ_DA, TKV_DA = 1024, 512
TK_SB = 256
TS_DA = 512
HEADS_PER_STEP_DA = 2
TM_OUT = 512
TM_MOE, TS_MOE = 2176, 544
TF_MOE, TN_MOE = 256, 256
TM_FINAL = 512


def _cparams(sem):
    return pltpu.CompilerParams(dimension_semantics=sem, vmem_limit_bytes=VMEM_LIMIT)


def _in_proj_kernel(x_ref, g_ref, w_ref, c_ref, sa_ref, sb_ref, o_ref, u_scr, *, tn, rope_lo, rope_hi):
    n = pl.program_id(1)

    @pl.when(n == 0)
    def _():
        x = x_ref[...]
        ms = jnp.mean(x * x, axis=-1, keepdims=True)
        u_scr[...] = ((x * lax.rsqrt(ms + RMS_EPS)) * g_ref[...]).astype(BF16)

    acc = jnp.dot(u_scr[...], w_ref[...], preferred_element_type=F32)
    is_rope = jnp.logical_and(n >= rope_lo, n < rope_hi)
    o_ref[...] = acc.astype(BF16)

    @pl.when(is_rope)
    def _():
        for c in range(tn // LANES):
            xc = acc[:, c * LANES:(c + 1) * LANES]
            r = (xc * c_ref[...] + pltpu.roll(xc, LANES - ROPE_DIM // 2, 1) * sa_ref[...]
                 + pltpu.roll(xc, ROPE_DIM // 2, 1) * sb_ref[...])
            o_ref[:, c * LANES:(c + 1) * LANES] = r.astype(BF16)


def _rope_tables(pos):
    half = ROPE_DIM // 2
    inv_freq = ROPE_THETA ** (-(jnp.arange(half, dtype=F32) * 2.0 / ROPE_DIM))
    ang = pos.astype(F32)[:, None] * inv_freq[None, :]
    cos, sin = jnp.cos(ang), jnp.sin(ang)
    p = pos.shape[0]
    ones = jnp.ones((p, DA_HEAD_DIM - ROPE_DIM), F32)
    zeros8 = jnp.zeros((p, half), F32)
    zeros48 = jnp.zeros((p, DA_HEAD_DIM - ROPE_DIM), F32)
    c64 = jnp.concatenate([cos, cos, ones], axis=1)
    sa64 = jnp.concatenate([-sin, zeros8, zeros48], axis=1)
    sb64 = jnp.concatenate([zeros8, sin, zeros48], axis=1)
    tile2 = lambda t: jnp.concatenate([t, t], axis=1)
    return tile2(c64), tile2(sa64), tile2(sb64)


def _in_proj(x2d, g, w_bf, tables, *, tm, tn, rope_cols, pos_blocks):
    m, d = x2d.shape
    n_cols = w_bf.shape[1]
    c_t, sa_t, sb_t = tables
    kern = functools.partial(_in_proj_kernel, tn=tn, rope_lo=rope_cols[0] // tn, rope_hi=rope_cols[1] // tn)
    tab_spec = pl.BlockSpec((tm, LANES), lambda i, n: (i % pos_blocks, 0))
    return pl.pallas_call(
        kern,
        grid=(m // tm, n_cols // tn),
        in_specs=[
            pl.BlockSpec((tm, d), lambda i, n: (i, 0)),
            pl.BlockSpec((1, d), lambda i, n: (0, 0)),
            pl.BlockSpec((d, tn), lambda i, n: (0, n)),
            tab_spec, tab_spec, tab_spec,
        ],
        out_specs=pl.BlockSpec((tm, tn), lambda i, n: (i, n)),
        out_shape=jax.ShapeDtypeStruct((m, n_cols), BF16),
        scratch_shapes=[pltpu.VMEM((tm, d), BF16)],
        compiler_params=_cparams(("parallel", "arbitrary")),
        name="in_proj",
    )(x2d, g, w_bf, c_t, sa_t, sb_t)


def _dot_nt(a, b):
    return lax.dot_general(a, b, (((1,), (1,)), ((), ())), preferred_element_type=F32)


def _suffix_sum_matrix(n):
    j = lax.broadcasted_iota(jnp.int32, (2 * n, n), 0)
    s = lax.broadcasted_iota(jnp.int32, (2 * n, n), 1)
    return jnp.where(jnp.where(j >= n, j - n, j) > s, 1.0, 0.0).astype(BF16)


def _sb_block(q, kb, vb, u2, run, acc, mask, scale):
    z = _dot_nt(q, kb) * scale
    sp = jnp.maximum(z, 0.0) + jnp.log(1.0 + jnp.exp(-jnp.abs(z)))
    if mask is not None:
        sp = jnp.where(mask, sp, 0.0)
    hi = sp.astype(BF16)
    lo = (sp - hi.astype(F32)).astype(BF16)
    cs = jnp.dot(jnp.concatenate([hi, lo], axis=1), u2, preferred_element_type=F32)
    w = jnp.exp(z - sp - cs - run)
    if mask is not None:
        w = jnp.where(mask, w, 0.0)
    acc = acc + jnp.dot(w.astype(BF16), vb, preferred_element_type=F32)
    run = run + (cs[:, :1] + sp[:, :1])
    return run, acc


def _sb_kernel(q_ref, k_ref, v_ref, km_ref, vm_ref, g_ref, o_ref, *, tq, tk, hp, scale):
    i = pl.program_id(2)
    nsub = tq // tk
    hd = SB_HEAD_DIM
    u_blk = _suffix_sum_matrix(tk)
    row = lax.broadcasted_iota(jnp.int32, (tk, tk), 0)
    col = lax.broadcasted_iota(jnp.int32, (tk, tk), 1)
    diag_mask = col < row
    heads = range(hp)

    def kv(blk, h):
        s0 = pl.multiple_of(blk * tk, tk)
        return k_ref[pl.ds(s0, tk), h * hd:(h + 1) * hd], v_ref[pl.ds(s0, tk), h * hd:(h + 1) * hd]

    runs, accs = [], []
    for h in heads:
        slab_runs, slab_accs = [], []
        for sl in range(nsub):
            qs = q_ref[sl * tk:(sl + 1) * tk, h * hd:(h + 1) * hd]
            run = jnp.zeros((tk, 1), F32)
            acc = jnp.zeros((tk, hd), F32)
            for c in range(sl, -1, -1):
                kb, vb = kv(i * nsub + c, h)
                run, acc = _sb_block(qs, kb, vb, u_blk, run, acc, diag_mask if c == sl else None, scale)
            slab_runs.append(run)
            slab_accs.append(acc)
        runs.append(jnp.concatenate(slab_runs, axis=0))
        accs.append(jnp.concatenate(slab_accs, axis=0))
    q = [q_ref[:, h * hd:(h + 1) * hd] for h in heads]

    def alive_flag(rs):
        lowest = functools.reduce(jnp.minimum, [jnp.min(r) for r in rs])
        return (lowest < SB_DEAD_RUN).astype(jnp.int32)

    def cond(carry):
        jb, alive, _, _ = carry
        return jnp.logical_and(jb >= 0, alive > 0)

    def body(carry):
        jb, _, rs, as_ = carry
        out = [_sb_block(q[h], *kv(jb, h), u_blk, rs[h], as_[h], None, scale) for h in heads]
        rs = tuple(o[0] for o in out)
        return jb - 1, alive_flag(rs), rs, tuple(o[1] for o in out)

    runs, accs = tuple(runs), tuple(accs)
    _, alive, runs, accs = lax.while_loop(cond, body, (i * nsub - 1, alive_flag(runs), runs, accs))

    def meta_block():
        mcol = lax.broadcasted_iota(jnp.int32, (tq, META_PAD), 1)
        u_meta = _suffix_sum_matrix(META_PAD)
        return tuple(_sb_block(q[h], km_ref[:, h * hd:(h + 1) * hd], vm_ref[:, h * hd:(h + 1) * hd], u_meta,
                               runs[h], accs[h], mcol < N_META, scale)[1] for h in heads)

    accs = lax.cond(alive > 0, meta_block, lambda: accs)

    for h in heads:
        acc = accs[h]
        ms = jnp.mean(acc * acc, axis=-1, keepdims=True)
        gain = g_ref[:, h * hd:(h + 1) * hd]
        o_ref[:, h * hd:(h + 1) * hd] = ((acc * lax.rsqrt(ms + RMS_EPS)) * gain).astype(BF16)


def _sb_attention(proj, proj_meta, g_sb, *, batch, seq, heads, tq, tk, hp):
    hd = SB_HEAD_DIM
    wd = hp * hd
    ng = heads // hp
    kern = functools.partial(_sb_kernel, tq=tq, tk=tk, hp=hp, scale=hd ** -0.5)
    return pl.pallas_call(
        kern,
        grid=(batch, ng, seq // tq),
        in_specs=[
            pl.BlockSpec((None, tq, wd), lambda b, h, i: (b, i, h)),
            pl.BlockSpec((None, seq, wd), lambda b, h, i: (b, 0, ng + h)),
            pl.BlockSpec((None, seq, wd), lambda b, h, i: (b, 0, 2 * ng + h)),
            pl.BlockSpec((META_PAD, wd), lambda b, h, i: (0, ng + h)),
            pl.BlockSpec((META_PAD, wd), lambda b, h, i: (0, 2 * ng + h)),
            pl.BlockSpec((1, wd), lambda b, h, i: (0, h)),
        ],
        out_specs=pl.BlockSpec((None, tq, wd), lambda b, h, i: (b, i, h)),
        out_shape=jax.ShapeDtypeStruct((batch, seq, heads * hd), BF16),
        compiler_params=_cparams(("parallel", "parallel", "arbitrary")),
        name="sb_attn",
    )(proj, proj, proj, proj_meta, proj_meta, g_sb)


def _with_ones(vb):
    return jnp.concatenate([vb, jnp.ones_like(vb)], axis=1)


def _da_block(q1, q2, kb, vb1, st, mask):
    m1, a1, m2, a2 = st

    def one(qc, m, a):
        s = _dot_nt(qc, kb)
        if mask is not None:
            s = jnp.where(mask, s, NEG_BIG)
        mn = jnp.maximum(m, jnp.max(s, axis=-1, keepdims=True))
        p = jnp.exp(s - mn)
        a = jnp.exp(m - mn) * a + jnp.dot(p.astype(BF16), vb1, preferred_element_type=F32)
        return mn, a

    m1, a1 = one(q1, m1, a1)
    m2, a2 = one(q2, m2, a2)
    return m1, a1, m2, a2


def _da_kernel(lq1_ref, lk1_ref, lq2_ref, lk2_ref, q_ref, k_ref, v_ref, km_ref, vm_ref, g_ref, o_ref,
               *, tq, tkv, ts, hp, scale, lam_init):
    i = pl.program_id(2)
    nsub = tq // ts
    hd = DA_V_DIM
    heads = range(hp)
    lam = (jnp.exp(jnp.sum(lq1_ref[...] * lk1_ref[...], axis=-1, keepdims=True))
           - jnp.exp(jnp.sum(lq2_ref[...] * lk2_ref[...], axis=-1, keepdims=True)) + lam_init)

    def cols(ref, rows, h):
        return ref[rows, h * hd:(h + 1) * hd]

    lane = lax.broadcasted_iota(jnp.int32, (tq, hd), 1)
    q1, q2 = [], []
    for h in heads:
        qs = cols(q_ref, slice(None), h) * jnp.asarray(scale, BF16)
        zero = jnp.zeros_like(qs)
        q1.append(jnp.where(lane < DA_HEAD_DIM, qs, zero))
        q2.append(jnp.where(lane >= DA_HEAD_DIM, qs, zero))

    neg = jnp.full((tq, 1), NEG_BIG, F32)
    za = jnp.zeros((tq, 2 * hd), F32)
    st = tuple((neg, za, neg, za) for _ in heads)

    def body(j, carry):
        carry = list(carry)
        for c in range(tq // tkv):
            rows = pl.ds(pl.multiple_of(j * tq + c * tkv, tkv), tkv)
            for h in heads:
                carry[h] = _da_block(q1[h], q2[h], cols(k_ref, rows, h), _with_ones(cols(v_ref, rows, h)),
                                     carry[h], None)
        return tuple(carry)

    st = lax.fori_loop(0, i, body, st)

    shift = CHUNK.bit_length() - 1
    s0 = pl.multiple_of(i * tq, tq)
    for h in heads:
        outs = []
        for sl in range(nsub):
            rows = slice(sl * ts, (sl + 1) * ts)
            width = (sl + 1) * ts
            kb = jnp.concatenate([cols(km_ref, slice(None), h), cols(k_ref, pl.ds(s0, width), h)], axis=0)
            vb = jnp.concatenate([cols(vm_ref, slice(None), h), cols(v_ref, pl.ds(s0, width), h)], axis=0)
            col = lax.broadcasted_iota(jnp.int32, (ts, META_PAD + width), 1)
            row = lax.broadcasted_iota(jnp.int32, (ts, META_PAD + width), 0) + sl * ts
            frame_vis = jnp.logical_and(col >= META_PAD,
                                        jnp.right_shift(col - META_PAD, shift) <= jnp.right_shift(row, shift))
            vis = jnp.logical_or(col < N_META, frame_vis)
            sth = _da_block(q1[h][rows], q2[h][rows], kb, _with_ones(vb), tuple(t[rows] for t in st[h]), vis)
            _, a1, _, a2 = sth
            outs.append(a1[:, :hd] / a1[:, hd:] - lam * (a2[:, :hd] / a2[:, hd:]))
        o = jnp.concatenate(outs, axis=0)
        ms = jnp.mean(o * o, axis=-1, keepdims=True)
        gain = g_ref[:, h * hd:(h + 1) * hd]
        o_ref[:, h * hd:(h + 1) * hd] = (((o * lax.rsqrt(ms + RMS_EPS)) * gain) * (1.0 - lam_init)).astype(BF16)


def _da_attention(proj, proj_meta, lams, g_da, *, batch, seq, heads, tq, tkv, ts, hp, col0, lam_init):
    hd = DA_V_DIM
    wd = hp * hd
    ng = heads // hp
    qb = col0 // wd
    kb, vb = qb + ng, qb + 2 * ng
    kern = functools.partial(_da_kernel, tq=tq, tkv=tkv, ts=ts, hp=hp, scale=DA_HEAD_DIM ** -0.5,
                             lam_init=lam_init)
    lam_spec = pl.BlockSpec((1, DA_HEAD_DIM), lambda b, h, i: (0, 0))
    return pl.pallas_call(
        kern,
        grid=(batch, ng, seq // tq),
        in_specs=[
            lam_spec, lam_spec, lam_spec, lam_spec,
            pl.BlockSpec((None, tq, wd), lambda b, h, i: (b, i, qb + h)),
            pl.BlockSpec((None, seq, wd), lambda b, h, i: (b, 0, kb + h)),
            pl.BlockSpec((None, seq, wd), lambda b, h, i: (b, 0, vb + h)),
            pl.BlockSpec((META_PAD, wd), lambda b, h, i: (0, kb + h)),
            pl.BlockSpec((META_PAD, wd), lambda b, h, i: (0, vb + h)),
            pl.BlockSpec((1, wd), lambda b, h, i: (0, h)),
        ],
        out_specs=pl.BlockSpec((None, tq, wd), lambda b, h, i: (b, i, h)),
        out_shape=jax.ShapeDtypeStruct((batch, seq, heads * hd), BF16),
        compiler_params=_cparams(("parallel", "parallel", "arbitrary")),
        name="da_attn",
    )(*lams, proj, proj, proj, proj_meta, proj_meta, g_da)


def _pack_bf16_pairs(x):
    n = x.shape[1] // 2
    lo = lax.bitcast_convert_type(x[:, :n].astype(BF16).astype(F32), jnp.int32)
    hi = lax.bitcast_convert_type(x[:, n:].astype(BF16).astype(F32), jnp.int32)
    return jnp.bitwise_or(lax.shift_right_logical(lo, 16), jnp.bitwise_and(hi, jnp.int32(-65536)))


def _unpack_bf16_pairs(w):
    lo = lax.bitcast_convert_type(lax.shift_left(w, 16), F32)
    hi = lax.bitcast_convert_type(jnp.bitwise_and(w, jnp.int32(-65536)), F32)
    return jnp.concatenate([lo.astype(BF16), hi.astype(BF16)], axis=1)


def _sc_gather_rows(table, idx):
    n_idx = idx.shape[0]
    width = table.shape[1]
    n_workers = SC_CORES * SC_SUBCORES
    per_worker = n_idx // n_workers
    win = SC_GATHER_ROWS
    assert n_idx % (n_workers * win) == 0
    mesh = plsc.VectorSubcoreMesh(core_axis_name="c", subcore_axis_name="s")

    n_win = per_worker // win
    assert n_win % 2 == 0
    buf = lambda: pltpu.VMEM((win, width), table.dtype)
    dma = pltpu.SemaphoreType.DMA

    @functools.partial(
        pl.kernel, mesh=mesh,
        out_type=jax.ShapeDtypeStruct((n_idx, width), table.dtype),
        scratch_types=[pltpu.VMEM((n_win, win), jnp.int32), buf(), buf(), dma, dma, dma, dma],
    )
    def gather(table_hbm, idx_hbm, out_hbm, idx_v, rows0, rows1, gsem0, gsem1, wsem0, wsem1):
        worker = lax.axis_index("s") * SC_CORES + lax.axis_index("c")
        base = worker * per_worker
        pltpu.sync_copy(idx_hbm.at[pl.ds(worker * n_win, n_win)], idx_v)

        def gather_copy(t, rows_v, sem):
            return pltpu.make_async_copy(table_hbm.at[idx_v.at[t]], rows_v, sem)

        def writeback(t, rows_v, sem):
            return pltpu.make_async_copy(rows_v, out_hbm.at[pl.ds(pl.multiple_of(base + t * win, win), win)], sem)

        gather_copy(0, rows0, gsem0).start()

        @pl.loop(0, n_win, step=2)
        def _(t):
            gather_copy(t, rows0, gsem0).wait()
            writeback(t, rows0, wsem0).start()

            @pl.when(t > 0)
            def _():
                writeback(t - 1, rows1, wsem1).wait()

            gather_copy(t + 1, rows1, gsem1).start()
            gather_copy(t + 1, rows1, gsem1).wait()
            writeback(t + 1, rows1, wsem1).start()
            writeback(t, rows0, wsem0).wait()

            @pl.when(t + 2 < n_win)
            def _():
                gather_copy(t + 2, rows0, gsem0).start()

        writeback(n_win - 1, rows1, wsem1).wait()

    return gather(table, idx.reshape(n_idx // win, win))


def _sc_scatter_rows(src, dest_km, n_rows):
    n_tok, width = src.shape
    n_slots = dest_km.shape[0]
    n_workers = SC_CORES * SC_SUBCORES
    win = SC_GATHER_ROWS
    per_worker = n_tok // n_workers
    n_win = per_worker // win
    assert n_tok % (n_workers * win) == 0 and n_win % 2 == 0
    mesh = plsc.VectorSubcoreMesh(core_axis_name="c", subcore_axis_name="s")
    buf = lambda: pltpu.VMEM((win, width), src.dtype)
    dma = pltpu.SemaphoreType.DMA

    @functools.partial(
        pl.kernel, mesh=mesh,
        out_type=jax.ShapeDtypeStruct((n_rows, width), src.dtype),
        scratch_types=[pltpu.VMEM((n_slots, n_win, win), jnp.int32), buf(), buf(), dma, dma, dma, dma],
    )
    def scatter(src_hbm, idx_hbm, out_hbm, idx_v, rows0, rows1, lsem0, lsem1, ssem0, ssem1):
        worker = lax.axis_index("s") * SC_CORES + lax.axis_index("c")
        base = worker * per_worker
        for k in range(n_slots):
            pltpu.sync_copy(idx_hbm.at[k, pl.ds(worker * n_win, n_win)], idx_v.at[k])

        def load(t, rows_v, sem):
            return pltpu.make_async_copy(src_hbm.at[pl.ds(pl.multiple_of(base + t * win, win), win)], rows_v, sem)

        def start_scatters(t, rows_v, sem):
            for k in range(n_slots):
                pltpu.make_async_copy(rows_v, out_hbm.at[idx_v.at[k, t]], sem).start()

        def wait_scatters(t, rows_v, sem):
            for k in range(n_slots):
                pltpu.make_async_copy(rows_v, out_hbm.at[idx_v.at[k, t]], sem).wait()

        load(0, rows0, lsem0).start()

        @pl.loop(0, n_win, step=2)
        def _(t):
            load(t, rows0, lsem0).wait()
            start_scatters(t, rows0, ssem0)

            @pl.when(t > 0)
            def _():
                wait_scatters(t - 1, rows1, ssem1)

            load(t + 1, rows1, lsem1).start()
            load(t + 1, rows1, lsem1).wait()
            start_scatters(t + 1, rows1, ssem1)
            wait_scatters(t, rows0, ssem0)

            @pl.when(t + 2 < n_win)
            def _():
                load(t + 2, rows0, lsem0).start()

        wait_scatters(n_win - 1, rows1, ssem1)

    return scatter(src, dest_km.reshape(n_slots, n_tok // win, win))


def _split3_dot(a, b_hi, b_lo):
    a_hi = a.astype(BF16)
    a_lo = (a - a_hi.astype(F32)).astype(BF16)
    return (jnp.dot(a_hi, b_hi, preferred_element_type=F32)
            + jnp.dot(a_hi, b_lo, preferred_element_type=F32)
            + jnp.dot(a_lo, b_hi, preferred_element_type=F32))


def _out_proj_kernel(ms_ref, md_ref, ws_ref, wd_ref, x_ref, g_ref, wr_hi_ref, wr_lo_ref, br_ref,
                     h_ref, u_ref, e_ref, p_ref, c_ref, *, n_exp):
    h = (x_ref[...]
         + jnp.dot(ms_ref[...], ws_ref[...], preferred_element_type=F32)
         + jnp.dot(md_ref[...], wd_ref[...], preferred_element_type=F32))
    h_ref[...] = h
    msq = jnp.mean(h * h, axis=-1, keepdims=True)
    u = (h * lax.rsqrt(msq + RMS_EPS)) * g_ref[...]
    u_ref[...] = _pack_bf16_pairs(u)

    logits = _split3_dot(u, wr_hi_ref[...], wr_lo_ref[...]) + br_ref[...]
    lane = lax.broadcasted_iota(jnp.int32, logits.shape, 1).astype(F32)
    work = logits
    tops, idxs = [], []
    for _ in range(TOP_K):
        mx = jnp.max(work, axis=-1, keepdims=True)
        ix = jnp.min(jnp.where(work == mx, lane, float(n_exp)), axis=-1, keepdims=True)
        tops.append(mx)
        idxs.append(ix)
        work = jnp.where(lane == ix, -jnp.inf, work)
    ex = [jnp.exp(t - tops[0]) for t in tops]
    den = ex[0] + ex[1] + ex[2] + ex[3]
    kl = lax.broadcasted_iota(jnp.int32, (logits.shape[0], TOP_K), 1)
    e_out = jnp.zeros((logits.shape[0], TOP_K), F32)
    p_out = jnp.zeros((logits.shape[0], TOP_K), F32)
    for k in range(TOP_K):
        e_out = jnp.where(kl == k, idxs[k], e_out)
        p_out = jnp.where(kl == k, ex[k] / den, p_out)
    e_ref[...] = e_out.astype(jnp.int32)
    p_ref[...] = p_out
    hits = jnp.zeros(logits.shape, F32)
    for k in range(TOP_K):
        hits = hits + jnp.where(lane == idxs[k], 1.0, 0.0)
    c_ref[...] = jnp.sum(hits, axis=0, keepdims=True)


def _out_proj(mix_sb, mix_da, w_sb, w_da, x2d, g_ffn, wr_hi, wr_lo, b_r, *, tm):
    m, d = x2d.shape
    ks, kd = mix_sb.shape[1], mix_da.shape[1]
    n_exp = wr_hi.shape[1]
    kern = functools.partial(_out_proj_kernel, n_exp=n_exp)
    const = lambda shape: pl.BlockSpec(shape, lambda i: (0, 0))
    return pl.pallas_call(
        kern,
        grid=(m // tm,),
        in_specs=[
            pl.BlockSpec((tm, ks), lambda i: (i, 0)),
            pl.BlockSpec((tm, kd), lambda i: (i, 0)),
            const((ks, d)), const((kd, d)),
            pl.BlockSpec((tm, d), lambda i: (i, 0)),
            const((1, d)), const((d, n_exp)), const((d, n_exp)), const((1, n_exp)),
        ],
        out_specs=[
            pl.BlockSpec((tm, d), lambda i: (i, 0)),
            pl.BlockSpec((tm, d // 2), lambda i: (i, 0)),
            pl.BlockSpec((tm, TOP_K), lambda i: (i, 0)),
            pl.BlockSpec((tm, TOP_K), lambda i: (i, 0)),
            pl.BlockSpec((None, 1, n_exp), lambda i: (i, 0, 0)),
        ],
        out_shape=[
            jax.ShapeDtypeStruct((m, d), F32),
            jax.ShapeDtypeStruct((m, d // 2), jnp.int32),
            jax.ShapeDtypeStruct((m, TOP_K), jnp.int32),
            jax.ShapeDtypeStruct((m, TOP_K), F32),
            jax.ShapeDtypeStruct((m // tm, 1, n_exp), F32),
        ],
        compiler_params=_cparams(("parallel",)),
        name="out_proj_router",
    )(mix_sb, mix_da, w_sb, w_da, x2d, g_ffn, wr_hi, wr_lo, b_r)


def _moe_kernel(be_ref, bc_ref, rb_ref, x_ref, wg_ref, wu_ref, wda_ref, wdb_ref, bg_ref, bup_ref, bda_ref, bdb_ref,
                o_ref, act_ref, *, nf, ts):
    blk = pl.program_id(0)
    s = pl.program_id(1)
    cnt = bc_ref[blk]
    tg = 2 * ts
    tm = x_ref.shape[0]

    def row_groups(run, skip):
        for lo in range(0, tm, tg):
            size = min(tg, tm - lo)

            @pl.when(cnt > lo + size - ts)
            def _():
                run(lo, size)

            if size > ts:
                @pl.when(jnp.logical_and(cnt > lo, cnt <= lo + ts))
                def _():
                    run(lo, ts)
                    skip(lo + ts, ts)

            @pl.when(cnt <= lo)
            def _():
                skip(lo, size)

    @pl.when(jnp.logical_and(cnt > 0, s < nf))
    def _():
        slot = jnp.minimum(s, nf - 1)

        def up_proj(r0, nrows):
            x = _unpack_bf16_pairs(x_ref[r0:r0 + nrows, :])
            g = jnp.dot(x, wg_ref[...].astype(BF16), preferred_element_type=F32) + bg_ref[...]
            u = jnp.dot(x, wu_ref[...].astype(BF16), preferred_element_type=F32) + bup_ref[...]
            gate = jnp.minimum(g, SWIGLU_LIMIT)
            up = jnp.clip(u, -SWIGLU_LIMIT, SWIGLU_LIMIT)
            act = (up + 1.0) * gate * jax.nn.sigmoid(SWIGLU_ALPHA * gate)
            act_ref[slot, r0:r0 + nrows, :] = act.astype(BF16)

        row_groups(up_proj, lambda r0, nrows: None)

    @pl.when(jnp.logical_and(cnt > 0, s >= nf))
    def _():
        tf = act_ref.shape[2]

        def down_proj(r0, nrows):
            ya = bda_ref[...]
            yb = bdb_ref[...]
            for f in range(nf):
                a = act_ref[f, r0:r0 + nrows, :]
                wda = wda_ref[f * tf:(f + 1) * tf, :].astype(BF16)
                wdb = wdb_ref[f * tf:(f + 1) * tf, :].astype(BF16)
                ya = ya + jnp.dot(a, wda, preferred_element_type=F32)
                yb = yb + jnp.dot(a, wdb, preferred_element_type=F32)
            o_ref[r0:r0 + nrows, :] = _pack_bf16_pairs(jnp.concatenate([ya, yb], axis=1))

        def zero_fill(r0, nrows):
            o_ref[r0:r0 + nrows, :] = jnp.zeros((nrows, o_ref.shape[1]), o_ref.dtype)

        row_groups(down_proj, zero_fill)


def _moe_ffn(blk_e, blk_cnt, blk_row, xs, w_gu, w_dn, b_gu, b_dn, *, tm, ts, tf, tn):
    n_rows = xs.shape[0]
    n_exp, d, f2 = w_gu.shape
    d_ff = f2 // 2
    nf = d_ff // tf
    nn = d // tn
    n_blk = n_rows // tm
    b_gu3 = b_gu.reshape(n_exp, 1, f2)
    b_dn3 = b_dn.reshape(n_exp, 1, d)

    def fi(s, bc, b):
        return jnp.where(bc[b] > 0, jnp.minimum(s, nf - 1), nf - 1)

    nh = nn // 2

    def ni(s, bc, b):
        return jnp.where(bc[b] > 0, jnp.clip(s - nf, 0, nh - 1), nh - 1)

    grid_spec = pltpu.PrefetchScalarGridSpec(
        num_scalar_prefetch=3,
        grid=(n_blk, nf + nh),
        in_specs=[
            pl.BlockSpec((tm, d // 2), lambda b, s, be, bc, rb: (rb[b], 0)),
            pl.BlockSpec((None, d, tf), lambda b, s, be, bc, rb: (be[b], 0, fi(s, bc, b))),
            pl.BlockSpec((None, d, tf), lambda b, s, be, bc, rb: (be[b], 0, nf + fi(s, bc, b))),
            pl.BlockSpec((None, d_ff, tn), lambda b, s, be, bc, rb: (be[b], 0, ni(s, bc, b))),
            pl.BlockSpec((None, d_ff, tn), lambda b, s, be, bc, rb: (be[b], 0, nh + ni(s, bc, b))),
            pl.BlockSpec((None, 1, tf), lambda b, s, be, bc, rb: (be[b], 0, fi(s, bc, b))),
            pl.BlockSpec((None, 1, tf), lambda b, s, be, bc, rb: (be[b], 0, nf + fi(s, bc, b))),
            pl.BlockSpec((None, 1, tn), lambda b, s, be, bc, rb: (be[b], 0, ni(s, bc, b))),
            pl.BlockSpec((None, 1, tn), lambda b, s, be, bc, rb: (be[b], 0, nh + ni(s, bc, b))),
        ],
        out_specs=pl.BlockSpec((tm, tn), lambda b, s, be, bc, rb: (rb[b], ni(s, bc, b))),
        scratch_shapes=[pltpu.VMEM((nf, tm, tf), BF16)],
    )
    return pl.pallas_call(
        functools.partial(_moe_kernel, nf=nf, ts=ts),
        grid_spec=grid_spec,
        out_shape=jax.ShapeDtypeStruct((n_rows, d // 2), jnp.int32),
        compiler_params=_cparams(("arbitrary", "arbitrary")),
        name="moe_ffn",
    )(blk_e, blk_cnt, blk_row, xs, w_gu, w_gu, w_dn, w_dn, b_gu3, b_gu3, b_dn3, b_dn3)


def _final_kernel(h_ref, y0_ref, y1_ref, y2_ref, y3_ref, p_ref, g_ref, o_ref):
    p = p_ref[...]
    y = None
    for k, y_ref in enumerate((y0_ref, y1_ref, y2_ref, y3_ref)):
        t = _unpack_bf16_pairs(y_ref[...]).astype(F32) * p[:, k:k + 1]
        y = t if y is None else y + t
    h = h_ref[...] + y
    ms = jnp.mean(h * h, axis=-1, keepdims=True)
    o_ref[...] = (h * lax.rsqrt(ms + RMS_EPS)) * g_ref[...]


def _final(h1, y_km, gates, g, *, tm):
    m, d = h1.shape
    nb = m // tm
    assert TOP_K == 4
    y_specs = [pl.BlockSpec((tm, d // 2), functools.partial(lambda i, k: (k * nb + i, 0), k=k))
               for k in range(TOP_K)]
    return pl.pallas_call(
        _final_kernel,
        grid=(nb,),
        in_specs=([pl.BlockSpec((tm, d), lambda i: (i, 0))] + y_specs
                  + [pl.BlockSpec((tm, TOP_K), lambda i: (i, 0)), pl.BlockSpec((1, d), lambda i: (0, 0))]),
        out_specs=pl.BlockSpec((tm, d), lambda i: (i, 0)),
        out_shape=jax.ShapeDtypeStruct((m, d), F32),
        compiler_params=_cparams(("parallel",)),
        name="final_norm",
    )(h1, y_km, y_km, y_km, y_km, gates, g)


def _moe_blocks(n_assign, n_exp, tm):
    n_blk = -(-n_assign // tm) + n_exp
    unit = SC_CORES * SC_SUBCORES * SC_GATHER_ROWS
    while (n_blk * tm) % unit:
        n_blk += 1
    return n_blk


def _routing(top_e, counts, tm):
    n_exp = counts.shape[0]
    n_tok = top_e.shape[0]
    n_assign = n_tok * TOP_K
    flat_e = top_e.reshape(-1)
    padded = (counts + tm - 1) // tm * tm
    pad_end = jnp.cumsum(padded).astype(jnp.int32)
    pad_start = pad_end - padded
    grp_start = jnp.cumsum(counts).astype(jnp.int32) - counts
    n_blk = _moe_blocks(n_assign, n_exp, tm)
    blk_start = jnp.arange(n_blk, dtype=jnp.int32) * tm
    blk_e = jnp.minimum(jnp.sum((blk_start[:, None] >= pad_end[None, :]).astype(jnp.int32), axis=1), n_exp - 1)
    blk_off = blk_start - pad_start[blk_e]
    blk_cnt = jnp.clip(counts[blk_e] - blk_off, 0, tm).astype(jnp.int32)
    order = jnp.argsort(flat_e, stable=True).astype(jnp.int32)
    rank = jnp.argsort(order).astype(jnp.int32)
    delta = pad_start - grp_start
    e2d = flat_e.reshape(-1, LANES)
    hit = e2d[None] == jnp.arange(n_exp, dtype=jnp.int32)[:, None, None]
    dest_flat = rank + jnp.sum(jnp.where(hit, delta[:, None, None], 0), axis=0).reshape(-1)
    last_used = jnp.maximum(pad_end[-1] // tm - 1, 0)
    blk_e = jnp.where(blk_cnt > 0, blk_e, blk_e[last_used]).astype(jnp.int32)
    blk_row = jnp.where(blk_cnt > 0, jnp.arange(n_blk, dtype=jnp.int32), last_used).astype(jnp.int32)
    return dest_flat, blk_e, blk_cnt, blk_row, n_blk * tm


def kernel(x, meta_tokens, g_mix, w_in, lam_q1, lam_k1, lam_q2, lam_k2, g_sb_out, g_da_out, w_out, g_ffn,
           w_router, b_router, w_gate_up, b_gate_up, w_down, b_down, g_final):
    b, s, d = x.shape
    depth = w_in.shape[0]
    assert depth == 1, "single-layer trunk"
    layer = 0
    sb_heads = (d // 2) // SB_HEAD_DIM
    da_heads = (d // 2) // DA_V_DIM
    sb_w = sb_heads * SB_HEAD_DIM
    da_col0 = 3 * sb_w
    da_qk_w = da_heads * 2 * DA_HEAD_DIM
    n_exp = w_router.shape[-1]
    lam_init = 0.8 - 0.6 * math.exp(-0.3 * layer)

    tm_proj, tn_proj = TM_PROJ, TN_PROJ
    tq, tk_sb, ts_da = TQ_ATTN, TK_SB, TS_DA
    tm_out = TM_OUT
    tm_moe, tf_moe = TM_MOE, TF_MOE
    tm_fin = TM_FINAL

    x2d = x.reshape(b * s, d)
    w_in_bf = w_in.reshape(w_in.shape[1:]).astype(BF16)
    g_mix2 = g_mix[layer].reshape(1, d)
    rope_cols = (da_col0, da_col0 + 2 * da_qk_w)

    pos_f = N_META + jnp.arange(s, dtype=jnp.int32)
    proj = _in_proj(x2d, g_mix2, w_in_bf, _rope_tables(pos_f), tm=tm_proj, tn=tn_proj,
                    rope_cols=rope_cols, pos_blocks=s // tm_proj)
    meta_pad = jnp.zeros((META_PAD, d), x.dtype).at[:N_META].set(meta_tokens.astype(x.dtype))
    pos_m = jnp.arange(META_PAD, dtype=jnp.int32)
    proj_meta = _in_proj(meta_pad, g_mix2, w_in_bf, _rope_tables(pos_m), tm=META_PAD, tn=tn_proj,
                         rope_cols=rope_cols, pos_blocks=1)
    proj3 = proj.reshape(b, s, -1)

    mix_sb = _sb_attention(proj3, proj_meta, g_sb_out[layer].reshape(1, -1), batch=b, seq=s, heads=sb_heads,
                           tq=tq, tk=tk_sb, hp=HEADS_PER_STEP_SB)
    lams = tuple(t[layer].reshape(1, DA_HEAD_DIM).astype(F32) for t in (lam_q1, lam_k1, lam_q2, lam_k2))
    mix_da = _da_attention(proj3, proj_meta, lams, g_da_out[layer].reshape(1, -1), batch=b, seq=s, heads=da_heads,
                           tq=TQ_DA, tkv=TKV_DA, ts=ts_da, hp=HEADS_PER_STEP_DA, col0=da_col0, lam_init=lam_init)

    w_out_bf = w_out[layer].astype(BF16)
    wr = w_router[layer]
    wr_hi = wr.astype(BF16)
    wr_lo = (wr - wr_hi.astype(F32)).astype(BF16)
    h1, u_packed, top_e, gates, tile_counts = _out_proj(
        mix_sb.reshape(b * s, -1), mix_da.reshape(b * s, -1), w_out_bf[:sb_w], w_out_bf[sb_w:], x2d,
        g_ffn[layer].reshape(1, d), wr_hi, wr_lo, b_router[layer].reshape(1, n_exp), tm=tm_out)

    counts = jnp.sum(tile_counts, axis=(0, 1)).astype(jnp.int32)
    dest_flat, blk_e, blk_cnt, blk_row, n_rows = _routing(top_e, counts, tm_moe)
    dest_km = dest_flat.reshape(b * s, TOP_K).T
    xs = _sc_scatter_rows(u_packed, dest_km, n_rows)
    rows = _moe_ffn(blk_e, blk_cnt, blk_row, xs, w_gate_up.reshape(w_gate_up.shape[1:]),
                    w_down.reshape(w_down.shape[1:]), b_gate_up[layer], b_down[layer],
                    tm=tm_moe, ts=TS_MOE, tf=tf_moe, tn=TN_MOE)
    y_km = _sc_gather_rows(rows, dest_km.reshape(-1))

    out = _final(h1, y_km, gates, g_final.reshape(1, d), tm=tm_fin)
    return out.reshape(b, s, d)
```

```python
import functools
import math

import jax
import jax.numpy as jnp
from jax import lax
from jax.experimental import pallas as pl
from jax.experimental.pallas import tpu as pltpu
from jax.experimental.pallas import tpu_sc as plsc

F32 = jnp.float32
BF16 = jnp.bfloat16

CHUNK = 64
N_META = 16
RMS_EPS = 1e-5
SB_HEAD_DIM = 128
DA_HEAD_DIM = 64
DA_V_DIM = 2 * DA_HEAD_DIM
ROPE_THETA = 500000.0
ROPE_DIM = DA_HEAD_DIM // 4
TOP_K = 4
SWIGLU_LIMIT = 7.0
SWIGLU_ALPHA = 1.702

LANES = 128
META_PAD = 128
NEG_BIG = -1e30
SB_DEAD_RUN = 110.0
VMEM_LIMIT = 56 * 1024 * 1024
SC_CORES, SC_SUBCORES = 2, 16
SC_GATHER_ROWS = 32

TM_PROJ, TN_PROJ = 1024, 1024
TQ_ATTN = 512
HEADS_PER_STEP_SB = 4
TQ_DA, TKV_DA = 1024, 512
TK_SB = 256
TS_DA = 512
HEADS_PER_STEP_DA = 2
TM_OUT = 512
TM_MOE, TS_MOE = 2176, 544
TF_MOE, TN_MOE = 256, 256
TM_FINAL = 512


def _cparams(sem):
    return pltpu.CompilerParams(dimension_semantics=sem, vmem_limit_bytes=VMEM_LIMIT)


def _in_proj_kernel(x_ref, g_ref, w_ref, c_ref, sa_ref, sb_ref, o_ref, u_scr, *, tn, rope_lo, rope_hi):
    n = pl.program_id(1)

    @pl.when(n == 0)
    def _():
        x = x_ref[...]
        ms = jnp.mean(x * x, axis=-1, keepdims=True)
        u_scr[...] = ((x * lax.rsqrt(ms + RMS_EPS)) * g_ref[...]).astype(BF16)

    acc = jnp.dot(u_scr[...], w_ref[...], preferred_element_type=F32)
    is_rope = jnp.logical_and(n >= rope_lo, n < rope_hi)
    o_ref[...] = acc.astype(BF16)

    @pl.when(is_rope)
    def _():
        for c in range(tn // LANES):
            xc = acc[:, c * LANES:(c + 1) * LANES]
            r = (xc * c_ref[...] + pltpu.roll(xc, LANES - ROPE_DIM // 2, 1) * sa_ref[...]
                 + pltpu.roll(xc, ROPE_DIM // 2, 1) * sb_ref[...])
            o_ref[:, c * LANES:(c + 1) * LANES] = r.astype(BF16)


def _rope_tables(pos):
    half = ROPE_DIM // 2
    inv_freq = ROPE_THETA ** (-(jnp.arange(half, dtype=F32) * 2.0 / ROPE_DIM))
    ang = pos.astype(F32)[:, None] * inv_freq[None, :]
    cos, sin = jnp.cos(ang), jnp.sin(ang)
    p = pos.shape[0]
    ones = jnp.ones((p, DA_HEAD_DIM - ROPE_DIM), F32)
    zeros8 = jnp.zeros((p, half), F32)
    zeros48 = jnp.zeros((p, DA_HEAD_DIM - ROPE_DIM), F32)
    c64 = jnp.concatenate([cos, cos, ones], axis=1)
    sa64 = jnp.concatenate([-sin, zeros8, zeros48], axis=1)
    sb64 = jnp.concatenate([zeros8, sin, zeros48], axis=1)
    tile2 = lambda t: jnp.concatenate([t, t], axis=1)
    return tile2(c64), tile2(sa64), tile2(sb64)


def _in_proj(x2d, g, w_bf, tables, *, tm, tn, rope_cols, pos_blocks):
    m, d = x2d.shape
    n_cols = w_bf.shape[1]
    c_t, sa_t, sb_t = tables
    kern = functools.partial(_in_proj_kernel, tn=tn, rope_lo=rope_cols[0] // tn, rope_hi=rope_cols[1] // tn)
    tab_spec = pl.BlockSpec((tm, LANES), lambda i, n: (i % pos_blocks, 0))
    return pl.pallas_call(
        kern,
        grid=(m // tm, n_cols // tn),
        in_specs=[
            pl.BlockSpec((tm, d), lambda i, n: (i, 0)),
            pl.BlockSpec((1, d), lambda i, n: (0, 0)),
            pl.BlockSpec((d, tn), lambda i, n: (0, n)),
            tab_spec, tab_spec, tab_spec,
        ],
        out_specs=pl.BlockSpec((tm, tn), lambda i, n: (i, n)),
        out_shape=jax.ShapeDtypeStruct((m, n_cols), BF16),
        scratch_shapes=[pltpu.VMEM((tm, d), BF16)],
        compiler_params=_cparams(("parallel", "arbitrary")),
        name="in_proj",
    )(x2d, g, w_bf, c_t, sa_t, sb_t)


def _dot_nt(a, b):
    return lax.dot_general(a, b, (((1,), (1,)), ((), ())), preferred_element_type=F32)


def _suffix_sum_matrix(n):
    j = lax.broadcasted_iota(jnp.int32, (2 * n, n), 0)
    s = lax.broadcasted_iota(jnp.int32, (2 * n, n), 1)
    return jnp.where(jnp.where(j >= n, j - n, j) > s, 1.0, 0.0).astype(BF16)


def _sb_block(q, kb, vb, u2, run, acc, mask, scale):
    z = _dot_nt(q, kb) * scale
    sp = jnp.maximum(z, 0.0) + jnp.log(1.0 + jnp.exp(-jnp.abs(z)))
    if mask is not None:
        sp = jnp.where(mask, sp, 0.0)
    hi = sp.astype(BF16)
    lo = (sp - hi.astype(F32)).astype(BF16)
    cs = jnp.dot(jnp.concatenate([hi, lo], axis=1), u2, preferred_element_type=F32)
    w = jnp.exp(z - sp - cs - run)
    if mask is not None:
        w = jnp.where(mask, w, 0.0)
    acc = acc + jnp.dot(w.astype(BF16), vb, preferred_element_type=F32)
    run = run + (cs[:, :1] + sp[:, :1])
    return run, acc


def _sb_kernel(q_ref, k_ref, v_ref, km_ref, vm_ref, g_ref, o_ref, *, tq, tk, hp, scale):
    i = pl.program_id(2)
    nsub = tq // tk
    hd = SB_HEAD_DIM
    u_blk = _suffix_sum_matrix(tk)
    row = lax.broadcasted_iota(jnp.int32, (tk, tk), 0)
    col = lax.broadcasted_iota(jnp.int32, (tk, tk), 1)
    diag_mask = col < row
    heads = range(hp)

    def kv(blk, h):
        s0 = pl.multiple_of(blk * tk, tk)
        return k_ref[pl.ds(s0, tk), h * hd:(h + 1) * hd], v_ref[pl.ds(s0, tk), h * hd:(h + 1) * hd]

    runs, accs = [], []
    for h in heads:
        slab_runs, slab_accs = [], []
        for sl in range(nsub):
            qs = q_ref[sl * tk:(sl + 1) * tk, h * hd:(h + 1) * hd]
            run = jnp.zeros((tk, 1), F32)
            acc = jnp.zeros((tk, hd), F32)
            for c in range(sl, -1, -1):
                kb, vb = kv(i * nsub + c, h)
                run, acc = _sb_block(qs, kb, vb, u_blk, run, acc, diag_mask if c == sl else None, scale)
            slab_runs.append(run)
            slab_accs.append(acc)
        runs.append(jnp.concatenate(slab_runs, axis=0))
        accs.append(jnp.concatenate(slab_accs, axis=0))
    q = [q_ref[:, h * hd:(h + 1) * hd] for h in heads]

    def alive_flag(rs):
        lowest = functools.reduce(jnp.minimum, [jnp.min(r) for r in rs])
        return (lowest < SB_DEAD_RUN).astype(jnp.int32)

    def cond(carry):
        jb, alive, _, _ = carry
        return jnp.logical_and(jb >= 0, alive > 0)

    def body(carry):
        jb, _, rs, as_ = carry
        out = [_sb_block(q[h], *kv(jb, h), u_blk, rs[h], as_[h], None, scale) for h in heads]
        rs = tuple(o[0] for o in out)
        return jb - 1, alive_flag(rs), rs, tuple(o[1] for o in out)

    runs, accs = tuple(runs), tuple(accs)
    _, alive, runs, accs = lax.while_loop(cond, body, (i * nsub - 1, alive_flag(runs), runs, accs))

    def meta_block():
        mcol = lax.broadcasted_iota(jnp.int32, (tq, META_PAD), 1)
        u_meta = _suffix_sum_matrix(META_PAD)
        return tuple(_sb_block(q[h], km_ref[:, h * hd:(h + 1) * hd], vm_ref[:, h * hd:(h + 1) * hd], u_meta,
                               runs[h], accs[h], mcol < N_META, scale)[1] for h in heads)

    accs = lax.cond(alive > 0, meta_block, lambda: accs)

    for h in heads:
        acc = accs[h]
        ms = jnp.mean(acc * acc, axis=-1, keepdims=True)
        gain = g_ref[:, h * hd:(h + 1) * hd]
        o_ref[:, h * hd:(h + 1) * hd] = ((acc * lax.rsqrt(ms + RMS_EPS)) * gain).astype(BF16)


def _sb_attention(proj, proj_meta, g_sb, *, batch, seq, heads, tq, tk, hp):
    hd = SB_HEAD_DIM
    wd = hp * hd
    ng = heads // hp
    kern = functools.partial(_sb_kernel, tq=tq, tk=tk, hp=hp, scale=hd ** -0.5)
    return pl.pallas_call(
        kern,
        grid=(batch, ng, seq // tq),
        in_specs=[
            pl.BlockSpec((None, tq, wd), lambda b, h, i: (b, i, h)),
            pl.BlockSpec((None, seq, wd), lambda b, h, i: (b, 0, ng + h)),
            pl.BlockSpec((None, seq, wd), lambda b, h, i: (b, 0, 2 * ng + h)),
            pl.BlockSpec((META_PAD, wd), lambda b, h, i: (0, ng + h)),
            pl.BlockSpec((META_PAD, wd), lambda b, h, i: (0, 2 * ng + h)),
            pl.BlockSpec((1, wd), lambda b, h, i: (0, h)),
        ],
        out_specs=pl.BlockSpec((None, tq, wd), lambda b, h, i: (b, i, h)),
        out_shape=jax.ShapeDtypeStruct((batch, seq, heads * hd), BF16),
        compiler_params=_cparams(("parallel", "parallel", "arbitrary")),
        name="sb_attn",
    )(proj, proj, proj, proj_meta, proj_meta, g_sb)


def _with_ones(vb):
    return jnp.concatenate([vb, jnp.ones_like(vb)], axis=1)


def _da_block(q1, q2, kb, vb1, st, mask):
    m1, a1, m2, a2 = st

    def one(qc, m, a):
        s = _dot_nt(qc, kb)
        if mask is not None:
            s = jnp.where(mask, s, NEG_BIG)
        mn = jnp.maximum(m, jnp.max(s, axis=-1, keepdims=True))
        p = jnp.exp(s - mn)
        a = jnp.exp(m - mn) * a + jnp.dot(p.astype(BF16), vb1, preferred_element_type=F32)
        return mn, a

    m1, a1 = one(q1, m1, a1)
    m2, a2 = one(q2, m2, a2)
    return m1, a1, m2, a2


def _da_kernel(lq1_ref, lk1_ref, lq2_ref, lk2_ref, q_ref, k_ref, v_ref, km_ref, vm_ref, g_ref, o_ref,
               *, tq, tkv, ts, hp, scale, lam_init):
    i = pl.program_id(2)
    nsub = tq // ts
    hd = DA_V_DIM
    heads = range(hp)
    lam = (jnp.exp(jnp.sum(lq1_ref[...] * lk1_ref[...], axis=-1, keepdims=True))
           - jnp.exp(jnp.sum(lq2_ref[...] * lk2_ref[...], axis=-1, keepdims=True)) + lam_init)

    def cols(ref, rows, h):
        return ref[rows, h * hd:(h + 1) * hd]

    lane = lax.broadcasted_iota(jnp.int32, (tq, hd), 1)
    q1, q2 = [], []
    for h in heads:
        qs = cols(q_ref, slice(None), h) * jnp.asarray(scale, BF16)
        zero = jnp.zeros_like(qs)
        q1.append(jnp.where(lane < DA_HEAD_DIM, qs, zero))
        q2.append(jnp.where(lane >= DA_HEAD_DIM, qs, zero))

    neg = jnp.full((tq, 1), NEG_BIG, F32)
    za = jnp.zeros((tq, 2 * hd), F32)
    st = tuple((neg, za, neg, za) for _ in heads)

    def body(j, carry):
        carry = list(carry)
        for c in range(tq // tkv):
            rows = pl.ds(pl.multiple_of(j * tq + c * tkv, tkv), tkv)
            for h in heads:
                carry[h] = _da_block(q1[h], q2[h], cols(k_ref, rows, h), _with_ones(cols(v_ref, rows, h)),
                                     carry[h], None)
        return tuple(carry)

    st = lax.fori_loop(0, i, body, st)

    shift = CHUNK.bit_length() - 1
    s0 = pl.multiple_of(i * tq, tq)
    for h in heads:
        outs = []
        for sl in range(nsub):
            rows = slice(sl * ts, (sl + 1) * ts)
            width = (sl + 1) * ts
            kb = jnp.concatenate([cols(km_ref, slice(None), h), cols(k_ref, pl.ds(s0, width), h)], axis=0)
            vb = jnp.concatenate([cols(vm_ref, slice(None), h), cols(v_ref, pl.ds(s0, width), h)], axis=0)
            col = lax.broadcasted_iota(jnp.int32, (ts, META_PAD + width), 1)
            row = lax.broadcasted_iota(jnp.int32, (ts, META_PAD + width), 0) + sl * ts
            frame_vis = jnp.logical_and(col >= META_PAD,
                                        jnp.right_shift(col - META_PAD, shift) <= jnp.right_shift(row, shift))
            vis = jnp.logical_or(col < N_META, frame_vis)
            sth = _da_block(q1[h][rows], q2[h][rows], kb, _with_ones(vb), tuple(t[rows] for t in st[h]), vis)
            _, a1, _, a2 = sth
            outs.append(a1[:, :hd] / a1[:, hd:] - lam * (a2[:, :hd] / a2[:, hd:]))
        o = jnp.concatenate(outs, axis=0)
        ms = jnp.mean(o * o, axis=-1, keepdims=True)
        gain = g_ref[:, h * hd:(h + 1) * hd]
        o_ref[:, h * hd:(h + 1) * hd] = (((o * lax.rsqrt(ms + RMS_EPS)) * gain) * (1.0 - lam_init)).astype(BF16)


def _da_attention(proj, proj_meta, lams, g_da, *, batch, seq, heads, tq, tkv, ts, hp, col0, lam_init):
    hd = DA_V_DIM
    wd = hp * hd
    ng = heads // hp
    qb = col0 // wd
    kb, vb = qb + ng, qb + 2 * ng
    kern = functools.partial(_da_kernel, tq=tq, tkv=tkv, ts=ts, hp=hp, scale=DA_HEAD_DIM ** -0.5,
                             lam_init=lam_init)
    lam_spec = pl.BlockSpec((1, DA_HEAD_DIM), lambda b, h, i: (0, 0))
    return pl.pallas_call(
        kern,
        grid=(batch, ng, seq // tq),
        in_specs=[
            lam_spec, lam_spec, lam_spec, lam_spec,
            pl.BlockSpec((None, tq, wd), lambda b, h, i: (b, i, qb + h)),
            pl.BlockSpec((None, seq, wd), lambda b, h, i: (b, 0, kb + h)),
            pl.BlockSpec((None, seq, wd), lambda b, h, i: (b, 0, vb + h)),
            pl.BlockSpec((META_PAD, wd), lambda b, h, i: (0, kb + h)),
            pl.BlockSpec((META_PAD, wd), lambda b, h, i: (0, vb + h)),
            pl.BlockSpec((1, wd), lambda b, h, i: (0, h)),
        ],
        out_specs=pl.BlockSpec((None, tq, wd), lambda b, h, i: (b, i, h)),
        out_shape=jax.ShapeDtypeStruct((batch, seq, heads * hd), BF16),
        compiler_params=_cparams(("parallel", "parallel", "arbitrary")),
        name="da_attn",
    )(*lams, proj, proj, proj, proj_meta, proj_meta, g_da)


def _pack_bf16_pairs(x):
    n = x.shape[1] // 2
    lo = lax.bitcast_convert_type(x[:, :n].astype(BF16).astype(F32), jnp.int32)
    hi = lax.bitcast_convert_type(x[:, n:].astype(BF16).astype(F32), jnp.int32)
    return jnp.bitwise_or(lax.shift_right_logical(lo, 16), jnp.bitwise_and(hi, jnp.int32(-65536)))


def _unpack_bf16_pairs(w):
    lo = lax.bitcast_convert_type(lax.shift_left(w, 16), F32)
    hi = lax.bitcast_convert_type(jnp.bitwise_and(w, jnp.int32(-65536)), F32)
    return jnp.concatenate([lo.astype(BF16), hi.astype(BF16)], axis=1)


def _sc_gather_rows(table, idx):
    n_idx = idx.shape[0]
    width = table.shape[1]
    n_workers = SC_CORES * SC_SUBCORES
    per_worker = n_idx // n_workers
    win = SC_GATHER_ROWS
    assert n_idx % (n_workers * win) == 0
    mesh = plsc.VectorSubcoreMesh(core_axis_name="c", subcore_axis_name="s")

    n_win = per_worker // win
    assert n_win % 2 == 0
    buf = lambda: pltpu.VMEM((win, width), table.dtype)
    dma = pltpu.SemaphoreType.DMA

    @functools.partial(
        pl.kernel, mesh=mesh,
        out_type=jax.ShapeDtypeStruct((n_idx, width), table.dtype),
        scratch_types=[pltpu.VMEM((n_win, win), jnp.int32), buf(), buf(), dma, dma, dma, dma],
    )
    def gather(table_hbm, idx_hbm, out_hbm, idx_v, rows0, rows1, gsem0, gsem1, wsem0, wsem1):
        worker = lax.axis_index("s") * SC_CORES + lax.axis_index("c")
        base = worker * per_worker
        pltpu.sync_copy(idx_hbm.at[pl.ds(worker * n_win, n_win)], idx_v)

        def gather_copy(t, rows_v, sem):
            return pltpu.make_async_copy(table_hbm.at[idx_v.at[t]], rows_v, sem)

        def writeback(t, rows_v, sem):
            return pltpu.make_async_copy(rows_v, out_hbm.at[pl.ds(pl.multiple_of(base + t * win, win), win)], sem)

        gather_copy(0, rows0, gsem0).start()

        @pl.loop(0, n_win, step=2)
        def _(t):
            gather_copy(t, rows0, gsem0).wait()
            writeback(t, rows0, wsem0).start()

            @pl.when(t > 0)
            def _():
                writeback(t - 1, rows1, wsem1).wait()

            gather_copy(t + 1, rows1, gsem1).start()
            gather_copy(t + 1, rows1, gsem1).wait()
            writeback(t + 1, rows1, wsem1).start()
            writeback(t, rows0, wsem0).wait()

            @pl.when(t + 2 < n_win)
            def _():
                gather_copy(t + 2, rows0, gsem0).start()

        writeback(n_win - 1, rows1, wsem1).wait()

    return gather(table, idx.reshape(n_idx // win, win))


def _sc_scatter_rows(src, dest_km, n_rows):
    n_tok, width = src.shape
    n_slots = dest_km.shape[0]
    n_workers = SC_CORES * SC_SUBCORES
    win = SC_GATHER_ROWS
    per_worker = n_tok // n_workers
    n_win = per_worker // win
    assert n_tok % (n_workers * win) == 0 and n_win % 2 == 0
    mesh = plsc.VectorSubcoreMesh(core_axis_name="c", subcore_axis_name="s")
    buf = lambda: pltpu.VMEM((win, width), src.dtype)
    dma = pltpu.SemaphoreType.DMA

    @functools.partial(
        pl.kernel, mesh=mesh,
        out_type=jax.ShapeDtypeStruct((n_rows, width), src.dtype),
        scratch_types=[pltpu.VMEM((n_slots, n_win, win), jnp.int32), buf(), buf(), dma, dma, dma, dma],
    )
    def scatter(src_hbm, idx_hbm, out_hbm, idx_v, rows0, rows1, lsem0, lsem1, ssem0, ssem1):
        worker = lax.axis_index("s") * SC_CORES + lax.axis_index("c")
        base = worker * per_worker
        for k in range(n_slots):
            pltpu.sync_copy(idx_hbm.at[k, pl.ds(worker * n_win, n_win)], idx_v.at[k])

        def load(t, rows_v, sem):
            return pltpu.make_async_copy(src_hbm.at[pl.ds(pl.multiple_of(base + t * win, win), win)], rows_v, sem)

        def start_scatters(t, rows_v, sem):
            for k in range(n_slots):
                pltpu.make_async_copy(rows_v, out_hbm.at[idx_v.at[k, t]], sem).start()

        def wait_scatters(t, rows_v, sem):
            for k in range(n_slots):
                pltpu.make_async_copy(rows_v, out_hbm.at[idx_v.at[k, t]], sem).wait()

        load(0, rows0, lsem0).start()

        @pl.loop(0, n_win, step=2)
        def _(t):
            load(t, rows0, lsem0).wait()
            start_scatters(t, rows0, ssem0)

            @pl.when(t > 0)
            def _():
                wait_scatters(t - 1, rows1, ssem1)

            load(t + 1, rows1, lsem1).start()
            load(t + 1, rows1, lsem1).wait()
            start_scatters(t + 1, rows1, ssem1)
            wait_scatters(t, rows0, ssem0)

            @pl.when(t + 2 < n_win)
            def _():
                load(t + 2, rows0, lsem0).start()

        wait_scatters(n_win - 1, rows1, ssem1)

    return scatter(src, dest_km.reshape(n_slots, n_tok // win, win))


def _split3_dot(a, b_hi, b_lo):
    a_hi = a.astype(BF16)
    a_lo = (a - a_hi.astype(F32)).astype(BF16)
    return (jnp.dot(a_hi, b_hi, preferred_element_type=F32)
            + jnp.dot(a_hi, b_lo, preferred_element_type=F32)
            + jnp.dot(a_lo, b_hi, preferred_element_type=F32))


def _out_proj_kernel(ms_ref, md_ref, ws_ref, wd_ref, x_ref, g_ref, wr_hi_ref, wr_lo_ref, br_ref,
                     h_ref, u_ref, e_ref, p_ref, c_ref, *, n_exp):
    h = (x_ref[...]
         + jnp.dot(ms_ref[...], ws_ref[...], preferred_element_type=F32)
         + jnp.dot(md_ref[...], wd_ref[...], preferred_element_type=F32))
    h_ref[...] = h
    msq = jnp.mean(h * h, axis=-1, keepdims=True)
    u = (h * lax.rsqrt(msq + RMS_EPS)) * g_ref[...]
    u_ref[...] = _pack_bf16_pairs(u)

    logits = _split3_dot(u, wr_hi_ref[...], wr_lo_ref[...]) + br_ref[...]
    lane = lax.broadcasted_iota(jnp.int32, logits.shape, 1).astype(F32)
    work = logits
    tops, idxs = [], []
    for _ in range(TOP_K):
        mx = jnp.max(work, axis=-1, keepdims=True)
        ix = jnp.min(jnp.where(work == mx, lane, float(n_exp)), axis=-1, keepdims=True)
        tops.append(mx)
        idxs.append(ix)
        work = jnp.where(lane == ix, -jnp.inf, work)
    ex = [jnp.exp(t - tops[0]) for t in tops]
    den = ex[0] + ex[1] + ex[2] + ex[3]
    kl = lax.broadcasted_iota(jnp.int32, (logits.shape[0], TOP_K), 1)
    e_out = jnp.zeros((logits.shape[0], TOP_K), F32)
    p_out = jnp.zeros((logits.shape[0], TOP_K), F32)
    for k in range(TOP_K):
        e_out = jnp.where(kl == k, idxs[k], e_out)
        p_out = jnp.where(kl == k, ex[k] / den, p_out)
    e_ref[...] = e_out.astype(jnp.int32)
    p_ref[...] = p_out
    hits = jnp.zeros(logits.shape, F32)
    for k in range(TOP_K):
        hits = hits + jnp.where(lane == idxs[k], 1.0, 0.0)
    c_ref[...] = jnp.sum(hits, axis=0, keepdims=True)


def _out_proj(mix_sb, mix_da, w_sb, w_da, x2d, g_ffn, wr_hi, wr_lo, b_r, *, tm):
    m, d = x2d.shape
    ks, kd = mix_sb.shape[1], mix_da.shape[1]
    n_exp = wr_hi.shape[1]
    kern = functools.partial(_out_proj_kernel, n_exp=n_exp)
    const = lambda shape: pl.BlockSpec(shape, lambda i: (0, 0))
    return pl.pallas_call(
        kern,
        grid=(m // tm,),
        in_specs=[
            pl.BlockSpec((tm, ks), lambda i: (i, 0)),
            pl.BlockSpec((tm, kd), lambda i: (i, 0)),
            const((ks, d)), const((kd, d)),
            pl.BlockSpec((tm, d), lambda i: (i, 0)),
            const((1, d)), const((d, n_exp)), const((d, n_exp)), const((1, n_exp)),
        ],
        out_specs=[
            pl.BlockSpec((tm, d), lambda i: (i, 0)),
            pl.BlockSpec((tm, d // 2), lambda i: (i, 0)),
            pl.BlockSpec((tm, TOP_K), lambda i: (i, 0)),
            pl.BlockSpec((tm, TOP_K), lambda i: (i, 0)),
            pl.BlockSpec((None, 1, n_exp), lambda i: (i, 0, 0)),
        ],
        out_shape=[
            jax.ShapeDtypeStruct((m, d), F32),
            jax.ShapeDtypeStruct((m, d // 2), jnp.int32),
            jax.ShapeDtypeStruct((m, TOP_K), jnp.int32),
            jax.ShapeDtypeStruct((m, TOP_K), F32),
            jax.ShapeDtypeStruct((m // tm, 1, n_exp), F32),
        ],
        compiler_params=_cparams(("parallel",)),
        name="out_proj_router",
    )(mix_sb, mix_da, w_sb, w_da, x2d, g_ffn, wr_hi, wr_lo, b_r)


def _moe_kernel(be_ref, bc_ref, rb_ref, x_ref, wg_ref, wu_ref, wda_ref, wdb_ref, bg_ref, bup_ref, bda_ref, bdb_ref,
                o_ref, act_ref, *, nf, ts):
    blk = pl.program_id(0)
    s = pl.program_id(1)
    cnt = bc_ref[blk]
    tg = 2 * ts
    tm = x_ref.shape[0]

    def row_groups(run, skip):
        for lo in range(0, tm, tg):
            size = min(tg, tm - lo)

            @pl.when(cnt > lo + size - ts)
            def _():
                run(lo, size)

            if size > ts:
                @pl.when(jnp.logical_and(cnt > lo, cnt <= lo + ts))
                def _():
                    run(lo, ts)
                    skip(lo + ts, ts)

            @pl.when(cnt <= lo)
            def _():
                skip(lo, size)

    @pl.when(jnp.logical_and(cnt > 0, s < nf))
    def _():
        slot = jnp.minimum(s, nf - 1)

        def up_proj(r0, nrows):
            x = _unpack_bf16_pairs(x_ref[r0:r0 + nrows, :])
            g = jnp.dot(x, wg_ref[...].astype(BF16), preferred_element_type=F32) + bg_ref[...]
            u = jnp.dot(x, wu_ref[...].astype(BF16), preferred_element_type=F32) + bup_ref[...]
            gate = jnp.minimum(g, SWIGLU_LIMIT)
            up = jnp.clip(u, -SWIGLU_LIMIT, SWIGLU_LIMIT)
            act = (up + 1.0) * gate * jax.nn.sigmoid(SWIGLU_ALPHA * gate)
            act_ref[slot, r0:r0 + nrows, :] = act.astype(BF16)

        row_groups(up_proj, lambda r0, nrows: None)

    @pl.when(jnp.logical_and(cnt > 0, s >= nf))
    def _():
        tf = act_ref.shape[2]

        def down_proj(r0, nrows):
            ya = bda_ref[...]
            yb = bdb_ref[...]
            for f in range(nf):
                a = act_ref[f, r0:r0 + nrows, :]
                wda = wda_ref[f * tf:(f + 1) * tf, :].astype(BF16)
                wdb = wdb_ref[f * tf:(f + 1) * tf, :].astype(BF16)
                ya = ya + jnp.dot(a, wda, preferred_element_type=F32)
                yb = yb + jnp.dot(a, wdb, preferred_element_type=F32)
            o_ref[r0:r0 + nrows, :] = _pack_bf16_pairs(jnp.concatenate([ya, yb], axis=1))

        def zero_fill(r0, nrows):
            o_ref[r0:r0 + nrows, :] = jnp.zeros((nrows, o_ref.shape[1]), o_ref.dtype)

        row_groups(down_proj, zero_fill)


def _moe_ffn(blk_e, blk_cnt, blk_row, xs, w_gu, w_dn, b_gu, b_dn, *, tm, ts, tf, tn):
    n_rows = xs.shape[0]
    n_exp, d, f2 = w_gu.shape
    d_ff = f2 // 2
    nf = d_ff // tf
    nn = d // tn
    n_blk = n_rows // tm
    b_gu3 = b_gu.reshape(n_exp, 1, f2)
    b_dn3 = b_dn.reshape(n_exp, 1, d)

    def fi(s, bc, b):
        return jnp.where(bc[b] > 0, jnp.minimum(s, nf - 1), nf - 1)

    nh = nn // 2

    def ni(s, bc, b):
        return jnp.where(bc[b] > 0, jnp.clip(s - nf, 0, nh - 1), nh - 1)

    grid_spec = pltpu.PrefetchScalarGridSpec(
        num_scalar_prefetch=3,
        grid=(n_blk, nf + nh),
        in_specs=[
            pl.BlockSpec((tm, d // 2), lambda b, s, be, bc, rb: (rb[b], 0)),
            pl.BlockSpec((None, d, tf), lambda b, s, be, bc, rb: (be[b], 0, fi(s, bc, b))),
            pl.BlockSpec((None, d, tf), lambda b, s, be, bc, rb: (be[b], 0, nf + fi(s, bc, b))),
            pl.BlockSpec((None, d_ff, tn), lambda b, s, be, bc, rb: (be[b], 0, ni(s, bc, b))),
            pl.BlockSpec((None, d_ff, tn), lambda b, s, be, bc, rb: (be[b], 0, nh + ni(s, bc, b))),
            pl.BlockSpec((None, 1, tf), lambda b, s, be, bc, rb: (be[b], 0, fi(s, bc, b))),
            pl.BlockSpec((None, 1, tf), lambda b, s, be, bc, rb: (be[b], 0, nf + fi(s, bc, b))),
            pl.BlockSpec((None, 1, tn), lambda b, s, be, bc, rb: (be[b], 0, ni(s, bc, b))),
            pl.BlockSpec((None, 1, tn), lambda b, s, be, bc, rb: (be[b], 0, nh + ni(s, bc, b))),
        ],
        out_specs=pl.BlockSpec((tm, tn), lambda b, s, be, bc, rb: (rb[b], ni(s, bc, b))),
        scratch_shapes=[pltpu.VMEM((nf, tm, tf), BF16)],
    )
    return pl.pallas_call(
        functools.partial(_moe_kernel, nf=nf, ts=ts),
        grid_spec=grid_spec,
        out_shape=jax.ShapeDtypeStruct((n_rows, d // 2), jnp.int32),
        compiler_params=_cparams(("arbitrary", "arbitrary")),
        name="moe_ffn",
    )(blk_e, blk_cnt, blk_row, xs, w_gu, w_gu, w_dn, w_dn, b_gu3, b_gu3, b_dn3, b_dn3)


def _final_kernel(h_ref, y0_ref, y1_ref, y2_ref, y3_ref, p_ref, g_ref, o_ref):
    p = p_ref[...]
    y = None
    for k, y_ref in enumerate((y0_ref, y1_ref, y2_ref, y3_ref)):
        t = _unpack_bf16_pairs(y_ref[...]).astype(F32) * p[:, k:k + 1]
        y = t if y is None else y + t
    h = h_ref[...] + y
    ms = jnp.mean(h * h, axis=-1, keepdims=True)
    o_ref[...] = (h * lax.rsqrt(ms + RMS_EPS)) * g_ref[...]


def _final(h1, y_km, gates, g, *, tm):
    m, d = h1.shape
    nb = m // tm
    assert TOP_K == 4
    y_specs = [pl.BlockSpec((tm, d // 2), functools.partial(lambda i, k: (k * nb + i, 0), k=k))
               for k in range(TOP_K)]
    return pl.pallas_call(
        _final_kernel,
        grid=(nb,),
        in_specs=([pl.BlockSpec((tm, d), lambda i: (i, 0))] + y_specs
                  + [pl.BlockSpec((tm, TOP_K), lambda i: (i, 0)), pl.BlockSpec((1, d), lambda i: (0, 0))]),
        out_specs=pl.BlockSpec((tm, d), lambda i: (i, 0)),
        out_shape=jax.ShapeDtypeStruct((m, d), F32),
        compiler_params=_cparams(("parallel",)),
        name="final_norm",
    )(h1, y_km, y_km, y_km, y_km, gates, g)


def _moe_blocks(n_assign, n_exp, tm):
    return -(-n_assign // tm) + n_exp


def _routing(top_e, counts, tm):
    n_exp = counts.shape[0]
    n_tok = top_e.shape[0]
    n_assign = n_tok * TOP_K
    flat_e = top_e.reshape(-1)
    padded = (counts + tm - 1) // tm * tm
    pad_end = jnp.cumsum(padded).astype(jnp.int32)
    pad_start = pad_end - padded
    grp_start = jnp.cumsum(counts).astype(jnp.int32) - counts
    n_blk = _moe_blocks(n_assign, n_exp, tm)
    blk_start = jnp.arange(n_blk, dtype=jnp.int32) * tm
    blk_e = jnp.minimum(jnp.sum((blk_start[:, None] >= pad_end[None, :]).astype(jnp.int32), axis=1), n_exp - 1)
    blk_off = blk_start - pad_start[blk_e]
    blk_cnt = jnp.clip(counts[blk_e] - blk_off, 0, tm).astype(jnp.int32)
    order = jnp.argsort(flat_e, stable=True).astype(jnp.int32)
    rank = jnp.argsort(order).astype(jnp.int32)
    delta = pad_start - grp_start
    e2d = flat_e.reshape(-1, LANES)
    hit = e2d[None] == jnp.arange(n_exp, dtype=jnp.int32)[:, None, None]
    dest_flat = rank + jnp.sum(jnp.where(hit, delta[:, None, None], 0), axis=0).reshape(-1)
    last_used = jnp.maximum(pad_end[-1] // tm - 1, 0)
    blk_e = jnp.where(blk_cnt > 0, blk_e, blk_e[last_used]).astype(jnp.int32)
    blk_row = jnp.where(blk_cnt > 0, jnp.arange(n_blk, dtype=jnp.int32), last_used).astype(jnp.int32)
    return dest_flat, blk_e, blk_cnt, blk_row, n_blk * tm


def kernel(x, meta_tokens, g_mix, w_in, lam_q1, lam_k1, lam_q2, lam_k2, g_sb_out, g_da_out, w_out, g_ffn,
           w_router, b_router, w_gate_up, b_gate_up, w_down, b_down, g_final):
    b, s, d = x.shape
    depth = w_in.shape[0]
    assert depth == 1, "single-layer trunk"
    layer = 0
    sb_heads = (d // 2) // SB_HEAD_DIM
    da_heads = (d // 2) // DA_V_DIM
    sb_w = sb_heads * SB_HEAD_DIM
    da_col0 = 3 * sb_w
    da_qk_w = da_heads * 2 * DA_HEAD_DIM
    n_exp = w_router.shape[-1]
    lam_init = 0.8 - 0.6 * math.exp(-0.3 * layer)

    tm_proj, tn_proj = TM_PROJ, TN_PROJ
    tq, tk_sb, ts_da = TQ_ATTN, TK_SB, TS_DA
    tm_out = TM_OUT
    tm_moe, tf_moe = TM_MOE, TF_MOE
    tm_fin = TM_FINAL

    x2d = x.reshape(b * s, d)
    w_in_bf = w_in.reshape(w_in.shape[1:]).astype(BF16)
    g_mix2 = g_mix[layer].reshape(1, d)
    rope_cols = (da_col0, da_col0 + 2 * da_qk_w)

    pos_f = N_META + jnp.arange(s, dtype=jnp.int32)
    proj = _in_proj(x2d, g_mix2, w_in_bf, _rope_tables(pos_f), tm=tm_proj, tn=tn_proj,
                    rope_cols=rope_cols, pos_blocks=s // tm_proj)
    meta_pad = jnp.zeros((META_PAD, d), x.dtype).at[:N_META].set(meta_tokens.astype(x.dtype))
    pos_m = jnp.arange(META_PAD, dtype=jnp.int32)
    proj_meta = _in_proj(meta_pad, g_mix2, w_in_bf, _rope_tables(pos_m), tm=META_PAD, tn=tn_proj,
                         rope_cols=rope_cols, pos_blocks=1)
    proj3 = proj.reshape(b, s, -1)

    mix_sb = _sb_attention(proj3, proj_meta, g_sb_out[layer].reshape(1, -1), batch=b, seq=s, heads=sb_heads,
                           tq=tq, tk=tk_sb, hp=HEADS_PER_STEP_SB)
    lams = tuple(t[layer].reshape(1, DA_HEAD_DIM).astype(F32) for t in (lam_q1, lam_k1, lam_q2, lam_k2))
    mix_da = _da_attention(proj3, proj_meta, lams, g_da_out[layer].reshape(1, -1), batch=b, seq=s, heads=da_heads,
                           tq=TQ_DA, tkv=TKV_DA, ts=ts_da, hp=HEADS_PER_STEP_DA, col0=da_col0, lam_init=lam_init)

    w_out_bf = w_out[layer].astype(BF16)
    wr = w_router[layer]
    wr_hi = wr.astype(BF16)
    wr_lo = (wr - wr_hi.astype(F32)).astype(BF16)
    h1, u_packed, top_e, gates, tile_counts = _out_proj(
        mix_sb.reshape(b * s, -1), mix_da.reshape(b * s, -1), w_out_bf[:sb_w], w_out_bf[sb_w:], x2d,
        g_ffn[layer].reshape(1, d), wr_hi, wr_lo, b_router[layer].reshape(1, n_exp), tm=tm_out)

    counts = jnp.sum(tile_counts, axis=(0, 1)).astype(jnp.int32)
    dest_flat, blk_e, blk_cnt, blk_row, n_rows = _routing(top_e, counts, tm_moe)
    dest_km = dest_flat.reshape(b * s, TOP_K).T
    xs = _sc_scatter_rows(u_packed, dest_km, n_rows)
    rows = _moe_ffn(blk_e, blk_cnt, blk_row, xs, w_gate_up.reshape(w_gate_up.shape[1:]),
                    w_down.reshape(w_down.shape[1:]), b_gate_up[layer], b_down[layer],
                    tm=tm_moe, ts=TS_MOE, tf=tf_moe, tn=TN_MOE)
    y_km = _sc_gather_rows(rows, dest_km.reshape(-1))

    out = _final(h1, y_km, gates, g_final.reshape(1, d), tm=tm_fin)
    return out.reshape(b, s, d)
```

```python
import functools
import math

import jax
import jax.numpy as jnp
from jax import lax
from jax.experimental import pallas as pl
from jax.experimental.pallas import tpu as pltpu
from jax.experimental.pallas import tpu_sc as plsc

F32 = jnp.float32
BF16 = jnp.bfloat16

CHUNK = 64
N_META = 16
RMS_EPS = 1e-5
SB_HEAD_DIM = 128
DA_HEAD_DIM = 64
DA_V_DIM = 2 * DA_HEAD_DIM
ROPE_THETA = 500000.0
ROPE_DIM = DA_HEAD_DIM // 4
TOP_K = 4
SWIGLU_LIMIT = 7.0
SWIGLU_ALPHA = 1.702

LANES = 128
META_PAD = 128
NEG_BIG = -1e30
SB_DEAD_RUN = 110.0
VMEM_LIMIT = 56 * 1024 * 1024
SC_CORES, SC_SUBCORES = 2, 16
SC_GATHER_ROWS = 32

TM_PROJ, TN_PROJ = 1024, 1024
TQ_ATTN = 512
HEADS_PER_STEP_SB = 4
TQ_DA, TKV_DA = 1024, 512
TK_SB = 128
TS_DA = 512
HEADS_PER_STEP_DA = 2
TM_OUT = 512
TM_MOE, TS_MOE = 2176, 544
TF_MOE, TN_MOE = 256, 256
TM_FINAL = 512


def _cparams(sem):
    return pltpu.CompilerParams(dimension_semantics=sem, vmem_limit_bytes=VMEM_LIMIT)


def _in_proj_kernel(x_ref, g_ref, w_ref, c_ref, sa_ref, sb_ref, o_ref, u_scr, *, tn, rope_lo, rope_hi):
    n = pl.program_id(1)

    @pl.when(n == 0)
    def _():
        x = x_ref[...]
        ms = jnp.mean(x * x, axis=-1, keepdims=True)
        u_scr[...] = ((x * lax.rsqrt(ms + RMS_EPS)) * g_ref[...]).astype(BF16)

    acc = jnp.dot(u_scr[...], w_ref[...], preferred_element_type=F32)
    is_rope = jnp.logical_and(n >= rope_lo, n < rope_hi)
    o_ref[...] = acc.astype(BF16)

    @pl.when(is_rope)
    def _():
        for c in range(tn // LANES):
            xc = acc[:, c * LANES:(c + 1) * LANES]
            r = (xc * c_ref[...] + pltpu.roll(xc, LANES - ROPE_DIM // 2, 1) * sa_ref[...]
                 + pltpu.roll(xc, ROPE_DIM // 2, 1) * sb_ref[...])
            o_ref[:, c * LANES:(c + 1) * LANES] = r.astype(BF16)


def _rope_tables(pos):
    half = ROPE_DIM // 2
    inv_freq = ROPE_THETA ** (-(jnp.arange(half, dtype=F32) * 2.0 / ROPE_DIM))
    ang = pos.astype(F32)[:, None] * inv_freq[None, :]
    cos, sin = jnp.cos(ang), jnp.sin(ang)
    p = pos.shape[0]
    ones = jnp.ones((p, DA_HEAD_DIM - ROPE_DIM), F32)
    zeros8 = jnp.zeros((p, half), F32)
    zeros48 = jnp.zeros((p, DA_HEAD_DIM - ROPE_DIM), F32)
    c64 = jnp.concatenate([cos, cos, ones], axis=1)
    sa64 = jnp.concatenate([-sin, zeros8, zeros48], axis=1)
    sb64 = jnp.concatenate([zeros8, sin, zeros48], axis=1)
    tile2 = lambda t: jnp.concatenate([t, t], axis=1)
    return tile2(c64), tile2(sa64), tile2(sb64)


def _in_proj(x2d, g, w_bf, tables, *, tm, tn, rope_cols, pos_blocks):
    m, d = x2d.shape
    n_cols = w_bf.shape[1]
    c_t, sa_t, sb_t = tables
    kern = functools.partial(_in_proj_kernel, tn=tn, rope_lo=rope_cols[0] // tn, rope_hi=rope_cols[1] // tn)
    tab_spec = pl.BlockSpec((tm, LANES), lambda i, n: (i % pos_blocks, 0))
    return pl.pallas_call(
        kern,
        grid=(m // tm, n_cols // tn),
        in_specs=[
            pl.BlockSpec((tm, d), lambda i, n: (i, 0)),
            pl.BlockSpec((1, d), lambda i, n: (0, 0)),
            pl.BlockSpec((d, tn), lambda i, n: (0, n)),
            tab_spec, tab_spec, tab_spec,
        ],
        out_specs=pl.BlockSpec((tm, tn), lambda i, n: (i, n)),
        out_shape=jax.ShapeDtypeStruct((m, n_cols), BF16),
        scratch_shapes=[pltpu.VMEM((tm, d), BF16)],
        compiler_params=_cparams(("parallel", "arbitrary")),
        name="in_proj",
    )(x2d, g, w_bf, c_t, sa_t, sb_t)


def _dot_nt(a, b):
    return lax.dot_general(a, b, (((1,), (1,)), ((), ())), preferred_element_type=F32)


def _suffix_sum_matrix(n):
    j = lax.broadcasted_iota(jnp.int32, (2 * n, n), 0)
    s = lax.broadcasted_iota(jnp.int32, (2 * n, n), 1)
    return jnp.where(jnp.where(j >= n, j - n, j) > s, 1.0, 0.0).astype(BF16)


def _sb_block(q, kb, vb, u2, run, acc, mask, scale):
    z = _dot_nt(q, kb) * scale
    sp = jnp.maximum(z, 0.0) + jnp.log(1.0 + jnp.exp(-jnp.abs(z)))
    if mask is not None:
        sp = jnp.where(mask, sp, 0.0)
    hi = sp.astype(BF16)
    lo = (sp - hi.astype(F32)).astype(BF16)
    cs = jnp.dot(jnp.concatenate([hi, lo], axis=1), u2, preferred_element_type=F32)
    w = jnp.exp(z - sp - cs - run)
    if mask is not None:
        w = jnp.where(mask, w, 0.0)
    acc = acc + jnp.dot(w.astype(BF16), vb, preferred_element_type=F32)
    run = run + (cs[:, :1] + sp[:, :1])
    return run, acc


def _sb_kernel(q_ref, k_ref, v_ref, km_ref, vm_ref, g_ref, o_ref, *, tq, tk, hp, scale):
    i = pl.program_id(2)
    nsub = tq // tk
    hd = SB_HEAD_DIM
    u_blk = _suffix_sum_matrix(tk)
    row = lax.broadcasted_iota(jnp.int32, (tk, tk), 0)
    col = lax.broadcasted_iota(jnp.int32, (tk, tk), 1)
    diag_mask = col < row
    heads = range(hp)

    def kv(blk, h):
        s0 = pl.multiple_of(blk * tk, tk)
        return k_ref[pl.ds(s0, tk), h * hd:(h + 1) * hd], v_ref[pl.ds(s0, tk), h * hd:(h + 1) * hd]

    runs, accs = [], []
    for h in heads:
        slab_runs, slab_accs = [], []
        for sl in range(nsub):
            qs = q_ref[sl * tk:(sl + 1) * tk, h * hd:(h + 1) * hd]
            run = jnp.zeros((tk, 1), F32)
            acc = jnp.zeros((tk, hd), F32)
            for c in range(sl, -1, -1):
                kb, vb = kv(i * nsub + c, h)
                run, acc = _sb_block(qs, kb, vb, u_blk, run, acc, diag_mask if c == sl else None, scale)
            slab_runs.append(run)
            slab_accs.append(acc)
        runs.append(jnp.concatenate(slab_runs, axis=0))
        accs.append(jnp.concatenate(slab_accs, axis=0))
    q = [q_ref[:, h * hd:(h + 1) * hd] for h in heads]

    def alive_flag(rs):
        lowest = functools.reduce(jnp.minimum, [jnp.min(r) for r in rs])
        return (lowest < SB_DEAD_RUN).astype(jnp.int32)

    def cond(carry):
        jb, alive, _, _ = carry
        return jnp.logical_and(jb >= 0, alive > 0)

    def body(carry):
        jb, _, rs, as_ = carry
        out = [_sb_block(q[h], *kv(jb, h), u_blk, rs[h], as_[h], None, scale) for h in heads]
        rs = tuple(o[0] for o in out)
        return jb - 1, alive_flag(rs), rs, tuple(o[1] for o in out)

    runs, accs = tuple(runs), tuple(accs)
    _, alive, runs, accs = lax.while_loop(cond, body, (i * nsub - 1, alive_flag(runs), runs, accs))

    def meta_block():
        mcol = lax.broadcasted_iota(jnp.int32, (tq, META_PAD), 1)
        u_meta = _suffix_sum_matrix(META_PAD)
        return tuple(_sb_block(q[h], km_ref[:, h * hd:(h + 1) * hd], vm_ref[:, h * hd:(h + 1) * hd], u_meta,
                               runs[h], accs[h], mcol < N_META, scale)[1] for h in heads)

    accs = lax.cond(alive > 0, meta_block, lambda: accs)

    for h in heads:
        acc = accs[h]
        ms = jnp.mean(acc * acc, axis=-1, keepdims=True)
        gain = g_ref[:, h * hd:(h + 1) * hd]
        o_ref[:, h * hd:(h + 1) * hd] = ((acc * lax.rsqrt(ms + RMS_EPS)) * gain).astype(BF16)


def _sb_attention(proj, proj_meta, g_sb, *, batch, seq, heads, tq, tk, hp):
    hd = SB_HEAD_DIM
    wd = hp * hd
    ng = heads // hp
    kern = functools.partial(_sb_kernel, tq=tq, tk=tk, hp=hp, scale=hd ** -0.5)
    return pl.pallas_call(
        kern,
        grid=(batch, ng, seq // tq),
        in_specs=[
            pl.BlockSpec((None, tq, wd), lambda b, h, i: (b, i, h)),
            pl.BlockSpec((None, seq, wd), lambda b, h, i: (b, 0, ng + h)),
            pl.BlockSpec((None, seq, wd), lambda b, h, i: (b, 0, 2 * ng + h)),
            pl.BlockSpec((META_PAD, wd), lambda b, h, i: (0, ng + h)),
            pl.BlockSpec((META_PAD, wd), lambda b, h, i: (0, 2 * ng + h)),
            pl.BlockSpec((1, wd), lambda b, h, i: (0, h)),
        ],
        out_specs=pl.BlockSpec((None, tq, wd), lambda b, h, i: (b, i, h)),
        out_shape=jax.ShapeDtypeStruct((batch, seq, heads * hd), BF16),
        compiler_params=_cparams(("parallel", "parallel", "arbitrary")),
        name="sb_attn",
    )(proj, proj, proj, proj_meta, proj_meta, g_sb)


def _with_ones(vb):
    return jnp.concatenate([vb, jnp.ones_like(vb)], axis=1)


def _da_block(q1, q2, kb, vb1, st, mask):
    m1, a1, m2, a2 = st

    def one(qc, m, a):
        s = _dot_nt(qc, kb)
        if mask is not None:
            s = jnp.where(mask, s, NEG_BIG)
        mn = jnp.maximum(m, jnp.max(s, axis=-1, keepdims=True))
        p = jnp.exp(s - mn)
        a = jnp.exp(m - mn) * a + jnp.dot(p.astype(BF16), vb1, preferred_element_type=F32)
        return mn, a

    m1, a1 = one(q1, m1, a1)
    m2, a2 = one(q2, m2, a2)
    return m1, a1, m2, a2


def _da_kernel(lq1_ref, lk1_ref, lq2_ref, lk2_ref, q_ref, k_ref, v_ref, km_ref, vm_ref, g_ref, o_ref,
               *, tq, tkv, ts, hp, scale, lam_init):
    i = pl.program_id(2)
    nsub = tq // ts
    hd = DA_V_DIM
    heads = range(hp)
    lam = (jnp.exp(jnp.sum(lq1_ref[...] * lk1_ref[...], axis=-1, keepdims=True))
           - jnp.exp(jnp.sum(lq2_ref[...] * lk2_ref[...], axis=-1, keepdims=True)) + lam_init)

    def cols(ref, rows, h):
        return ref[rows, h * hd:(h + 1) * hd]

    lane = lax.broadcasted_iota(jnp.int32, (tq, hd), 1)
    q1, q2 = [], []
    for h in heads:
        qs = cols(q_ref, slice(None), h) * jnp.asarray(scale, BF16)
        zero = jnp.zeros_like(qs)
        q1.append(jnp.where(lane < DA_HEAD_DIM, qs, zero))
        q2.append(jnp.where(lane >= DA_HEAD_DIM, qs, zero))

    neg = jnp.full((tq, 1), NEG_BIG, F32)
    za = jnp.zeros((tq, 2 * hd), F32)
    st = tuple((neg, za, neg, za) for _ in heads)

    def body(j, carry):
        carry = list(carry)
        for c in range(tq // tkv):
            rows = pl.ds(pl.multiple_of(j * tq + c * tkv, tkv), tkv)
            for h in heads:
                carry[h] = _da_block(q1[h], q2[h], cols(k_ref, rows, h), _with_ones(cols(v_ref, rows, h)),
                                     carry[h], None)
        return tuple(carry)

    st = lax.fori_loop(0, i, body, st)

    shift = CHUNK.bit_length() - 1
    s0 = pl.multiple_of(i * tq, tq)
    for h in heads:
        outs = []
        for sl in range(nsub):
            rows = slice(sl * ts, (sl + 1) * ts)
            width = (sl + 1) * ts
            kb = jnp.concatenate([cols(km_ref, slice(None), h), cols(k_ref, pl.ds(s0, width), h)], axis=0)
            vb = jnp.concatenate([cols(vm_ref, slice(None), h), cols(v_ref, pl.ds(s0, width), h)], axis=0)
            col = lax.broadcasted_iota(jnp.int32, (ts, META_PAD + width), 1)
            row = lax.broadcasted_iota(jnp.int32, (ts, META_PAD + width), 0) + sl * ts
            frame_vis = jnp.logical_and(col >= META_PAD,
                                        jnp.right_shift(col - META_PAD, shift) <= jnp.right_shift(row, shift))
            vis = jnp.logical_or(col < N_META, frame_vis)
            sth = _da_block(q1[h][rows], q2[h][rows], kb, _with_ones(vb), tuple(t[rows] for t in st[h]), vis)
            _, a1, _, a2 = sth
            outs.append(a1[:, :hd] / a1[:, hd:] - lam * (a2[:, :hd] / a2[:, hd:]))
        o = jnp.concatenate(outs, axis=0)
        ms = jnp.mean(o * o, axis=-1, keepdims=True)
        gain = g_ref[:, h * hd:(h + 1) * hd]
        o_ref[:, h * hd:(h + 1) * hd] = (((o * lax.rsqrt(ms + RMS_EPS)) * gain) * (1.0 - lam_init)).astype(BF16)


def _da_attention(proj, proj_meta, lams, g_da, *, batch, seq, heads, tq, tkv, ts, hp, col0, lam_init):
    hd = DA_V_DIM
    wd = hp * hd
    ng = heads // hp
    qb = col0 // wd
    kb, vb = qb + ng, qb + 2 * ng
    kern = functools.partial(_da_kernel, tq=tq, tkv=tkv, ts=ts, hp=hp, scale=DA_HEAD_DIM ** -0.5,
                             lam_init=lam_init)
    lam_spec = pl.BlockSpec((1, DA_HEAD_DIM), lambda b, h, i: (0, 0))
    return pl.pallas_call(
        kern,
        grid=(batch, ng, seq // tq),
        in_specs=[
            lam_spec, lam_spec, lam_spec, lam_spec,
            pl.BlockSpec((None, tq, wd), lambda b, h, i: (b, i, qb + h)),
            pl.BlockSpec((None, seq, wd), lambda b, h, i: (b, 0, kb + h)),
            pl.BlockSpec((None, seq, wd), lambda b, h, i: (b, 0, vb + h)),
            pl.BlockSpec((META_PAD, wd), lambda b, h, i: (0, kb + h)),
            pl.BlockSpec((META_PAD, wd), lambda b, h, i: (0, vb + h)),
            pl.BlockSpec((1, wd), lambda b, h, i: (0, h)),
        ],
        out_specs=pl.BlockSpec((None, tq, wd), lambda b, h, i: (b, i, h)),
        out_shape=jax.ShapeDtypeStruct((batch, seq, heads * hd), BF16),
        compiler_params=_cparams(("parallel", "parallel", "arbitrary")),
        name="da_attn",
    )(*lams, proj, proj, proj, proj_meta, proj_meta, g_da)


def _pack_bf16_pairs(x):
    n = x.shape[1] // 2
    lo = lax.bitcast_convert_type(x[:, :n].astype(BF16).astype(F32), jnp.int32)
    hi = lax.bitcast_convert_type(x[:, n:].astype(BF16).astype(F32), jnp.int32)
    return jnp.bitwise_or(lax.shift_right_logical(lo, 16), jnp.bitwise_and(hi, jnp.int32(-65536)))


def _unpack_bf16_pairs(w):
    lo = lax.bitcast_convert_type(lax.shift_left(w, 16), F32)
    hi = lax.bitcast_convert_type(jnp.bitwise_and(w, jnp.int32(-65536)), F32)
    return jnp.concatenate([lo.astype(BF16), hi.astype(BF16)], axis=1)


def _sc_gather_rows(table, idx):
    n_idx = idx.shape[0]
    width = table.shape[1]
    n_workers = SC_CORES * SC_SUBCORES
    per_worker = n_idx // n_workers
    win = SC_GATHER_ROWS
    assert n_idx % (n_workers * win) == 0
    mesh = plsc.VectorSubcoreMesh(core_axis_name="c", subcore_axis_name="s")

    n_win = per_worker // win
    assert n_win % 2 == 0
    buf = lambda: pltpu.VMEM((win, width), table.dtype)
    dma = pltpu.SemaphoreType.DMA

    @functools.partial(
        pl.kernel, mesh=mesh,
        out_type=jax.ShapeDtypeStruct((n_idx, width), table.dtype),
        scratch_types=[pltpu.VMEM((n_win, win), jnp.int32), buf(), buf(), dma, dma, dma, dma],
    )
    def gather(table_hbm, idx_hbm, out_hbm, idx_v, rows0, rows1, gsem0, gsem1, wsem0, wsem1):
        worker = lax.axis_index("s") * SC_CORES + lax.axis_index("c")
        base = worker * per_worker
        pltpu.sync_copy(idx_hbm.at[pl.ds(worker * n_win, n_win)], idx_v)

        def gather_copy(t, rows_v, sem):
            return pltpu.make_async_copy(table_hbm.at[idx_v.at[t]], rows_v, sem)

        def writeback(t, rows_v, sem):
            return pltpu.make_async_copy(rows_v, out_hbm.at[pl.ds(pl.multiple_of(base + t * win, win), win)], sem)

        gather_copy(0, rows0, gsem0).start()

        @pl.loop(0, n_win, step=2)
        def _(t):
            gather_copy(t, rows0, gsem0).wait()
            writeback(t, rows0, wsem0).start()

            @pl.when(t > 0)
            def _():
                writeback(t - 1, rows1, wsem1).wait()

            gather_copy(t + 1, rows1, gsem1).start()
            gather_copy(t + 1, rows1, gsem1).wait()
            writeback(t + 1, rows1, wsem1).start()
            writeback(t, rows0, wsem0).wait()

            @pl.when(t + 2 < n_win)
            def _():
                gather_copy(t + 2, rows0, gsem0).start()

        writeback(n_win - 1, rows1, wsem1).wait()

    return gather(table, idx.reshape(n_idx // win, win))


def _sc_scatter_rows(src, dest_km, n_rows):
    n_tok, width = src.shape
    n_slots = dest_km.shape[0]
    n_workers = SC_CORES * SC_SUBCORES
    win = SC_GATHER_ROWS
    per_worker = n_tok // n_workers
    n_win = per_worker // win
    assert n_tok % (n_workers * win) == 0 and n_win % 2 == 0
    mesh = plsc.VectorSubcoreMesh(core_axis_name="c", subcore_axis_name="s")
    buf = lambda: pltpu.VMEM((win, width), src.dtype)
    dma = pltpu.SemaphoreType.DMA

    @functools.partial(
        pl.kernel, mesh=mesh,
        out_type=jax.ShapeDtypeStruct((n_rows, width), src.dtype),
        scratch_types=[pltpu.VMEM((n_slots, n_win, win), jnp.int32), buf(), buf(), dma, dma, dma, dma],
    )
    def scatter(src_hbm, idx_hbm, out_hbm, idx_v, rows0, rows1, lsem0, lsem1, ssem0, ssem1):
        worker = lax.axis_index("s") * SC_CORES + lax.axis_index("c")
        base = worker * per_worker
        for k in range(n_slots):
            pltpu.sync_copy(idx_hbm.at[k, pl.ds(worker * n_win, n_win)], idx_v.at[k])

        def load(t, rows_v, sem):
            return pltpu.make_async_copy(src_hbm.at[pl.ds(pl.multiple_of(base + t * win, win), win)], rows_v, sem)

        def start_scatters(t, rows_v, sem):
            for k in range(n_slots):
                pltpu.make_async_copy(rows_v, out_hbm.at[idx_v.at[k, t]], sem).start()

        def wait_scatters(t, rows_v, sem):
            for k in range(n_slots):
                pltpu.make_async_copy(rows_v, out_hbm.at[idx_v.at[k, t]], sem).wait()

        load(0, rows0, lsem0).start()

        @pl.loop(0, n_win, step=2)
        def _(t):
            load(t, rows0, lsem0).wait()
            start_scatters(t, rows0, ssem0)

            @pl.when(t > 0)
            def _():
                wait_scatters(t - 1, rows1, ssem1)

            load(t + 1, rows1, lsem1).start()
            load(t + 1, rows1, lsem1).wait()
            start_scatters(t + 1, rows1, ssem1)
            wait_scatters(t, rows0, ssem0)

            @pl.when(t + 2 < n_win)
            def _():
                load(t + 2, rows0, lsem0).start()

        wait_scatters(n_win - 1, rows1, ssem1)

    return scatter(src, dest_km.reshape(n_slots, n_tok // win, win))


def _split3_dot(a, b_hi, b_lo):
    a_hi = a.astype(BF16)
    a_lo = (a - a_hi.astype(F32)).astype(BF16)
    return (jnp.dot(a_hi, b_hi, preferred_element_type=F32)
            + jnp.dot(a_hi, b_lo, preferred_element_type=F32)
            + jnp.dot(a_lo, b_hi, preferred_element_type=F32))


def _out_proj_kernel(ms_ref, md_ref, ws_ref, wd_ref, x_ref, g_ref, wr_hi_ref, wr_lo_ref, br_ref,
                     h_ref, u_ref, e_ref, p_ref, c_ref, *, n_exp):
    h = (x_ref[...]
         + jnp.dot(ms_ref[...], ws_ref[...], preferred_element_type=F32)
         + jnp.dot(md_ref[...], wd_ref[...], preferred_element_type=F32))
    h_ref[...] = h
    msq = jnp.mean(h * h, axis=-1, keepdims=True)
    u = (h * lax.rsqrt(msq + RMS_EPS)) * g_ref[...]
    u_ref[...] = _pack_bf16_pairs(u)

    logits = _split3_dot(u, wr_hi_ref[...], wr_lo_ref[...]) + br_ref[...]
    lane = lax.broadcasted_iota(jnp.int32, logits.shape, 1).astype(F32)
    work = logits
    tops, idxs = [], []
    for _ in range(TOP_K):
        mx = jnp.max(work, axis=-1, keepdims=True)
        ix = jnp.min(jnp.where(work == mx, lane, float(n_exp)), axis=-1, keepdims=True)
        tops.append(mx)
        idxs.append(ix)
        work = jnp.where(lane == ix, -jnp.inf, work)
    ex = [jnp.exp(t - tops[0]) for t in tops]
    den = ex[0] + ex[1] + ex[2] + ex[3]
    kl = lax.broadcasted_iota(jnp.int32, (logits.shape[0], TOP_K), 1)
    e_out = jnp.zeros((logits.shape[0], TOP_K), F32)
    p_out = jnp.zeros((logits.shape[0], TOP_K), F32)
    for k in range(TOP_K):
        e_out = jnp.where(kl == k, idxs[k], e_out)
        p_out = jnp.where(kl == k, ex[k] / den, p_out)
    e_ref[...] = e_out.astype(jnp.int32)
    p_ref[...] = p_out
    hits = jnp.zeros(logits.shape, F32)
    for k in range(TOP_K):
        hits = hits + jnp.where(lane == idxs[k], 1.0, 0.0)
    c_ref[...] = jnp.sum(hits, axis=0, keepdims=True)


def _out_proj(mix_sb, mix_da, w_sb, w_da, x2d, g_ffn, wr_hi, wr_lo, b_r, *, tm):
    m, d = x2d.shape
    ks, kd = mix_sb.shape[1], mix_da.shape[1]
    n_exp = wr_hi.shape[1]
    kern = functools.partial(_out_proj_kernel, n_exp=n_exp)
    const = lambda shape: pl.BlockSpec(shape, lambda i: (0, 0))
    return pl.pallas_call(
        kern,
        grid=(m // tm,),
        in_specs=[
            pl.BlockSpec((tm, ks), lambda i: (i, 0)),
            pl.BlockSpec((tm, kd), lambda i: (i, 0)),
            const((ks, d)), const((kd, d)),
            pl.BlockSpec((tm, d), lambda i: (i, 0)),
            const((1, d)), const((d, n_exp)), const((d, n_exp)), const((1, n_exp)),
        ],
        out_specs=[
            pl.BlockSpec((tm, d), lambda i: (i, 0)),
            pl.BlockSpec((tm, d // 2), lambda i: (i, 0)),
            pl.BlockSpec((tm, TOP_K), lambda i: (i, 0)),
            pl.BlockSpec((tm, TOP_K), lambda i: (i, 0)),
            pl.BlockSpec((None, 1, n_exp), lambda i: (i, 0, 0)),
        ],
        out_shape=[
            jax.ShapeDtypeStruct((m, d), F32),
            jax.ShapeDtypeStruct((m, d // 2), jnp.int32),
            jax.ShapeDtypeStruct((m, TOP_K), jnp.int32),
            jax.ShapeDtypeStruct((m, TOP_K), F32),
            jax.ShapeDtypeStruct((m // tm, 1, n_exp), F32),
        ],
        compiler_params=_cparams(("parallel",)),
        name="out_proj_router",
    )(mix_sb, mix_da, w_sb, w_da, x2d, g_ffn, wr_hi, wr_lo, b_r)


def _moe_kernel(be_ref, bc_ref, rb_ref, x_ref, wg_ref, wu_ref, wda_ref, wdb_ref, bg_ref, bup_ref, bda_ref, bdb_ref,
                o_ref, act_ref, *, nf, ts):
    blk = pl.program_id(0)
    s = pl.program_id(1)
    cnt = bc_ref[blk]
    tg = 2 * ts
    tm = x_ref.shape[0]

    def row_groups(run, skip):
        for lo in range(0, tm, tg):
            size = min(tg, tm - lo)

            @pl.when(cnt > lo + size - ts)
            def _():
                run(lo, size)

            if size > ts:
                @pl.when(jnp.logical_and(cnt > lo, cnt <= lo + ts))
                def _():
                    run(lo, ts)
                    skip(lo + ts, ts)

            @pl.when(cnt <= lo)
            def _():
                skip(lo, size)

    @pl.when(jnp.logical_and(cnt > 0, s < nf))
    def _():
        slot = jnp.minimum(s, nf - 1)

        def up_proj(r0, nrows):
            x = _unpack_bf16_pairs(x_ref[r0:r0 + nrows, :])
            g = jnp.dot(x, wg_ref[...].astype(BF16), preferred_element_type=F32) + bg_ref[...]
            u = jnp.dot(x, wu_ref[...].astype(BF16), preferred_element_type=F32) + bup_ref[...]
            gate = jnp.minimum(g, SWIGLU_LIMIT)
            up = jnp.clip(u, -SWIGLU_LIMIT, SWIGLU_LIMIT)
            act = (up + 1.0) * gate * jax.nn.sigmoid(SWIGLU_ALPHA * gate)
            act_ref[slot, r0:r0 + nrows, :] = act.astype(BF16)

        row_groups(up_proj, lambda r0, nrows: None)

    @pl.when(jnp.logical_and(cnt > 0, s >= nf))
    def _():
        tf = act_ref.shape[2]

        def down_proj(r0, nrows):
            ya = bda_ref[...]
            yb = bdb_ref[...]
            for f in range(nf):
                a = act_ref[f, r0:r0 + nrows, :]
                wda = wda_ref[f * tf:(f + 1) * tf, :].astype(BF16)
                wdb = wdb_ref[f * tf:(f + 1) * tf, :].astype(BF16)
                ya = ya + jnp.dot(a, wda, preferred_element_type=F32)
                yb = yb + jnp.dot(a, wdb, preferred_element_type=F32)
            o_ref[r0:r0 + nrows, :] = _pack_bf16_pairs(jnp.concatenate([ya, yb], axis=1))

        def zero_fill(r0, nrows):
            o_ref[r0:r0 + nrows, :] = jnp.zeros((nrows, o_ref.shape[1]), o_ref.dtype)

        row_groups(down_proj, zero_fill)


def _moe_ffn(blk_e, blk_cnt, blk_row, xs, w_gu, w_dn, b_gu, b_dn, *, tm, ts, tf, tn):
    n_rows = xs.shape[0]
    n_exp, d, f2 = w_gu.shape
    d_ff = f2 // 2
    nf = d_ff // tf
    nn = d // tn
    n_blk = n_rows // tm
    b_gu3 = b_gu.reshape(n_exp, 1, f2)
    b_dn3 = b_dn.reshape(n_exp, 1, d)

    def fi(s, bc, b):
        return jnp.where(bc[b] > 0, jnp.minimum(s, nf - 1), nf - 1)

    nh = nn // 2

    def ni(s, bc, b):
        return jnp.where(bc[b] > 0, jnp.clip(s - nf, 0, nh - 1), nh - 1)

    grid_spec = pltpu.PrefetchScalarGridSpec(
        num_scalar_prefetch=3,
        grid=(n_blk, nf + nh),
        in_specs=[
            pl.BlockSpec((tm, d // 2), lambda b, s, be, bc, rb: (rb[b], 0)),
            pl.BlockSpec((None, d, tf), lambda b, s, be, bc, rb: (be[b], 0, fi(s, bc, b))),
            pl.BlockSpec((None, d, tf), lambda b, s, be, bc, rb: (be[b], 0, nf + fi(s, bc, b))),
            pl.BlockSpec((None, d_ff, tn), lambda b, s, be, bc, rb: (be[b], 0, ni(s, bc, b))),
            pl.BlockSpec((None, d_ff, tn), lambda b, s, be, bc, rb: (be[b], 0, nh + ni(s, bc, b))),
            pl.BlockSpec((None, 1, tf), lambda b, s, be, bc, rb: (be[b], 0, fi(s, bc, b))),
            pl.BlockSpec((None, 1, tf), lambda b, s, be, bc, rb: (be[b], 0, nf + fi(s, bc, b))),
            pl.BlockSpec((None, 1, tn), lambda b, s, be, bc, rb: (be[b], 0, ni(s, bc, b))),
            pl.BlockSpec((None, 1, tn), lambda b, s, be, bc, rb: (be[b], 0, nh + ni(s, bc, b))),
        ],
        out_specs=pl.BlockSpec((tm, tn), lambda b, s, be, bc, rb: (rb[b], ni(s, bc, b))),
        scratch_shapes=[pltpu.VMEM((nf, tm, tf), BF16)],
    )
    return pl.pallas_call(
        functools.partial(_moe_kernel, nf=nf, ts=ts),
        grid_spec=grid_spec,
        out_shape=jax.ShapeDtypeStruct((n_rows, d // 2), jnp.int32),
        compiler_params=_cparams(("arbitrary", "arbitrary")),
        name="moe_ffn",
    )(blk_e, blk_cnt, blk_row, xs, w_gu, w_gu, w_dn, w_dn, b_gu3, b_gu3, b_dn3, b_dn3)


def _final_kernel(h_ref, y0_ref, y1_ref, y2_ref, y3_ref, p_ref, g_ref, o_ref):
    p = p_ref[...]
    y = None
    for k, y_ref in enumerate((y0_ref, y1_ref, y2_ref, y3_ref)):
        t = _unpack_bf16_pairs(y_ref[...]).astype(F32) * p[:, k:k + 1]
        y = t if y is None else y + t
    h = h_ref[...] + y
    ms = jnp.mean(h * h, axis=-1, keepdims=True)
    o_ref[...] = (h * lax.rsqrt(ms + RMS_EPS)) * g_ref[...]


def _final(h1, y_km, gates, g, *, tm):
    m, d = h1.shape
    nb = m // tm
    assert TOP_K == 4
    y_specs = [pl.BlockSpec((tm, d // 2), functools.partial(lambda i, k: (k * nb + i, 0), k=k))
               for k in range(TOP_K)]
    return pl.pallas_call(
        _final_kernel,
        grid=(nb,),
        in_specs=([pl.BlockSpec((tm, d), lambda i: (i, 0))] + y_specs
                  + [pl.BlockSpec((tm, TOP_K), lambda i: (i, 0)), pl.BlockSpec((1, d), lambda i: (0, 0))]),
        out_specs=pl.BlockSpec((tm, d), lambda i: (i, 0)),
        out_shape=jax.ShapeDtypeStruct((m, d), F32),
        compiler_params=_cparams(("parallel",)),
        name="final_norm",
    )(h1, y_km, y_km, y_km, y_km, gates, g)


def _moe_blocks(n_assign, n_exp, tm):
    return -(-n_assign // tm) + n_exp


def _routing(top_e, counts, tm):
    n_exp = counts.shape[0]
    n_tok = top_e.shape[0]
    n_assign = n_tok * TOP_K
    flat_e = top_e.reshape(-1)
    padded = (counts + tm - 1) // tm * tm
    pad_end = jnp.cumsum(padded).astype(jnp.int32)
    pad_start = pad_end - padded
    grp_start = jnp.cumsum(counts).astype(jnp.int32) - counts
    n_blk = _moe_blocks(n_assign, n_exp, tm)
    blk_start = jnp.arange(n_blk, dtype=jnp.int32) * tm
    blk_e = jnp.minimum(jnp.sum((blk_start[:, None] >= pad_end[None, :]).astype(jnp.int32), axis=1), n_exp - 1)
    blk_off = blk_start - pad_start[blk_e]
    blk_cnt = jnp.clip(counts[blk_e] - blk_off, 0, tm).astype(jnp.int32)
    order = jnp.argsort(flat_e, stable=True).astype(jnp.int32)
    rank = jnp.argsort(order).astype(jnp.int32)
    delta = pad_start - grp_start
    e2d = flat_e.reshape(-1, LANES)
    hit = e2d[None] == jnp.arange(n_exp, dtype=jnp.int32)[:, None, None]
    dest_flat = rank + jnp.sum(jnp.where(hit, delta[:, None, None], 0), axis=0).reshape(-1)
    last_used = jnp.maximum(pad_end[-1] // tm - 1, 0)
    blk_e = jnp.where(blk_cnt > 0, blk_e, blk_e[last_used]).astype(jnp.int32)
    blk_row = jnp.where(blk_cnt > 0, jnp.arange(n_blk, dtype=jnp.int32), last_used).astype(jnp.int32)
    return dest_flat, blk_e, blk_cnt, blk_row, n_blk * tm


def kernel(x, meta_tokens, g_mix, w_in, lam_q1, lam_k1, lam_q2, lam_k2, g_sb_out, g_da_out, w_out, g_ffn,
           w_router, b_router, w_gate_up, b_gate_up, w_down, b_down, g_final):
    b, s, d = x.shape
    depth = w_in.shape[0]
    assert depth == 1, "single-layer trunk"
    layer = 0
    sb_heads = (d // 2) // SB_HEAD_DIM
    da_heads = (d // 2) // DA_V_DIM
    sb_w = sb_heads * SB_HEAD_DIM
    da_col0 = 3 * sb_w
    da_qk_w = da_heads * 2 * DA_HEAD_DIM
    n_exp = w_router.shape[-1]
    lam_init = 0.8 - 0.6 * math.exp(-0.3 * layer)

    tm_proj, tn_proj = TM_PROJ, TN_PROJ
    tq, tk_sb, ts_da = TQ_ATTN, TK_SB, TS_DA
    tm_out = TM_OUT
    tm_moe, tf_moe = TM_MOE, TF_MOE
    tm_fin = TM_FINAL

    x2d = x.reshape(b * s, d)
    w_in_bf = w_in.reshape(w_in.shape[1:]).astype(BF16)
    g_mix2 = g_mix[layer].reshape(1, d)
    rope_cols = (da_col0, da_col0 + 2 * da_qk_w)

    pos_f = N_META + jnp.arange(s, dtype=jnp.int32)
    proj = _in_proj(x2d, g_mix2, w_in_bf, _rope_tables(pos_f), tm=tm_proj, tn=tn_proj,
                    rope_cols=rope_cols, pos_blocks=s // tm_proj)
    meta_pad = jnp.zeros((META_PAD, d), x.dtype).at[:N_META].set(meta_tokens.astype(x.dtype))
    pos_m = jnp.arange(META_PAD, dtype=jnp.int32)
    proj_meta = _in_proj(meta_pad, g_mix2, w_in_bf, _rope_tables(pos_m), tm=META_PAD, tn=tn_proj,
                         rope_cols=rope_cols, pos_blocks=1)
    proj3 = proj.reshape(b, s, -1)

    mix_sb = _sb_attention(proj3, proj_meta, g_sb_out[layer].reshape(1, -1), batch=b, seq=s, heads=sb_heads,
                           tq=tq, tk=tk_sb, hp=HEADS_PER_STEP_SB)
    lams = tuple(t[layer].reshape(1, DA_HEAD_DIM).astype(F32) for t in (lam_q1, lam_k1, lam_q2, lam_k2))
    mix_da = _da_attention(proj3, proj_meta, lams, g_da_out[layer].reshape(1, -1), batch=b, seq=s, heads=da_heads,
                           tq=TQ_DA, tkv=TKV_DA, ts=ts_da, hp=HEADS_PER_STEP_DA, col0=da_col0, lam_init=lam_init)

    w_out_bf = w_out[layer].astype(BF16)
    wr = w_router[layer]
    wr_hi = wr.astype(BF16)
    wr_lo = (wr - wr_hi.astype(F32)).astype(BF16)
    h1, u_packed, top_e, gates, tile_counts = _out_proj(
        mix_sb.reshape(b * s, -1), mix_da.reshape(b * s, -1), w_out_bf[:sb_w], w_out_bf[sb_w:], x2d,
        g_ffn[layer].reshape(1, d), wr_hi, wr_lo, b_router[layer].reshape(1, n_exp), tm=tm_out)

    counts = jnp.sum(tile_counts, axis=(0, 1)).astype(jnp.int32)
    dest_flat, blk_e, blk_cnt, blk_row, n_rows = _routing(top_e, counts, tm_moe)
    dest_km = dest_flat.reshape(b * s, TOP_K).T
    xs = _sc_scatter_rows(u_packed, dest_km, n_rows)
    rows = _moe_ffn(blk_e, blk_cnt, blk_row, xs, w_gate_up.reshape(w_gate_up.shape[1:]),
                    w_down.reshape(w_down.shape[1:]), b_gate_up[layer], b_down[layer],
                    tm=tm_moe, ts=TS_MOE, tf=tf_moe, tn=TN_MOE)
    y_km = _sc_gather_rows(rows, dest_km.reshape(-1))

    out = _final(h1, y_km, gates, g_final.reshape(1, d), tm=tm_fin)
    return out.reshape(b, s, d)
```

```python
import functools
import math

import jax
import jax.numpy as jnp
from jax import lax
from jax.experimental import pallas as pl
from jax.experimental.pallas import tpu as pltpu
from jax.experimental.pallas import tpu_sc as plsc

F32 = jnp.float32
BF16 = jnp.bfloat16

CHUNK = 64
N_META = 16
RMS_EPS = 1e-5
SB_HEAD_DIM = 128
DA_HEAD_DIM = 64
DA_V_DIM = 2 * DA_HEAD_DIM
ROPE_THETA = 500000.0
ROPE_DIM = DA_HEAD_DIM // 4
TOP_K = 4
SWIGLU_LIMIT = 7.0
SWIGLU_ALPHA = 1.702

LANES = 128
META_PAD = 128
NEG_BIG = -1e30
SB_DEAD_RUN = 110.0
VMEM_LIMIT = 56 * 1024 * 1024
SC_CORES, SC_SUBCORES = 2, 16
SC_GATHER_ROWS = 32

TM_PROJ, TN_PROJ = 1024, 1024
TQ_ATTN = 256
HEADS_PER_STEP_SB = 4
TQ_DA, TKV_DA = 1024, 512
TK_SB = 256
TS_DA = 512
HEADS_PER_STEP_DA = 2
TM_OUT = 512
TM_MOE, TS_MOE = 2176, 544
TF_MOE, TN_MOE = 256, 256
TM_FINAL = 512


def _cparams(sem):
    return pltpu.CompilerParams(dimension_semantics=sem, vmem_limit_bytes=VMEM_LIMIT)


def _in_proj_kernel(x_ref, g_ref, w_ref, c_ref, sa_ref, sb_ref, o_ref, u_scr, *, tn, rope_lo, rope_hi):
    n = pl.program_id(1)

    @pl.when(n == 0)
    def _():
        x = x_ref[...]
        ms = jnp.mean(x * x, axis=-1, keepdims=True)
        u_scr[...] = ((x * lax.rsqrt(ms + RMS_EPS)) * g_ref[...]).astype(BF16)

    acc = jnp.dot(u_scr[...], w_ref[...], preferred_element_type=F32)
    is_rope = jnp.logical_and(n >= rope_lo, n < rope_hi)
    o_ref[...] = acc.astype(BF16)

    @pl.when(is_rope)
    def _():
        for c in range(tn // LANES):
            xc = acc[:, c * LANES:(c + 1) * LANES]
            r = (xc * c_ref[...] + pltpu.roll(xc, LANES - ROPE_DIM // 2, 1) * sa_ref[...]
                 + pltpu.roll(xc, ROPE_DIM // 2, 1) * sb_ref[...])
            o_ref[:, c * LANES:(c + 1) * LANES] = r.astype(BF16)


def _rope_tables(pos):
    half = ROPE_DIM // 2
    inv_freq = ROPE_THETA ** (-(jnp.arange(half, dtype=F32) * 2.0 / ROPE_DIM))
    ang = pos.astype(F32)[:, None] * inv_freq[None, :]
    cos, sin = jnp.cos(ang), jnp.sin(ang)
    p = pos.shape[0]
    ones = jnp.ones((p, DA_HEAD_DIM - ROPE_DIM), F32)
    zeros8 = jnp.zeros((p, half), F32)
    zeros48 = jnp.zeros((p, DA_HEAD_DIM - ROPE_DIM), F32)
    c64 = jnp.concatenate([cos, cos, ones], axis=1)
    sa64 = jnp.concatenate([-sin, zeros8, zeros48], axis=1)
    sb64 = jnp.concatenate([zeros8, sin, zeros48], axis=1)
    tile2 = lambda t: jnp.concatenate([t, t], axis=1)
    return tile2(c64), tile2(sa64), tile2(sb64)


def _in_proj(x2d, g, w_bf, tables, *, tm, tn, rope_cols, pos_blocks):
    m, d = x2d.shape
    n_cols = w_bf.shape[1]
    c_t, sa_t, sb_t = tables
    kern = functools.partial(_in_proj_kernel, tn=tn, rope_lo=rope_cols[0] // tn, rope_hi=rope_cols[1] // tn)
    tab_spec = pl.BlockSpec((tm, LANES), lambda i, n: (i % pos_blocks, 0))
    return pl.pallas_call(
        kern,
        grid=(m // tm, n_cols // tn),
        in_specs=[
            pl.BlockSpec((tm, d), lambda i, n: (i, 0)),
            pl.BlockSpec((1, d), lambda i, n: (0, 0)),
            pl.BlockSpec((d, tn), lambda i, n: (0, n)),
            tab_spec, tab_spec, tab_spec,
        ],
        out_specs=pl.BlockSpec((tm, tn), lambda i, n: (i, n)),
        out_shape=jax.ShapeDtypeStruct((m, n_cols), BF16),
        scratch_shapes=[pltpu.VMEM((tm, d), BF16)],
        compiler_params=_cparams(("parallel", "arbitrary")),
        name="in_proj",
    )(x2d, g, w_bf, c_t, sa_t, sb_t)


def _dot_nt(a, b):
    return lax.dot_general(a, b, (((1,), (1,)), ((), ())), preferred_element_type=F32)


def _suffix_sum_matrix(n):
    j = lax.broadcasted_iota(jnp.int32, (2 * n, n), 0)
    s = lax.broadcasted_iota(jnp.int32, (2 * n, n), 1)
    return jnp.where(jnp.where(j >= n, j - n, j) > s, 1.0, 0.0).astype(BF16)


def _sb_block(q, kb, vb, u2, run, acc, mask, scale):
    z = _dot_nt(q, kb) * scale
    sp = jnp.maximum(z, 0.0) + jnp.log(1.0 + jnp.exp(-jnp.abs(z)))
    if mask is not None:
        sp = jnp.where(mask, sp, 0.0)
    hi = sp.astype(BF16)
    lo = (sp - hi.astype(F32)).astype(BF16)
    cs = jnp.dot(jnp.concatenate([hi, lo], axis=1), u2, preferred_element_type=F32)
    w = jnp.exp(z - sp - cs - run)
    if mask is not None:
        w = jnp.where(mask, w, 0.0)
    acc = acc + jnp.dot(w.astype(BF16), vb, preferred_element_type=F32)
    run = run + (cs[:, :1] + sp[:, :1])
    return run, acc


def _sb_kernel(q_ref, k_ref, v_ref, km_ref, vm_ref, g_ref, o_ref, *, tq, tk, hp, scale):
    i = pl.program_id(2)
    nsub = tq // tk
    hd = SB_HEAD_DIM
    u_blk = _suffix_sum_matrix(tk)
    row = lax.broadcasted_iota(jnp.int32, (tk, tk), 0)
    col = lax.broadcasted_iota(jnp.int32, (tk, tk), 1)
    diag_mask = col < row
    heads = range(hp)

    def kv(blk, h):
        s0 = pl.multiple_of(blk * tk, tk)
        return k_ref[pl.ds(s0, tk), h * hd:(h + 1) * hd], v_ref[pl.ds(s0, tk), h * hd:(h + 1) * hd]

    runs, accs = [], []
    for h in heads:
        slab_runs, slab_accs = [], []
        for sl in range(nsub):
            qs = q_ref[sl * tk:(sl + 1) * tk, h * hd:(h + 1) * hd]
            run = jnp.zeros((tk, 1), F32)
            acc = jnp.zeros((tk, hd), F32)
            for c in range(sl, -1, -1):
                kb, vb = kv(i * nsub + c, h)
                run, acc = _sb_block(qs, kb, vb, u_blk, run, acc, diag_mask if c == sl else None, scale)
            slab_runs.append(run)
            slab_accs.append(acc)
        runs.append(jnp.concatenate(slab_runs, axis=0))
        accs.append(jnp.concatenate(slab_accs, axis=0))
    q = [q_ref[:, h * hd:(h + 1) * hd] for h in heads]

    def alive_flag(rs):
        lowest = functools.reduce(jnp.minimum, [jnp.min(r) for r in rs])
        return (lowest < SB_DEAD_RUN).astype(jnp.int32)

    def cond(carry):
        jb, alive, _, _ = carry
        return jnp.logical_and(jb >= 0, alive > 0)

    def body(carry):
        jb, _, rs, as_ = carry
        out = [_sb_block(q[h], *kv(jb, h), u_blk, rs[h], as_[h], None, scale) for h in heads]
        rs = tuple(o[0] for o in out)
        return jb - 1, alive_flag(rs), rs, tuple(o[1] for o in out)

    runs, accs = tuple(runs), tuple(accs)
    _, alive, runs, accs = lax.while_loop(cond, body, (i * nsub - 1, alive_flag(runs), runs, accs))

    def meta_block():
        mcol = lax.broadcasted_iota(jnp.int32, (tq, META_PAD), 1)
        u_meta = _suffix_sum_matrix(META_PAD)
        return tuple(_sb_block(q[h], km_ref[:, h * hd:(h + 1) * hd], vm_ref[:, h * hd:(h + 1) * hd], u_meta,
                               runs[h], accs[h], mcol < N_META, scale)[1] for h in heads)

    accs = lax.cond(alive > 0, meta_block, lambda: accs)

    for h in heads:
        acc = accs[h]
        ms = jnp.mean(acc * acc, axis=-1, keepdims=True)
        gain = g_ref[:, h * hd:(h + 1) * hd]
        o_ref[:, h * hd:(h + 1) * hd] = ((acc * lax.rsqrt(ms + RMS_EPS)) * gain).astype(BF16)


def _sb_attention(proj, proj_meta, g_sb, *, batch, seq, heads, tq, tk, hp):
    hd = SB_HEAD_DIM
    wd = hp * hd
    ng = heads // hp
    kern = functools.partial(_sb_kernel, tq=tq, tk=tk, hp=hp, scale=hd ** -0.5)
    return pl.pallas_call(
        kern,
        grid=(batch, ng, seq // tq),
        in_specs=[
            pl.BlockSpec((None, tq, wd), lambda b, h, i: (b, i, h)),
            pl.BlockSpec((None, seq, wd), lambda b, h, i: (b, 0, ng + h)),
            pl.BlockSpec((None, seq, wd), lambda b, h, i: (b, 0, 2 * ng + h)),
            pl.BlockSpec((META_PAD, wd), lambda b, h, i: (0, ng + h)),
            pl.BlockSpec((META_PAD, wd), lambda b, h, i: (0, 2 * ng + h)),
            pl.BlockSpec((1, wd), lambda b, h, i: (0, h)),
        ],
        out_specs=pl.BlockSpec((None, tq, wd), lambda b, h, i: (b, i, h)),
        out_shape=jax.ShapeDtypeStruct((batch, seq, heads * hd), BF16),
        compiler_params=_cparams(("parallel", "parallel", "arbitrary")),
        name="sb_attn",
    )(proj, proj, proj, proj_meta, proj_meta, g_sb)


def _with_ones(vb):
    return jnp.concatenate([vb, jnp.ones_like(vb)], axis=1)


def _da_block(q1, q2, kb, vb1, st, mask):
    m1, a1, m2, a2 = st

    def one(qc, m, a):
        s = _dot_nt(qc, kb)
        if mask is not None:
            s = jnp.where(mask, s, NEG_BIG)
        mn = jnp.maximum(m, jnp.max(s, axis=-1, keepdims=True))
        p = jnp.exp(s - mn)
        a = jnp.exp(m - mn) * a + jnp.dot(p.astype(BF16), vb1, preferred_element_type=F32)
        return mn, a

    m1, a1 = one(q1, m1, a1)
    m2, a2 = one(q2, m2, a2)
    return m1, a1, m2, a2


def _da_kernel(lq1_ref, lk1_ref, lq2_ref, lk2_ref, q_ref, k_ref, v_ref, km_ref, vm_ref, g_ref, o_ref,
               *, tq, tkv, ts, hp, scale, lam_init):
    i = pl.program_id(2)
    nsub = tq // ts
    hd = DA_V_DIM
    heads = range(hp)
    lam = (jnp.exp(jnp.sum(lq1_ref[...] * lk1_ref[...], axis=-1, keepdims=True))
           - jnp.exp(jnp.sum(lq2_ref[...] * lk2_ref[...], axis=-1, keepdims=True)) + lam_init)

    def cols(ref, rows, h):
        return ref[rows, h * hd:(h + 1) * hd]

    lane = lax.broadcasted_iota(jnp.int32, (tq, hd), 1)
    q1, q2 = [], []
    for h in heads:
        qs = cols(q_ref, slice(None), h) * jnp.asarray(scale, BF16)
        zero = jnp.zeros_like(qs)
        q1.append(jnp.where(lane < DA_HEAD_DIM, qs, zero))
        q2.append(jnp.where(lane >= DA_HEAD_DIM, qs, zero))

    neg = jnp.full((tq, 1), NEG_BIG, F32)
    za = jnp.zeros((tq, 2 * hd), F32)
    st = tuple((neg, za, neg, za) for _ in heads)

    def body(j, carry):
        carry = list(carry)
        for c in range(tq // tkv):
            rows = pl.ds(pl.multiple_of(j * tq + c * tkv, tkv), tkv)
            for h in heads:
                carry[h] = _da_block(q1[h], q2[h], cols(k_ref, rows, h), _with_ones(cols(v_ref, rows, h)),
                                     carry[h], None)
        return tuple(carry)

    st = lax.fori_loop(0, i, body, st)

    shift = CHUNK.bit_length() - 1
    s0 = pl.multiple_of(i * tq, tq)
    for h in heads:
        outs = []
        for sl in range(nsub):
            rows = slice(sl * ts, (sl + 1) * ts)
            width = (sl + 1) * ts
            kb = jnp.concatenate([cols(km_ref, slice(None), h), cols(k_ref, pl.ds(s0, width), h)], axis=0)
            vb = jnp.concatenate([cols(vm_ref, slice(None), h), cols(v_ref, pl.ds(s0, width), h)], axis=0)
            col = lax.broadcasted_iota(jnp.int32, (ts, META_PAD + width), 1)
            row = lax.broadcasted_iota(jnp.int32, (ts, META_PAD + width), 0) + sl * ts
            frame_vis = jnp.logical_and(col >= META_PAD,
                                        jnp.right_shift(col - META_PAD, shift) <= jnp.right_shift(row, shift))
            vis = jnp.logical_or(col < N_META, frame_vis)
            sth = _da_block(q1[h][rows], q2[h][rows], kb, _with_ones(vb), tuple(t[rows] for t in st[h]), vis)
            _, a1, _, a2 = sth
            outs.append(a1[:, :hd] / a1[:, hd:] - lam * (a2[:, :hd] / a2[:, hd:]))
        o = jnp.concatenate(outs, axis=0)
        ms = jnp.mean(o * o, axis=-1, keepdims=True)
        gain = g_ref[:, h * hd:(h + 1) * hd]
        o_ref[:, h * hd:(h + 1) * hd] = (((o * lax.rsqrt(ms + RMS_EPS)) * gain) * (1.0 - lam_init)).astype(BF16)


def _da_attention(proj, proj_meta, lams, g_da, *, batch, seq, heads, tq, tkv, ts, hp, col0, lam_init):
    hd = DA_V_DIM
    wd = hp * hd
    ng = heads // hp
    qb = col0 // wd
    kb, vb = qb + ng, qb + 2 * ng
    kern = functools.partial(_da_kernel, tq=tq, tkv=tkv, ts=ts, hp=hp, scale=DA_HEAD_DIM ** -0.5,
                             lam_init=lam_init)
    lam_spec = pl.BlockSpec((1, DA_HEAD_DIM), lambda b, h, i: (0, 0))
    return pl.pallas_call(
        kern,
        grid=(batch, ng, seq // tq),
        in_specs=[
            lam_spec, lam_spec, lam_spec, lam_spec,
            pl.BlockSpec((None, tq, wd), lambda b, h, i: (b, i, qb + h)),
            pl.BlockSpec((None, seq, wd), lambda b, h, i: (b, 0, kb + h)),
            pl.BlockSpec((None, seq, wd), lambda b, h, i: (b, 0, vb + h)),
            pl.BlockSpec((META_PAD, wd), lambda b, h, i: (0, kb + h)),
            pl.BlockSpec((META_PAD, wd), lambda b, h, i: (0, vb + h)),
            pl.BlockSpec((1, wd), lambda b, h, i: (0, h)),
        ],
        out_specs=pl.BlockSpec((None, tq, wd), lambda b, h, i: (b, i, h)),
        out_shape=jax.ShapeDtypeStruct((batch, seq, heads * hd), BF16),
        compiler_params=_cparams(("parallel", "parallel", "arbitrary")),
        name="da_attn",
    )(*lams, proj, proj, proj, proj_meta, proj_meta, g_da)


def _pack_bf16_pairs(x):
    n = x.shape[1] // 2
    lo = lax.bitcast_convert_type(x[:, :n].astype(BF16).astype(F32), jnp.int32)
    hi = lax.bitcast_convert_type(x[:, n:].astype(BF16).astype(F32), jnp.int32)
    return jnp.bitwise_or(lax.shift_right_logical(lo, 16), jnp.bitwise_and(hi, jnp.int32(-65536)))


def _unpack_bf16_pairs(w):
    lo = lax.bitcast_convert_type(lax.shift_left(w, 16), F32)
    hi = lax.bitcast_convert_type(jnp.bitwise_and(w, jnp.int32(-65536)), F32)
    return jnp.concatenate([lo.astype(BF16), hi.astype(BF16)], axis=1)


def _sc_gather_rows(table, idx):
    n_idx = idx.shape[0]
    width = table.shape[1]
    n_workers = SC_CORES * SC_SUBCORES
    per_worker = n_idx // n_workers
    win = SC_GATHER_ROWS
    assert n_idx % (n_workers * win) == 0
    mesh = plsc.VectorSubcoreMesh(core_axis_name="c", subcore_axis_name="s")

    n_win = per_worker // win
    assert n_win % 2 == 0
    buf = lambda: pltpu.VMEM((win, width), table.dtype)
    dma = pltpu.SemaphoreType.DMA

    @functools.partial(
        pl.kernel, mesh=mesh,
        out_type=jax.ShapeDtypeStruct((n_idx, width), table.dtype),
        scratch_types=[pltpu.VMEM((n_win, win), jnp.int32), buf(), buf(), dma, dma, dma, dma],
    )
    def gather(table_hbm, idx_hbm, out_hbm, idx_v, rows0, rows1, gsem0, gsem1, wsem0, wsem1):
        worker = lax.axis_index("s") * SC_CORES + lax.axis_index("c")
        base = worker * per_worker
        pltpu.sync_copy(idx_hbm.at[pl.ds(worker * n_win, n_win)], idx_v)

        def gather_copy(t, rows_v, sem):
            return pltpu.make_async_copy(table_hbm.at[idx_v.at[t]], rows_v, sem)

        def writeback(t, rows_v, sem):
            return pltpu.make_async_copy(rows_v, out_hbm.at[pl.ds(pl.multiple_of(base + t * win, win), win)], sem)

        gather_copy(0, rows0, gsem0).start()

        @pl.loop(0, n_win, step=2)
        def _(t):
            gather_copy(t, rows0, gsem0).wait()
            writeback(t, rows0, wsem0).start()

            @pl.when(t > 0)
            def _():
                writeback(t - 1, rows1, wsem1).wait()

            gather_copy(t + 1, rows1, gsem1).start()
            gather_copy(t + 1, rows1, gsem1).wait()
            writeback(t + 1, rows1, wsem1).start()
            writeback(t, rows0, wsem0).wait()

            @pl.when(t + 2 < n_win)
            def _():
                gather_copy(t + 2, rows0, gsem0).start()

        writeback(n_win - 1, rows1, wsem1).wait()

    return gather(table, idx.reshape(n_idx // win, win))


def _sc_scatter_rows(src, dest_km, n_rows):
    n_tok, width = src.shape
    n_slots = dest_km.shape[0]
    n_workers = SC_CORES * SC_SUBCORES
    win = SC_GATHER_ROWS
    per_worker = n_tok // n_workers
    n_win = per_worker // win
    assert n_tok % (n_workers * win) == 0 and n_win % 2 == 0
    mesh = plsc.VectorSubcoreMesh(core_axis_name="c", subcore_axis_name="s")
    buf = lambda: pltpu.VMEM((win, width), src.dtype)
    dma = pltpu.SemaphoreType.DMA

    @functools.partial(
        pl.kernel, mesh=mesh,
        out_type=jax.ShapeDtypeStruct((n_rows, width), src.dtype),
        scratch_types=[pltpu.VMEM((n_slots, n_win, win), jnp.int32), buf(), buf(), dma, dma, dma, dma],
    )
    def scatter(src_hbm, idx_hbm, out_hbm, idx_v, rows0, rows1, lsem0, lsem1, ssem0, ssem1):
        worker = lax.axis_index("s") * SC_CORES + lax.axis_index("c")
        base = worker * per_worker
        for k in range(n_slots):
            pltpu.sync_copy(idx_hbm.at[k, pl.ds(worker * n_win, n_win)], idx_v.at[k])

        def load(t, rows_v, sem):
            return pltpu.make_async_copy(src_hbm.at[pl.ds(pl.multiple_of(base + t * win, win), win)], rows_v, sem)

        def start_scatters(t, rows_v, sem):
            for k in range(n_slots):
                pltpu.make_async_copy(rows_v, out_hbm.at[idx_v.at[k, t]], sem).start()

        def wait_scatters(t, rows_v, sem):
            for k in range(n_slots):
                pltpu.make_async_copy(rows_v, out_hbm.at[idx_v.at[k, t]], sem).wait()

        load(0, rows0, lsem0).start()

        @pl.loop(0, n_win, step=2)
        def _(t):
            load(t, rows0, lsem0).wait()
            start_scatters(t, rows0, ssem0)

            @pl.when(t > 0)
            def _():
                wait_scatters(t - 1, rows1, ssem1)

            load(t + 1, rows1, lsem1).start()
            load(t + 1, rows1, lsem1).wait()
            start_scatters(t + 1, rows1, ssem1)
            wait_scatters(t, rows0, ssem0)

            @pl.when(t + 2 < n_win)
            def _():
                load(t + 2, rows0, lsem0).start()

        wait_scatters(n_win - 1, rows1, ssem1)

    return scatter(src, dest_km.reshape(n_slots, n_tok // win, win))


def _split3_dot(a, b_hi, b_lo):
    a_hi = a.astype(BF16)
    a_lo = (a - a_hi.astype(F32)).astype(BF16)
    return (jnp.dot(a_hi, b_hi, preferred_element_type=F32)
            + jnp.dot(a_hi, b_lo, preferred_element_type=F32)
            + jnp.dot(a_lo, b_hi, preferred_element_type=F32))


def _out_proj_kernel(ms_ref, md_ref, ws_ref, wd_ref, x_ref, g_ref, wr_hi_ref, wr_lo_ref, br_ref,
                     h_ref, u_ref, e_ref, p_ref, c_ref, *, n_exp):
    h = (x_ref[...]
         + jnp.dot(ms_ref[...], ws_ref[...], preferred_element_type=F32)
         + jnp.dot(md_ref[...], wd_ref[...], preferred_element_type=F32))
    h_ref[...] = h
    msq = jnp.mean(h * h, axis=-1, keepdims=True)
    u = (h * lax.rsqrt(msq + RMS_EPS)) * g_ref[...]
    u_ref[...] = _pack_bf16_pairs(u)

    logits = _split3_dot(u, wr_hi_ref[...], wr_lo_ref[...]) + br_ref[...]
    lane = lax.broadcasted_iota(jnp.int32, logits.shape, 1).astype(F32)
    work = logits
    tops, idxs = [], []
    for _ in range(TOP_K):
        mx = jnp.max(work, axis=-1, keepdims=True)
        ix = jnp.min(jnp.where(work == mx, lane, float(n_exp)), axis=-1, keepdims=True)
        tops.append(mx)
        idxs.append(ix)
        work = jnp.where(lane == ix, -jnp.inf, work)
    ex = [jnp.exp(t - tops[0]) for t in tops]
    den = ex[0] + ex[1] + ex[2] + ex[3]
    kl = lax.broadcasted_iota(jnp.int32, (logits.shape[0], TOP_K), 1)
    e_out = jnp.zeros((logits.shape[0], TOP_K), F32)
    p_out = jnp.zeros((logits.shape[0], TOP_K), F32)
    for k in range(TOP_K):
        e_out = jnp.where(kl == k, idxs[k], e_out)
        p_out = jnp.where(kl == k, ex[k] / den, p_out)
    e_ref[...] = e_out.astype(jnp.int32)
    p_ref[...] = p_out
    hits = jnp.zeros(logits.shape, F32)
    for k in range(TOP_K):
        hits = hits + jnp.where(lane == idxs[k], 1.0, 0.0)
    c_ref[...] = jnp.sum(hits, axis=0, keepdims=True)


def _out_proj(mix_sb, mix_da, w_sb, w_da, x2d, g_ffn, wr_hi, wr_lo, b_r, *, tm):
    m, d = x2d.shape
    ks, kd = mix_sb.shape[1], mix_da.shape[1]
    n_exp = wr_hi.shape[1]
    kern = functools.partial(_out_proj_kernel, n_exp=n_exp)
    const = lambda shape: pl.BlockSpec(shape, lambda i: (0, 0))
    return pl.pallas_call(
        kern,
        grid=(m // tm,),
        in_specs=[
            pl.BlockSpec((tm, ks), lambda i: (i, 0)),
            pl.BlockSpec((tm, kd), lambda i: (i, 0)),
            const((ks, d)), const((kd, d)),
            pl.BlockSpec((tm, d), lambda i: (i, 0)),
            const((1, d)), const((d, n_exp)), const((d, n_exp)), const((1, n_exp)),
        ],
        out_specs=[
            pl.BlockSpec((tm, d), lambda i: (i, 0)),
            pl.BlockSpec((tm, d // 2), lambda i: (i, 0)),
            pl.BlockSpec((tm, TOP_K), lambda i: (i, 0)),
            pl.BlockSpec((tm, TOP_K), lambda i: (i, 0)),
            pl.BlockSpec((None, 1, n_exp), lambda i: (i, 0, 0)),
        ],
        out_shape=[
            jax.ShapeDtypeStruct((m, d), F32),
            jax.ShapeDtypeStruct((m, d // 2), jnp.int32),
            jax.ShapeDtypeStruct((m, TOP_K), jnp.int32),
            jax.ShapeDtypeStruct((m, TOP_K), F32),
            jax.ShapeDtypeStruct((m // tm, 1, n_exp), F32),
        ],
        compiler_params=_cparams(("parallel",)),
        name="out_proj_router",
    )(mix_sb, mix_da, w_sb, w_da, x2d, g_ffn, wr_hi, wr_lo, b_r)


def _moe_kernel(be_ref, bc_ref, rb_ref, x_ref, wg_ref, wu_ref, wda_ref, wdb_ref, bg_ref, bup_ref, bda_ref, bdb_ref,
                o_ref, act_ref, *, nf, ts):
    blk = pl.program_id(0)
    s = pl.program_id(1)
    cnt = bc_ref[blk]
    tg = 2 * ts
    tm = x_ref.shape[0]

    def row_groups(run, skip):
        for lo in range(0, tm, tg):
            size = min(tg, tm - lo)

            @pl.when(cnt > lo + size - ts)
            def _():
                run(lo, size)

            if size > ts:
                @pl.when(jnp.logical_and(cnt > lo, cnt <= lo + ts))
                def _():
                    run(lo, ts)
                    skip(lo + ts, ts)

            @pl.when(cnt <= lo)
            def _():
                skip(lo, size)

    @pl.when(jnp.logical_and(cnt > 0, s < nf))
    def _():
        slot = jnp.minimum(s, nf - 1)

        def up_proj(r0, nrows):
            x = _unpack_bf16_pairs(x_ref[r0:r0 + nrows, :])
            g = jnp.dot(x, wg_ref[...].astype(BF16), preferred_element_type=F32) + bg_ref[...]
            u = jnp.dot(x, wu_ref[...].astype(BF16), preferred_element_type=F32) + bup_ref[...]
            gate = jnp.minimum(g, SWIGLU_LIMIT)
            up = jnp.clip(u, -SWIGLU_LIMIT, SWIGLU_LIMIT)
            act = (up + 1.0) * gate * jax.nn.sigmoid(SWIGLU_ALPHA * gate)
            act_ref[slot, r0:r0 + nrows, :] = act.astype(BF16)

        row_groups(up_proj, lambda r0, nrows: None)

    @pl.when(jnp.logical_and(cnt > 0, s >= nf))
    def _():
        tf = act_ref.shape[2]

        def down_proj(r0, nrows):
            ya = bda_ref[...]
            yb = bdb_ref[...]
            for f in range(nf):
                a = act_ref[f, r0:r0 + nrows, :]
                wda = wda_ref[f * tf:(f + 1) * tf, :].astype(BF16)
                wdb = wdb_ref[f * tf:(f + 1) * tf, :].astype(BF16)
                ya = ya + jnp.dot(a, wda, preferred_element_type=F32)
                yb = yb + jnp.dot(a, wdb, preferred_element_type=F32)
            o_ref[r0:r0 + nrows, :] = _pack_bf16_pairs(jnp.concatenate([ya, yb], axis=1))

        def zero_fill(r0, nrows):
            o_ref[r0:r0 + nrows, :] = jnp.zeros((nrows, o_ref.shape[1]), o_ref.dtype)

        row_groups(down_proj, zero_fill)


def _moe_ffn(blk_e, blk_cnt, blk_row, xs, w_gu, w_dn, b_gu, b_dn, *, tm, ts, tf, tn):
    n_rows = xs.shape[0]
    n_exp, d, f2 = w_gu.shape
    d_ff = f2 // 2
    nf = d_ff // tf
    nn = d // tn
    n_blk = n_rows // tm
    b_gu3 = b_gu.reshape(n_exp, 1, f2)
    b_dn3 = b_dn.reshape(n_exp, 1, d)

    def fi(s, bc, b):
        return jnp.where(bc[b] > 0, jnp.minimum(s, nf - 1), nf - 1)

    nh = nn // 2

    def ni(s, bc, b):
        return jnp.where(bc[b] > 0, jnp.clip(s - nf, 0, nh - 1), nh - 1)

    grid_spec = pltpu.PrefetchScalarGridSpec(
        num_scalar_prefetch=3,
        grid=(n_blk, nf + nh),
        in_specs=[
            pl.BlockSpec((tm, d // 2), lambda b, s, be, bc, rb: (rb[b], 0)),
            pl.BlockSpec((None, d, tf), lambda b, s, be, bc, rb: (be[b], 0, fi(s, bc, b))),
            pl.BlockSpec((None, d, tf), lambda b, s, be, bc, rb: (be[b], 0, nf + fi(s, bc, b))),
            pl.BlockSpec((None, d_ff, tn), lambda b, s, be, bc, rb: (be[b], 0, ni(s, bc, b))),
            pl.BlockSpec((None, d_ff, tn), lambda b, s, be, bc, rb: (be[b], 0, nh + ni(s, bc, b))),
            pl.BlockSpec((None, 1, tf), lambda b, s, be, bc, rb: (be[b], 0, fi(s, bc, b))),
            pl.BlockSpec((None, 1, tf), lambda b, s, be, bc, rb: (be[b], 0, nf + fi(s, bc, b))),
            pl.BlockSpec((None, 1, tn), lambda b, s, be, bc, rb: (be[b], 0, ni(s, bc, b))),
            pl.BlockSpec((None, 1, tn), lambda b, s, be, bc, rb: (be[b], 0, nh + ni(s, bc, b))),
        ],
        out_specs=pl.BlockSpec((tm, tn), lambda b, s, be, bc, rb: (rb[b], ni(s, bc, b))),
        scratch_shapes=[pltpu.VMEM((nf, tm, tf), BF16)],
    )
    return pl.pallas_call(
        functools.partial(_moe_kernel, nf=nf, ts=ts),
        grid_spec=grid_spec,
        out_shape=jax.ShapeDtypeStruct((n_rows, d // 2), jnp.int32),
        compiler_params=_cparams(("arbitrary", "arbitrary")),
        name="moe_ffn",
    )(blk_e, blk_cnt, blk_row, xs, w_gu, w_gu, w_dn, w_dn, b_gu3, b_gu3, b_dn3, b_dn3)


def _final_kernel(h_ref, y0_ref, y1_ref, y2_ref, y3_ref, p_ref, g_ref, o_ref):
    p = p_ref[...]
    y = None
    for k, y_ref in enumerate((y0_ref, y1_ref, y2_ref, y3_ref)):
        t = _unpack_bf16_pairs(y_ref[...]).astype(F32) * p[:, k:k + 1]
        y = t if y is None else y + t
    h = h_ref[...] + y
    ms = jnp.mean(h * h, axis=-1, keepdims=True)
    o_ref[...] = (h * lax.rsqrt(ms + RMS_EPS)) * g_ref[...]


def _final(h1, y_km, gates, g, *, tm):
    m, d = h1.shape
    nb = m // tm
    assert TOP_K == 4
    y_specs = [pl.BlockSpec((tm, d // 2), functools.partial(lambda i, k: (k * nb + i, 0), k=k))
               for k in range(TOP_K)]
    return pl.pallas_call(
        _final_kernel,
        grid=(nb,),
        in_specs=([pl.BlockSpec((tm, d), lambda i: (i, 0))] + y_specs
                  + [pl.BlockSpec((tm, TOP_K), lambda i: (i, 0)), pl.BlockSpec((1, d), lambda i: (0, 0))]),
        out_specs=pl.BlockSpec((tm, d), lambda i: (i, 0)),
        out_shape=jax.ShapeDtypeStruct((m, d), F32),
        compiler_params=_cparams(("parallel",)),
        name="final_norm",
    )(h1, y_km, y_km, y_km, y_km, gates, g)


def _moe_blocks(n_assign, n_exp, tm):
    return -(-n_assign // tm) + n_exp


def _routing(top_e, counts, tm):
    n_exp = counts.shape[0]
    n_tok = top_e.shape[0]
    n_assign = n_tok * TOP_K
    flat_e = top_e.reshape(-1)
    padded = (counts + tm - 1) // tm * tm
    pad_end = jnp.cumsum(padded).astype(jnp.int32)
    pad_start = pad_end - padded
    grp_start = jnp.cumsum(counts).astype(jnp.int32) - counts
    n_blk = _moe_blocks(n_assign, n_exp, tm)
    blk_start = jnp.arange(n_blk, dtype=jnp.int32) * tm
    blk_e = jnp.minimum(jnp.sum((blk_start[:, None] >= pad_end[None, :]).astype(jnp.int32), axis=1), n_exp - 1)
    blk_off = blk_start - pad_start[blk_e]
    blk_cnt = jnp.clip(counts[blk_e] - blk_off, 0, tm).astype(jnp.int32)
    order = jnp.argsort(flat_e, stable=True).astype(jnp.int32)
    rank = jnp.argsort(order).astype(jnp.int32)
    delta = pad_start - grp_start
    e2d = flat_e.reshape(-1, LANES)
    hit = e2d[None] == jnp.arange(n_exp, dtype=jnp.int32)[:, None, None]
    dest_flat = rank + jnp.sum(jnp.where(hit, delta[:, None, None], 0), axis=0).reshape(-1)
    last_used = jnp.maximum(pad_end[-1] // tm - 1, 0)
    blk_e = jnp.where(blk_cnt > 0, blk_e, blk_e[last_used]).astype(jnp.int32)
    blk_row = jnp.where(blk_cnt > 0, jnp.arange(n_blk, dtype=jnp.int32), last_used).astype(jnp.int32)
    return dest_flat, blk_e, blk_cnt, blk_row, n_blk * tm


def kernel(x, meta_tokens, g_mix, w_in, lam_q1, lam_k1, lam_q2, lam_k2, g_sb_out, g_da_out, w_out, g_ffn,
           w_router, b_router, w_gate_up, b_gate_up, w_down, b_down, g_final):
    b, s, d = x.shape
    depth = w_in.shape[0]
    assert depth == 1, "single-layer trunk"
    layer = 0
    sb_heads = (d // 2) // SB_HEAD_DIM
    da_heads = (d // 2) // DA_V_DIM
    sb_w = sb_heads * SB_HEAD_DIM
    da_col0 = 3 * sb_w
    da_qk_w = da_heads * 2 * DA_HEAD_DIM
    n_exp = w_router.shape[-1]
    lam_init = 0.8 - 0.6 * math.exp(-0.3 * layer)

    tm_proj, tn_proj = TM_PROJ, TN_PROJ
    tq, tk_sb, ts_da = TQ_ATTN, TK_SB, TS_DA
    tm_out = TM_OUT
    tm_moe, tf_moe = TM_MOE, TF_MOE
    tm_fin = TM_FINAL

    x2d = x.reshape(b * s, d)
    w_in_bf = w_in.reshape(w_in.shape[1:]).astype(BF16)
    g_mix2 = g_mix[layer].reshape(1, d)
    rope_cols = (da_col0, da_col0 + 2 * da_qk_w)

    pos_f = N_META + jnp.arange(s, dtype=jnp.int32)
    proj = _in_proj(x2d, g_mix2, w_in_bf, _rope_tables(pos_f), tm=tm_proj, tn=tn_proj,
                    rope_cols=rope_cols, pos_blocks=s // tm_proj)
    meta_pad = jnp.zeros((META_PAD, d), x.dtype).at[:N_META].set(meta_tokens.astype(x.dtype))
    pos_m = jnp.arange(META_PAD, dtype=jnp.int32)
    proj_meta = _in_proj(meta_pad, g_mix2, w_in_bf, _rope_tables(pos_m), tm=META_PAD, tn=tn_proj,
                         rope_cols=rope_cols, pos_blocks=1)
    proj3 = proj.reshape(b, s, -1)

    mix_sb = _sb_attention(proj3, proj_meta, g_sb_out[layer].reshape(1, -1), batch=b, seq=s, heads=sb_heads,
                           tq=tq, tk=tk_sb, hp=HEADS_PER_STEP_SB)
    lams = tuple(t[layer].reshape(1, DA_HEAD_DIM).astype(F32) for t in (lam_q1, lam_k1, lam_q2, lam_k2))
    mix_da = _da_attention(proj3, proj_meta, lams, g_da_out[layer].reshape(1, -1), batch=b, seq=s, heads=da_heads,
                           tq=TQ_DA, tkv=TKV_DA, ts=ts_da, hp=HEADS_PER_STEP_DA, col0=da_col0, lam_init=lam_init)

    w_out_bf = w_out[layer].astype(BF16)
    wr = w_router[layer]
    wr_hi = wr.astype(BF16)
    wr_lo = (wr - wr_hi.astype(F32)).astype(BF16)
    h1, u_packed, top_e, gates, tile_counts = _out_proj(
        mix_sb.reshape(b * s, -1), mix_da.reshape(b * s, -1), w_out_bf[:sb_w], w_out_bf[sb_w:], x2d,
        g_ffn[layer].reshape(1, d), wr_hi, wr_lo, b_router[layer].reshape(1, n_exp), tm=tm_out)

    counts = jnp.sum(tile_counts, axis=(0, 1)).astype(jnp.int32)
    dest_flat, blk_e, blk_cnt, blk_row, n_rows = _routing(top_e, counts, tm_moe)
    dest_km = dest_flat.reshape(b * s, TOP_K).T
    xs = _sc_scatter_rows(u_packed, dest_km, n_rows)
    rows = _moe_ffn(blk_e, blk_cnt, blk_row, xs, w_gate_up.reshape(w_gate_up.shape[1:]),
                    w_down.reshape(w_down.shape[1:]), b_gate_up[layer], b_down[layer],
                    tm=tm_moe, ts=TS_MOE, tf=tf_moe, tn=TN_MOE)
    y_km = _sc_gather_rows(rows, dest_km.reshape(-1))

    out = _final(h1, y_km, gates, g_final.reshape(1, d), tm=tm_fin)
    return out.reshape(b, s, d)
```

```python
import functools
import math

import jax
import jax.numpy as jnp
from jax import lax
from jax.experimental import pallas as pl
from jax.experimental.pallas import tpu as pltpu
from jax.experimental.pallas import tpu_sc as plsc

F32 = jnp.float32
BF16 = jnp.bfloat16

CHUNK = 64
N_META = 16
RMS_EPS = 1e-5
SB_HEAD_DIM = 128
DA_HEAD_DIM = 64
DA_V_DIM = 2 * DA_HEAD_DIM
ROPE_THETA = 500000.0
ROPE_DIM = DA_HEAD_DIM // 4
TOP_K = 4
SWIGLU_LIMIT = 7.0
SWIGLU_ALPHA = 1.702

LANES = 128
META_PAD = 128
NEG_BIG = -1e30
SB_DEAD_RUN = 110.0
VMEM_LIMIT = 60 * 1024 * 1024
SC_CORES, SC_SUBCORES = 2, 16
SC_GATHER_ROWS = 32

TM_PROJ, TN_PROJ = 1024, 1024
TQ_ATTN = 512
HEADS_PER_STEP_SB = 4
TQ_DA, TKV_DA = 1024, 512
TK_SB = 256
TS_DA = 512
HEADS_PER_STEP_DA = 4
TM_OUT = 512
TM_MOE, TS_MOE = 2176, 544
TF_MOE, TN_MOE = 256, 256
TM_FINAL = 512


def _cparams(sem):
    return pltpu.CompilerParams(dimension_semantics=sem, vmem_limit_bytes=VMEM_LIMIT)


def _in_proj_kernel(x_ref, g_ref, w_ref, c_ref, sa_ref, sb_ref, o_ref, u_scr, *, tn, rope_lo, rope_hi):
    n = pl.program_id(1)

    @pl.when(n == 0)
    def _():
        x = x_ref[...]
        ms = jnp.mean(x * x, axis=-1, keepdims=True)
        u_scr[...] = ((x * lax.rsqrt(ms + RMS_EPS)) * g_ref[...]).astype(BF16)

    acc = jnp.dot(u_scr[...], w_ref[...], preferred_element_type=F32)
    is_rope = jnp.logical_and(n >= rope_lo, n < rope_hi)
    o_ref[...] = acc.astype(BF16)

    @pl.when(is_rope)
    def _():
        for c in range(tn // LANES):
            xc = acc[:, c * LANES:(c + 1) * LANES]
            r = (xc * c_ref[...] + pltpu.roll(xc, LANES - ROPE_DIM // 2, 1) * sa_ref[...]
                 + pltpu.roll(xc, ROPE_DIM // 2, 1) * sb_ref[...])
            o_ref[:, c * LANES:(c + 1) * LANES] = r.astype(BF16)


def _rope_tables(pos):
    half = ROPE_DIM // 2
    inv_freq = ROPE_THETA ** (-(jnp.arange(half, dtype=F32) * 2.0 / ROPE_DIM))
    ang = pos.astype(F32)[:, None] * inv_freq[None, :]
    cos, sin = jnp.cos(ang), jnp.sin(ang)
    p = pos.shape[0]
    ones = jnp.ones((p, DA_HEAD_DIM - ROPE_DIM), F32)
    zeros8 = jnp.zeros((p, half), F32)
    zeros48 = jnp.zeros((p, DA_HEAD_DIM - ROPE_DIM), F32)
    c64 = jnp.concatenate([cos, cos, ones], axis=1)
    sa64 = jnp.concatenate([-sin, zeros8, zeros48], axis=1)
    sb64 = jnp.concatenate([zeros8, sin, zeros48], axis=1)
    tile2 = lambda t: jnp.concatenate([t, t], axis=1)
    return tile2(c64), tile2(sa64), tile2(sb64)


def _in_proj(x2d, g, w_bf, tables, *, tm, tn, rope_cols, pos_blocks):
    m, d = x2d.shape
    n_cols = w_bf.shape[1]
    c_t, sa_t, sb_t = tables
    kern = functools.partial(_in_proj_kernel, tn=tn, rope_lo=rope_cols[0] // tn, rope_hi=rope_cols[1] // tn)
    tab_spec = pl.BlockSpec((tm, LANES), lambda i, n: (i % pos_blocks, 0))
    return pl.pallas_call(
        kern,
        grid=(m // tm, n_cols // tn),
        in_specs=[
            pl.BlockSpec((tm, d), lambda i, n: (i, 0)),
            pl.BlockSpec((1, d), lambda i, n: (0, 0)),
            pl.BlockSpec((d, tn), lambda i, n: (0, n)),
            tab_spec, tab_spec, tab_spec,
        ],
        out_specs=pl.BlockSpec((tm, tn), lambda i, n: (i, n)),
        out_shape=jax.ShapeDtypeStruct((m, n_cols), BF16),
        scratch_shapes=[pltpu.VMEM((tm, d), BF16)],
        compiler_params=_cparams(("parallel", "arbitrary")),
        name="in_proj",
    )(x2d, g, w_bf, c_t, sa_t, sb_t)


def _dot_nt(a, b):
    return lax.dot_general(a, b, (((1,), (1,)), ((), ())), preferred_element_type=F32)


def _suffix_sum_matrix(n):
    j = lax.broadcasted_iota(jnp.int32, (2 * n, n), 0)
    s = lax.broadcasted_iota(jnp.int32, (2 * n, n), 1)
    return jnp.where(jnp.where(j >= n, j - n, j) > s, 1.0, 0.0).astype(BF16)


def _sb_block(q, kb, vb, u2, run, acc, mask, scale):
    z = _dot_nt(q, kb) * scale
    sp = jnp.maximum(z, 0.0) + jnp.log(1.0 + jnp.exp(-jnp.abs(z)))
    if mask is not None:
        sp = jnp.where(mask, sp, 0.0)
    hi = sp.astype(BF16)
    lo = (sp - hi.astype(F32)).astype(BF16)
    cs = jnp.dot(jnp.concatenate([hi, lo], axis=1), u2, preferred_element_type=F32)
    w = jnp.exp(z - sp - cs - run)
    if mask is not None:
        w = jnp.where(mask, w, 0.0)
    acc = acc + jnp.dot(w.astype(BF16), vb, preferred_element_type=F32)
    run = run + (cs[:, :1] + sp[:, :1])
    return run, acc


def _sb_kernel(q_ref, k_ref, v_ref, km_ref, vm_ref, g_ref, o_ref, *, tq, tk, hp, scale):
    i = pl.program_id(2)
    nsub = tq // tk
    hd = SB_HEAD_DIM
    u_blk = _suffix_sum_matrix(tk)
    row = lax.broadcasted_iota(jnp.int32, (tk, tk), 0)
    col = lax.broadcasted_iota(jnp.int32, (tk, tk), 1)
    diag_mask = col < row
    heads = range(hp)

    def kv(blk, h):
        s0 = pl.multiple_of(blk * tk, tk)
        return k_ref[pl.ds(s0, tk), h * hd:(h + 1) * hd], v_ref[pl.ds(s0, tk), h * hd:(h + 1) * hd]

    runs, accs = [], []
    for h in heads:
        slab_runs, slab_accs = [], []
        for sl in range(nsub):
            qs = q_ref[sl * tk:(sl + 1) * tk, h * hd:(h + 1) * hd]
            run = jnp.zeros((tk, 1), F32)
            acc = jnp.zeros((tk, hd), F32)
            for c in range(sl, -1, -1):
                kb, vb = kv(i * nsub + c, h)
                run, acc = _sb_block(qs, kb, vb, u_blk, run, acc, diag_mask if c == sl else None, scale)
            slab_runs.append(run)
            slab_accs.append(acc)
        runs.append(jnp.concatenate(slab_runs, axis=0))
        accs.append(jnp.concatenate(slab_accs, axis=0))
    q = [q_ref[:, h * hd:(h + 1) * hd] for h in heads]

    def alive_flag(rs):
        lowest = functools.reduce(jnp.minimum, [jnp.min(r) for r in rs])
        return (lowest < SB_DEAD_RUN).astype(jnp.int32)

    def cond(carry):
        jb, alive, _, _ = carry
        return jnp.logical_and(jb >= 0, alive > 0)

    def body(carry):
        jb, _, rs, as_ = carry
        out = [_sb_block(q[h], *kv(jb, h), u_blk, rs[h], as_[h], None, scale) for h in heads]
        rs = tuple(o[0] for o in out)
        return jb - 1, alive_flag(rs), rs, tuple(o[1] for o in out)

    runs, accs = tuple(runs), tuple(accs)
    _, alive, runs, accs = lax.while_loop(cond, body, (i * nsub - 1, alive_flag(runs), runs, accs))

    def meta_block():
        mcol = lax.broadcasted_iota(jnp.int32, (tq, META_PAD), 1)
        u_meta = _suffix_sum_matrix(META_PAD)
        return tuple(_sb_block(q[h], km_ref[:, h * hd:(h + 1) * hd], vm_ref[:, h * hd:(h + 1) * hd], u_meta,
                               runs[h], accs[h], mcol < N_META, scale)[1] for h in heads)

    accs = lax.cond(alive > 0, meta_block, lambda: accs)

    for h in heads:
        acc = accs[h]
        ms = jnp.mean(acc * acc, axis=-1, keepdims=True)
        gain = g_ref[:, h * hd:(h + 1) * hd]
        o_ref[:, h * hd:(h + 1) * hd] = ((acc * lax.rsqrt(ms + RMS_EPS)) * gain).astype(BF16)


def _sb_attention(proj, proj_meta, g_sb, *, batch, seq, heads, tq, tk, hp):
    hd = SB_HEAD_DIM
    wd = hp * hd
    ng = heads // hp
    kern = functools.partial(_sb_kernel, tq=tq, tk=tk, hp=hp, scale=hd ** -0.5)
    return pl.pallas_call(
        kern,
        grid=(batch, ng, seq // tq),
        in_specs=[
            pl.BlockSpec((None, tq, wd), lambda b, h, i: (b, i, h)),
            pl.BlockSpec((None, seq, wd), lambda b, h, i: (b, 0, ng + h)),
            pl.BlockSpec((None, seq, wd), lambda b, h, i: (b, 0, 2 * ng + h)),
            pl.BlockSpec((META_PAD, wd), lambda b, h, i: (0, ng + h)),
            pl.BlockSpec((META_PAD, wd), lambda b, h, i: (0, 2 * ng + h)),
            pl.BlockSpec((1, wd), lambda b, h, i: (0, h)),
        ],
        out_specs=pl.BlockSpec((None, tq, wd), lambda b, h, i: (b, i, h)),
        out_shape=jax.ShapeDtypeStruct((batch, seq, heads * hd), BF16),
        compiler_params=_cparams(("parallel", "parallel", "arbitrary")),
        name="sb_attn",
    )(proj, proj, proj, proj_meta, proj_meta, g_sb)


def _with_ones(vb):
    return jnp.concatenate([vb, jnp.ones_like(vb)], axis=1)


def _da_block(q1, q2, kb, vb1, st, mask):
    m1, a1, m2, a2 = st

    def one(qc, m, a):
        s = _dot_nt(qc, kb)
        if mask is not None:
            s = jnp.where(mask, s, NEG_BIG)
        mn = jnp.maximum(m, jnp.max(s, axis=-1, keepdims=True))
        p = jnp.exp(s - mn)
        a = jnp.exp(m - mn) * a + jnp.dot(p.astype(BF16), vb1, preferred_element_type=F32)
        return mn, a

    m1, a1 = one(q1, m1, a1)
    m2, a2 = one(q2, m2, a2)
    return m1, a1, m2, a2


def _da_kernel(lq1_ref, lk1_ref, lq2_ref, lk2_ref, q_ref, k_ref, v_ref, km_ref, vm_ref, g_ref, o_ref,
               *, tq, tkv, ts, hp, scale, lam_init):
    i = pl.program_id(2)
    nsub = tq // ts
    hd = DA_V_DIM
    heads = range(hp)
    lam = (jnp.exp(jnp.sum(lq1_ref[...] * lk1_ref[...], axis=-1, keepdims=True))
           - jnp.exp(jnp.sum(lq2_ref[...] * lk2_ref[...], axis=-1, keepdims=True)) + lam_init)

    def cols(ref, rows, h):
        return ref[rows, h * hd:(h + 1) * hd]

    lane = lax.broadcasted_iota(jnp.int32, (tq, hd), 1)
    q1, q2 = [], []
    for h in heads:
        qs = cols(q_ref, slice(None), h) * jnp.asarray(scale, BF16)
        zero = jnp.zeros_like(qs)
        q1.append(jnp.where(lane < DA_HEAD_DIM, qs, zero))
        q2.append(jnp.where(lane >= DA_HEAD_DIM, qs, zero))

    neg = jnp.full((tq, 1), NEG_BIG, F32)
    za = jnp.zeros((tq, 2 * hd), F32)
    st = tuple((neg, za, neg, za) for _ in heads)

    def body(j, carry):
        carry = list(carry)
        for c in range(tq // tkv):
            rows = pl.ds(pl.multiple_of(j * tq + c * tkv, tkv), tkv)
            for h in heads:
                carry[h] = _da_block(q1[h], q2[h], cols(k_ref, rows, h), _with_ones(cols(v_ref, rows, h)),
                                     carry[h], None)
        return tuple(carry)

    st = lax.fori_loop(0, i, body, st)

    shift = CHUNK.bit_length() - 1
    s0 = pl.multiple_of(i * tq, tq)
    for h in heads:
        outs = []
        for sl in range(nsub):
            rows = slice(sl * ts, (sl + 1) * ts)
            width = (sl + 1) * ts
            kb = jnp.concatenate([cols(km_ref, slice(None), h), cols(k_ref, pl.ds(s0, width), h)], axis=0)
            vb = jnp.concatenate([cols(vm_ref, slice(None), h), cols(v_ref, pl.ds(s0, width), h)], axis=0)
            col = lax.broadcasted_iota(jnp.int32, (ts, META_PAD + width), 1)
            row = lax.broadcasted_iota(jnp.int32, (ts, META_PAD + width), 0) + sl * ts
            frame_vis = jnp.logical_and(col >= META_PAD,
                                        jnp.right_shift(col - META_PAD, shift) <= jnp.right_shift(row, shift))
            vis = jnp.logical_or(col < N_META, frame_vis)
            sth = _da_block(q1[h][rows], q2[h][rows], kb, _with_ones(vb), tuple(t[rows] for t in st[h]), vis)
            _, a1, _, a2 = sth
            outs.append(a1[:, :hd] / a1[:, hd:] - lam * (a2[:, :hd] / a2[:, hd:]))
        o = jnp.concatenate(outs, axis=0)
        ms = jnp.mean(o * o, axis=-1, keepdims=True)
        gain = g_ref[:, h * hd:(h + 1) * hd]
        o_ref[:, h * hd:(h + 1) * hd] = (((o * lax.rsqrt(ms + RMS_EPS)) * gain) * (1.0 - lam_init)).astype(BF16)


def _da_attention(proj, proj_meta, lams, g_da, *, batch, seq, heads, tq, tkv, ts, hp, col0, lam_init):
    hd = DA_V_DIM
    wd = hp * hd
    ng = heads // hp
    qb = col0 // wd
    kb, vb = qb + ng, qb + 2 * ng
    kern = functools.partial(_da_kernel, tq=tq, tkv=tkv, ts=ts, hp=hp, scale=DA_HEAD_DIM ** -0.5,
                             lam_init=lam_init)
    lam_spec = pl.BlockSpec((1, DA_HEAD_DIM), lambda b, h, i: (0, 0))
    return pl.pallas_call(
        kern,
        grid=(batch, ng, seq // tq),
        in_specs=[
            lam_spec, lam_spec, lam_spec, lam_spec,
            pl.BlockSpec((None, tq, wd), lambda b, h, i: (b, i, qb + h)),
            pl.BlockSpec((None, seq, wd), lambda b, h, i: (b, 0, kb + h)),
            pl.BlockSpec((None, seq, wd), lambda b, h, i: (b, 0, vb + h)),
            pl.BlockSpec((META_PAD, wd), lambda b, h, i: (0, kb + h)),
            pl.BlockSpec((META_PAD, wd), lambda b, h, i: (0, vb + h)),
            pl.BlockSpec((1, wd), lambda b, h, i: (0, h)),
        ],
        out_specs=pl.BlockSpec((None, tq, wd), lambda b, h, i: (b, i, h)),
        out_shape=jax.ShapeDtypeStruct((batch, seq, heads * hd), BF16),
        compiler_params=_cparams(("parallel", "parallel", "arbitrary")),
        name="da_attn",
    )(*lams, proj, proj, proj, proj_meta, proj_meta, g_da)


def _pack_bf16_pairs(x):
    n = x.shape[1] // 2
    lo = lax.bitcast_convert_type(x[:, :n].astype(BF16).astype(F32), jnp.int32)
    hi = lax.bitcast_convert_type(x[:, n:].astype(BF16).astype(F32), jnp.int32)
    return jnp.bitwise_or(lax.shift_right_logical(lo, 16), jnp.bitwise_and(hi, jnp.int32(-65536)))


def _unpack_bf16_pairs(w):
    lo = lax.bitcast_convert_type(lax.shift_left(w, 16), F32)
    hi = lax.bitcast_convert_type(jnp.bitwise_and(w, jnp.int32(-65536)), F32)
    return jnp.concatenate([lo.astype(BF16), hi.astype(BF16)], axis=1)


def _sc_gather_rows(table, idx):
    n_idx = idx.shape[0]
    width = table.shape[1]
    n_workers = SC_CORES * SC_SUBCORES
    per_worker = n_idx // n_workers
    win = SC_GATHER_ROWS
    assert n_idx % (n_workers * win) == 0
    mesh = plsc.VectorSubcoreMesh(core_axis_name="c", subcore_axis_name="s")

    n_win = per_worker // win
    assert n_win % 2 == 0
    buf = lambda: pltpu.VMEM((win, width), table.dtype)
    dma = pltpu.SemaphoreType.DMA

    @functools.partial(
        pl.kernel, mesh=mesh,
        out_type=jax.ShapeDtypeStruct((n_idx, width), table.dtype),
        scratch_types=[pltpu.VMEM((n_win, win), jnp.int32), buf(), buf(), dma, dma, dma, dma],
    )
    def gather(table_hbm, idx_hbm, out_hbm, idx_v, rows0, rows1, gsem0, gsem1, wsem0, wsem1):
        worker = lax.axis_index("s") * SC_CORES + lax.axis_index("c")
        base = worker * per_worker
        pltpu.sync_copy(idx_hbm.at[pl.ds(worker * n_win, n_win)], idx_v)

        def gather_copy(t, rows_v, sem):
            return pltpu.make_async_copy(table_hbm.at[idx_v.at[t]], rows_v, sem)

        def writeback(t, rows_v, sem):
            return pltpu.make_async_copy(rows_v, out_hbm.at[pl.ds(pl.multiple_of(base + t * win, win), win)], sem)

        gather_copy(0, rows0, gsem0).start()

        @pl.loop(0, n_win, step=2)
        def _(t):
            gather_copy(t, rows0, gsem0).wait()
            writeback(t, rows0, wsem0).start()

            @pl.when(t > 0)
            def _():
                writeback(t - 1, rows1, wsem1).wait()

            gather_copy(t + 1, rows1, gsem1).start()
            gather_copy(t + 1, rows1, gsem1).wait()
            writeback(t + 1, rows1, wsem1).start()
            writeback(t, rows0, wsem0).wait()

            @pl.when(t + 2 < n_win)
            def _():
                gather_copy(t + 2, rows0, gsem0).start()

        writeback(n_win - 1, rows1, wsem1).wait()

    return gather(table, idx.reshape(n_idx // win, win))


def _sc_scatter_rows(src, dest_km, n_rows):
    n_tok, width = src.shape
    n_slots = dest_km.shape[0]
    n_workers = SC_CORES * SC_SUBCORES
    win = SC_GATHER_ROWS
    per_worker = n_tok // n_workers
    n_win = per_worker // win
    assert n_tok % (n_workers * win) == 0 and n_win % 2 == 0
    mesh = plsc.VectorSubcoreMesh(core_axis_name="c", subcore_axis_name="s")
    buf = lambda: pltpu.VMEM((win, width), src.dtype)
    dma = pltpu.SemaphoreType.DMA

    @functools.partial(
        pl.kernel, mesh=mesh,
        out_type=jax.ShapeDtypeStruct((n_rows, width), src.dtype),
        scratch_types=[pltpu.VMEM((n_slots, n_win, win), jnp.int32), buf(), buf(), dma, dma, dma, dma],
    )
    def scatter(src_hbm, idx_hbm, out_hbm, idx_v, rows0, rows1, lsem0, lsem1, ssem0, ssem1):
        worker = lax.axis_index("s") * SC_CORES + lax.axis_index("c")
        base = worker * per_worker
        for k in range(n_slots):
            pltpu.sync_copy(idx_hbm.at[k, pl.ds(worker * n_win, n_win)], idx_v.at[k])

        def load(t, rows_v, sem):
            return pltpu.make_async_copy(src_hbm.at[pl.ds(pl.multiple_of(base + t * win, win), win)], rows_v, sem)

        def start_scatters(t, rows_v, sem):
            for k in range(n_slots):
                pltpu.make_async_copy(rows_v, out_hbm.at[idx_v.at[k, t]], sem).start()

        def wait_scatters(t, rows_v, sem):
            for k in range(n_slots):
                pltpu.make_async_copy(rows_v, out_hbm.at[idx_v.at[k, t]], sem).wait()

        load(0, rows0, lsem0).start()

        @pl.loop(0, n_win, step=2)
        def _(t):
            load(t, rows0, lsem0).wait()
            start_scatters(t, rows0, ssem0)

            @pl.when(t > 0)
            def _():
                wait_scatters(t - 1, rows1, ssem1)

            load(t + 1, rows1, lsem1).start()
            load(t + 1, rows1, lsem1).wait()
            start_scatters(t + 1, rows1, ssem1)
            wait_scatters(t, rows0, ssem0)

            @pl.when(t + 2 < n_win)
            def _():
                load(t + 2, rows0, lsem0).start()

        wait_scatters(n_win - 1, rows1, ssem1)

    return scatter(src, dest_km.reshape(n_slots, n_tok // win, win))


def _split3_dot(a, b_hi, b_lo):
    a_hi = a.astype(BF16)
    a_lo = (a - a_hi.astype(F32)).astype(BF16)
    return (jnp.dot(a_hi, b_hi, preferred_element_type=F32)
            + jnp.dot(a_hi, b_lo, preferred_element_type=F32)
            + jnp.dot(a_lo, b_hi, preferred_element_type=F32))


def _out_proj_kernel(ms_ref, md_ref, ws_ref, wd_ref, x_ref, g_ref, wr_hi_ref, wr_lo_ref, br_ref,
                     h_ref, u_ref, e_ref, p_ref, c_ref, *, n_exp):
    h = (x_ref[...]
         + jnp.dot(ms_ref[...], ws_ref[...], preferred_element_type=F32)
         + jnp.dot(md_ref[...], wd_ref[...], preferred_element_type=F32))
    h_ref[...] = h
    msq = jnp.mean(h * h, axis=-1, keepdims=True)
    u = (h * lax.rsqrt(msq + RMS_EPS)) * g_ref[...]
    u_ref[...] = _pack_bf16_pairs(u)

    logits = _split3_dot(u, wr_hi_ref[...], wr_lo_ref[...]) + br_ref[...]
    lane = lax.broadcasted_iota(jnp.int32, logits.shape, 1).astype(F32)
    work = logits
    tops, idxs = [], []
    for _ in range(TOP_K):
        mx = jnp.max(work, axis=-1, keepdims=True)
        ix = jnp.min(jnp.where(work == mx, lane, float(n_exp)), axis=-1, keepdims=True)
        tops.append(mx)
        idxs.append(ix)
        work = jnp.where(lane == ix, -jnp.inf, work)
    ex = [jnp.exp(t - tops[0]) for t in tops]
    den = ex[0] + ex[1] + ex[2] + ex[3]
    kl = lax.broadcasted_iota(jnp.int32, (logits.shape[0], TOP_K), 1)
    e_out = jnp.zeros((logits.shape[0], TOP_K), F32)
    p_out = jnp.zeros((logits.shape[0], TOP_K), F32)
    for k in range(TOP_K):
        e_out = jnp.where(kl == k, idxs[k], e_out)
        p_out = jnp.where(kl == k, ex[k] / den, p_out)
    e_ref[...] = e_out.astype(jnp.int32)
    p_ref[...] = p_out
    hits = jnp.zeros(logits.shape, F32)
    for k in range(TOP_K):
        hits = hits + jnp.where(lane == idxs[k], 1.0, 0.0)
    c_ref[...] = jnp.sum(hits, axis=0, keepdims=True)


def _out_proj(mix_sb, mix_da, w_sb, w_da, x2d, g_ffn, wr_hi, wr_lo, b_r, *, tm):
    m, d = x2d.shape
    ks, kd = mix_sb.shape[1], mix_da.shape[1]
    n_exp = wr_hi.shape[1]
    kern = functools.partial(_out_proj_kernel, n_exp=n_exp)
    const = lambda shape: pl.BlockSpec(shape, lambda i: (0, 0))
    return pl.pallas_call(
        kern,
        grid=(m // tm,),
        in_specs=[
            pl.BlockSpec((tm, ks), lambda i: (i, 0)),
            pl.BlockSpec((tm, kd), lambda i: (i, 0)),
            const((ks, d)), const((kd, d)),
            pl.BlockSpec((tm, d), lambda i: (i, 0)),
            const((1, d)), const((d, n_exp)), const((d, n_exp)), const((1, n_exp)),
        ],
        out_specs=[
            pl.BlockSpec((tm, d), lambda i: (i, 0)),
            pl.BlockSpec((tm, d // 2), lambda i: (i, 0)),
            pl.BlockSpec((tm, TOP_K), lambda i: (i, 0)),
            pl.BlockSpec((tm, TOP_K), lambda i: (i, 0)),
            pl.BlockSpec((None, 1, n_exp), lambda i: (i, 0, 0)),
        ],
        out_shape=[
            jax.ShapeDtypeStruct((m, d), F32),
            jax.ShapeDtypeStruct((m, d // 2), jnp.int32),
            jax.ShapeDtypeStruct((m, TOP_K), jnp.int32),
            jax.ShapeDtypeStruct((m, TOP_K), F32),
            jax.ShapeDtypeStruct((m // tm, 1, n_exp), F32),
        ],
        compiler_params=_cparams(("parallel",)),
        name="out_proj_router",
    )(mix_sb, mix_da, w_sb, w_da, x2d, g_ffn, wr_hi, wr_lo, b_r)


def _moe_kernel(be_ref, bc_ref, rb_ref, x_ref, wg_ref, wu_ref, wda_ref, wdb_ref, bg_ref, bup_ref, bda_ref, bdb_ref,
                o_ref, act_ref, *, nf, ts):
    blk = pl.program_id(0)
    s = pl.program_id(1)
    cnt = bc_ref[blk]
    tg = 2 * ts
    tm = x_ref.shape[0]

    def row_groups(run, skip):
        for lo in range(0, tm, tg):
            size = min(tg, tm - lo)

            @pl.when(cnt > lo + size - ts)
            def _():
                run(lo, size)

            if size > ts:
                @pl.when(jnp.logical_and(cnt > lo, cnt <= lo + ts))
                def _():
                    run(lo, ts)
                    skip(lo + ts, ts)

            @pl.when(cnt <= lo)
            def _():
                skip(lo, size)

    @pl.when(jnp.logical_and(cnt > 0, s < nf))
    def _():
        slot = jnp.minimum(s, nf - 1)

        def up_proj(r0, nrows):
            x = _unpack_bf16_pairs(x_ref[r0:r0 + nrows, :])
            g = jnp.dot(x, wg_ref[...].astype(BF16), preferred_element_type=F32) + bg_ref[...]
            u = jnp.dot(x, wu_ref[...].astype(BF16), preferred_element_type=F32) + bup_ref[...]
            gate = jnp.minimum(g, SWIGLU_LIMIT)
            up = jnp.clip(u, -SWIGLU_LIMIT, SWIGLU_LIMIT)
            act = (up + 1.0) * gate * jax.nn.sigmoid(SWIGLU_ALPHA * gate)
            act_ref[slot, r0:r0 + nrows, :] = act.astype(BF16)

        row_groups(up_proj, lambda r0, nrows: None)

    @pl.when(jnp.logical_and(cnt > 0, s >= nf))
    def _():
        tf = act_ref.shape[2]

        def down_proj(r0, nrows):
            ya = bda_ref[...]
            yb = bdb_ref[...]
            for f in range(nf):
                a = act_ref[f, r0:r0 + nrows, :]
                wda = wda_ref[f * tf:(f + 1) * tf, :].astype(BF16)
                wdb = wdb_ref[f * tf:(f + 1) * tf, :].astype(BF16)
                ya = ya + jnp.dot(a, wda, preferred_element_type=F32)
                yb = yb + jnp.dot(a, wdb, preferred_element_type=F32)
            o_ref[r0:r0 + nrows, :] = _pack_bf16_pairs(jnp.concatenate([ya, yb], axis=1))

        def zero_fill(r0, nrows):
            o_ref[r0:r0 + nrows, :] = jnp.zeros((nrows, o_ref.shape[1]), o_ref.dtype)

        row_groups(down_proj, zero_fill)


def _moe_ffn(blk_e, blk_cnt, blk_row, xs, w_gu, w_dn, b_gu, b_dn, *, tm, ts, tf, tn):
    n_rows = xs.shape[0]
    n_exp, d, f2 = w_gu.shape
    d_ff = f2 // 2
    nf = d_ff // tf
    nn = d // tn
    n_blk = n_rows // tm
    b_gu3 = b_gu.reshape(n_exp, 1, f2)
    b_dn3 = b_dn.reshape(n_exp, 1, d)

    def fi(s, bc, b):
        return jnp.where(bc[b] > 0, jnp.minimum(s, nf - 1), nf - 1)

    nh = nn // 2

    def ni(s, bc, b):
        return jnp.where(bc[b] > 0, jnp.clip(s - nf, 0, nh - 1), nh - 1)

    grid_spec = pltpu.PrefetchScalarGridSpec(
        num_scalar_prefetch=3,
        grid=(n_blk, nf + nh),
        in_specs=[
            pl.BlockSpec((tm, d // 2), lambda b, s, be, bc, rb: (rb[b], 0)),
            pl.BlockSpec((None, d, tf), lambda b, s, be, bc, rb: (be[b], 0, fi(s, bc, b))),
            pl.BlockSpec((None, d, tf), lambda b, s, be, bc, rb: (be[b], 0, nf + fi(s, bc, b))),
            pl.BlockSpec((None, d_ff, tn), lambda b, s, be, bc, rb: (be[b], 0, ni(s, bc, b))),
            pl.BlockSpec((None, d_ff, tn), lambda b, s, be, bc, rb: (be[b], 0, nh + ni(s, bc, b))),
            pl.BlockSpec((None, 1, tf), lambda b, s, be, bc, rb: (be[b], 0, fi(s, bc, b))),
            pl.BlockSpec((None, 1, tf), lambda b, s, be, bc, rb: (be[b], 0, nf + fi(s, bc, b))),
            pl.BlockSpec((None, 1, tn), lambda b, s, be, bc, rb: (be[b], 0, ni(s, bc, b))),
            pl.BlockSpec((None, 1, tn), lambda b, s, be, bc, rb: (be[b], 0, nh + ni(s, bc, b))),
        ],
        out_specs=pl.BlockSpec((tm, tn), lambda b, s, be, bc, rb: (rb[b], ni(s, bc, b))),
        scratch_shapes=[pltpu.VMEM((nf, tm, tf), BF16)],
    )
    return pl.pallas_call(
        functools.partial(_moe_kernel, nf=nf, ts=ts),
        grid_spec=grid_spec,
        out_shape=jax.ShapeDtypeStruct((n_rows, d // 2), jnp.int32),
        compiler_params=_cparams(("arbitrary", "arbitrary")),
        name="moe_ffn",
    )(blk_e, blk_cnt, blk_row, xs, w_gu, w_gu, w_dn, w_dn, b_gu3, b_gu3, b_dn3, b_dn3)


def _final_kernel(h_ref, y0_ref, y1_ref, y2_ref, y3_ref, p_ref, g_ref, o_ref):
    p = p_ref[...]
    y = None
    for k, y_ref in enumerate((y0_ref, y1_ref, y2_ref, y3_ref)):
        t = _unpack_bf16_pairs(y_ref[...]).astype(F32) * p[:, k:k + 1]
        y = t if y is None else y + t
    h = h_ref[...] + y
    ms = jnp.mean(h * h, axis=-1, keepdims=True)
    o_ref[...] = (h * lax.rsqrt(ms + RMS_EPS)) * g_ref[...]


def _final(h1, y_km, gates, g, *, tm):
    m, d = h1.shape
    nb = m // tm
    assert TOP_K == 4
    y_specs = [pl.BlockSpec((tm, d // 2), functools.partial(lambda i, k: (k * nb + i, 0), k=k))
               for k in range(TOP_K)]
    return pl.pallas_call(
        _final_kernel,
        grid=(nb,),
        in_specs=([pl.BlockSpec((tm, d), lambda i: (i, 0))] + y_specs
                  + [pl.BlockSpec((tm, TOP_K), lambda i: (i, 0)), pl.BlockSpec((1, d), lambda i: (0, 0))]),
        out_specs=pl.BlockSpec((tm, d), lambda i: (i, 0)),
        out_shape=jax.ShapeDtypeStruct((m, d), F32),
        compiler_params=_cparams(("parallel",)),
        name="final_norm",
    )(h1, y_km, y_km, y_km, y_km, gates, g)


def _moe_blocks(n_assign, n_exp, tm):
    return -(-n_assign // tm) + n_exp


def _routing(top_e, counts, tm):
    n_exp = counts.shape[0]
    n_tok = top_e.shape[0]
    n_assign = n_tok * TOP_K
    flat_e = top_e.reshape(-1)
    padded = (counts + tm - 1) // tm * tm
    pad_end = jnp.cumsum(padded).astype(jnp.int32)
    pad_start = pad_end - padded
    grp_start = jnp.cumsum(counts).astype(jnp.int32) - counts
    n_blk = _moe_blocks(n_assign, n_exp, tm)
    blk_start = jnp.arange(n_blk, dtype=jnp.int32) * tm
    blk_e = jnp.minimum(jnp.sum((blk_start[:, None] >= pad_end[None, :]).astype(jnp.int32), axis=1), n_exp - 1)
    blk_off = blk_start - pad_start[blk_e]
    blk_cnt = jnp.clip(counts[blk_e] - blk_off, 0, tm).astype(jnp.int32)
    order = jnp.argsort(flat_e, stable=True).astype(jnp.int32)
    rank = jnp.argsort(order).astype(jnp.int32)
    delta = pad_start - grp_start
    e2d = flat_e.reshape(-1, LANES)
    hit = e2d[None] == jnp.arange(n_exp, dtype=jnp.int32)[:, None, None]
    dest_flat = rank + jnp.sum(jnp.where(hit, delta[:, None, None], 0), axis=0).reshape(-1)
    last_used = jnp.maximum(pad_end[-1] // tm - 1, 0)
    blk_e = jnp.where(blk_cnt > 0, blk_e, blk_e[last_used]).astype(jnp.int32)
    blk_row = jnp.where(blk_cnt > 0, jnp.arange(n_blk, dtype=jnp.int32), last_used).astype(jnp.int32)
    return dest_flat, blk_e, blk_cnt, blk_row, n_blk * tm


def kernel(x, meta_tokens, g_mix, w_in, lam_q1, lam_k1, lam_q2, lam_k2, g_sb_out, g_da_out, w_out, g_ffn,
           w_router, b_router, w_gate_up, b_gate_up, w_down, b_down, g_final):
    b, s, d = x.shape
    depth = w_in.shape[0]
    assert depth == 1, "single-layer trunk"
    layer = 0
    sb_heads = (d // 2) // SB_HEAD_DIM
    da_heads = (d // 2) // DA_V_DIM
    sb_w = sb_heads * SB_HEAD_DIM
    da_col0 = 3 * sb_w
    da_qk_w = da_heads * 2 * DA_HEAD_DIM
    n_exp = w_router.shape[-1]
    lam_init = 0.8 - 0.6 * math.exp(-0.3 * layer)

    tm_proj, tn_proj = TM_PROJ, TN_PROJ
    tq, tk_sb, ts_da = TQ_ATTN, TK_SB, TS_DA
    tm_out = TM_OUT
    tm_moe, tf_moe = TM_MOE, TF_MOE
    tm_fin = TM_FINAL

    x2d = x.reshape(b * s, d)
    w_in_bf = w_in.reshape(w_in.shape[1:]).astype(BF16)
    g_mix2 = g_mix[layer].reshape(1, d)
    rope_cols = (da_col0, da_col0 + 2 * da_qk_w)

    pos_f = N_META + jnp.arange(s, dtype=jnp.int32)
    proj = _in_proj(x2d, g_mix2, w_in_bf, _rope_tables(pos_f), tm=tm_proj, tn=tn_proj,
                    rope_cols=rope_cols, pos_blocks=s // tm_proj)
    meta_pad = jnp.zeros((META_PAD, d), x.dtype).at[:N_META].set(meta_tokens.astype(x.dtype))
    pos_m = jnp.arange(META_PAD, dtype=jnp.int32)
    proj_meta = _in_proj(meta_pad, g_mix2, w_in_bf, _rope_tables(pos_m), tm=META_PAD, tn=tn_proj,
                         rope_cols=rope_cols, pos_blocks=1)
    proj3 = proj.reshape(b, s, -1)

    mix_sb = _sb_attention(proj3, proj_meta, g_sb_out[layer].reshape(1, -1), batch=b, seq=s, heads=sb_heads,
                           tq=tq, tk=tk_sb, hp=HEADS_PER_STEP_SB)
    lams = tuple(t[layer].reshape(1, DA_HEAD_DIM).astype(F32) for t in (lam_q1, lam_k1, lam_q2, lam_k2))
    mix_da = _da_attention(proj3, proj_meta, lams, g_da_out[layer].reshape(1, -1), batch=b, seq=s, heads=da_heads,
                           tq=TQ_DA, tkv=TKV_DA, ts=ts_da, hp=HEADS_PER_STEP_DA, col0=da_col0, lam_init=lam_init)

    w_out_bf = w_out[layer].astype(BF16)
    wr = w_router[layer]
    wr_hi = wr.astype(BF16)
    wr_lo = (wr - wr_hi.astype(F32)).astype(BF16)
    h1, u_packed, top_e, gates, tile_counts = _out_proj(
        mix_sb.reshape(b * s, -1), mix_da.reshape(b * s, -1), w_out_bf[:sb_w], w_out_bf[sb_w:], x2d,
        g_ffn[layer].reshape(1, d), wr_hi, wr_lo, b_router[layer].reshape(1, n_exp), tm=tm_out)

    counts = jnp.sum(tile_counts, axis=(0, 1)).astype(jnp.int32)
    dest_flat, blk_e, blk_cnt, blk_row, n_rows = _routing(top_e, counts, tm_moe)
    dest_km = dest_flat.reshape(b * s, TOP_K).T
    xs = _sc_scatter_rows(u_packed, dest_km, n_rows)
    rows = _moe_ffn(blk_e, blk_cnt, blk_row, xs, w_gate_up.reshape(w_gate_up.shape[1:]),
                    w_down.reshape(w_down.shape[1:]), b_gate_up[layer], b_down[layer],
                    tm=tm_moe, ts=TS_MOE, tf=tf_moe, tn=TN_MOE)
    y_km = _sc_gather_rows(rows, dest_km.reshape(-1))

    out = _final(h1, y_km, gates, g_final.reshape(1, d), tm=tm_fin)
    return out.reshape(b, s, d)
```
